```python
import jax, jax.numpy as jnp
from jax import lax
import numpy as np

D_MODEL = 1024
BATCH = 32
SEQ = 2048
DEPTH = 1

GRID_W = 64
CTX_LEN = 256
NA_HEADS = 8
NA_HEAD_DIM = 64
NA_WIDTH = NA_HEADS * NA_HEAD_DIM
WIN_H = 8
WIN_W = 16
NA_KEY_COLS = 2 * WIN_W
ML_HEADS = 4
ML_QK_DIM = 64
ML_V_DIM = 128
ML_QK_WIDTH = ML_HEADS * ML_QK_DIM
ML_WIDTH = ML_HEADS * ML_V_DIM
ML_CHUNK = 64
GATE_SOFTCAP = 15.0
ROPE_BASE = 10000.0
N_EXPERTS = 32
TOP_K = 4
D_EXPERT = D_MODEL
SWIGLU_ALPHA = 1.702
SWIGLU_LIMIT = 7.0
EXPERT_BLOCK = 128
NORM_EPS = 1e-6
NEG_INF = -1e30
CTX_SPLITS = (NA_WIDTH, NA_WIDTH, ML_QK_WIDTH, ML_WIDTH, 4 * ML_HEADS)
LAT_SPLITS = (NA_WIDTH, ML_QK_WIDTH, ML_WIDTH, D_MODEL, D_MODEL)
IN_SPLITS = CTX_SPLITS + LAT_SPLITS
N_CTX_COLS = 2 * NA_WIDTH + ML_QK_WIDTH + ML_WIDTH + 4 * ML_HEADS
N_IN_COLS = N_CTX_COLS + NA_WIDTH + ML_QK_WIDTH + ML_WIDTH + 2 * D_MODEL

kernel_name = 'hybrid_natten_mlstm_moe_dit_layer'


def _rmsnorm(x, g):
    xf = x.astype(jnp.float32)
    y = xf * lax.rsqrt(jnp.mean(xf * xf, axis=-1, keepdims=True) + NORM_EPS)
    return (y * g.astype(jnp.float32)).astype(x.dtype)


def _split(a, sizes):
    return jnp.split(a, np.cumsum(sizes)[:-1].tolist(), axis=-1)


def _na_column_tables():
    n_qb = GRID_W // WIN_W
    qcol = np.arange(GRID_W).reshape(n_qb, WIN_W)
    kstart = np.clip(np.arange(n_qb) * WIN_W - WIN_W // 2, 0, GRID_W - NA_KEY_COLS)
    kcol = kstart[:, None] + np.arange(NA_KEY_COLS)
    cs = np.clip(qcol - WIN_W // 2, 0, GRID_W - WIN_W)
    kc = kcol[:, None, :]
    mask = (kc >= cs[:, :, None]) & (kc < cs[:, :, None] + WIN_W)
    dc = np.clip(kc - qcol[:, :, None], -(WIN_W - 1), WIN_W - 1) + (WIN_W - 1)
    return kcol, mask, dc


def _neighbourhood_attention(q, k, v, k_ctx, v_ctx, rpb):
    B, S, _ = q.shape
    rows = S // GRID_W
    wr = min(WIN_H, rows)
    n_qb = GRID_W // WIN_W
    kcol, mask, dc = _na_column_tables()
    grid = (B, rows, GRID_W, NA_HEADS, NA_HEAD_DIM)
    q = q.reshape(grid) * NA_HEAD_DIM ** -0.5
    k = k.reshape(grid)
    v = v.reshape(grid)
    k_ctx = k_ctx.reshape(B, -1, NA_HEADS, NA_HEAD_DIM)
    v_ctx = v_ctx.reshape(B, -1, NA_HEADS, NA_HEAD_DIM)
    n_loc = wr * NA_KEY_COLS

    def one_row(r):
        rs = jnp.clip(r - WIN_H // 2, 0, rows - wr)
        qb = lax.dynamic_index_in_dim(q, r, axis=1, keepdims=False).reshape(B, n_qb, WIN_W, NA_HEADS, NA_HEAD_DIM)
        kb = lax.dynamic_slice_in_dim(k, rs, wr, axis=1)[:, :, kcol]
        vb = lax.dynamic_slice_in_dim(v, rs, wr, axis=1)[:, :, kcol]
        dr = rs + jnp.arange(wr) - r + (WIN_H - 1)
        bias = rpb[:, dr[None, None, :, None], dc[:, :, None, :]].astype(jnp.float32)
        s_loc = jnp.einsum('bjqhd,brjkhd->bhjqrk', qb, kb).astype(jnp.float32) + bias
        s_loc = jnp.where(mask[:, :, None, :], s_loc, NEG_INF).reshape(B, NA_HEADS, n_qb, WIN_W, n_loc)
        s_ctx = jnp.einsum('bjqhd,bchd->bhjqc', qb, k_ctx).astype(jnp.float32)
        p = jax.nn.softmax(jnp.concatenate([s_loc, s_ctx], axis=-1), axis=-1).astype(v.dtype)
        p_loc = p[..., :n_loc].reshape(B, NA_HEADS, n_qb, WIN_W, wr, NA_KEY_COLS)
        o = (jnp.einsum('bhjqrk,brjkhd->bjqhd', p_loc, vb)
             + jnp.einsum('bhjqc,bchd->bjqhd', p[..., n_loc:], v_ctx))
        return o.reshape(B, GRID_W, NA_WIDTH)

    o = lax.map(one_row, jnp.arange(rows, dtype=jnp.int32))
    return jnp.moveaxis(o, 0, 1).reshape(B, S, NA_WIDTH)


def _axial_rope(x, row, col):
    n_freq = ML_QK_DIM // 4
    inv_freq = ROPE_BASE ** (-jnp.arange(n_freq, dtype=jnp.float32) / n_freq)

    def rotate(xh, pos):
        ang = pos[:, None] * inv_freq
        cos = jnp.cos(ang)[:, None, :]
        sin = jnp.sin(ang)[:, None, :]
        x1, x2 = xh[..., :n_freq], xh[..., n_freq:]
        return jnp.concatenate([x1 * cos - x2 * sin, x2 * cos + x1 * sin], axis=-1)

    half = ML_QK_DIM // 2
    return jnp.concatenate([rotate(x[..., :half], row), rotate(x[..., half:], col)], axis=-1)


def _to_chunks(a):
    B, H, T = a.shape[:3]
    a = a.reshape((B, H, T // ML_CHUNK, ML_CHUNK) + a.shape[3:])
    return jnp.moveaxis(a, 2, 0)


def _mlstm_chunkwise(q, k, v, log_i, log_f, state):
    tril = np.tril(np.ones((ML_CHUNK, ML_CHUNK), dtype=bool))
    with_out = q is not None

    def step(carry, inp):
        C, n, m = carry
        if with_out:
            qc, kc, vc, li, lf = inp
        else:
            kc, vc, li, lf = inp
        b = jnp.cumsum(lf, axis=-1)
        bL = b[..., -1]
        dec = bL[..., None] - b + li
        m_new = jnp.maximum(bL + m, jnp.max(dec, axis=-1))
        w_s = jnp.exp(dec - m_new[..., None])
        w_c = jnp.exp(bL + m - m_new)
        C_new = w_c[..., None, None] * C + jnp.einsum('bhs,bhsd,bhse->bhde', w_s, kc, vc)
        n_new = w_c[..., None] * n + jnp.einsum('bhs,bhsd->bhd', w_s, kc)
        if not with_out:
            return (C_new, n_new, m_new), None
        d_mat = jnp.where(tril, b[..., :, None] - b[..., None, :] + li[..., None, :], -jnp.inf)
        m_inter = b + m[..., None]
        m_t = jnp.maximum(m_inter, jnp.max(d_mat, axis=-1))
        w_intra = jnp.exp(d_mat - m_t[..., None])
        w_inter = jnp.exp(m_inter - m_t)
        s = jnp.einsum('bhtd,bhsd->bhts', qc, kc) * w_intra
        num = (w_inter[..., None] * jnp.einsum('bhtd,bhde->bhte', qc, C)
               + jnp.einsum('bhts,bhse->bhte', s, vc))
        den = w_inter * jnp.einsum('bhtd,bhd->bht', qc, n) + jnp.sum(s, axis=-1)
        h = num / jnp.maximum(jnp.abs(den), jnp.exp(-m_t))[..., None]
        return (C_new, n_new, m_new), h

    if with_out:
        xs = (_to_chunks(q), _to_chunks(k), _to_chunks(v), _to_chunks(log_i), _to_chunks(log_f))
    else:
        xs = (_to_chunks(k), _to_chunks(v), _to_chunks(log_i), _to_chunks(log_f))
    final, hs = lax.scan(step, state, xs)
    if not with_out:
        return final
    hs = jnp.moveaxis(hs, 0, 2)
    return hs.reshape(hs.shape[0], hs.shape[1], -1, hs.shape[-1])


def _mlstm_branch(q, k, v, o_pre, gates, k_ctx, v_ctx, gates_ctx, b_gates, g_head, row, col):
    B, S, _ = q.shape
    n_ctx = k_ctx.shape[1]
    f32 = jnp.float32
    qk_scale = ML_QK_DIM ** -0.5

    def heads_first(a):
        return jnp.swapaxes(a, 1, 2)

    def flip(a):
        return jnp.flip(a, axis=2)

    q = heads_first(_axial_rope(q.reshape(B, S, ML_HEADS, ML_QK_DIM).astype(f32), row, col))
    k = heads_first(_axial_rope(k.reshape(B, S, ML_HEADS, ML_QK_DIM).astype(f32), row, col) * qk_scale)
    v = heads_first(v.reshape(B, S, ML_HEADS, ML_V_DIM).astype(f32))
    k_ctx = heads_first(k_ctx.reshape(B, n_ctx, ML_HEADS, ML_QK_DIM).astype(f32) * qk_scale)
    v_ctx = heads_first(v_ctx.reshape(B, n_ctx, ML_HEADS, ML_V_DIM).astype(f32))

    def gate_logs(g):
        t = g.shape[1]
        g = g.reshape(B, t, 4, ML_HEADS).astype(f32) + b_gates.astype(f32)
        g = GATE_SOFTCAP * jnp.tanh(g / GATE_SOFTCAP)
        g = jnp.transpose(g, (2, 0, 3, 1))
        return g[0], jax.nn.log_sigmoid(g[1]), g[2], jax.nn.log_sigmoid(g[3])

    li_f, lf_f, li_b, lf_b = gate_logs(gates)
    lic_f, lfc_f, lic_b, lfc_b = gate_logs(gates_ctx)
    state0 = (jnp.zeros((B, ML_HEADS, ML_QK_DIM, ML_V_DIM), f32),
              jnp.zeros((B, ML_HEADS, ML_QK_DIM), f32),
              jnp.zeros((B, ML_HEADS), f32))
    st_f = _mlstm_chunkwise(None, k_ctx, v_ctx, lic_f, lfc_f, state0)
    h_f = _mlstm_chunkwise(q, k, v, li_f, lf_f, st_f)
    st_b = _mlstm_chunkwise(None, flip(k_ctx), flip(v_ctx), flip(lic_b), flip(lfc_b), state0)
    h_b = flip(_mlstm_chunkwise(flip(q), flip(k), flip(v), flip(li_b), flip(lf_b), st_b))
    h = heads_first(h_f + h_b)
    h = h * lax.rsqrt(jnp.mean(h * h, axis=-1, keepdims=True) + NORM_EPS) * g_head.reshape(ML_HEADS, ML_V_DIM).astype(f32)
    return (jax.nn.sigmoid(o_pre.astype(f32)) * h.reshape(B, S, ML_WIDTH)).astype(o_pre.dtype)


def _clamped_swiglu(g, l):
    g = jnp.minimum(g, SWIGLU_LIMIT)
    l = jnp.clip(l, -SWIGLU_LIMIT, SWIGLU_LIMIT)
    return g * jax.nn.sigmoid(SWIGLU_ALPHA * g) * (l + 1.0)


def _moe_ffn(h, w_router, b_router, w_gate, b_gate, w_lin, b_lin, w_down, b_down):
    B, S, D = h.shape
    n_tok = B * S
    n_assign = n_tok * TOP_K
    xf = h.reshape(n_tok, D)
    logits = (xf @ w_router + b_router).astype(jnp.float32)
    top_logit, top_e = lax.top_k(logits, TOP_K)
    top_w = jax.nn.softmax(top_logit, axis=-1)
    e_flat = top_e.reshape(-1)
    order = jnp.argsort(e_flat)
    e_sorted = e_flat[order]
    counts = jnp.bincount(e_flat, length=N_EXPERTS)
    padded = (counts + EXPERT_BLOCK - 1) // EXPERT_BLOCK * EXPERT_BLOCK
    pad_end = jnp.cumsum(padded)
    pad_start = pad_end - padded
    start = jnp.cumsum(counts) - counts
    dest = pad_start[e_sorted] + jnp.arange(n_assign) - start[e_sorted]
    n_rows = -(-n_assign // EXPERT_BLOCK) * EXPERT_BLOCK + N_EXPERTS * EXPERT_BLOCK
    n_blocks = n_rows // EXPERT_BLOCK
    row_tok = jnp.full((n_rows,), n_tok, jnp.int32).at[dest].set((order // TOP_K).astype(jnp.int32))
    row_w = jnp.zeros((n_rows,), jnp.float32).at[dest].set(top_w.reshape(-1)[order])
    blk_start = jnp.arange(n_blocks) * EXPERT_BLOCK
    blk_e = jnp.minimum(jnp.sum(blk_start[:, None] >= pad_end[None, :], axis=1), N_EXPERTS - 1)
    x_pad = jnp.concatenate([xf, jnp.zeros((1, D), xf.dtype)], axis=0)

    def expert_block(args):
        idx, wts, e = args
        xb = x_pad[idx]
        a = _clamped_swiglu(xb @ w_gate[e] + b_gate[e], xb @ w_lin[e] + b_lin[e])
        y = a @ w_down[e] + b_down[e]
        return (y * wts[:, None]).astype(h.dtype)

    y = lax.map(expert_block, (row_tok.reshape(n_blocks, EXPERT_BLOCK),
                               row_w.reshape(n_blocks, EXPERT_BLOCK), blk_e))
    out = jnp.zeros((n_tok + 1, D), h.dtype).at[row_tok].add(y.reshape(n_rows, D))
    return out[:n_tok].reshape(B, S, D)


def setup_inputs(seed: int = 0) -> dict:
    key = jax.random.key(seed)
    ks = jax.random.split(key, 28)
    f32 = jnp.float32
    D, E, F = D_MODEL, N_EXPERTS, D_EXPERT

    def nrm(k, shape, scale):
        return scale * jax.random.normal(k, shape, f32)

    def gain(k, shape):
        return 1.0 + 0.02 * jax.random.normal(k, shape, f32)

    f_bias = 3.0 + 3.0 * jax.random.uniform(ks[11], (DEPTH, 2, ML_HEADS), f32)
    i_bias = nrm(ks[12], (DEPTH, 2, ML_HEADS), 0.1)
    b_mlstm_gates = jnp.stack([i_bias[:, 0], f_bias[:, 0], i_bias[:, 1], f_bias[:, 1]], axis=1)
    return {
        'x': jax.random.normal(ks[0], (BATCH, SEQ, D), f32),
        'c': jax.random.normal(ks[1], (BATCH, D), f32),
        'ctx': jax.random.normal(ks[2], (BATCH, CTX_LEN, D), f32),
        'c_ctx': jax.random.normal(ks[3], (D,), f32),
        'w_ada': nrm(ks[4], (DEPTH, D, 6 * D), 0.5 * D ** -0.5),
        'b_ada': nrm(ks[5], (DEPTH, 6 * D), 0.02),
        'g_mix_pre': gain(ks[6], (DEPTH, D)),
        'g_mix_post': gain(ks[7], (DEPTH, D)),
        'g_ffn_pre': gain(ks[8], (DEPTH, D)),
        'g_ffn_post': gain(ks[9], (DEPTH, D)),
        'w_in': nrm(ks[10], (DEPTH, D, N_IN_COLS), D ** -0.5),
        'b_mlstm_gates': b_mlstm_gates,
        'rpb': nrm(ks[13], (DEPTH, NA_HEADS, 2 * WIN_H - 1, 2 * WIN_W - 1), 0.1),
        'g_mlstm_head': gain(ks[14], (DEPTH, ML_WIDTH)),
        'w_branch_na': nrm(ks[15], (DEPTH, NA_WIDTH, D), NA_WIDTH ** -0.5),
        'w_branch_ml': nrm(ks[16], (DEPTH, ML_WIDTH, D), ML_WIDTH ** -0.5),
        'w_out': nrm(ks[17], (DEPTH, D, D), D ** -0.5),
        'w_router': nrm(ks[18], (DEPTH, D, E), D ** -0.5),
        'b_router': nrm(ks[19], (DEPTH, E), 0.01),
        'w_gate': nrm(ks[20], (DEPTH, E, D, F), D ** -0.5),
        'b_gate': nrm(ks[21], (DEPTH, E, F), 0.01),
        'w_lin': nrm(ks[22], (DEPTH, E, D, F), D ** -0.5),
        'b_lin': nrm(ks[23], (DEPTH, E, F), 0.01),
        'w_down': nrm(ks[24], (DEPTH, E, F, D), F ** -0.5),
        'b_down': nrm(ks[25], (DEPTH, E, D), 0.01),
    }


def reference(x, c, ctx, c_ctx, w_ada, b_ada, g_mix_pre, g_mix_post, g_ffn_pre, g_ffn_post,
              w_in, b_mlstm_gates, rpb, g_mlstm_head, w_branch_na, w_branch_ml, w_out,
              w_router, b_router, w_gate, b_gate, w_lin, b_lin, w_down, b_down):
    B, S, D = x.shape
    t = jnp.arange(S)
    row = (t // GRID_W).astype(jnp.float32)
    col = (t % GRID_W).astype(jnp.float32)
    silu_c = jax.nn.silu(c)
    silu_cc = jax.nn.silu(c_ctx)
    for layer in range(DEPTH):
        mod = silu_c @ w_ada[layer] + b_ada[layer]
        sh_m, sc_m, gt_m, sh_f, sc_f, gt_f = jnp.split(mod, 6, axis=-1)
        mod_c = silu_cc @ w_ada[layer][:, :2 * D] + b_ada[layer][:2 * D]
        sh_c, sc_c = jnp.split(mod_c, 2, axis=-1)

        h = _rmsnorm(x, g_mix_pre[layer]) * (1.0 + sc_m[:, None]) + sh_m[:, None]
        hc = _rmsnorm(ctx, g_mix_pre[layer]) * (1.0 + sc_c) + sh_c
        (na_k, na_v, ml_k, ml_v, ml_g, na_q, ml_q, ml_o, gate_na, gate_ml) = _split(h @ w_in[layer], IN_SPLITS)
        (na_kc, na_vc, ml_kc, ml_vc, ml_gc) = _split(hc @ w_in[layer][:, :N_CTX_COLS], CTX_SPLITS)
        o_na = _neighbourhood_attention(na_q, na_k, na_v, na_kc, na_vc, rpb[layer])
        o_ml = _mlstm_branch(ml_q, ml_k, ml_v, ml_o, ml_g, ml_kc, ml_vc, ml_gc,
                             b_mlstm_gates[layer], g_mlstm_head[layer], row, col)
        merged = (jax.nn.sigmoid(gate_na) * (o_na @ w_branch_na[layer])
                  + jax.nn.sigmoid(gate_ml) * (o_ml @ w_branch_ml[layer]))
        x = x + gt_m[:, None] * _rmsnorm(merged @ w_out[layer], g_mix_post[layer])

        h2 = _rmsnorm(x, g_ffn_pre[layer]) * (1.0 + sc_f[:, None]) + sh_f[:, None]
        ffn = _moe_ffn(h2, w_router[layer], b_router[layer], w_gate[layer], b_gate[layer],
                       w_lin[layer], b_lin[layer], w_down[layer], b_down[layer])
        x = x + gt_f[:, None] * _rmsnorm(ffn, g_ffn_post[layer])
    return x
```

```python
import functools

import numpy as np
import jax
import jax.numpy as jnp
from jax import lax
from jax.experimental import pallas as pl
from jax.experimental.pallas import tpu as pltpu

F32 = jnp.float32
BF16 = jnp.bfloat16
HIGHEST = lax.Precision.HIGHEST

GRID_W = 64
NA_HEADS = 8
NA_HEAD_DIM = 64
NA_WIDTH = NA_HEADS * NA_HEAD_DIM
WIN_H = 8
WIN_W = 16
ML_HEADS = 4
ML_QK_DIM = 64
ML_V_DIM = 128
ML_QK_WIDTH = ML_HEADS * ML_QK_DIM
ML_WIDTH = ML_HEADS * ML_V_DIM
N_GATES = 4 * ML_HEADS
GATE_SOFTCAP = 15.0
ROPE_BASE = 10000.0
N_EXPERTS = 32
TOP_K = 4
SWIGLU_ALPHA = 1.702
SWIGLU_LIMIT = 7.0
NORM_EPS = 1e-6
NEG_INF = -1e30

LANES = 128
ML_CHUNK = 256
PROJ_ROWS = 512
MERGE_ROWS = 256
EXPERT_ROWS = 256
MOVE_ROWS = 256
VMEM_LIMIT = 56 * 1024 * 1024

NT_DIMS = (((1,), (1,)), ((), ()))
TN_DIMS = (((0,), (0,)), ((), ()))


def _cparams(*sem):
    return pltpu.CompilerParams(dimension_semantics=sem, vmem_limit_bytes=VMEM_LIMIT)


def _rms(x):
    return x * lax.rsqrt(jnp.mean(x * x, axis=-1, keepdims=True) + NORM_EPS)


def _ada_kernel(c_ref, w_ref, b_ref, o_ref):
    c = c_ref[...]
    s = c * jax.nn.sigmoid(c)
    o_ref[...] = jnp.dot(s, w_ref[...], preferred_element_type=F32, precision=HIGHEST) + b_ref[...]


def _ada(c_all, w_ada, b_ada):
    rows, d = c_all.shape
    n_out = w_ada.shape[1]
    tn = 1536
    return pl.pallas_call(
        _ada_kernel,
        grid=(n_out // tn,),
        in_specs=[pl.BlockSpec((rows, d), lambda j: (0, 0)),
                  pl.BlockSpec((d, tn), lambda j: (0, j)),
                  pl.BlockSpec((1, tn), lambda j: (0, j))],
        out_specs=pl.BlockSpec((rows, tn), lambda j: (0, j)),
        out_shape=jax.ShapeDtypeStruct((rows, n_out), F32),
        compiler_params=_cparams("arbitrary"),
        name="ada_mod",
    )(c_all, w_ada, b_ada.reshape(1, n_out))


def _gate_logs(g, is_forget):
    g = GATE_SOFTCAP * jnp.tanh(g / GATE_SOFTCAP)
    log_sig = jnp.minimum(g, 0.0) - jnp.log(1.0 + jnp.exp(-jnp.abs(g)))
    return jnp.where(is_forget, log_sig, g)


def _proj_kernel(*refs, latent):
    if latent:
        (x_ref, g_ref, sc_ref, sh_ref, cos_ref, sin_ref, bgc_ref, bgr_ref,
         w_naq, w_nak, w_nav, w_mq, w_mqp, w_mk, w_mkp, w_mv, w_mo, w_gna, w_gml, w_gc, w_gr,
         o_naq, o_nak, o_nav, o_mq, o_mk, o_mv, o_mo, o_gna, o_gml, o_gc, o_gr) = refs
    else:
        (x_ref, g_ref, sc_ref, sh_ref, bgc_ref, bgr_ref,
         w_nak, w_nav, w_mk, w_mv, w_gc, w_gr,
         o_nak, o_nav, o_mk, o_mv, o_gc, o_gr) = refs
    x = x_ref[...]
    h = _rms(x) * g_ref[...]
    h = h * (1.0 + sc_ref[...]) + sh_ref[...]
    hb = h.astype(BF16)

    def mm(w_ref):
        return jnp.dot(hb, w_ref[...], preferred_element_type=F32)

    o_nak[...] = mm(w_nak).astype(BF16)
    o_nav[...] = mm(w_nav).astype(BF16)
    o_mv[...] = mm(w_mv).astype(BF16)
    if latent:
        cos = cos_ref[...]
        sin = sin_ref[...]
        o_naq[...] = mm(w_naq).astype(BF16)
        o_mq[...] = (mm(w_mq) * cos + mm(w_mqp) * sin).astype(BF16)
        o_mk[...] = (mm(w_mk) * cos + mm(w_mkp) * sin).astype(BF16)
        o_mo[...] = mm(w_mo).astype(BF16)
        o_gna[...] = mm(w_gna).astype(BF16)
        o_gml[...] = mm(w_gml).astype(BF16)
    else:
        o_mk[...] = mm(w_mk).astype(BF16)
    gc = mm(w_gc) + bgc_ref[...]
    col_id = lax.broadcasted_iota(jnp.int32, gc.shape, 1)
    o_gc[...] = _gate_logs(gc, (col_id // ML_HEADS) % 2 == 1)
    gr = lax.dot_general(w_gr[...], hb, NT_DIMS, preferred_element_type=F32) + bgr_ref[...]
    row_id = lax.broadcasted_iota(jnp.int32, gr.shape, 0)
    o_gr[...] = _gate_logs(gr, (row_id // ML_HEADS) % 2 == 1)


def _const_spec(shape):
    nd = len(shape)
    return pl.BlockSpec(shape, lambda i, _nd=nd: (0,) * _nd)


def _project(x2, mod4, mod_row_fn, g_pre, bg_col, bg_row, weights, tables, latent):
    n, d = x2.shape
    tm = PROJ_ROWS
    grid = (n // tm,)

    def mod_spec(j):
        return pl.BlockSpec((None, None, 1, d), lambda i, _j=j: (mod_row_fn(i), _j, 0, 0))

    in_specs = [pl.BlockSpec((tm, d), lambda i: (i, 0)), _const_spec((1, d)), mod_spec(1), mod_spec(0)]
    args = [x2, g_pre.reshape(1, d), mod4, mod4]
    if latent:
        cos, sin, tiles_per_seq = tables
        in_specs += [pl.BlockSpec((tm, ML_QK_WIDTH), lambda i: (i % tiles_per_seq, 0))] * 2
        args += [cos, sin]
    in_specs += [_const_spec(bg_col.shape), _const_spec(bg_row.shape)]
    args += [bg_col, bg_row]
    for w in weights:
        in_specs.append(_const_spec(w.shape))
        args.append(w)

    def out(width, dtype=BF16):
        return (jax.ShapeDtypeStruct((n, width), dtype), pl.BlockSpec((tm, width), lambda i: (i, 0)))

    if latent:
        outs = [out(NA_WIDTH), out(NA_WIDTH), out(NA_WIDTH), out(ML_QK_WIDTH), out(ML_QK_WIDTH),
                out(ML_WIDTH), out(ML_WIDTH), out(d), out(d), out(N_GATES, F32)]
    else:
        outs = [out(NA_WIDTH), out(NA_WIDTH), out(ML_QK_WIDTH), out(ML_WIDTH), out(N_GATES, F32)]
    outs.append((jax.ShapeDtypeStruct((N_GATES, n), F32), pl.BlockSpec((N_GATES, tm), lambda i: (0, i))))
    return pl.pallas_call(
        functools.partial(_proj_kernel, latent=latent),
        grid=grid,
        in_specs=in_specs,
        out_specs=[o[1] for o in outs],
        out_shape=[o[0] for o in outs],
        compiler_params=_cparams("arbitrary"),
        name="in_proj_latent" if latent else "in_proj_ctx",
    )(*args)


def _na_bias_table(rpb):
    qc = np.arange(GRID_W)[:, None]
    kc = np.arange(GRID_W)[None, :]
    cs = np.clip(qc - WIN_W // 2, 0, GRID_W - WIN_W)
    mask = (kc >= cs) & (kc < cs + WIN_W)
    dc = np.clip(kc - qc, -(WIN_W - 1), WIN_W - 1) + (WIN_W - 1)
    off = np.arange(WIN_H)[:, None]
    dr = np.arange(WIN_H)[None, :] - off + (WIN_H - 1)
    bias = rpb[:, dr[:, :, None, None], dc[None, None, :, :]].astype(F32)
    bias = jnp.where(mask[None, None, None], bias, NEG_INF)
    bias = jnp.transpose(bias, (1, 0, 3, 2, 4))
    return bias.reshape(WIN_H, NA_HEADS, GRID_W, WIN_H * GRID_W)


def _na_kernel(q_ref, k_ref, v_ref, kc_ref, vc_ref, bias_ref, o_ref, *, rows):
    r = pl.program_id(1)
    rs = jnp.clip(r - WIN_H // 2, 0, rows - WIN_H)
    start = pl.multiple_of(rs * GRID_W, GRID_W)
    n_win = WIN_H * GRID_W
    q = q_ref[...]
    kw = k_ref[pl.ds(start, n_win), :]
    vw = v_ref[pl.ds(start, n_win), :]
    kc = kc_ref[...]
    vc = vc_ref[...]
    outs = []
    for h in range(NA_HEADS):
        sl = slice(h * NA_HEAD_DIM, (h + 1) * NA_HEAD_DIM)
        qh = q[:, sl]
        s_loc = lax.dot_general(qh, kw[:, sl], NT_DIMS, preferred_element_type=F32) + bias_ref[h]
        s_ctx = lax.dot_general(qh, kc[:, sl], NT_DIMS, preferred_element_type=F32)
        m = jnp.maximum(jnp.max(s_loc, axis=-1, keepdims=True), jnp.max(s_ctx, axis=-1, keepdims=True))
        p_loc = jnp.exp(s_loc - m)
        p_ctx = jnp.exp(s_ctx - m)
        denom = jnp.sum(p_loc, axis=-1, keepdims=True) + jnp.sum(p_ctx, axis=-1, keepdims=True)
        o = (jnp.dot(p_loc.astype(BF16), vw[:, sl], preferred_element_type=F32)
             + jnp.dot(p_ctx.astype(BF16), vc[:, sl], preferred_element_type=F32))
        outs.append(o / denom)
    o_ref[...] = jnp.concatenate(outs, axis=-1).astype(BF16)


def _neighbourhood_attention(q, k, v, kc, vc, bias):
    b, s, w = q.shape
    rows = s // GRID_W
    n_ctx = kc.shape[1]

    def bias_idx(bi, r):
        return (r - jnp.clip(r - WIN_H // 2, 0, rows - WIN_H), 0, 0, 0)

    return pl.pallas_call(
        functools.partial(_na_kernel, rows=rows),
        grid=(b, rows),
        in_specs=[pl.BlockSpec((None, GRID_W, w), lambda bi, r: (bi, r, 0)),
                  pl.BlockSpec((None, s, w), lambda bi, r: (bi, 0, 0)),
                  pl.BlockSpec((None, s, w), lambda bi, r: (bi, 0, 0)),
                  pl.BlockSpec((None, n_ctx, w), lambda bi, r: (bi, 0, 0)),
                  pl.BlockSpec((None, n_ctx, w), lambda bi, r: (bi, 0, 0)),
                  pl.BlockSpec((None, NA_HEADS, GRID_W, WIN_H * GRID_W), bias_idx)],
        out_specs=pl.BlockSpec((None, GRID_W, w), lambda bi, r: (bi, r, 0)),
        out_shape=jax.ShapeDtypeStruct((b, s, w), BF16),
        compiler_params=_cparams("arbitrary", "arbitrary"),
        name="na_attention",
    )(q, k, v, kc, vc, bias)


def _mlstm_kernel(q_ref, k_ref, v_ref, gc_ref, gr_ref, kc_ref, vc_ref, gcc_ref, gcr_ref,
                  o_ref, c_scr, n_scr, m_scr, *, reverse):
    step = pl.program_id(1)
    li_base = 2 * ML_HEADS if reverse else 0
    lf_base = li_base + ML_HEADS

    def tri(length):
        t = lax.broadcasted_iota(jnp.int32, (length, length), 0)
        s = lax.broadcasted_iota(jnp.int32, (length, length), 1)
        valid = (s >= t) if reverse else (s <= t)
        valid_t = (t >= s) if reverse else (t <= s)
        return valid, valid_t

    def cumulative(gc, gr, valid, valid_t):
        b_col = jnp.dot(valid.astype(F32), gc, preferred_element_type=F32, precision=HIGHEST)
        b_row = jnp.dot(gr, valid_t.astype(F32), preferred_element_type=F32, precision=HIGHEST)
        return b_col, b_row

    def update_state(h, k, v, gc, gr, b_col, b_row):
        li_c = gc[:, li_base + h:li_base + h + 1]
        li_r = gr[li_base + h:li_base + h + 1, :]
        lf_r = gr[lf_base + h:lf_base + h + 1, :]
        bc = b_col[:, lf_base + h:lf_base + h + 1]
        br = b_row[lf_base + h:lf_base + h + 1, :]
        total = jnp.sum(lf_r, axis=-1, keepdims=True)
        m_prev = m_scr[h]
        m_new = jnp.maximum(total + m_prev, jnp.max(total - br + li_r, axis=-1, keepdims=True))
        w_src = jnp.exp(total - bc + li_c - m_new)
        w_carry = jnp.exp(total + m_prev - m_new)
        kh = k[:, h * ML_QK_DIM:(h + 1) * ML_QK_DIM].astype(F32) * w_src
        vh = v[:, h * ML_V_DIM:(h + 1) * ML_V_DIM]
        c_scr[h] = w_carry * c_scr[h] + lax.dot_general(kh.astype(BF16), vh, TN_DIMS,
                                                        preferred_element_type=F32)
        n_scr[h] = w_carry * n_scr[h] + jnp.sum(kh, axis=0, keepdims=True)
        m_scr[h] = m_new

    @pl.when(step == 0)
    def _():
        c_scr[...] = jnp.zeros_like(c_scr)
        n_scr[...] = jnp.zeros_like(n_scr)
        m_scr[...] = jnp.zeros_like(m_scr)
        k = kc_ref[...]
        v = vc_ref[...]
        gc = gcc_ref[...]
        gr = gcr_ref[...]
        valid, valid_t = tri(k.shape[0])
        b_col, b_row = cumulative(gc, gr, valid, valid_t)
        for h in range(ML_HEADS):
            update_state(h, k, v, gc, gr, b_col, b_row)

    @pl.when(step > 0)
    def _():
        q = q_ref[...]
        k = k_ref[...]
        v = v_ref[...]
        gc = gc_ref[...]
        gr = gr_ref[...]
        valid, valid_t = tri(k.shape[0])
        b_col, b_row = cumulative(gc, gr, valid, valid_t)
        outs = []
        for h in range(ML_HEADS):
            li_r = gr[li_base + h:li_base + h + 1, :]
            bc = b_col[:, lf_base + h:lf_base + h + 1]
            br = b_row[lf_base + h:lf_base + h + 1, :]
            m_prev = m_scr[h]
            qh = q[:, h * ML_QK_DIM:(h + 1) * ML_QK_DIM]
            kh = k[:, h * ML_QK_DIM:(h + 1) * ML_QK_DIM]
            vh = v[:, h * ML_V_DIM:(h + 1) * ML_V_DIM]
            d_mat = jnp.where(valid, bc - br + li_r, NEG_INF)
            m_inter = bc + m_prev
            m_t = jnp.maximum(m_inter, jnp.max(d_mat, axis=-1, keepdims=True))
            w_intra = jnp.exp(d_mat - m_t)
            w_inter = jnp.exp(m_inter - m_t)
            s = lax.dot_general(qh, kh, NT_DIMS, preferred_element_type=F32) * w_intra
            num = (w_inter * jnp.dot(qh, c_scr[h].astype(BF16), preferred_element_type=F32)
                   + jnp.dot(s.astype(BF16), vh, preferred_element_type=F32))
            den = (w_inter * jnp.sum(qh.astype(F32) * n_scr[h], axis=-1, keepdims=True)
                   + jnp.sum(s, axis=-1, keepdims=True))
            outs.append(num / jnp.maximum(jnp.abs(den), jnp.exp(-m_t)))
        o_ref[...] = jnp.concatenate(outs, axis=-1).astype(BF16)
        for h in range(ML_HEADS):
            update_state(h, k, v, gc, gr, b_col, b_row)


def _mlstm_direction(q, k, v, gc, gr, kc, vc, gcc, gcr, reverse):
    b, s, _ = q.shape
    n_ctx = kc.shape[1]
    length = ML_CHUNK
    n_chunks = s // length

    def chunk(step):
        c = jnp.maximum(step - 1, 0)
        return (n_chunks - 1 - c) if reverse else c

    return pl.pallas_call(
        functools.partial(_mlstm_kernel, reverse=reverse),
        grid=(b, n_chunks + 1),
        in_specs=[pl.BlockSpec((None, length, ML_QK_WIDTH), lambda bi, st: (bi, chunk(st), 0)),
                  pl.BlockSpec((None, length, ML_QK_WIDTH), lambda bi, st: (bi, chunk(st), 0)),
                  pl.BlockSpec((None, length, ML_WIDTH), lambda bi, st: (bi, chunk(st), 0)),
                  pl.BlockSpec((None, length, N_GATES), lambda bi, st: (bi, chunk(st), 0)),
                  pl.BlockSpec((N_GATES, length), lambda bi, st: (0, bi * n_chunks + chunk(st))),
                  pl.BlockSpec((None, n_ctx, ML_QK_WIDTH), lambda bi, st: (bi, 0, 0)),
                  pl.BlockSpec((None, n_ctx, ML_WIDTH), lambda bi, st: (bi, 0, 0)),
                  pl.BlockSpec((None, n_ctx, N_GATES), lambda bi, st: (bi, 0, 0)),
                  pl.BlockSpec((N_GATES, n_ctx), lambda bi, st: (0, bi))],
        out_specs=pl.BlockSpec((None, length, ML_WIDTH), lambda bi, st: (bi, chunk(st), 0)),
        out_shape=jax.ShapeDtypeStruct((b, s, ML_WIDTH), BF16),
        scratch_shapes=[pltpu.VMEM((ML_HEADS, ML_QK_DIM, ML_V_DIM), F32),
                        pltpu.VMEM((ML_HEADS, 1, ML_QK_DIM), F32),
                        pltpu.VMEM((ML_HEADS, 1, 1), F32)],
        compiler_params=_cparams("arbitrary", "arbitrary"),
        name="mlstm_bwd" if reverse else "mlstm_fwd",
    )(q, k, v, gc, gr, kc, vc, gcc, gcr)


def _merge_kernel(x_ref, ona_ref, hf_ref, hb_ref, opre_ref, gna_ref, gml_ref, gtm_ref, scf_ref, shf_ref,
                  ghead_ref, wbna_ref, wbml_ref, wout_ref, gpost_ref, gpre_ref, wr_ref, br_ref,
                  x1_ref, h2_ref, tope_ref, topw_ref, rank_ref, cnt_ref):
    step = pl.program_id(0)
    tm = x_ref.shape[0]

    @pl.when(step == 0)
    def _():
        cnt_ref[...] = jnp.zeros_like(cnt_ref)

    hsum = hf_ref[...].astype(F32) + hb_ref[...].astype(F32)
    heads = [_rms(hsum[:, h * ML_V_DIM:(h + 1) * ML_V_DIM]) for h in range(ML_HEADS)]
    hn = jnp.concatenate(heads, axis=-1) * ghead_ref[...]
    o_ml = jax.nn.sigmoid(opre_ref[...].astype(F32)) * hn
    merged = (jax.nn.sigmoid(gna_ref[...].astype(F32))
              * jnp.dot(ona_ref[...], wbna_ref[...], preferred_element_type=F32)
              + jax.nn.sigmoid(gml_ref[...].astype(F32))
              * jnp.dot(o_ml.astype(BF16), wbml_ref[...], preferred_element_type=F32))
    mixed = jnp.dot(merged.astype(BF16), wout_ref[...], preferred_element_type=F32)
    x1 = x_ref[...] + gtm_ref[...] * (_rms(mixed) * gpost_ref[...])
    x1_ref[...] = x1
    h2 = _rms(x1) * gpre_ref[...] * (1.0 + scf_ref[...]) + shf_ref[...]
    h2_ref[...] = h2
    logits = jnp.dot(h2, wr_ref[...], preferred_element_type=F32, precision=HIGHEST) + br_ref[...]

    lane = lax.broadcasted_iota(jnp.int32, logits.shape, 1)
    onehots, top_e, top_l = [], [], []
    for _ in range(TOP_K):
        best = jnp.max(logits, axis=-1, keepdims=True)
        e = jnp.min(jnp.where(logits == best, lane, N_EXPERTS), axis=-1, keepdims=True)
        hit = lane == e
        onehots.append(hit)
        top_e.append(e)
        top_l.append(best)
        logits = jnp.where(hit, -jnp.inf, logits)
    exps = [jnp.exp(l - top_l[0]) for l in top_l]
    total = exps[0] + exps[1] + exps[2] + exps[3]

    counts = (onehots[0].astype(F32) + onehots[1].astype(F32)
              + onehots[2].astype(F32) + onehots[3].astype(F32))
    t = lax.broadcasted_iota(jnp.int32, (tm, tm), 0)
    s = lax.broadcasted_iota(jnp.int32, (tm, tm), 1)
    before = jnp.dot((s < t).astype(BF16), counts.astype(BF16), preferred_element_type=F32) + cnt_ref[...]
    out_lane = lax.broadcasted_iota(jnp.int32, (tm, LANES), 1)
    e_out = jnp.zeros((tm, LANES), jnp.int32)
    w_out = jnp.zeros((tm, LANES), F32)
    r_out = jnp.zeros((tm, LANES), jnp.int32)
    for j in range(TOP_K):
        rank = jnp.sum(jnp.where(onehots[j], before, 0.0), axis=-1, keepdims=True).astype(jnp.int32)
        e_out = jnp.where(out_lane == j, top_e[j], e_out)
        w_out = jnp.where(out_lane == j, exps[j] / total, w_out)
        r_out = jnp.where(out_lane == j, rank, r_out)
    tope_ref[...] = e_out
    topw_ref[...] = w_out
    rank_ref[...] = r_out
    cnt_ref[...] += jnp.sum(counts, axis=0, keepdims=True)


def _merge_and_route(x2, o_na, h_f, h_b, o_pre, g_na, g_ml, mod4, seq, g_head, wbna, wbml, wout,
                     g_post, g_pre, w_router, b_router):
    n, d = x2.shape
    tm = MERGE_ROWS
    per_seq = seq // tm

    def rows(width):
        return pl.BlockSpec((tm, width), lambda i: (i, 0))

    def mod_spec(j):
        return pl.BlockSpec((None, None, 1, d), lambda i, _j=j: (i // per_seq, _j, 0, 0))

    return pl.pallas_call(
        _merge_kernel,
        grid=(n // tm,),
        in_specs=[rows(d), rows(NA_WIDTH), rows(ML_WIDTH), rows(ML_WIDTH), rows(ML_WIDTH), rows(d), rows(d),
                  mod_spec(2), mod_spec(4), mod_spec(3),
                  _const_spec((1, ML_WIDTH)), _const_spec(wbna.shape), _const_spec(wbml.shape),
                  _const_spec(wout.shape), _const_spec((1, d)), _const_spec((1, d)),
                  _const_spec(w_router.shape), _const_spec((1, N_EXPERTS))],
        out_specs=[rows(d), rows(d), rows(LANES), rows(LANES), rows(LANES),
                   pl.BlockSpec((1, N_EXPERTS), lambda i: (0, 0))],
        out_shape=[jax.ShapeDtypeStruct((n, d), F32), jax.ShapeDtypeStruct((n, d), F32),
                   jax.ShapeDtypeStruct((n, LANES), jnp.int32), jax.ShapeDtypeStruct((n, LANES), F32),
                   jax.ShapeDtypeStruct((n, LANES), jnp.int32),
                   jax.ShapeDtypeStruct((1, N_EXPERTS), F32)],
        compiler_params=_cparams("arbitrary"),
        name="merge_route",
    )(x2, o_na, h_f, h_b, o_pre, g_na, g_ml, mod4, mod4, mod4,
      g_head.reshape(1, ML_WIDTH), wbna, wbml, wout, g_post.reshape(1, d), g_pre.reshape(1, d),
      w_router, b_router.reshape(1, N_EXPERTS))


def _dispatch_kernel(dest_ref, h_ref, zero_ref, xs_ref, sem):
    del zero_ref
    tm = h_ref.shape[0]

    def row_copy(j):
        t = j // TOP_K
        dest = dest_ref[j // LANES, j % LANES]
        return pltpu.make_async_copy(h_ref.at[pl.ds(t, 1), :], xs_ref.at[pl.ds(dest, 1), :], sem)

    def start(j, carry):
        row_copy(j).start()
        return carry

    def wait(j, carry):
        row_copy(j).wait()
        return carry

    lax.fori_loop(0, tm * TOP_K, start, 0)
    lax.fori_loop(0, tm * TOP_K, wait, 0)


def _dispatch(h2, dest2, n_rows):
    n, d = h2.shape
    tm = MOVE_ROWS
    idx_rows = tm * TOP_K // LANES
    zeros = jnp.zeros((n_rows, d), h2.dtype)
    return pl.pallas_call(
        _dispatch_kernel,
        grid=(n // tm,),
        in_specs=[pl.BlockSpec((idx_rows, LANES), lambda i: (i, 0), memory_space=pltpu.SMEM),
                  pl.BlockSpec((tm, d), lambda i: (i, 0)),
                  pl.BlockSpec(memory_space=pl.ANY)],
        out_specs=pl.BlockSpec(memory_space=pl.ANY),
        out_shape=jax.ShapeDtypeStruct((n_rows, d), h2.dtype),
        scratch_shapes=[pltpu.SemaphoreType.DMA(())],
        input_output_aliases={2: 0},
        compiler_params=_cparams("arbitrary"),
        name="moe_dispatch",
    )(dest2, h2, zeros)


def _expert_kernel(blk_e_ref, n_used_ref, x_ref, wg_ref, bg_ref, wl_ref, bl_ref, wd_ref, bd_ref, y_ref,
                   wg_s, wl_s, wd_s):
    i = pl.program_id(0)
    prev = blk_e_ref[jnp.maximum(i - 1, 0)]
    changed = jnp.logical_or(i == 0, blk_e_ref[i] != prev)
    used = i < n_used_ref[0]

    @pl.when(jnp.logical_and(used, changed))
    def _():
        wg_s[...] = wg_ref[...].astype(BF16)
        wl_s[...] = wl_ref[...].astype(BF16)
        wd_s[...] = wd_ref[...].astype(BF16)

    @pl.when(used)
    def _():
        xb = x_ref[...].astype(BF16)
        g = jnp.dot(xb, wg_s[...], preferred_element_type=F32) + bg_ref[...]
        l = jnp.dot(xb, wl_s[...], preferred_element_type=F32) + bl_ref[...]
        g = jnp.minimum(g, SWIGLU_LIMIT)
        l = jnp.clip(l, -SWIGLU_LIMIT, SWIGLU_LIMIT)
        a = g * jax.nn.sigmoid(SWIGLU_ALPHA * g) * (l + 1.0)
        y_ref[...] = jnp.dot(a.astype(BF16), wd_s[...], preferred_element_type=F32) + bd_ref[...]

    @pl.when(jnp.logical_not(used))
    def _():
        y_ref[...] = jnp.zeros_like(y_ref)


def _experts(blk_e, n_used, xs, w_gate, b_gate, w_lin, b_lin, w_down, b_down):
    n_rows, d = xs.shape
    e, _, f = w_gate.shape
    tm = EXPERT_ROWS

    def w_spec(shape):
        return pl.BlockSpec((None,) + shape, lambda i, be, nu: (be[i], 0, 0))

    grid_spec = pltpu.PrefetchScalarGridSpec(
        num_scalar_prefetch=2,
        grid=(n_rows // tm,),
        in_specs=[pl.BlockSpec((tm, d), lambda i, be, nu: (i, 0)),
                  w_spec((d, f)), w_spec((1, f)), w_spec((d, f)), w_spec((1, f)),
                  w_spec((f, d)), w_spec((1, d))],
        out_specs=pl.BlockSpec((tm, d), lambda i, be, nu: (i, 0)),
        scratch_shapes=[pltpu.VMEM((d, f), BF16), pltpu.VMEM((d, f), BF16), pltpu.VMEM((f, d), BF16)],
    )
    return pl.pallas_call(
        _expert_kernel,
        grid_spec=grid_spec,
        out_shape=jax.ShapeDtypeStruct((n_rows, d), F32),
        compiler_params=_cparams("arbitrary"),
        name="moe_experts",
    )(blk_e, n_used, xs, w_gate, b_gate.reshape(e, 1, f), w_lin, b_lin.reshape(e, 1, f),
      w_down, b_down.reshape(e, 1, d))


def _combine_kernel(dest_ref, x1_ref, w_ref, gtf_ref, gpost_ref, y_ref, o_ref, buf, sem):
    tm = x1_ref.shape[0]

    def row_copy(j):
        t = j // TOP_K
        slot = j % TOP_K
        src = dest_ref[j // LANES, j % LANES]
        return pltpu.make_async_copy(y_ref.at[pl.ds(src, 1), :], buf.at[slot, pl.ds(t, 1), :], sem)

    def start(j, carry):
        row_copy(j).start()
        return carry

    def wait(j, carry):
        row_copy(j).wait()
        return carry

    lax.fori_loop(0, tm * TOP_K, start, 0)
    lax.fori_loop(0, tm * TOP_K, wait, 0)
    w = w_ref[...]
    ffn = (buf[0] * w[:, 0:1] + buf[1] * w[:, 1:2]) + (buf[2] * w[:, 2:3] + buf[3] * w[:, 3:4])
    o_ref[...] = x1_ref[...] + gtf_ref[...] * (_rms(ffn) * gpost_ref[...])


def _combine(dest2, x1, top_w, mod4, seq, g_post, y):
    n, d = x1.shape
    tm = MOVE_ROWS
    per_seq = seq // tm
    idx_rows = tm * TOP_K // LANES
    return pl.pallas_call(
        _combine_kernel,
        grid=(n // tm,),
        in_specs=[pl.BlockSpec((idx_rows, LANES), lambda i: (i, 0), memory_space=pltpu.SMEM),
                  pl.BlockSpec((tm, d), lambda i: (i, 0)),
                  pl.BlockSpec((tm, LANES), lambda i: (i, 0)),
                  pl.BlockSpec((None, None, 1, d), lambda i: (i // per_seq, 5, 0, 0)),
                  _const_spec((1, d)),
                  pl.BlockSpec(memory_space=pl.ANY)],
        out_specs=pl.BlockSpec((tm, d), lambda i: (i, 0)),
        out_shape=jax.ShapeDtypeStruct((n, d), F32),
        scratch_shapes=[pltpu.VMEM((TOP_K, tm, d), F32), pltpu.SemaphoreType.DMA(())],
        compiler_params=_cparams("arbitrary"),
        name="moe_combine",
    )(dest2, x1, top_w, mod4, g_post.reshape(1, d), y)


def _rope_partner(w):
    n_freq = ML_QK_DIM // 4
    d = w.shape[0]
    w4 = w.reshape(d, -1, 2, n_freq)
    return jnp.stack([-w4[:, :, 1], w4[:, :, 0]], axis=2).reshape(w.shape)


def _rope_tables(seq):
    n_freq = ML_QK_DIM // 4
    t = jnp.arange(seq)
    row = (t // GRID_W).astype(F32)
    col = (t % GRID_W).astype(F32)
    inv_freq = ROPE_BASE ** (-jnp.arange(n_freq, dtype=F32) / n_freq)
    ang = jnp.concatenate([row[:, None] * inv_freq] * 2 + [col[:, None] * inv_freq] * 2, axis=-1)
    cos = jnp.tile(jnp.cos(ang), (1, ML_HEADS))
    sin = jnp.tile(jnp.sin(ang), (1, ML_HEADS))
    return cos, sin


def _layer(x, ctx, mod4, g_mix_pre, g_mix_post, g_ffn_pre, g_ffn_post, w_in, b_gates, rpb, g_head,
           w_branch_na, w_branch_ml, w_out, w_router, b_router, w_gate, b_gate, w_lin, b_lin, w_down, b_down):
    b, s, d = x.shape
    n = b * s
    n_ctx = ctx.shape[1]
    x2 = x.reshape(n, d)

    ctx_cols = (NA_WIDTH, NA_WIDTH, ML_QK_WIDTH, ML_WIDTH, N_GATES)
    lat_cols = (NA_WIDTH, ML_QK_WIDTH, ML_WIDTH, d, d)
    bounds = np.cumsum(ctx_cols + lat_cols)[:-1].tolist()
    (w_nak, w_nav, w_mk, w_mv, w_g, w_naq, w_mq, w_mo, w_gna, w_gml) = jnp.split(w_in, bounds, axis=-1)
    w_naq = w_naq * NA_HEAD_DIM ** -0.5
    w_mk = w_mk * ML_QK_DIM ** -0.5
    bf = lambda a: a.astype(BF16)
    lat_w = [bf(w_naq), bf(w_nak), bf(w_nav), bf(w_mq), bf(_rope_partner(w_mq)), bf(w_mk),
             bf(_rope_partner(w_mk)), bf(w_mv), bf(w_mo), bf(w_gna), bf(w_gml), bf(w_g), bf(w_g.T)]
    ctx_w = [bf(w_nak), bf(w_nav), bf(w_mk), bf(w_mv), bf(w_g), bf(w_g.T)]
    bg_col = b_gates.reshape(1, N_GATES).astype(F32)
    bg_row = b_gates.reshape(N_GATES, 1).astype(F32)
    cos, sin = _rope_tables(s)
    per_seq = s // PROJ_ROWS

    (na_q, na_k, na_v, ml_q, ml_k, ml_v, ml_o, gate_na, gate_ml, g_col, g_row) = _project(
        x2, mod4, lambda i: i // per_seq, g_mix_pre, bg_col, bg_row, lat_w, (cos, sin, per_seq), True)
    (na_kc, na_vc, ml_kc, ml_vc, gc_col, gc_row) = _project(
        ctx.reshape(b * n_ctx, d), mod4, lambda i: b, g_mix_pre, bg_col, bg_row, ctx_w, None, False)

    def seq3(a, length):
        return a.reshape(b, length, a.shape[-1])

    o_na = _neighbourhood_attention(seq3(na_q, s), seq3(na_k, s), seq3(na_v, s),
                                    seq3(na_kc, n_ctx), seq3(na_vc, n_ctx), _na_bias_table(rpb))
    ml_args = (seq3(ml_q, s), seq3(ml_k, s), seq3(ml_v, s), seq3(g_col, s), g_row,
               seq3(ml_kc, n_ctx), seq3(ml_vc, n_ctx), seq3(gc_col, n_ctx), gc_row)
    h_f = _mlstm_direction(*ml_args, reverse=False)
    h_b = _mlstm_direction(*ml_args, reverse=True)

    x1, h2, top_e, top_w, rank, counts = _merge_and_route(
        x2, o_na.reshape(n, NA_WIDTH), h_f.reshape(n, ML_WIDTH), h_b.reshape(n, ML_WIDTH), ml_o,
        gate_na, gate_ml, mod4, s, g_head, bf(w_branch_na), bf(w_branch_ml), bf(w_out),
        g_mix_post, g_ffn_pre, w_router, b_router)

    tm = EXPERT_ROWS
    counts = counts.reshape(N_EXPERTS).astype(jnp.int32)
    padded = (counts + tm - 1) // tm * tm
    pad_end = jnp.cumsum(padded)
    pad_start = pad_end - padded
    n_rows = n * TOP_K + N_EXPERTS * tm
    n_blocks = n_rows // tm
    dest = pad_start[top_e[:, :TOP_K]] + rank[:, :TOP_K]
    dest2 = dest.reshape(n * TOP_K // LANES, LANES).astype(jnp.int32)
    blk_start = jnp.arange(n_blocks, dtype=jnp.int32) * tm
    blk_e = jnp.minimum(jnp.sum(blk_start[:, None] >= pad_end[None, :], axis=1), N_EXPERTS - 1).astype(jnp.int32)
    n_used = (pad_end[-1:] // tm).astype(jnp.int32)

    xs = _dispatch(h2, dest2, n_rows)
    y = _experts(blk_e, n_used, xs, w_gate, b_gate, w_lin, b_lin, w_down, b_down)
    out = _combine(dest2, x1, top_w, mod4, s, g_ffn_post, y)
    return out.reshape(b, s, d)


def kernel(x, c, ctx, c_ctx, w_ada, b_ada, g_mix_pre, g_mix_post, g_ffn_pre, g_ffn_post, w_in, b_mlstm_gates,
           rpb, g_mlstm_head, w_branch_na, w_branch_ml, w_out, w_router, b_router, w_gate, b_gate, w_lin,
           b_lin, w_down, b_down):
    b, s, d = x.shape
    depth = w_ada.shape[0]
    pad = (-(b + 1)) % 8
    c_all = jnp.concatenate([c, c_ctx[None, :], jnp.zeros((pad, d), c.dtype)], axis=0)
    for layer in range(depth):
        mod = _ada(c_all, w_ada[layer], b_ada[layer])
        mod4 = mod.reshape(mod.shape[0], 6, 1, d)
        x = _layer(x, ctx, mod4, g_mix_pre[layer], g_mix_post[layer], g_ffn_pre[layer], g_ffn_post[layer],
                   w_in[layer], b_mlstm_gates[layer], rpb[layer], g_mlstm_head[layer], w_branch_na[layer],
                   w_branch_ml[layer], w_out[layer], w_router[layer], b_router[layer], w_gate[layer],
                   b_gate[layer], w_lin[layer], b_lin[layer], w_down[layer], b_down[layer])
    return x
```

```python
import functools

import numpy as np
import jax
import jax.numpy as jnp
from jax import lax
from jax.experimental import pallas as pl
from jax.experimental.pallas import tpu as pltpu

F32 = jnp.float32
BF16 = jnp.bfloat16
HIGHEST = lax.Precision.HIGHEST

GRID_W = 64
NA_HEADS = 8
NA_HEAD_DIM = 64
NA_WIDTH = NA_HEADS * NA_HEAD_DIM
WIN_H = 8
WIN_W = 16
ML_HEADS = 4
ML_QK_DIM = 64
ML_V_DIM = 128
ML_QK_WIDTH = ML_HEADS * ML_QK_DIM
ML_WIDTH = ML_HEADS * ML_V_DIM
N_GATES = 4 * ML_HEADS
GATE_SOFTCAP = 15.0
ROPE_BASE = 10000.0
N_EXPERTS = 32
TOP_K = 4
SWIGLU_ALPHA = 1.702
SWIGLU_LIMIT = 7.0
NORM_EPS = 1e-6
NEG_INF = -1e30

LANES = 128
SUBLANES = 8
ML_CHUNK = 256
PROJ_ROWS = 512
MERGE_ROWS = 256
EXPERT_ROWS = 256
MOVE_ROWS = 256
VMEM_LIMIT = 56 * 1024 * 1024

NT_DIMS = (((1,), (1,)), ((), ()))
TN_DIMS = (((0,), (0,)), ((), ()))


def _cparams(*sem):
    return pltpu.CompilerParams(dimension_semantics=sem, vmem_limit_bytes=VMEM_LIMIT)


def _rms(x):
    return x * lax.rsqrt(jnp.mean(x * x, axis=-1, keepdims=True) + NORM_EPS)


def _store_row_tiles(ref, x):
    for s in range(SUBLANES):
        ref[:, s, :] = x[:, s * LANES:(s + 1) * LANES]


def _load_row_tiles(ref):
    return jnp.concatenate([ref[:, s, :] for s in range(SUBLANES)], axis=-1)


def _ada_kernel(c_ref, w_ref, b_ref, o_ref):
    c = c_ref[...]
    s = c * jax.nn.sigmoid(c)
    o_ref[...] = jnp.dot(s, w_ref[...], preferred_element_type=F32, precision=HIGHEST) + b_ref[...]


def _ada(c_all, w_ada, b_ada):
    rows, d = c_all.shape
    n_out = w_ada.shape[1]
    tn = 1536
    return pl.pallas_call(
        _ada_kernel,
        grid=(n_out // tn,),
        in_specs=[pl.BlockSpec((rows, d), lambda j: (0, 0)),
                  pl.BlockSpec((d, tn), lambda j: (0, j)),
                  pl.BlockSpec((1, tn), lambda j: (0, j))],
        out_specs=pl.BlockSpec((rows, tn), lambda j: (0, j)),
        out_shape=jax.ShapeDtypeStruct((rows, n_out), F32),
        compiler_params=_cparams("arbitrary"),
        name="ada_mod",
    )(c_all, w_ada, b_ada.reshape(1, n_out))


def _gate_logs(g, is_forget):
    g = GATE_SOFTCAP * jnp.tanh(g / GATE_SOFTCAP)
    log_sig = jnp.minimum(g, 0.0) - jnp.log(1.0 + jnp.exp(-jnp.abs(g)))
    return jnp.where(is_forget, log_sig, g)


def _proj_kernel(*refs, latent):
    if latent:
        (x_ref, g_ref, sc_ref, sh_ref, cos_ref, sin_ref, bgc_ref, bgr_ref,
         w_naq, w_nak, w_nav, w_mq, w_mqp, w_mk, w_mkp, w_mv, w_mo, w_gna, w_gml, w_gc, w_gr,
         o_naq, o_nak, o_nav, o_mq, o_mk, o_mv, o_mo, o_gna, o_gml, o_gc, o_gr) = refs
    else:
        (x_ref, g_ref, sc_ref, sh_ref, bgc_ref, bgr_ref,
         w_nak, w_nav, w_mk, w_mv, w_gc, w_gr,
         o_nak, o_nav, o_mk, o_mv, o_gc, o_gr) = refs
    x = x_ref[...]
    h = _rms(x) * g_ref[...]
    h = h * (1.0 + sc_ref[...]) + sh_ref[...]
    hb = h.astype(BF16)

    def mm(w_ref):
        return jnp.dot(hb, w_ref[...], preferred_element_type=F32)

    o_nak[...] = mm(w_nak).astype(BF16)
    o_nav[...] = mm(w_nav).astype(BF16)
    o_mv[...] = mm(w_mv).astype(BF16)
    if latent:
        cos = cos_ref[...]
        sin = sin_ref[...]
        o_naq[...] = mm(w_naq).astype(BF16)
        o_mq[...] = (mm(w_mq) * cos + mm(w_mqp) * sin).astype(BF16)
        o_mk[...] = (mm(w_mk) * cos + mm(w_mkp) * sin).astype(BF16)
        o_mo[...] = mm(w_mo).astype(BF16)
        o_gna[...] = mm(w_gna).astype(BF16)
        o_gml[...] = mm(w_gml).astype(BF16)
    else:
        o_mk[...] = mm(w_mk).astype(BF16)
    gc = mm(w_gc) + bgc_ref[...]
    col_id = lax.broadcasted_iota(jnp.int32, gc.shape, 1)
    o_gc[...] = _gate_logs(gc, (col_id // ML_HEADS) % 2 == 1)
    gr = lax.dot_general(w_gr[...], hb, NT_DIMS, preferred_element_type=F32) + bgr_ref[...]
    row_id = lax.broadcasted_iota(jnp.int32, gr.shape, 0)
    o_gr[...] = _gate_logs(gr, (row_id // ML_HEADS) % 2 == 1)


def _const_spec(shape):
    nd = len(shape)
    return pl.BlockSpec(shape, lambda i, _nd=nd: (0,) * _nd)


def _project(x2, mod4, mod_row_fn, g_pre, bg_col, bg_row, weights, tables, latent):
    n, d = x2.shape
    tm = PROJ_ROWS
    grid = (n // tm,)

    def mod_spec(j):
        return pl.BlockSpec((None, None, 1, d), lambda i, _j=j: (mod_row_fn(i), _j, 0, 0))

    in_specs = [pl.BlockSpec((tm, d), lambda i: (i, 0)), _const_spec((1, d)), mod_spec(1), mod_spec(0)]
    args = [x2, g_pre.reshape(1, d), mod4, mod4]
    if latent:
        cos, sin, tiles_per_seq = tables
        in_specs += [pl.BlockSpec((tm, ML_QK_WIDTH), lambda i: (i % tiles_per_seq, 0))] * 2
        args += [cos, sin]
    in_specs += [_const_spec(bg_col.shape), _const_spec(bg_row.shape)]
    args += [bg_col, bg_row]
    for w in weights:
        in_specs.append(_const_spec(w.shape))
        args.append(w)

    def out(width, dtype=BF16):
        return (jax.ShapeDtypeStruct((n, width), dtype), pl.BlockSpec((tm, width), lambda i: (i, 0)))

    if latent:
        outs = [out(NA_WIDTH), out(NA_WIDTH), out(NA_WIDTH), out(ML_QK_WIDTH), out(ML_QK_WIDTH),
                out(ML_WIDTH), out(ML_WIDTH), out(d), out(d), out(N_GATES, F32)]
    else:
        outs = [out(NA_WIDTH), out(NA_WIDTH), out(ML_QK_WIDTH), out(ML_WIDTH), out(N_GATES, F32)]
    outs.append((jax.ShapeDtypeStruct((N_GATES, n), F32), pl.BlockSpec((N_GATES, tm), lambda i: (0, i))))
    return pl.pallas_call(
        functools.partial(_proj_kernel, latent=latent),
        grid=grid,
        in_specs=in_specs,
        out_specs=[o[1] for o in outs],
        out_shape=[o[0] for o in outs],
        compiler_params=_cparams("arbitrary"),
        name="in_proj_latent" if latent else "in_proj_ctx",
    )(*args)


def _na_bias_table(rpb):
    qc = np.arange(GRID_W)[:, None]
    kc = np.arange(GRID_W)[None, :]
    cs = np.clip(qc - WIN_W // 2, 0, GRID_W - WIN_W)
    mask = (kc >= cs) & (kc < cs + WIN_W)
    dc = np.clip(kc - qc, -(WIN_W - 1), WIN_W - 1) + (WIN_W - 1)
    off = np.arange(WIN_H)[:, None]
    dr = np.arange(WIN_H)[None, :] - off + (WIN_H - 1)
    sel_r = (dr[:, :, None] == np.arange(2 * WIN_H - 1)).astype(np.float32)
    sel_c = (dc[:, :, None] == np.arange(2 * WIN_W - 1)).astype(np.float32)
    bias = jnp.einsum('hrc,ojr,qkc->hojqk', rpb.astype(F32), sel_r, sel_c, precision=HIGHEST)
    bias = jnp.where(mask[None, None, None], bias, NEG_INF)
    bias = jnp.transpose(bias, (1, 0, 3, 2, 4))
    return bias.reshape(WIN_H, NA_HEADS, GRID_W, WIN_H * GRID_W)


def _na_kernel(q_ref, k_ref, v_ref, kc_ref, vc_ref, bias_ref, o_ref, *, rows):
    r = pl.program_id(1)
    rs = jnp.clip(r - WIN_H // 2, 0, rows - WIN_H)
    start = pl.multiple_of(rs * GRID_W, GRID_W)
    n_win = WIN_H * GRID_W
    q = q_ref[...]
    kw = k_ref[pl.ds(start, n_win), :]
    vw = v_ref[pl.ds(start, n_win), :]
    kc = kc_ref[...]
    vc = vc_ref[...]
    outs = []
    for h in range(NA_HEADS):
        sl = slice(h * NA_HEAD_DIM, (h + 1) * NA_HEAD_DIM)
        qh = q[:, sl]
        s_loc = lax.dot_general(qh, kw[:, sl], NT_DIMS, preferred_element_type=F32) + bias_ref[h]
        s_ctx = lax.dot_general(qh, kc[:, sl], NT_DIMS, preferred_element_type=F32)
        m = jnp.maximum(jnp.max(s_loc, axis=-1, keepdims=True), jnp.max(s_ctx, axis=-1, keepdims=True))
        p_loc = jnp.exp(s_loc - m)
        p_ctx = jnp.exp(s_ctx - m)
        denom = jnp.sum(p_loc, axis=-1, keepdims=True) + jnp.sum(p_ctx, axis=-1, keepdims=True)
        o = (jnp.dot(p_loc.astype(BF16), vw[:, sl], preferred_element_type=F32)
             + jnp.dot(p_ctx.astype(BF16), vc[:, sl], preferred_element_type=F32))
        outs.append(o / denom)
    o_ref[...] = jnp.concatenate(outs, axis=-1).astype(BF16)


def _neighbourhood_attention(q, k, v, kc, vc, bias):
    b, s, w = q.shape
    rows = s // GRID_W
    n_ctx = kc.shape[1]

    def bias_idx(bi, r):
        return (r - jnp.clip(r - WIN_H // 2, 0, rows - WIN_H), 0, 0, 0)

    return pl.pallas_call(
        functools.partial(_na_kernel, rows=rows),
        grid=(b, rows),
        in_specs=[pl.BlockSpec((None, GRID_W, w), lambda bi, r: (bi, r, 0)),
                  pl.BlockSpec((None, s, w), lambda bi, r: (bi, 0, 0)),
                  pl.BlockSpec((None, s, w), lambda bi, r: (bi, 0, 0)),
                  pl.BlockSpec((None, n_ctx, w), lambda bi, r: (bi, 0, 0)),
                  pl.BlockSpec((None, n_ctx, w), lambda bi, r: (bi, 0, 0)),
                  pl.BlockSpec((None, NA_HEADS, GRID_W, WIN_H * GRID_W), bias_idx)],
        out_specs=pl.BlockSpec((None, GRID_W, w), lambda bi, r: (bi, r, 0)),
        out_shape=jax.ShapeDtypeStruct((b, s, w), BF16),
        compiler_params=_cparams("arbitrary", "arbitrary"),
        name="na_attention",
    )(q, k, v, kc, vc, bias)


def _mlstm_kernel(q_ref, k_ref, v_ref, gc_ref, gr_ref, kc_ref, vc_ref, gcc_ref, gcr_ref,
                  o_ref, c_scr, n_scr, m_scr, *, reverse):
    step = pl.program_id(1)
    li_base = 2 * ML_HEADS if reverse else 0
    lf_base = li_base + ML_HEADS

    def tri(length):
        t = lax.broadcasted_iota(jnp.int32, (length, length), 0)
        s = lax.broadcasted_iota(jnp.int32, (length, length), 1)
        valid = (s >= t) if reverse else (s <= t)
        valid_t = (t >= s) if reverse else (t <= s)
        return valid, valid_t

    def cumulative(gc, gr, valid, valid_t):
        b_col = jnp.dot(valid.astype(F32), gc, preferred_element_type=F32, precision=HIGHEST)
        b_row = jnp.dot(gr, valid_t.astype(F32), preferred_element_type=F32, precision=HIGHEST)
        return b_col, b_row

    def update_state(h, k, v, gc, gr, b_col, b_row):
        li_c = gc[:, li_base + h:li_base + h + 1]
        li_r = gr[li_base + h:li_base + h + 1, :]
        lf_r = gr[lf_base + h:lf_base + h + 1, :]
        bc = b_col[:, lf_base + h:lf_base + h + 1]
        br = b_row[lf_base + h:lf_base + h + 1, :]
        total = jnp.sum(lf_r, axis=-1, keepdims=True)
        m_prev = m_scr[h]
        m_new = jnp.maximum(total + m_prev, jnp.max(total - br + li_r, axis=-1, keepdims=True))
        w_src = jnp.exp(total - bc + li_c - m_new)
        w_carry = jnp.exp(total + m_prev - m_new)
        kh = k[:, h * ML_QK_DIM:(h + 1) * ML_QK_DIM].astype(F32) * w_src
        vh = v[:, h * ML_V_DIM:(h + 1) * ML_V_DIM]
        c_scr[h] = w_carry * c_scr[h] + lax.dot_general(kh.astype(BF16), vh, TN_DIMS,
                                                        preferred_element_type=F32)
        n_scr[h] = w_carry * n_scr[h] + jnp.sum(kh, axis=0, keepdims=True)
        m_scr[h] = m_new

    @pl.when(step == 0)
    def _():
        c_scr[...] = jnp.zeros_like(c_scr)
        n_scr[...] = jnp.zeros_like(n_scr)
        m_scr[...] = jnp.zeros_like(m_scr)
        k = kc_ref[...]
        v = vc_ref[...]
        gc = gcc_ref[...]
        gr = gcr_ref[...]
        valid, valid_t = tri(k.shape[0])
        b_col, b_row = cumulative(gc, gr, valid, valid_t)
        for h in range(ML_HEADS):
            update_state(h, k, v, gc, gr, b_col, b_row)

    @pl.when(step > 0)
    def _():
        q = q_ref[...]
        k = k_ref[...]
        v = v_ref[...]
        gc = gc_ref[...]
        gr = gr_ref[...]
        valid, valid_t = tri(k.shape[0])
        b_col, b_row = cumulative(gc, gr, valid, valid_t)
        outs = []
        for h in range(ML_HEADS):
            li_r = gr[li_base + h:li_base + h + 1, :]
            bc = b_col[:, lf_base + h:lf_base + h + 1]
            br = b_row[lf_base + h:lf_base + h + 1, :]
            m_prev = m_scr[h]
            qh = q[:, h * ML_QK_DIM:(h + 1) * ML_QK_DIM]
            kh = k[:, h * ML_QK_DIM:(h + 1) * ML_QK_DIM]
            vh = v[:, h * ML_V_DIM:(h + 1) * ML_V_DIM]
            d_mat = jnp.where(valid, bc - br + li_r, NEG_INF)
            m_inter = bc + m_prev
            m_t = jnp.maximum(m_inter, jnp.max(d_mat, axis=-1, keepdims=True))
            w_intra = jnp.exp(d_mat - m_t)
            w_inter = jnp.exp(m_inter - m_t)
            s = lax.dot_general(qh, kh, NT_DIMS, preferred_element_type=F32) * w_intra
            num = (w_inter * jnp.dot(qh, c_scr[h].astype(BF16), preferred_element_type=F32)
                   + jnp.dot(s.astype(BF16), vh, preferred_element_type=F32))
            den = (w_inter * jnp.sum(qh.astype(F32) * n_scr[h], axis=-1, keepdims=True)
                   + jnp.sum(s, axis=-1, keepdims=True))
            outs.append(num / jnp.maximum(jnp.abs(den), jnp.exp(-m_t)))
        o_ref[...] = jnp.concatenate(outs, axis=-1).astype(BF16)
        for h in range(ML_HEADS):
            update_state(h, k, v, gc, gr, b_col, b_row)


def _mlstm_direction(q, k, v, gc, gr, kc, vc, gcc, gcr, reverse):
    b, s, _ = q.shape
    n_ctx = kc.shape[1]
    length = ML_CHUNK
    n_chunks = s // length

    def chunk(step):
        c = jnp.maximum(step - 1, 0)
        return (n_chunks - 1 - c) if reverse else c

    return pl.pallas_call(
        functools.partial(_mlstm_kernel, reverse=reverse),
        grid=(b, n_chunks + 1),
        in_specs=[pl.BlockSpec((None, length, ML_QK_WIDTH), lambda bi, st: (bi, chunk(st), 0)),
                  pl.BlockSpec((None, length, ML_QK_WIDTH), lambda bi, st: (bi, chunk(st), 0)),
                  pl.BlockSpec((None, length, ML_WIDTH), lambda bi, st: (bi, chunk(st), 0)),
                  pl.BlockSpec((None, length, N_GATES), lambda bi, st: (bi, chunk(st), 0)),
                  pl.BlockSpec((N_GATES, length), lambda bi, st: (0, bi * n_chunks + chunk(st))),
                  pl.BlockSpec((None, n_ctx, ML_QK_WIDTH), lambda bi, st: (bi, 0, 0)),
                  pl.BlockSpec((None, n_ctx, ML_WIDTH), lambda bi, st: (bi, 0, 0)),
                  pl.BlockSpec((None, n_ctx, N_GATES), lambda bi, st: (bi, 0, 0)),
                  pl.BlockSpec((N_GATES, n_ctx), lambda bi, st: (0, bi))],
        out_specs=pl.BlockSpec((None, length, ML_WIDTH), lambda bi, st: (bi, chunk(st), 0)),
        out_shape=jax.ShapeDtypeStruct((b, s, ML_WIDTH), BF16),
        scratch_shapes=[pltpu.VMEM((ML_HEADS, ML_QK_DIM, ML_V_DIM), F32),
                        pltpu.VMEM((ML_HEADS, 1, ML_QK_DIM), F32),
                        pltpu.VMEM((ML_HEADS, 1, 1), F32)],
        compiler_params=_cparams("arbitrary", "arbitrary"),
        name="mlstm_bwd" if reverse else "mlstm_fwd",
    )(q, k, v, gc, gr, kc, vc, gcc, gcr)


def _merge_kernel(x_ref, ona_ref, hf_ref, hb_ref, opre_ref, gna_ref, gml_ref, gtm_ref, scf_ref, shf_ref,
                  ghead_ref, wbna_ref, wbml_ref, wout_ref, gpost_ref, gpre_ref, wr_ref, br_ref,
                  x1_ref, h2_ref, tope_ref, topw_ref, rank_ref, cnt_ref):
    step = pl.program_id(0)
    tm = x_ref.shape[0]

    @pl.when(step == 0)
    def _():
        cnt_ref[...] = jnp.zeros_like(cnt_ref)

    hsum = hf_ref[...].astype(F32) + hb_ref[...].astype(F32)
    heads = [_rms(hsum[:, h * ML_V_DIM:(h + 1) * ML_V_DIM]) for h in range(ML_HEADS)]
    hn = jnp.concatenate(heads, axis=-1) * ghead_ref[...]
    o_ml = jax.nn.sigmoid(opre_ref[...].astype(F32)) * hn
    merged = (jax.nn.sigmoid(gna_ref[...].astype(F32))
              * jnp.dot(ona_ref[...], wbna_ref[...], preferred_element_type=F32)
              + jax.nn.sigmoid(gml_ref[...].astype(F32))
              * jnp.dot(o_ml.astype(BF16), wbml_ref[...], preferred_element_type=F32))
    mixed = jnp.dot(merged.astype(BF16), wout_ref[...], preferred_element_type=F32)
    x1 = x_ref[...] + gtm_ref[...] * (_rms(mixed) * gpost_ref[...])
    x1_ref[...] = x1
    h2 = _rms(x1) * gpre_ref[...] * (1.0 + scf_ref[...]) + shf_ref[...]
    _store_row_tiles(h2_ref, h2)
    logits = jnp.dot(h2, wr_ref[...], preferred_element_type=F32, precision=HIGHEST) + br_ref[...]

    lane = lax.broadcasted_iota(jnp.int32, logits.shape, 1)
    onehots, top_e, top_l = [], [], []
    for _ in range(TOP_K):
        best = jnp.max(logits, axis=-1, keepdims=True)
        e = jnp.min(jnp.where(logits == best, lane, N_EXPERTS), axis=-1, keepdims=True)
        hit = lane == e
        onehots.append(hit)
        top_e.append(e)
        top_l.append(best)
        logits = jnp.where(hit, -jnp.inf, logits)
    exps = [jnp.exp(l - top_l[0]) for l in top_l]
    total = exps[0] + exps[1] + exps[2] + exps[3]

    counts = (onehots[0].astype(F32) + onehots[1].astype(F32)
              + onehots[2].astype(F32) + onehots[3].astype(F32))
    t = lax.broadcasted_iota(jnp.int32, (tm, tm), 0)
    s = lax.broadcasted_iota(jnp.int32, (tm, tm), 1)
    before = jnp.dot((s < t).astype(BF16), counts.astype(BF16), preferred_element_type=F32) + cnt_ref[...]
    out_lane = lax.broadcasted_iota(jnp.int32, (tm, LANES), 1)
    e_out = jnp.zeros((tm, LANES), jnp.int32)
    w_out = jnp.zeros((tm, LANES), F32)
    r_out = jnp.zeros((tm, LANES), jnp.int32)
    for j in range(TOP_K):
        rank = jnp.sum(jnp.where(onehots[j], before, 0.0), axis=-1, keepdims=True).astype(jnp.int32)
        e_out = jnp.where(out_lane == j, top_e[j], e_out)
        w_out = jnp.where(out_lane == j, exps[j] / total, w_out)
        r_out = jnp.where(out_lane == j, rank, r_out)
    tope_ref[...] = e_out
    topw_ref[...] = w_out
    rank_ref[...] = r_out
    cnt_ref[...] += jnp.sum(counts, axis=0, keepdims=True)


def _merge_and_route(x2, o_na, h_f, h_b, o_pre, g_na, g_ml, mod4, seq, g_head, wbna, wbml, wout,
                     g_post, g_pre, w_router, b_router):
    n, d = x2.shape
    tm = MERGE_ROWS
    per_seq = seq // tm

    def rows(width):
        return pl.BlockSpec((tm, width), lambda i: (i, 0))

    def mod_spec(j):
        return pl.BlockSpec((None, None, 1, d), lambda i, _j=j: (i // per_seq, _j, 0, 0))

    return pl.pallas_call(
        _merge_kernel,
        grid=(n // tm,),
        in_specs=[rows(d), rows(NA_WIDTH), rows(ML_WIDTH), rows(ML_WIDTH), rows(ML_WIDTH), rows(d), rows(d),
                  mod_spec(2), mod_spec(4), mod_spec(3),
                  _const_spec((1, ML_WIDTH)), _const_spec(wbna.shape), _const_spec(wbml.shape),
                  _const_spec(wout.shape), _const_spec((1, d)), _const_spec((1, d)),
                  _const_spec(w_router.shape), _const_spec((1, N_EXPERTS))],
        out_specs=[rows(d), pl.BlockSpec((tm, SUBLANES, d // SUBLANES), lambda i: (i, 0, 0)),
                   rows(LANES), rows(LANES), rows(LANES),
                   pl.BlockSpec((1, N_EXPERTS), lambda i: (0, 0))],
        out_shape=[jax.ShapeDtypeStruct((n, d), F32), jax.ShapeDtypeStruct((n, SUBLANES, d // SUBLANES), F32),
                   jax.ShapeDtypeStruct((n, LANES), jnp.int32), jax.ShapeDtypeStruct((n, LANES), F32),
                   jax.ShapeDtypeStruct((n, LANES), jnp.int32),
                   jax.ShapeDtypeStruct((1, N_EXPERTS), F32)],
        compiler_params=_cparams("arbitrary"),
        name="merge_route",
    )(x2, o_na, h_f, h_b, o_pre, g_na, g_ml, mod4, mod4, mod4,
      g_head.reshape(1, ML_WIDTH), wbna, wbml, wout, g_post.reshape(1, d), g_pre.reshape(1, d),
      w_router, b_router.reshape(1, N_EXPERTS))


def _dispatch_kernel(pad_end_ref, padded_ref, dest_ref, h_ref, xs_ref, zero_scr, sem):
    tm = h_ref.shape[0]
    blk = zero_scr.shape[0]

    @pl.when(pl.program_id(0) == 0)
    def _():
        zero_scr[...] = jnp.zeros_like(zero_scr)

        def zero_copy(e):
            return pltpu.make_async_copy(zero_scr, xs_ref.at[pl.ds(pad_end_ref[e] - blk, blk)], sem)

        for e in range(N_EXPERTS):
            @pl.when(padded_ref[e] > 0)
            def _():
                zero_copy(e).start()
        for e in range(N_EXPERTS):
            @pl.when(padded_ref[e] > 0)
            def _():
                zero_copy(e).wait()

        def tail_copy(b):
            return pltpu.make_async_copy(zero_scr, xs_ref.at[pl.ds(b * blk, blk)], sem)

        def tail_start(b, carry):
            tail_copy(b).start()
            return carry

        def tail_wait(b, carry):
            tail_copy(b).wait()
            return carry

        first_unused = pad_end_ref[N_EXPERTS - 1] // blk
        lax.fori_loop(first_unused, xs_ref.shape[0] // blk, tail_start, 0)
        lax.fori_loop(first_unused, xs_ref.shape[0] // blk, tail_wait, 0)

    def row_copy(j):
        dest = dest_ref[j // LANES, j % LANES]
        return pltpu.make_async_copy(h_ref.at[j // TOP_K], xs_ref.at[dest], sem)

    def start(j, carry):
        row_copy(j).start()
        return carry

    def wait(j, carry):
        row_copy(j).wait()
        return carry

    lax.fori_loop(0, tm * TOP_K, start, 0, unroll=8)
    lax.fori_loop(0, tm * TOP_K, wait, 0, unroll=8)


def _dispatch(pad_end, padded, dest2, h2, n_rows):
    n = h2.shape[0]
    tm = MOVE_ROWS
    idx_rows = tm * TOP_K // LANES
    row_tile = h2.shape[1:]
    grid_spec = pltpu.PrefetchScalarGridSpec(
        num_scalar_prefetch=2,
        grid=(n // tm,),
        in_specs=[pl.BlockSpec((idx_rows, LANES), lambda i, pe, pd: (i, 0), memory_space=pltpu.SMEM),
                  pl.BlockSpec((tm,) + row_tile, lambda i, pe, pd: (i, 0, 0))],
        out_specs=pl.BlockSpec(memory_space=pl.ANY),
        scratch_shapes=[pltpu.VMEM((EXPERT_ROWS,) + row_tile, h2.dtype), pltpu.SemaphoreType.DMA(())],
    )
    return pl.pallas_call(
        _dispatch_kernel,
        grid_spec=grid_spec,
        out_shape=jax.ShapeDtypeStruct((n_rows,) + row_tile, h2.dtype),
        compiler_params=_cparams("arbitrary"),
        name="moe_dispatch",
    )(pad_end, padded, dest2, h2)


def _expert_kernel(blk_e_ref, n_used_ref, x_ref, wg_ref, bg_ref, wl_ref, bl_ref, wd_ref, bd_ref, y_ref,
                   wg_s, wl_s, wd_s):
    i = pl.program_id(0)
    prev = blk_e_ref[jnp.maximum(i - 1, 0)]
    changed = jnp.logical_or(i == 0, blk_e_ref[i] != prev)
    used = i < n_used_ref[0]

    @pl.when(jnp.logical_and(used, changed))
    def _():
        wg_s[...] = wg_ref[...].astype(BF16)
        wl_s[...] = wl_ref[...].astype(BF16)
        wd_s[...] = wd_ref[...].astype(BF16)

    @pl.when(used)
    def _():
        xb = _load_row_tiles(x_ref).astype(BF16)
        g = jnp.dot(xb, wg_s[...], preferred_element_type=F32) + bg_ref[...]
        l = jnp.dot(xb, wl_s[...], preferred_element_type=F32) + bl_ref[...]
        g = jnp.minimum(g, SWIGLU_LIMIT)
        l = jnp.clip(l, -SWIGLU_LIMIT, SWIGLU_LIMIT)
        a = g * jax.nn.sigmoid(SWIGLU_ALPHA * g) * (l + 1.0)
        y = jnp.dot(a.astype(BF16), wd_s[...], preferred_element_type=F32) + bd_ref[...]
        _store_row_tiles(y_ref, y)

    @pl.when(jnp.logical_not(used))
    def _():
        y_ref[...] = jnp.zeros_like(y_ref)


def _experts(blk_e, n_used, xs, w_gate, b_gate, w_lin, b_lin, w_down, b_down):
    n_rows = xs.shape[0]
    row_tile = xs.shape[1:]
    e, d, f = w_gate.shape
    tm = EXPERT_ROWS

    def w_spec(shape):
        return pl.BlockSpec((None,) + shape, lambda i, be, nu: (be[i], 0, 0))

    grid_spec = pltpu.PrefetchScalarGridSpec(
        num_scalar_prefetch=2,
        grid=(n_rows // tm,),
        in_specs=[pl.BlockSpec((tm,) + row_tile, lambda i, be, nu: (i, 0, 0)),
                  w_spec((d, f)), w_spec((1, f)), w_spec((d, f)), w_spec((1, f)),
                  w_spec((f, d)), w_spec((1, d))],
        out_specs=pl.BlockSpec((tm,) + row_tile, lambda i, be, nu: (i, 0, 0)),
        scratch_shapes=[pltpu.VMEM((d, f), BF16), pltpu.VMEM((d, f), BF16), pltpu.VMEM((f, d), BF16)],
    )
    return pl.pallas_call(
        _expert_kernel,
        grid_spec=grid_spec,
        out_shape=jax.ShapeDtypeStruct((n_rows,) + row_tile, F32),
        compiler_params=_cparams("arbitrary"),
        name="moe_experts",
    )(blk_e, n_used, xs, w_gate, b_gate.reshape(e, 1, f), w_lin, b_lin.reshape(e, 1, f),
      w_down, b_down.reshape(e, 1, d))


def _combine_kernel(dest_ref, x1_ref, w_ref, gtf_ref, gpost_ref, y_ref, o_ref, buf, sem):
    tm = x1_ref.shape[0]

    def row_copy(j):
        src = dest_ref[j // LANES, j % LANES]
        return pltpu.make_async_copy(y_ref.at[src], buf.at[j % TOP_K, j // TOP_K], sem)

    def start(j, carry):
        row_copy(j).start()
        return carry

    def wait(j, carry):
        row_copy(j).wait()
        return carry

    lax.fori_loop(0, tm * TOP_K, start, 0, unroll=8)
    lax.fori_loop(0, tm * TOP_K, wait, 0, unroll=8)
    w = w_ref[...]
    parts = []
    for s in range(SUBLANES):
        parts.append((buf[0, :, s, :] * w[:, 0:1] + buf[1, :, s, :] * w[:, 1:2])
                     + (buf[2, :, s, :] * w[:, 2:3] + buf[3, :, s, :] * w[:, 3:4]))
    ffn = jnp.concatenate(parts, axis=-1)
    o_ref[...] = x1_ref[...] + gtf_ref[...] * (_rms(ffn) * gpost_ref[...])


def _combine(dest2, x1, top_w, mod4, seq, g_post, y):
    n, d = x1.shape
    tm = MOVE_ROWS
    per_seq = seq // tm
    idx_rows = tm * TOP_K // LANES
    return pl.pallas_call(
        _combine_kernel,
        grid=(n // tm,),
        in_specs=[pl.BlockSpec((idx_rows, LANES), lambda i: (i, 0), memory_space=pltpu.SMEM),
                  pl.BlockSpec((tm, d), lambda i: (i, 0)),
                  pl.BlockSpec((tm, LANES), lambda i: (i, 0)),
                  pl.BlockSpec((None, None, 1, d), lambda i: (i // per_seq, 5, 0, 0)),
                  _const_spec((1, d)),
                  pl.BlockSpec(memory_space=pl.ANY)],
        out_specs=pl.BlockSpec((tm, d), lambda i: (i, 0)),
        out_shape=jax.ShapeDtypeStruct((n, d), F32),
        scratch_shapes=[pltpu.VMEM((TOP_K, tm) + y.shape[1:], F32), pltpu.SemaphoreType.DMA(())],
        compiler_params=_cparams("arbitrary"),
        name="moe_combine",
    )(dest2, x1, top_w, mod4, g_post.reshape(1, d), y)


def _rope_partner(w):
    n_freq = ML_QK_DIM // 4
    d = w.shape[0]
    w4 = w.reshape(d, -1, 2, n_freq)
    return jnp.stack([-w4[:, :, 1], w4[:, :, 0]], axis=2).reshape(w.shape)


def _rope_tables(seq):
    n_freq = ML_QK_DIM // 4
    t = jnp.arange(seq)
    row = (t // GRID_W).astype(F32)
    col = (t % GRID_W).astype(F32)
    inv_freq = ROPE_BASE ** (-jnp.arange(n_freq, dtype=F32) / n_freq)
    ang = jnp.concatenate([row[:, None] * inv_freq] * 2 + [col[:, None] * inv_freq] * 2, axis=-1)
    cos = jnp.tile(jnp.cos(ang), (1, ML_HEADS))
    sin = jnp.tile(jnp.sin(ang), (1, ML_HEADS))
    return cos, sin


def _layer(x, ctx, mod4, g_mix_pre, g_mix_post, g_ffn_pre, g_ffn_post, w_in, b_gates, rpb, g_head,
           w_branch_na, w_branch_ml, w_out, w_router, b_router, w_gate, b_gate, w_lin, b_lin, w_down, b_down):
    b, s, d = x.shape
    n = b * s
    n_ctx = ctx.shape[1]
    x2 = x.reshape(n, d)

    ctx_cols = (NA_WIDTH, NA_WIDTH, ML_QK_WIDTH, ML_WIDTH, N_GATES)
    lat_cols = (NA_WIDTH, ML_QK_WIDTH, ML_WIDTH, d, d)
    bounds = np.cumsum(ctx_cols + lat_cols)[:-1].tolist()
    (w_nak, w_nav, w_mk, w_mv, w_g, w_naq, w_mq, w_mo, w_gna, w_gml) = jnp.split(w_in, bounds, axis=-1)
    w_naq = w_naq * NA_HEAD_DIM ** -0.5
    w_mk = w_mk * ML_QK_DIM ** -0.5
    bf = lambda a: a.astype(BF16)
    lat_w = [bf(w_naq), bf(w_nak), bf(w_nav), bf(w_mq), bf(_rope_partner(w_mq)), bf(w_mk),
             bf(_rope_partner(w_mk)), bf(w_mv), bf(w_mo), bf(w_gna), bf(w_gml), bf(w_g), bf(w_g.T)]
    ctx_w = [bf(w_nak), bf(w_nav), bf(w_mk), bf(w_mv), bf(w_g), bf(w_g.T)]
    bg_col = b_gates.reshape(1, N_GATES).astype(F32)
    bg_row = b_gates.reshape(N_GATES, 1).astype(F32)
    cos, sin = _rope_tables(s)
    per_seq = s // PROJ_ROWS

    (na_q, na_k, na_v, ml_q, ml_k, ml_v, ml_o, gate_na, gate_ml, g_col, g_row) = _project(
        x2, mod4, lambda i: i // per_seq, g_mix_pre, bg_col, bg_row, lat_w, (cos, sin, per_seq), True)
    (na_kc, na_vc, ml_kc, ml_vc, gc_col, gc_row) = _project(
        ctx.reshape(b * n_ctx, d), mod4, lambda i: b, g_mix_pre, bg_col, bg_row, ctx_w, None, False)

    def seq3(a, length):
        return a.reshape(b, length, a.shape[-1])

    o_na = _neighbourhood_attention(seq3(na_q, s), seq3(na_k, s), seq3(na_v, s),
                                    seq3(na_kc, n_ctx), seq3(na_vc, n_ctx), _na_bias_table(rpb))
    ml_args = (seq3(ml_q, s), seq3(ml_k, s), seq3(ml_v, s), seq3(g_col, s), g_row,
               seq3(ml_kc, n_ctx), seq3(ml_vc, n_ctx), seq3(gc_col, n_ctx), gc_row)
    h_f = _mlstm_direction(*ml_args, reverse=False)
    h_b = _mlstm_direction(*ml_args, reverse=True)

    x1, h2, top_e, top_w, rank, counts = _merge_and_route(
        x2, o_na.reshape(n, NA_WIDTH), h_f.reshape(n, ML_WIDTH), h_b.reshape(n, ML_WIDTH), ml_o,
        gate_na, gate_ml, mod4, s, g_head, bf(w_branch_na), bf(w_branch_ml), bf(w_out),
        g_mix_post, g_ffn_pre, w_router, b_router)

    tm = EXPERT_ROWS
    counts = counts.reshape(N_EXPERTS).astype(jnp.int32)
    padded = (counts + tm - 1) // tm * tm
    pad_end = jnp.cumsum(padded)
    pad_start = pad_end - padded
    n_rows = n * TOP_K + N_EXPERTS * tm
    n_blocks = n_rows // tm
    e_sel = top_e[:, :TOP_K, None] == jnp.arange(N_EXPERTS, dtype=jnp.int32)
    dest = jnp.sum(jnp.where(e_sel, pad_start, 0), axis=-1) + rank[:, :TOP_K]
    dest2 = dest.reshape(n * TOP_K // LANES, LANES).astype(jnp.int32)
    blk_start = jnp.arange(n_blocks, dtype=jnp.int32) * tm
    blk_e = jnp.minimum(jnp.sum(blk_start[:, None] >= pad_end[None, :], axis=1), N_EXPERTS - 1).astype(jnp.int32)
    n_used = (pad_end[-1:] // tm).astype(jnp.int32)

    xs = _dispatch(pad_end.astype(jnp.int32), padded.astype(jnp.int32), dest2, h2, n_rows)
    y = _experts(blk_e, n_used, xs, w_gate, b_gate, w_lin, b_lin, w_down, b_down)
    out = _combine(dest2, x1, top_w, mod4, s, g_ffn_post, y)
    return out.reshape(b, s, d)


def kernel(x, c, ctx, c_ctx, w_ada, b_ada, g_mix_pre, g_mix_post, g_ffn_pre, g_ffn_post, w_in, b_mlstm_gates,
           rpb, g_mlstm_head, w_branch_na, w_branch_ml, w_out, w_router, b_router, w_gate, b_gate, w_lin,
           b_lin, w_down, b_down):
    b, s, d = x.shape
    depth = w_ada.shape[0]
    pad = (-(b + 1)) % 8
    c_all = jnp.concatenate([c, c_ctx[None, :], jnp.zeros((pad, d), c.dtype)], axis=0)
    for layer in range(depth):
        mod = _ada(c_all, w_ada[layer], b_ada[layer])
        mod4 = mod.reshape(mod.shape[0], 6, 1, d)
        x = _layer(x, ctx, mod4, g_mix_pre[layer], g_mix_post[layer], g_ffn_pre[layer], g_ffn_post[layer],
                   w_in[layer], b_mlstm_gates[layer], rpb[layer], g_mlstm_head[layer], w_branch_na[layer],
                   w_branch_ml[layer], w_out[layer], w_router[layer], b_router[layer], w_gate[layer],
                   b_gate[layer], w_lin[layer], b_lin[layer], w_down[layer], b_down[layer])
    return x
```

```python
import functools

import numpy as np
import jax
import jax.numpy as jnp
from jax import lax
from jax.experimental import pallas as pl
from jax.experimental.pallas import tpu as pltpu

F32 = jnp.float32
BF16 = jnp.bfloat16
HIGHEST = lax.Precision.HIGHEST

GRID_W = 64
NA_HEADS = 8
NA_HEAD_DIM = 64
NA_WIDTH = NA_HEADS * NA_HEAD_DIM
WIN_H = 8
WIN_W = 16
ML_HEADS = 4
ML_QK_DIM = 64
ML_V_DIM = 128
ML_QK_WIDTH = ML_HEADS * ML_QK_DIM
ML_WIDTH = ML_HEADS * ML_V_DIM
N_GATES = 4 * ML_HEADS
GATE_SOFTCAP = 15.0
ROPE_BASE = 10000.0
N_EXPERTS = 32
TOP_K = 4
SWIGLU_ALPHA = 1.702
SWIGLU_LIMIT = 7.0
NORM_EPS = 1e-6
NEG_INF = -1e30

LANES = 128
ML_CHUNK = 256
PROJ_ROWS = 512
MERGE_ROWS = 256
EXPERT_ROWS = 256
MOVE_ROWS = 256
VMEM_LIMIT = 56 * 1024 * 1024

NT_DIMS = (((1,), (1,)), ((), ()))
TN_DIMS = (((0,), (0,)), ((), ()))


def _cparams(*sem):
    return pltpu.CompilerParams(dimension_semantics=sem, vmem_limit_bytes=VMEM_LIMIT)


def _rms(x):
    return x * lax.rsqrt(jnp.mean(x * x, axis=-1, keepdims=True) + NORM_EPS)


def _ada_kernel(c_ref, w_ref, b_ref, o_ref):
    c = c_ref[...]
    s = c * jax.nn.sigmoid(c)
    o_ref[...] = jnp.dot(s, w_ref[...], preferred_element_type=F32, precision=HIGHEST) + b_ref[...]


def _ada(c_all, w_ada, b_ada):
    rows, d = c_all.shape
    n_out = w_ada.shape[1]
    tn = 1536
    return pl.pallas_call(
        _ada_kernel,
        grid=(n_out // tn,),
        in_specs=[pl.BlockSpec((rows, d), lambda j: (0, 0)),
                  pl.BlockSpec((d, tn), lambda j: (0, j)),
                  pl.BlockSpec((1, tn), lambda j: (0, j))],
        out_specs=pl.BlockSpec((rows, tn), lambda j: (0, j)),
        out_shape=jax.ShapeDtypeStruct((rows, n_out), F32),
        compiler_params=_cparams("arbitrary"),
        name="ada_mod",
    )(c_all, w_ada, b_ada.reshape(1, n_out))


def _gate_logs(g, is_forget):
    g = GATE_SOFTCAP * jnp.tanh(g / GATE_SOFTCAP)
    log_sig = jnp.minimum(g, 0.0) - jnp.log(1.0 + jnp.exp(-jnp.abs(g)))
    return jnp.where(is_forget, log_sig, g)


def _proj_kernel(*refs, latent):
    if latent:
        (x_ref, g_ref, sc_ref, sh_ref, cos_ref, sin_ref, bgc_ref, bgr_ref,
         w_naq, w_nak, w_nav, w_mq, w_mqp, w_mk, w_mkp, w_mv, w_mo, w_gna, w_gml, w_gc, w_gr,
         o_naq, o_nak, o_nav, o_mq, o_mk, o_mv, o_mo, o_gna, o_gml, o_gc, o_gr) = refs
    else:
        (x_ref, g_ref, sc_ref, sh_ref, bgc_ref, bgr_ref,
         w_nak, w_nav, w_mk, w_mv, w_gc, w_gr,
         o_nak, o_nav, o_mk, o_mv, o_gc, o_gr) = refs
    x = x_ref[...]
    h = _rms(x) * g_ref[...]
    h = h * (1.0 + sc_ref[...]) + sh_ref[...]
    hb = h.astype(BF16)

    def mm(w_ref):
        return jnp.dot(hb, w_ref[...], preferred_element_type=F32)

    o_nak[...] = mm(w_nak).astype(BF16)
    o_nav[...] = mm(w_nav).astype(BF16)
    o_mv[...] = mm(w_mv).astype(BF16)
    if latent:
        cos = cos_ref[...]
        sin = sin_ref[...]
        o_naq[...] = mm(w_naq).astype(BF16)
        o_mq[...] = (mm(w_mq) * cos + mm(w_mqp) * sin).astype(BF16)
        o_mk[...] = (mm(w_mk) * cos + mm(w_mkp) * sin).astype(BF16)
        o_mo[...] = mm(w_mo).astype(BF16)
        o_gna[...] = mm(w_gna).astype(BF16)
        o_gml[...] = mm(w_gml).astype(BF16)
    else:
        o_mk[...] = mm(w_mk).astype(BF16)
    gc = mm(w_gc) + bgc_ref[...]
    col_id = lax.broadcasted_iota(jnp.int32, gc.shape, 1)
    o_gc[...] = _gate_logs(gc, (col_id // ML_HEADS) % 2 == 1)
    gr = lax.dot_general(w_gr[...], hb, NT_DIMS, preferred_element_type=F32) + bgr_ref[...]
    row_id = lax.broadcasted_iota(jnp.int32, gr.shape, 0)
    o_gr[...] = _gate_logs(gr, (row_id // ML_HEADS) % 2 == 1)


def _const_spec(shape):
    nd = len(shape)
    return pl.BlockSpec(shape, lambda i, _nd=nd: (0,) * _nd)


def _project(x2, mod4, mod_row_fn, g_pre, bg_col, bg_row, weights, tables, latent):
    n, d = x2.shape
    tm = PROJ_ROWS
    grid = (n // tm,)

    def mod_spec(j):
        return pl.BlockSpec((None, None, 1, d), lambda i, _j=j: (mod_row_fn(i), _j, 0, 0))

    in_specs = [pl.BlockSpec((tm, d), lambda i: (i, 0)), _const_spec((1, d)), mod_spec(1), mod_spec(0)]
    args = [x2, g_pre.reshape(1, d), mod4, mod4]
    if latent:
        cos, sin, tiles_per_seq = tables
        in_specs += [pl.BlockSpec((tm, ML_QK_WIDTH), lambda i: (i % tiles_per_seq, 0))] * 2
        args += [cos, sin]
    in_specs += [_const_spec(bg_col.shape), _const_spec(bg_row.shape)]
    args += [bg_col, bg_row]
    for w in weights:
        in_specs.append(_const_spec(w.shape))
        args.append(w)

    def out(width, dtype=BF16):
        return (jax.ShapeDtypeStruct((n, width), dtype), pl.BlockSpec((tm, width), lambda i: (i, 0)))

    if latent:
        outs = [out(NA_WIDTH), out(NA_WIDTH), out(NA_WIDTH), out(ML_QK_WIDTH), out(ML_QK_WIDTH),
                out(ML_WIDTH), out(ML_WIDTH), out(d), out(d), out(N_GATES, F32)]
    else:
        outs = [out(NA_WIDTH), out(NA_WIDTH), out(ML_QK_WIDTH), out(ML_WIDTH), out(N_GATES, F32)]
    outs.append((jax.ShapeDtypeStruct((N_GATES, n), F32), pl.BlockSpec((N_GATES, tm), lambda i: (0, i))))
    return pl.pallas_call(
        functools.partial(_proj_kernel, latent=latent),
        grid=grid,
        in_specs=in_specs,
        out_specs=[o[1] for o in outs],
        out_shape=[o[0] for o in outs],
        compiler_params=_cparams("arbitrary"),
        name="in_proj_latent" if latent else "in_proj_ctx",
    )(*args)


def _na_bias_table(rpb):
    qc = np.arange(GRID_W)[:, None]
    kc = np.arange(GRID_W)[None, :]
    cs = np.clip(qc - WIN_W // 2, 0, GRID_W - WIN_W)
    mask = (kc >= cs) & (kc < cs + WIN_W)
    dc = np.clip(kc - qc, -(WIN_W - 1), WIN_W - 1) + (WIN_W - 1)
    off = np.arange(WIN_H)[:, None]
    dr = np.arange(WIN_H)[None, :] - off + (WIN_H - 1)
    sel_r = (dr[:, :, None] == np.arange(2 * WIN_H - 1)).astype(np.float32)
    sel_c = (dc[:, :, None] == np.arange(2 * WIN_W - 1)).astype(np.float32)
    bias = jnp.einsum('hrc,ojr,qkc->hojqk', rpb.astype(F32), sel_r, sel_c, precision=HIGHEST)
    bias = jnp.where(mask[None, None, None], bias, NEG_INF)
    bias = jnp.transpose(bias, (1, 0, 3, 2, 4))
    return bias.reshape(WIN_H, NA_HEADS, GRID_W, WIN_H * GRID_W)


def _na_kernel(q_ref, k_ref, v_ref, kc_ref, vc_ref, bias_ref, o_ref, *, rows):
    r = pl.program_id(1)
    rs = jnp.clip(r - WIN_H // 2, 0, rows - WIN_H)
    start = pl.multiple_of(rs * GRID_W, GRID_W)
    n_win = WIN_H * GRID_W
    q = q_ref[...]
    kw = k_ref[pl.ds(start, n_win), :]
    vw = v_ref[pl.ds(start, n_win), :]
    kc = kc_ref[...]
    vc = vc_ref[...]
    outs = []
    for h in range(NA_HEADS):
        sl = slice(h * NA_HEAD_DIM, (h + 1) * NA_HEAD_DIM)
        qh = q[:, sl]
        s_loc = lax.dot_general(qh, kw[:, sl], NT_DIMS, preferred_element_type=F32) + bias_ref[h]
        s_ctx = lax.dot_general(qh, kc[:, sl], NT_DIMS, preferred_element_type=F32)
        m = jnp.maximum(jnp.max(s_loc, axis=-1, keepdims=True), jnp.max(s_ctx, axis=-1, keepdims=True))
        p_loc = jnp.exp(s_loc - m)
        p_ctx = jnp.exp(s_ctx - m)
        denom = jnp.sum(p_loc, axis=-1, keepdims=True) + jnp.sum(p_ctx, axis=-1, keepdims=True)
        o = (jnp.dot(p_loc.astype(BF16), vw[:, sl], preferred_element_type=F32)
             + jnp.dot(p_ctx.astype(BF16), vc[:, sl], preferred_element_type=F32))
        outs.append(o / denom)
    o_ref[...] = jnp.concatenate(outs, axis=-1).astype(BF16)


def _neighbourhood_attention(q, k, v, kc, vc, bias):
    b, s, w = q.shape
    rows = s // GRID_W
    n_ctx = kc.shape[1]

    def bias_idx(bi, r):
        return (r - jnp.clip(r - WIN_H // 2, 0, rows - WIN_H), 0, 0, 0)

    return pl.pallas_call(
        functools.partial(_na_kernel, rows=rows),
        grid=(b, rows),
        in_specs=[pl.BlockSpec((None, GRID_W, w), lambda bi, r: (bi, r, 0)),
                  pl.BlockSpec((None, s, w), lambda bi, r: (bi, 0, 0)),
                  pl.BlockSpec((None, s, w), lambda bi, r: (bi, 0, 0)),
                  pl.BlockSpec((None, n_ctx, w), lambda bi, r: (bi, 0, 0)),
                  pl.BlockSpec((None, n_ctx, w), lambda bi, r: (bi, 0, 0)),
                  pl.BlockSpec((None, NA_HEADS, GRID_W, WIN_H * GRID_W), bias_idx)],
        out_specs=pl.BlockSpec((None, GRID_W, w), lambda bi, r: (bi, r, 0)),
        out_shape=jax.ShapeDtypeStruct((b, s, w), BF16),
        compiler_params=_cparams("arbitrary", "arbitrary"),
        name="na_attention",
    )(q, k, v, kc, vc, bias)


def _mlstm_kernel(q_ref, k_ref, v_ref, gc_ref, gr_ref, kc_ref, vc_ref, gcc_ref, gcr_ref,
                  o_ref, c_scr, n_scr, m_scr, *, reverse):
    step = pl.program_id(1)
    li_base = 2 * ML_HEADS if reverse else 0
    lf_base = li_base + ML_HEADS

    def tri(length):
        t = lax.broadcasted_iota(jnp.int32, (length, length), 0)
        s = lax.broadcasted_iota(jnp.int32, (length, length), 1)
        valid = (s >= t) if reverse else (s <= t)
        valid_t = (t >= s) if reverse else (t <= s)
        return valid, valid_t

    def cumulative(gc, gr, valid, valid_t):
        b_col = jnp.dot(valid.astype(F32), gc, preferred_element_type=F32, precision=HIGHEST)
        b_row = jnp.dot(gr, valid_t.astype(F32), preferred_element_type=F32, precision=HIGHEST)
        return b_col, b_row

    def update_state(h, k, v, gc, gr, b_col, b_row):
        li_c = gc[:, li_base + h:li_base + h + 1]
        li_r = gr[li_base + h:li_base + h + 1, :]
        lf_r = gr[lf_base + h:lf_base + h + 1, :]
        bc = b_col[:, lf_base + h:lf_base + h + 1]
        br = b_row[lf_base + h:lf_base + h + 1, :]
        total = jnp.sum(lf_r, axis=-1, keepdims=True)
        m_prev = m_scr[h]
        m_new = jnp.maximum(total + m_prev, jnp.max(total - br + li_r, axis=-1, keepdims=True))
        w_src = jnp.exp(total - bc + li_c - m_new)
        w_carry = jnp.exp(total + m_prev - m_new)
        kh = k[:, h * ML_QK_DIM:(h + 1) * ML_QK_DIM].astype(F32) * w_src
        vh = v[:, h * ML_V_DIM:(h + 1) * ML_V_DIM]
        c_scr[h] = w_carry * c_scr[h] + lax.dot_general(kh.astype(BF16), vh, TN_DIMS,
                                                        preferred_element_type=F32)
        n_scr[h] = w_carry * n_scr[h] + jnp.sum(kh, axis=0, keepdims=True)
        m_scr[h] = m_new

    @pl.when(step == 0)
    def _():
        c_scr[...] = jnp.zeros_like(c_scr)
        n_scr[...] = jnp.zeros_like(n_scr)
        m_scr[...] = jnp.zeros_like(m_scr)
        k = kc_ref[...]
        v = vc_ref[...]
        gc = gcc_ref[...]
        gr = gcr_ref[...]
        valid, valid_t = tri(k.shape[0])
        b_col, b_row = cumulative(gc, gr, valid, valid_t)
        for h in range(ML_HEADS):
            update_state(h, k, v, gc, gr, b_col, b_row)

    @pl.when(step > 0)
    def _():
        q = q_ref[...]
        k = k_ref[...]
        v = v_ref[...]
        gc = gc_ref[...]
        gr = gr_ref[...]
        valid, valid_t = tri(k.shape[0])
        b_col, b_row = cumulative(gc, gr, valid, valid_t)
        outs = []
        for h in range(ML_HEADS):
            li_r = gr[li_base + h:li_base + h + 1, :]
            bc = b_col[:, lf_base + h:lf_base + h + 1]
            br = b_row[lf_base + h:lf_base + h + 1, :]
            m_prev = m_scr[h]
            qh = q[:, h * ML_QK_DIM:(h + 1) * ML_QK_DIM]
            kh = k[:, h * ML_QK_DIM:(h + 1) * ML_QK_DIM]
            vh = v[:, h * ML_V_DIM:(h + 1) * ML_V_DIM]
            d_mat = jnp.where(valid, bc - br + li_r, NEG_INF)
            m_inter = bc + m_prev
            m_t = jnp.maximum(m_inter, jnp.max(d_mat, axis=-1, keepdims=True))
            w_intra = jnp.exp(d_mat - m_t)
            w_inter = jnp.exp(m_inter - m_t)
            s = lax.dot_general(qh, kh, NT_DIMS, preferred_element_type=F32) * w_intra
            num = (w_inter * jnp.dot(qh, c_scr[h].astype(BF16), preferred_element_type=F32)
                   + jnp.dot(s.astype(BF16), vh, preferred_element_type=F32))
            den = (w_inter * jnp.sum(qh.astype(F32) * n_scr[h], axis=-1, keepdims=True)
                   + jnp.sum(s, axis=-1, keepdims=True))
            outs.append(num / jnp.maximum(jnp.abs(den), jnp.exp(-m_t)))
        o_ref[...] = jnp.concatenate(outs, axis=-1).astype(BF16)
        for h in range(ML_HEADS):
            update_state(h, k, v, gc, gr, b_col, b_row)


def _mlstm_direction(q, k, v, gc, gr, kc, vc, gcc, gcr, reverse):
    b, s, _ = q.shape
    n_ctx = kc.shape[1]
    length = ML_CHUNK
    n_chunks = s // length

    def chunk(step):
        c = jnp.maximum(step - 1, 0)
        return (n_chunks - 1 - c) if reverse else c

    return pl.pallas_call(
        functools.partial(_mlstm_kernel, reverse=reverse),
        grid=(b, n_chunks + 1),
        in_specs=[pl.BlockSpec((None, length, ML_QK_WIDTH), lambda bi, st: (bi, chunk(st), 0)),
                  pl.BlockSpec((None, length, ML_QK_WIDTH), lambda bi, st: (bi, chunk(st), 0)),
                  pl.BlockSpec((None, length, ML_WIDTH), lambda bi, st: (bi, chunk(st), 0)),
                  pl.BlockSpec((None, length, N_GATES), lambda bi, st: (bi, chunk(st), 0)),
                  pl.BlockSpec((N_GATES, length), lambda bi, st: (0, bi * n_chunks + chunk(st))),
                  pl.BlockSpec((None, n_ctx, ML_QK_WIDTH), lambda bi, st: (bi, 0, 0)),
                  pl.BlockSpec((None, n_ctx, ML_WIDTH), lambda bi, st: (bi, 0, 0)),
                  pl.BlockSpec((None, n_ctx, N_GATES), lambda bi, st: (bi, 0, 0)),
                  pl.BlockSpec((N_GATES, n_ctx), lambda bi, st: (0, bi))],
        out_specs=pl.BlockSpec((None, length, ML_WIDTH), lambda bi, st: (bi, chunk(st), 0)),
        out_shape=jax.ShapeDtypeStruct((b, s, ML_WIDTH), BF16),
        scratch_shapes=[pltpu.VMEM((ML_HEADS, ML_QK_DIM, ML_V_DIM), F32),
                        pltpu.VMEM((ML_HEADS, 1, ML_QK_DIM), F32),
                        pltpu.VMEM((ML_HEADS, 1, 1), F32)],
        compiler_params=_cparams("arbitrary", "arbitrary"),
        name="mlstm_bwd" if reverse else "mlstm_fwd",
    )(q, k, v, gc, gr, kc, vc, gcc, gcr)


def _merge_kernel(x_ref, ona_ref, hf_ref, hb_ref, opre_ref, gna_ref, gml_ref, gtm_ref, scf_ref, shf_ref,
                  ghead_ref, wbna_ref, wbml_ref, wout_ref, gpost_ref, gpre_ref, wr_ref, br_ref,
                  x1_ref, h2_ref, tope_ref, topw_ref, rank_ref, cnt_ref):
    step = pl.program_id(0)
    tm = x_ref.shape[0]

    @pl.when(step == 0)
    def _():
        cnt_ref[...] = jnp.zeros_like(cnt_ref)

    hsum = hf_ref[...].astype(F32) + hb_ref[...].astype(F32)
    heads = [_rms(hsum[:, h * ML_V_DIM:(h + 1) * ML_V_DIM]) for h in range(ML_HEADS)]
    hn = jnp.concatenate(heads, axis=-1) * ghead_ref[...]
    o_ml = jax.nn.sigmoid(opre_ref[...].astype(F32)) * hn
    merged = (jax.nn.sigmoid(gna_ref[...].astype(F32))
              * jnp.dot(ona_ref[...], wbna_ref[...], preferred_element_type=F32)
              + jax.nn.sigmoid(gml_ref[...].astype(F32))
              * jnp.dot(o_ml.astype(BF16), wbml_ref[...], preferred_element_type=F32))
    mixed = jnp.dot(merged.astype(BF16), wout_ref[...], preferred_element_type=F32)
    x1 = x_ref[...] + gtm_ref[...] * (_rms(mixed) * gpost_ref[...])
    x1_ref[...] = x1
    h2 = _rms(x1) * gpre_ref[...] * (1.0 + scf_ref[...]) + shf_ref[...]
    h2_ref[...] = h2
    logits = jnp.dot(h2, wr_ref[...], preferred_element_type=F32, precision=HIGHEST) + br_ref[...]

    lane = lax.broadcasted_iota(jnp.int32, logits.shape, 1)
    onehots, top_e, top_l = [], [], []
    for _ in range(TOP_K):
        best = jnp.max(logits, axis=-1, keepdims=True)
        e = jnp.min(jnp.where(logits == best, lane, N_EXPERTS), axis=-1, keepdims=True)
        hit = lane == e
        onehots.append(hit)
        top_e.append(e)
        top_l.append(best)
        logits = jnp.where(hit, -jnp.inf, logits)
    exps = [jnp.exp(l - top_l[0]) for l in top_l]
    total = exps[0] + exps[1] + exps[2] + exps[3]

    counts = (onehots[0].astype(F32) + onehots[1].astype(F32)
              + onehots[2].astype(F32) + onehots[3].astype(F32))
    t = lax.broadcasted_iota(jnp.int32, (tm, tm), 0)
    s = lax.broadcasted_iota(jnp.int32, (tm, tm), 1)
    before = jnp.dot((s < t).astype(BF16), counts.astype(BF16), preferred_element_type=F32) + cnt_ref[...]
    out_lane = lax.broadcasted_iota(jnp.int32, (tm, LANES), 1)
    e_out = jnp.zeros((tm, LANES), jnp.int32)
    w_out = jnp.zeros((tm, LANES), F32)
    r_out = jnp.zeros((tm, LANES), jnp.int32)
    for j in range(TOP_K):
        rank = jnp.sum(jnp.where(onehots[j], before, 0.0), axis=-1, keepdims=True).astype(jnp.int32)
        e_out = jnp.where(out_lane == j, top_e[j], e_out)
        w_out = jnp.where(out_lane == j, exps[j] / total, w_out)
        r_out = jnp.where(out_lane == j, rank, r_out)
    tope_ref[...] = e_out
    topw_ref[...] = w_out
    rank_ref[...] = r_out
    cnt_ref[...] += jnp.sum(counts, axis=0, keepdims=True)


def _merge_and_route(x2, o_na, h_f, h_b, o_pre, g_na, g_ml, mod4, seq, g_head, wbna, wbml, wout,
                     g_post, g_pre, w_router, b_router):
    n, d = x2.shape
    tm = MERGE_ROWS
    per_seq = seq // tm

    def rows(width):
        return pl.BlockSpec((tm, width), lambda i: (i, 0))

    def mod_spec(j):
        return pl.BlockSpec((None, None, 1, d), lambda i, _j=j: (i // per_seq, _j, 0, 0))

    return pl.pallas_call(
        _merge_kernel,
        grid=(n // tm,),
        in_specs=[rows(d), rows(NA_WIDTH), rows(ML_WIDTH), rows(ML_WIDTH), rows(ML_WIDTH), rows(d), rows(d),
                  mod_spec(2), mod_spec(4), mod_spec(3),
                  _const_spec((1, ML_WIDTH)), _const_spec(wbna.shape), _const_spec(wbml.shape),
                  _const_spec(wout.shape), _const_spec((1, d)), _const_spec((1, d)),
                  _const_spec(w_router.shape), _const_spec((1, N_EXPERTS))],
        out_specs=[rows(d), rows(d), rows(LANES), rows(LANES), rows(LANES),
                   pl.BlockSpec((1, N_EXPERTS), lambda i: (0, 0))],
        out_shape=[jax.ShapeDtypeStruct((n, d), F32), jax.ShapeDtypeStruct((n, d), F32),
                   jax.ShapeDtypeStruct((n, LANES), jnp.int32), jax.ShapeDtypeStruct((n, LANES), F32),
                   jax.ShapeDtypeStruct((n, LANES), jnp.int32),
                   jax.ShapeDtypeStruct((1, N_EXPERTS), F32)],
        compiler_params=_cparams("arbitrary"),
        name="merge_route",
    )(x2, o_na, h_f, h_b, o_pre, g_na, g_ml, mod4, mod4, mod4,
      g_head.reshape(1, ML_WIDTH), wbna, wbml, wout, g_post.reshape(1, d), g_pre.reshape(1, d),
      w_router, b_router.reshape(1, N_EXPERTS))


def _dispatch_kernel(pad_end_ref, padded_ref, dest_ref, h_ref, xs_ref, zero_scr, sem):
    tm = h_ref.shape[0]
    blk = zero_scr.shape[0]

    @pl.when(pl.program_id(0) == 0)
    def _():
        zero_scr[...] = jnp.zeros_like(zero_scr)

        def zero_copy(e):
            first = pl.multiple_of(pad_end_ref[e] - blk, blk)
            return pltpu.make_async_copy(zero_scr, xs_ref.at[pl.ds(first, blk)], sem)

        for e in range(N_EXPERTS):
            @pl.when(padded_ref[e] > 0)
            def _():
                zero_copy(e).start()
        for e in range(N_EXPERTS):
            @pl.when(padded_ref[e] > 0)
            def _():
                zero_copy(e).wait()

        def tail_copy(b):
            return pltpu.make_async_copy(zero_scr, xs_ref.at[pl.ds(pl.multiple_of(b * blk, blk), blk)], sem)

        def tail_start(b, carry):
            tail_copy(b).start()
            return carry

        def tail_wait(b, carry):
            tail_copy(b).wait()
            return carry

        first_unused = pad_end_ref[N_EXPERTS - 1] // blk
        lax.fori_loop(first_unused, xs_ref.shape[0] // blk, tail_start, 0)
        lax.fori_loop(first_unused, xs_ref.shape[0] // blk, tail_wait, 0)

    tok_per_row = LANES // TOP_K

    def row_copy(r, c):
        return pltpu.make_async_copy(h_ref.at[pl.ds(r * tok_per_row + c // TOP_K, 1)],
                                     xs_ref.at[pl.ds(dest_ref[r, c], 1)], sem)

    def start(r, carry):
        for c in range(LANES):
            row_copy(r, c).start()
        return carry

    def wait(r, carry):
        for c in range(LANES):
            row_copy(r, c).wait()
        return carry

    lax.fori_loop(0, tm // tok_per_row, start, 0)
    lax.fori_loop(0, tm // tok_per_row, wait, 0)


def _dispatch(pad_end, padded, dest2, h2, n_rows):
    n = h2.shape[0]
    tm = MOVE_ROWS
    idx_rows = tm * TOP_K // LANES
    row_tile = h2.shape[1:]
    grid_spec = pltpu.PrefetchScalarGridSpec(
        num_scalar_prefetch=2,
        grid=(n // tm,),
        in_specs=[pl.BlockSpec((idx_rows, LANES), lambda i, pe, pd: (i, 0), memory_space=pltpu.SMEM),
                  pl.BlockSpec((tm,) + row_tile, lambda i, pe, pd: (i, 0))],
        out_specs=pl.BlockSpec(memory_space=pl.ANY),
        scratch_shapes=[pltpu.VMEM((EXPERT_ROWS,) + row_tile, h2.dtype), pltpu.SemaphoreType.DMA(())],
    )
    return pl.pallas_call(
        _dispatch_kernel,
        grid_spec=grid_spec,
        out_shape=jax.ShapeDtypeStruct((n_rows,) + row_tile, h2.dtype),
        compiler_params=_cparams("arbitrary"),
        name="moe_dispatch",
    )(pad_end, padded, dest2, h2)


def _expert_kernel(blk_e_ref, n_used_ref, x_ref, wg_ref, bg_ref, wl_ref, bl_ref, wd_ref, bd_ref, y_ref,
                   wg_s, wl_s, wd_s):
    i = pl.program_id(0)
    prev = blk_e_ref[jnp.maximum(i - 1, 0)]
    changed = jnp.logical_or(i == 0, blk_e_ref[i] != prev)
    used = i < n_used_ref[0]

    @pl.when(jnp.logical_and(used, changed))
    def _():
        wg_s[...] = wg_ref[...].astype(BF16)
        wl_s[...] = wl_ref[...].astype(BF16)
        wd_s[...] = wd_ref[...].astype(BF16)

    @pl.when(used)
    def _():
        xb = x_ref[...].astype(BF16)
        g = jnp.dot(xb, wg_s[...], preferred_element_type=F32) + bg_ref[...]
        l = jnp.dot(xb, wl_s[...], preferred_element_type=F32) + bl_ref[...]
        g = jnp.minimum(g, SWIGLU_LIMIT)
        l = jnp.clip(l, -SWIGLU_LIMIT, SWIGLU_LIMIT)
        a = g * jax.nn.sigmoid(SWIGLU_ALPHA * g) * (l + 1.0)
        y_ref[...] = jnp.dot(a.astype(BF16), wd_s[...], preferred_element_type=F32) + bd_ref[...]

    @pl.when(jnp.logical_not(used))
    def _():
        y_ref[...] = jnp.zeros_like(y_ref)


def _experts(blk_e, n_used, xs, w_gate, b_gate, w_lin, b_lin, w_down, b_down):
    n_rows = xs.shape[0]
    row_tile = xs.shape[1:]
    e, d, f = w_gate.shape
    tm = EXPERT_ROWS

    def w_spec(shape):
        return pl.BlockSpec((None,) + shape, lambda i, be, nu: (be[i], 0, 0))

    grid_spec = pltpu.PrefetchScalarGridSpec(
        num_scalar_prefetch=2,
        grid=(n_rows // tm,),
        in_specs=[pl.BlockSpec((tm,) + row_tile, lambda i, be, nu: (i, 0)),
                  w_spec((d, f)), w_spec((1, f)), w_spec((d, f)), w_spec((1, f)),
                  w_spec((f, d)), w_spec((1, d))],
        out_specs=pl.BlockSpec((tm,) + row_tile, lambda i, be, nu: (i, 0)),
        scratch_shapes=[pltpu.VMEM((d, f), BF16), pltpu.VMEM((d, f), BF16), pltpu.VMEM((f, d), BF16)],
    )
    return pl.pallas_call(
        _expert_kernel,
        grid_spec=grid_spec,
        out_shape=jax.ShapeDtypeStruct((n_rows,) + row_tile, F32),
        compiler_params=_cparams("arbitrary"),
        name="moe_experts",
    )(blk_e, n_used, xs, w_gate, b_gate.reshape(e, 1, f), w_lin, b_lin.reshape(e, 1, f),
      w_down, b_down.reshape(e, 1, d))


def _combine_kernel(dest_ref, x1_ref, w_ref, gtf_ref, gpost_ref, y_ref, o_ref, buf, sem):
    tm = x1_ref.shape[0]

    tok_per_row = LANES // TOP_K

    def row_copy(r, c):
        return pltpu.make_async_copy(y_ref.at[pl.ds(dest_ref[r, c], 1)],
                                     buf.at[c % TOP_K, pl.ds(r * tok_per_row + c // TOP_K, 1)], sem)

    def start(r, carry):
        for c in range(LANES):
            row_copy(r, c).start()
        return carry

    def wait(r, carry):
        for c in range(LANES):
            row_copy(r, c).wait()
        return carry

    lax.fori_loop(0, tm // tok_per_row, start, 0)
    lax.fori_loop(0, tm // tok_per_row, wait, 0)
    w = w_ref[...]
    ffn = (buf[0] * w[:, 0:1] + buf[1] * w[:, 1:2]) + (buf[2] * w[:, 2:3] + buf[3] * w[:, 3:4])
    o_ref[...] = x1_ref[...] + gtf_ref[...] * (_rms(ffn) * gpost_ref[...])


def _combine(dest2, x1, top_w, mod4, seq, g_post, y):
    n, d = x1.shape
    tm = MOVE_ROWS
    per_seq = seq // tm
    idx_rows = tm * TOP_K // LANES
    return pl.pallas_call(
        _combine_kernel,
        grid=(n // tm,),
        in_specs=[pl.BlockSpec((idx_rows, LANES), lambda i: (i, 0), memory_space=pltpu.SMEM),
                  pl.BlockSpec((tm, d), lambda i: (i, 0)),
                  pl.BlockSpec((tm, LANES), lambda i: (i, 0)),
                  pl.BlockSpec((None, None, 1, d), lambda i: (i // per_seq, 5, 0, 0)),
                  _const_spec((1, d)),
                  pl.BlockSpec(memory_space=pl.ANY)],
        out_specs=pl.BlockSpec((tm, d), lambda i: (i, 0)),
        out_shape=jax.ShapeDtypeStruct((n, d), F32),
        scratch_shapes=[pltpu.VMEM((TOP_K, tm) + y.shape[1:], F32), pltpu.SemaphoreType.DMA(())],
        compiler_params=_cparams("arbitrary"),
        name="moe_combine",
    )(dest2, x1, top_w, mod4, g_post.reshape(1, d), y)


def _rope_partner(w):
    n_freq = ML_QK_DIM // 4
    d = w.shape[0]
    w4 = w.reshape(d, -1, 2, n_freq)
    return jnp.stack([-w4[:, :, 1], w4[:, :, 0]], axis=2).reshape(w.shape)


def _rope_tables(seq):
    n_freq = ML_QK_DIM // 4
    t = jnp.arange(seq)
    row = (t // GRID_W).astype(F32)
    col = (t % GRID_W).astype(F32)
    inv_freq = ROPE_BASE ** (-jnp.arange(n_freq, dtype=F32) / n_freq)
    ang = jnp.concatenate([row[:, None] * inv_freq] * 2 + [col[:, None] * inv_freq] * 2, axis=-1)
    cos = jnp.tile(jnp.cos(ang), (1, ML_HEADS))
    sin = jnp.tile(jnp.sin(ang), (1, ML_HEADS))
    return cos, sin


def _layer(x, ctx, mod4, g_mix_pre, g_mix_post, g_ffn_pre, g_ffn_post, w_in, b_gates, rpb, g_head,
           w_branch_na, w_branch_ml, w_out, w_router, b_router, w_gate, b_gate, w_lin, b_lin, w_down, b_down):
    b, s, d = x.shape
    n = b * s
    n_ctx = ctx.shape[1]
    x2 = x.reshape(n, d)

    ctx_cols = (NA_WIDTH, NA_WIDTH, ML_QK_WIDTH, ML_WIDTH, N_GATES)
    lat_cols = (NA_WIDTH, ML_QK_WIDTH, ML_WIDTH, d, d)
    bounds = np.cumsum(ctx_cols + lat_cols)[:-1].tolist()
    (w_nak, w_nav, w_mk, w_mv, w_g, w_naq, w_mq, w_mo, w_gna, w_gml) = jnp.split(w_in, bounds, axis=-1)
    w_naq = w_naq * NA_HEAD_DIM ** -0.5
    w_mk = w_mk * ML_QK_DIM ** -0.5
    bf = lambda a: a.astype(BF16)
    lat_w = [bf(w_naq), bf(w_nak), bf(w_nav), bf(w_mq), bf(_rope_partner(w_mq)), bf(w_mk),
             bf(_rope_partner(w_mk)), bf(w_mv), bf(w_mo), bf(w_gna), bf(w_gml), bf(w_g), bf(w_g.T)]
    ctx_w = [bf(w_nak), bf(w_nav), bf(w_mk), bf(w_mv), bf(w_g), bf(w_g.T)]
    bg_col = b_gates.reshape(1, N_GATES).astype(F32)
    bg_row = b_gates.reshape(N_GATES, 1).astype(F32)
    cos, sin = _rope_tables(s)
    per_seq = s // PROJ_ROWS

    (na_q, na_k, na_v, ml_q, ml_k, ml_v, ml_o, gate_na, gate_ml, g_col, g_row) = _project(
        x2, mod4, lambda i: i // per_seq, g_mix_pre, bg_col, bg_row, lat_w, (cos, sin, per_seq), True)
    (na_kc, na_vc, ml_kc, ml_vc, gc_col, gc_row) = _project(
        ctx.reshape(b * n_ctx, d), mod4, lambda i: b, g_mix_pre, bg_col, bg_row, ctx_w, None, False)

    def seq3(a, length):
        return a.reshape(b, length, a.shape[-1])

    o_na = _neighbourhood_attention(seq3(na_q, s), seq3(na_k, s), seq3(na_v, s),
                                    seq3(na_kc, n_ctx), seq3(na_vc, n_ctx), _na_bias_table(rpb))
    ml_args = (seq3(ml_q, s), seq3(ml_k, s), seq3(ml_v, s), seq3(g_col, s), g_row,
               seq3(ml_kc, n_ctx), seq3(ml_vc, n_ctx), seq3(gc_col, n_ctx), gc_row)
    h_f = _mlstm_direction(*ml_args, reverse=False)
    h_b = _mlstm_direction(*ml_args, reverse=True)

    x1, h2, top_e, top_w, rank, counts = _merge_and_route(
        x2, o_na.reshape(n, NA_WIDTH), h_f.reshape(n, ML_WIDTH), h_b.reshape(n, ML_WIDTH), ml_o,
        gate_na, gate_ml, mod4, s, g_head, bf(w_branch_na), bf(w_branch_ml), bf(w_out),
        g_mix_post, g_ffn_pre, w_router, b_router)

    tm = EXPERT_ROWS
    counts = counts.reshape(N_EXPERTS).astype(jnp.int32)
    padded = (counts + tm - 1) // tm * tm
    pad_end = jnp.cumsum(padded)
    pad_start = pad_end - padded
    n_rows = n * TOP_K + N_EXPERTS * tm
    n_blocks = n_rows // tm
    e_sel = top_e[:, :TOP_K, None] == jnp.arange(N_EXPERTS, dtype=jnp.int32)
    dest = jnp.sum(jnp.where(e_sel, pad_start, 0), axis=-1) + rank[:, :TOP_K]
    dest2 = dest.reshape(n * TOP_K // LANES, LANES).astype(jnp.int32)
    blk_start = jnp.arange(n_blocks, dtype=jnp.int32) * tm
    blk_e = jnp.minimum(jnp.sum(blk_start[:, None] >= pad_end[None, :], axis=1), N_EXPERTS - 1).astype(jnp.int32)
    n_used = (pad_end[-1:] // tm).astype(jnp.int32)

    xs = _dispatch(pad_end.astype(jnp.int32), padded.astype(jnp.int32), dest2, h2, n_rows)
    y = _experts(blk_e, n_used, xs, w_gate, b_gate, w_lin, b_lin, w_down, b_down)
    out = _combine(dest2, x1, top_w, mod4, s, g_ffn_post, y)
    return out.reshape(b, s, d)


def kernel(x, c, ctx, c_ctx, w_ada, b_ada, g_mix_pre, g_mix_post, g_ffn_pre, g_ffn_post, w_in, b_mlstm_gates,
           rpb, g_mlstm_head, w_branch_na, w_branch_ml, w_out, w_router, b_router, w_gate, b_gate, w_lin,
           b_lin, w_down, b_down):
    b, s, d = x.shape
    depth = w_ada.shape[0]
    pad = (-(b + 1)) % 8
    c_all = jnp.concatenate([c, c_ctx[None, :], jnp.zeros((pad, d), c.dtype)], axis=0)
    for layer in range(depth):
        mod = _ada(c_all, w_ada[layer], b_ada[layer])
        mod4 = mod.reshape(mod.shape[0], 6, 1, d)
        x = _layer(x, ctx, mod4, g_mix_pre[layer], g_mix_post[layer], g_ffn_pre[layer], g_ffn_post[layer],
                   w_in[layer], b_mlstm_gates[layer], rpb[layer], g_mlstm_head[layer], w_branch_na[layer],
                   w_branch_ml[layer], w_out[layer], w_router[layer], b_router[layer], w_gate[layer],
                   b_gate[layer], w_lin[layer], b_lin[layer], w_down[layer], b_down[layer])
    return x
```

```python
import functools

import numpy as np
import jax
import jax.numpy as jnp
from jax import lax
from jax.experimental import pallas as pl
from jax.experimental.pallas import tpu as pltpu

F32 = jnp.float32
BF16 = jnp.bfloat16
HIGHEST = lax.Precision.HIGHEST

GRID_W = 64
NA_HEADS = 8
NA_HEAD_DIM = 64
NA_WIDTH = NA_HEADS * NA_HEAD_DIM
WIN_H = 8
WIN_W = 16
ML_HEADS = 4
ML_QK_DIM = 64
ML_V_DIM = 128
ML_QK_WIDTH = ML_HEADS * ML_QK_DIM
ML_WIDTH = ML_HEADS * ML_V_DIM
N_GATES = 4 * ML_HEADS
GATE_SOFTCAP = 15.0
ROPE_BASE = 10000.0
N_EXPERTS = 32
TOP_K = 4
SWIGLU_ALPHA = 1.702
SWIGLU_LIMIT = 7.0
NORM_EPS = 1e-6
NEG_INF = -1e30

LANES = 128
NA_QROWS = 4
NA_WROWS = 12
ML_CHUNK = 256
PROJ_ROWS = 512
MERGE_ROWS = 256
EXPERT_ROWS = 256
MOVE_ROWS = 256
VMEM_LIMIT = 56 * 1024 * 1024

NT_DIMS = (((1,), (1,)), ((), ()))
TN_DIMS = (((0,), (0,)), ((), ()))


def _cparams(*sem):
    return pltpu.CompilerParams(dimension_semantics=sem, vmem_limit_bytes=VMEM_LIMIT)


def _rms(x):
    return x * lax.rsqrt(jnp.mean(x * x, axis=-1, keepdims=True) + NORM_EPS)


def _ada_kernel(c_ref, w_ref, b_ref, o_ref):
    c = c_ref[...]
    s = c * jax.nn.sigmoid(c)
    o_ref[...] = jnp.dot(s, w_ref[...], preferred_element_type=F32, precision=HIGHEST) + b_ref[...]


def _ada(c_all, w_ada, b_ada):
    rows, d = c_all.shape
    n_out = w_ada.shape[1]
    tn = 1536
    return pl.pallas_call(
        _ada_kernel,
        grid=(n_out // tn,),
        in_specs=[pl.BlockSpec((rows, d), lambda j: (0, 0)),
                  pl.BlockSpec((d, tn), lambda j: (0, j)),
                  pl.BlockSpec((1, tn), lambda j: (0, j))],
        out_specs=pl.BlockSpec((rows, tn), lambda j: (0, j)),
        out_shape=jax.ShapeDtypeStruct((rows, n_out), F32),
        compiler_params=_cparams("arbitrary"),
        name="ada_mod",
    )(c_all, w_ada, b_ada.reshape(1, n_out))


def _gate_logs(g, is_forget):
    g = GATE_SOFTCAP * jnp.tanh(g / GATE_SOFTCAP)
    log_sig = jnp.minimum(g, 0.0) - jnp.log(1.0 + jnp.exp(-jnp.abs(g)))
    return jnp.where(is_forget, log_sig, g)


def _proj_kernel(*refs, latent):
    if latent:
        (x_ref, g_ref, sc_ref, sh_ref, cos_ref, sin_ref, bgc_ref, bgr_ref,
         w_naq, w_nak, w_nav, w_mq, w_mqp, w_mk, w_mkp, w_mv, w_mo, w_gna, w_gml, w_gc, w_gr,
         o_naq, o_nak, o_nav, o_mq, o_mk, o_mv, o_mo, o_gna, o_gml, o_gc, o_gr) = refs
    else:
        (x_ref, g_ref, sc_ref, sh_ref, bgc_ref, bgr_ref,
         w_nak, w_nav, w_mk, w_mv, w_gc, w_gr,
         o_nak, o_nav, o_mk, o_mv, o_gc, o_gr) = refs
    x = x_ref[...]
    h = _rms(x) * g_ref[...]
    h = h * (1.0 + sc_ref[...]) + sh_ref[...]
    hb = h.astype(BF16)

    def mm(w_ref):
        return jnp.dot(hb, w_ref[...], preferred_element_type=F32)

    o_nak[...] = mm(w_nak).astype(BF16)
    o_nav[...] = mm(w_nav).astype(BF16)
    o_mv[...] = mm(w_mv).astype(BF16)
    if latent:
        cos = cos_ref[...]
        sin = sin_ref[...]
        o_naq[...] = mm(w_naq).astype(BF16)
        o_mq[...] = (mm(w_mq) * cos + mm(w_mqp) * sin).astype(BF16)
        o_mk[...] = (mm(w_mk) * cos + mm(w_mkp) * sin).astype(BF16)
        o_mo[...] = mm(w_mo).astype(BF16)
        o_gna[...] = mm(w_gna).astype(BF16)
        o_gml[...] = mm(w_gml).astype(BF16)
    else:
        o_mk[...] = mm(w_mk).astype(BF16)
    gc = mm(w_gc) + bgc_ref[...]
    col_id = lax.broadcasted_iota(jnp.int32, gc.shape, 1)
    o_gc[...] = _gate_logs(gc, (col_id // ML_HEADS) % 2 == 1)
    gr = lax.dot_general(w_gr[...], hb, NT_DIMS, preferred_element_type=F32) + bgr_ref[...]
    row_id = lax.broadcasted_iota(jnp.int32, gr.shape, 0)
    o_gr[...] = _gate_logs(gr, (row_id // ML_HEADS) % 2 == 1)


def _const_spec(shape):
    nd = len(shape)
    return pl.BlockSpec(shape, lambda i, _nd=nd: (0,) * _nd)


def _project(x2, mod4, mod_row_fn, g_pre, bg_col, bg_row, weights, tables, latent):
    n, d = x2.shape
    tm = PROJ_ROWS
    grid = (n // tm,)

    def mod_spec(j):
        return pl.BlockSpec((None, None, 1, d), lambda i, _j=j: (mod_row_fn(i), _j, 0, 0))

    in_specs = [pl.BlockSpec((tm, d), lambda i: (i, 0)), _const_spec((1, d)), mod_spec(1), mod_spec(0)]
    args = [x2, g_pre.reshape(1, d), mod4, mod4]
    if latent:
        cos, sin, tiles_per_seq = tables
        in_specs += [pl.BlockSpec((tm, ML_QK_WIDTH), lambda i: (i % tiles_per_seq, 0))] * 2
        args += [cos, sin]
    in_specs += [_const_spec(bg_col.shape), _const_spec(bg_row.shape)]
    args += [bg_col, bg_row]
    for w in weights:
        in_specs.append(_const_spec(w.shape))
        args.append(w)

    def out(width, dtype=BF16):
        return (jax.ShapeDtypeStruct((n, width), dtype), pl.BlockSpec((tm, width), lambda i: (i, 0)))

    if latent:
        outs = [out(NA_WIDTH), out(NA_WIDTH), out(NA_WIDTH), out(ML_QK_WIDTH), out(ML_QK_WIDTH),
                out(ML_WIDTH), out(ML_WIDTH), out(d), out(d), out(N_GATES, F32)]
    else:
        outs = [out(NA_WIDTH), out(NA_WIDTH), out(ML_QK_WIDTH), out(ML_WIDTH), out(N_GATES, F32)]
    outs.append((jax.ShapeDtypeStruct((N_GATES, n), F32), pl.BlockSpec((N_GATES, tm), lambda i: (0, i))))
    return pl.pallas_call(
        functools.partial(_proj_kernel, latent=latent),
        grid=grid,
        in_specs=in_specs,
        out_specs=[o[1] for o in outs],
        out_shape=[o[0] for o in outs],
        compiler_params=_cparams("arbitrary"),
        name="in_proj_latent" if latent else "in_proj_ctx",
    )(*args)


def _na_window_start(r0, rows):
    return jnp.clip(r0 - WIN_H // 2, 0, rows - NA_WROWS)


def _na_classes(rows):
    keys, group_class = [], []
    for r0 in range(0, rows, NA_QROWS):
        start = min(max(r0 - WIN_H // 2, 0), rows - NA_WROWS)
        first = tuple(min(max(r0 + i - WIN_H // 2, 0), rows - WIN_H) - start for i in range(NA_QROWS))
        assert all(0 <= f and f + WIN_H <= NA_WROWS for f in first)
        key = (r0 - start, first)
        if key not in keys:
            keys.append(key)
        group_class.append(keys.index(key))
    return keys, np.asarray(group_class, np.int32)


def _na_bias_table(rpb, rows):
    keys, group_class = _na_classes(rows)
    qc = np.arange(GRID_W)[:, None]
    kc = np.arange(GRID_W)[None, :]
    cs = np.clip(qc - WIN_W // 2, 0, GRID_W - WIN_W)
    col_ok = (kc >= cs) & (kc < cs + WIN_W)
    dc = np.clip(kc - qc, -(WIN_W - 1), WIN_W - 1) + (WIN_W - 1)
    i = np.arange(NA_QROWS)[:, None]
    j = np.arange(NA_WROWS)[None, :]
    sel_r, row_ok = [], []
    for off, first in keys:
        f = np.asarray(first)[:, None]
        ok = (j >= f) & (j < f + WIN_H)
        dr = j - off - i + (WIN_H - 1)
        sel_r.append(((dr[:, :, None] == np.arange(2 * WIN_H - 1)) & ok[:, :, None]).astype(np.float32))
        row_ok.append(ok)
    sel_r = np.stack(sel_r)
    valid = np.stack(row_ok)[:, None, :, None, :, None] & col_ok[None, None, None, :, None, :]
    sel_c = (dc[:, :, None] == np.arange(2 * WIN_W - 1)).astype(np.float32)
    t = jnp.einsum('hrc,qkc->hrqk', rpb.astype(F32), sel_c, precision=HIGHEST)
    bias = jnp.einsum('hrqk,xijr->xhiqjk', t, sel_r, precision=HIGHEST)
    bias = jnp.where(valid, bias, NEG_INF)
    bias = bias.reshape(len(keys), NA_HEADS, NA_QROWS * GRID_W, NA_WROWS * GRID_W)
    return bias.astype(BF16), jnp.asarray(group_class)


def _na_kernel(cls_ref, q_ref, k_ref, v_ref, kc_ref, vc_ref, bias_ref, o_ref, *, rows):
    del cls_ref
    r0 = pl.program_id(1) * NA_QROWS
    start = pl.multiple_of(_na_window_start(r0, rows) * GRID_W, GRID_W)
    n_win = NA_WROWS * GRID_W
    low = lax.broadcasted_iota(jnp.int32, (1, LANES), 1) < NA_HEAD_DIM
    for pair in range(NA_HEADS // 2):
        sl = slice(pair * LANES, (pair + 1) * LANES)
        q2 = q_ref[:, sl]
        k2 = k_ref[pl.ds(start, n_win), sl]
        v2 = v_ref[pl.ds(start, n_win), sl]
        kc2 = kc_ref[:, sl]
        vc2 = vc_ref[:, sl]
        halves = []
        for half in range(2):
            keep = low if half == 0 else jnp.logical_not(low)
            qh = jnp.where(keep, q2, jnp.zeros_like(q2))
            s_loc = (lax.dot_general(qh, k2, NT_DIMS, preferred_element_type=F32)
                     + bias_ref[2 * pair + half].astype(F32))
            s_ctx = lax.dot_general(qh, kc2, NT_DIMS, preferred_element_type=F32)
            m = jnp.maximum(jnp.max(s_loc, axis=-1, keepdims=True), jnp.max(s_ctx, axis=-1, keepdims=True))
            p_loc = jnp.exp(s_loc - m)
            p_ctx = jnp.exp(s_ctx - m)
            denom = jnp.sum(p_loc, axis=-1, keepdims=True) + jnp.sum(p_ctx, axis=-1, keepdims=True)
            o = (jnp.dot(p_loc.astype(BF16), v2, preferred_element_type=F32)
                 + jnp.dot(p_ctx.astype(BF16), vc2, preferred_element_type=F32))
            halves.append(o / denom)
        o_ref[:, sl] = jnp.where(low, halves[0], halves[1]).astype(BF16)


def _neighbourhood_attention(q, k, v, kc, vc, rpb):
    b, s, w = q.shape
    rows = s // GRID_W
    n_ctx = kc.shape[1]
    bias, group_class = _na_bias_table(rpb, rows)
    nq = NA_QROWS * GRID_W
    grid_spec = pltpu.PrefetchScalarGridSpec(
        num_scalar_prefetch=1,
        grid=(b, rows // NA_QROWS),
        in_specs=[pl.BlockSpec((None, nq, w), lambda bi, g, cls: (bi, g, 0)),
                  pl.BlockSpec((None, s, w), lambda bi, g, cls: (bi, 0, 0)),
                  pl.BlockSpec((None, s, w), lambda bi, g, cls: (bi, 0, 0)),
                  pl.BlockSpec((None, n_ctx, w), lambda bi, g, cls: (bi, 0, 0)),
                  pl.BlockSpec((None, n_ctx, w), lambda bi, g, cls: (bi, 0, 0)),
                  pl.BlockSpec((None,) + bias.shape[1:], lambda bi, g, cls: (cls[g], 0, 0, 0))],
        out_specs=pl.BlockSpec((None, nq, w), lambda bi, g, cls: (bi, g, 0)),
    )
    return pl.pallas_call(
        functools.partial(_na_kernel, rows=rows),
        grid_spec=grid_spec,
        out_shape=jax.ShapeDtypeStruct((b, s, w), BF16),
        compiler_params=_cparams("arbitrary", "arbitrary"),
        name="na_attention",
    )(group_class, q, k, v, kc, vc, bias)


def _mlstm_kernel(q_ref, k_ref, v_ref, gc_ref, gr_ref, kc_ref, vc_ref, gcc_ref, gcr_ref,
                  o_ref, c_scr, n_scr, m_scr, *, reverse):
    step = pl.program_id(1)
    li_base = 2 * ML_HEADS if reverse else 0
    lf_base = li_base + ML_HEADS

    def tri(length):
        t = lax.broadcasted_iota(jnp.int32, (length, length), 0)
        s = lax.broadcasted_iota(jnp.int32, (length, length), 1)
        valid = (s >= t) if reverse else (s <= t)
        valid_t = (t >= s) if reverse else (t <= s)
        return valid, valid_t

    def cumulative(gc, gr, valid, valid_t):
        b_col = jnp.dot(valid.astype(F32), gc, preferred_element_type=F32, precision=HIGHEST)
        b_row = jnp.dot(gr, valid_t.astype(F32), preferred_element_type=F32, precision=HIGHEST)
        return b_col, b_row

    def update_state(h, k, v, gc, gr, b_col, b_row):
        li_c = gc[:, li_base + h:li_base + h + 1]
        li_r = gr[li_base + h:li_base + h + 1, :]
        lf_r = gr[lf_base + h:lf_base + h + 1, :]
        bc = b_col[:, lf_base + h:lf_base + h + 1]
        br = b_row[lf_base + h:lf_base + h + 1, :]
        total = jnp.sum(lf_r, axis=-1, keepdims=True)
        m_prev = m_scr[h]
        m_new = jnp.maximum(total + m_prev, jnp.max(total - br + li_r, axis=-1, keepdims=True))
        w_src = jnp.exp(total - bc + li_c - m_new)
        w_carry = jnp.exp(total + m_prev - m_new)
        kh = k[:, h * ML_QK_DIM:(h + 1) * ML_QK_DIM].astype(F32) * w_src
        vh = v[:, h * ML_V_DIM:(h + 1) * ML_V_DIM]
        c_scr[h] = w_carry * c_scr[h] + lax.dot_general(kh.astype(BF16), vh, TN_DIMS,
                                                        preferred_element_type=F32)
        n_scr[h] = w_carry * n_scr[h] + jnp.sum(kh, axis=0, keepdims=True)
        m_scr[h] = m_new

    @pl.when(step == 0)
    def _():
        c_scr[...] = jnp.zeros_like(c_scr)
        n_scr[...] = jnp.zeros_like(n_scr)
        m_scr[...] = jnp.zeros_like(m_scr)
        k = kc_ref[...]
        v = vc_ref[...]
        gc = gcc_ref[...]
        gr = gcr_ref[...]
        valid, valid_t = tri(k.shape[0])
        b_col, b_row = cumulative(gc, gr, valid, valid_t)
        for h in range(ML_HEADS):
            update_state(h, k, v, gc, gr, b_col, b_row)

    @pl.when(step > 0)
    def _():
        q = q_ref[...]
        k = k_ref[...]
        v = v_ref[...]
        gc = gc_ref[...]
        gr = gr_ref[...]
        valid, valid_t = tri(k.shape[0])
        b_col, b_row = cumulative(gc, gr, valid, valid_t)
        outs = []
        for h in range(ML_HEADS):
            li_r = gr[li_base + h:li_base + h + 1, :]
            bc = b_col[:, lf_base + h:lf_base + h + 1]
            br = b_row[lf_base + h:lf_base + h + 1, :]
            m_prev = m_scr[h]
            qh = q[:, h * ML_QK_DIM:(h + 1) * ML_QK_DIM]
            kh = k[:, h * ML_QK_DIM:(h + 1) * ML_QK_DIM]
            vh = v[:, h * ML_V_DIM:(h + 1) * ML_V_DIM]
            d_mat = jnp.where(valid, bc - br + li_r, NEG_INF)
            m_inter = bc + m_prev
            m_t = jnp.maximum(m_inter, jnp.max(d_mat, axis=-1, keepdims=True))
            w_intra = jnp.exp(d_mat - m_t)
            w_inter = jnp.exp(m_inter - m_t)
            s = lax.dot_general(qh, kh, NT_DIMS, preferred_element_type=F32) * w_intra
            num = (w_inter * jnp.dot(qh, c_scr[h].astype(BF16), preferred_element_type=F32)
                   + jnp.dot(s.astype(BF16), vh, preferred_element_type=F32))
            den = (w_inter * jnp.sum(qh.astype(F32) * n_scr[h], axis=-1, keepdims=True)
                   + jnp.sum(s, axis=-1, keepdims=True))
            outs.append(num / jnp.maximum(jnp.abs(den), jnp.exp(-m_t)))
        o_ref[...] = jnp.concatenate(outs, axis=-1).astype(BF16)
        for h in range(ML_HEADS):
            update_state(h, k, v, gc, gr, b_col, b_row)


def _mlstm_direction(q, k, v, gc, gr, kc, vc, gcc, gcr, reverse):
    b, s, _ = q.shape
    n_ctx = kc.shape[1]
    length = ML_CHUNK
    n_chunks = s // length

    def chunk(step):
        c = jnp.maximum(step - 1, 0)
        return (n_chunks - 1 - c) if reverse else c

    return pl.pallas_call(
        functools.partial(_mlstm_kernel, reverse=reverse),
        grid=(b, n_chunks + 1),
        in_specs=[pl.BlockSpec((None, length, ML_QK_WIDTH), lambda bi, st: (bi, chunk(st), 0)),
                  pl.BlockSpec((None, length, ML_QK_WIDTH), lambda bi, st: (bi, chunk(st), 0)),
                  pl.BlockSpec((None, length, ML_WIDTH), lambda bi, st: (bi, chunk(st), 0)),
                  pl.BlockSpec((None, length, N_GATES), lambda bi, st: (bi, chunk(st), 0)),
                  pl.BlockSpec((N_GATES, length), lambda bi, st: (0, bi * n_chunks + chunk(st))),
                  pl.BlockSpec((None, n_ctx, ML_QK_WIDTH), lambda bi, st: (bi, 0, 0)),
                  pl.BlockSpec((None, n_ctx, ML_WIDTH), lambda bi, st: (bi, 0, 0)),
                  pl.BlockSpec((None, n_ctx, N_GATES), lambda bi, st: (bi, 0, 0)),
                  pl.BlockSpec((N_GATES, n_ctx), lambda bi, st: (0, bi))],
        out_specs=pl.BlockSpec((None, length, ML_WIDTH), lambda bi, st: (bi, chunk(st), 0)),
        out_shape=jax.ShapeDtypeStruct((b, s, ML_WIDTH), BF16),
        scratch_shapes=[pltpu.VMEM((ML_HEADS, ML_QK_DIM, ML_V_DIM), F32),
                        pltpu.VMEM((ML_HEADS, 1, ML_QK_DIM), F32),
                        pltpu.VMEM((ML_HEADS, 1, 1), F32)],
        compiler_params=_cparams("arbitrary", "arbitrary"),
        name="mlstm_bwd" if reverse else "mlstm_fwd",
    )(q, k, v, gc, gr, kc, vc, gcc, gcr)


def _merge_kernel(x_ref, ona_ref, hf_ref, hb_ref, opre_ref, gna_ref, gml_ref, gtm_ref, scf_ref, shf_ref,
                  ghead_ref, wbna_ref, wbml_ref, wout_ref, gpost_ref, gpre_ref, wr_ref, br_ref,
                  x1_ref, h2_ref, tope_ref, topw_ref, rank_ref, cnt_ref):
    step = pl.program_id(0)
    tm = x_ref.shape[0]

    @pl.when(step == 0)
    def _():
        cnt_ref[...] = jnp.zeros_like(cnt_ref)

    hsum = hf_ref[...].astype(F32) + hb_ref[...].astype(F32)
    heads = [_rms(hsum[:, h * ML_V_DIM:(h + 1) * ML_V_DIM]) for h in range(ML_HEADS)]
    hn = jnp.concatenate(heads, axis=-1) * ghead_ref[...]
    o_ml = jax.nn.sigmoid(opre_ref[...].astype(F32)) * hn
    merged = (jax.nn.sigmoid(gna_ref[...].astype(F32))
              * jnp.dot(ona_ref[...], wbna_ref[...], preferred_element_type=F32)
              + jax.nn.sigmoid(gml_ref[...].astype(F32))
              * jnp.dot(o_ml.astype(BF16), wbml_ref[...], preferred_element_type=F32))
    mixed = jnp.dot(merged.astype(BF16), wout_ref[...], preferred_element_type=F32)
    x1 = x_ref[...] + gtm_ref[...] * (_rms(mixed) * gpost_ref[...])
    x1_ref[...] = x1
    h2 = _rms(x1) * gpre_ref[...] * (1.0 + scf_ref[...]) + shf_ref[...]
    h2_ref[...] = h2
    logits = jnp.dot(h2, wr_ref[...], preferred_element_type=F32, precision=HIGHEST) + br_ref[...]

    lane = lax.broadcasted_iota(jnp.int32, logits.shape, 1)
    onehots, top_e, top_l = [], [], []
    for _ in range(TOP_K):
        best = jnp.max(logits, axis=-1, keepdims=True)
        e = jnp.min(jnp.where(logits == best, lane, N_EXPERTS), axis=-1, keepdims=True)
        hit = lane == e
        onehots.append(hit)
        top_e.append(e)
        top_l.append(best)
        logits = jnp.where(hit, -jnp.inf, logits)
    exps = [jnp.exp(l - top_l[0]) for l in top_l]
    total = exps[0] + exps[1] + exps[2] + exps[3]

    counts = (onehots[0].astype(F32) + onehots[1].astype(F32)
              + onehots[2].astype(F32) + onehots[3].astype(F32))
    t = lax.broadcasted_iota(jnp.int32, (tm, tm), 0)
    s = lax.broadcasted_iota(jnp.int32, (tm, tm), 1)
    before = jnp.dot((s < t).astype(BF16), counts.astype(BF16), preferred_element_type=F32) + cnt_ref[...]
    out_lane = lax.broadcasted_iota(jnp.int32, (tm, LANES), 1)
    e_out = jnp.zeros((tm, LANES), jnp.int32)
    w_out = jnp.zeros((tm, LANES), F32)
    r_out = jnp.zeros((tm, LANES), jnp.int32)
    for j in range(TOP_K):
        rank = jnp.sum(jnp.where(onehots[j], before, 0.0), axis=-1, keepdims=True).astype(jnp.int32)
        e_out = jnp.where(out_lane == j, top_e[j], e_out)
        w_out = jnp.where(out_lane == j, exps[j] / total, w_out)
        r_out = jnp.where(out_lane == j, rank, r_out)
    tope_ref[...] = e_out
    topw_ref[...] = w_out
    rank_ref[...] = r_out
    cnt_ref[...] += jnp.sum(counts, axis=0, keepdims=True)


def _merge_and_route(x2, o_na, h_f, h_b, o_pre, g_na, g_ml, mod4, seq, g_head, wbna, wbml, wout,
                     g_post, g_pre, w_router, b_router):
    n, d = x2.shape
    tm = MERGE_ROWS
    per_seq = seq // tm

    def rows(width):
        return pl.BlockSpec((tm, width), lambda i: (i, 0))

    def mod_spec(j):
        return pl.BlockSpec((None, None, 1, d), lambda i, _j=j: (i // per_seq, _j, 0, 0))

    return pl.pallas_call(
        _merge_kernel,
        grid=(n // tm,),
        in_specs=[rows(d), rows(NA_WIDTH), rows(ML_WIDTH), rows(ML_WIDTH), rows(ML_WIDTH), rows(d), rows(d),
                  mod_spec(2), mod_spec(4), mod_spec(3),
                  _const_spec((1, ML_WIDTH)), _const_spec(wbna.shape), _const_spec(wbml.shape),
                  _const_spec(wout.shape), _const_spec((1, d)), _const_spec((1, d)),
                  _const_spec(w_router.shape), _const_spec((1, N_EXPERTS))],
        out_specs=[rows(d), rows(d), rows(LANES), rows(LANES), rows(LANES),
                   pl.BlockSpec((1, N_EXPERTS), lambda i: (0, 0))],
        out_shape=[jax.ShapeDtypeStruct((n, d), F32), jax.ShapeDtypeStruct((n, d), F32),
                   jax.ShapeDtypeStruct((n, LANES), jnp.int32), jax.ShapeDtypeStruct((n, LANES), F32),
                   jax.ShapeDtypeStruct((n, LANES), jnp.int32),
                   jax.ShapeDtypeStruct((1, N_EXPERTS), F32)],
        compiler_params=_cparams("arbitrary"),
        name="merge_route",
    )(x2, o_na, h_f, h_b, o_pre, g_na, g_ml, mod4, mod4, mod4,
      g_head.reshape(1, ML_WIDTH), wbna, wbml, wout, g_post.reshape(1, d), g_pre.reshape(1, d),
      w_router, b_router.reshape(1, N_EXPERTS))


def _dispatch_kernel(pad_end_ref, padded_ref, dest_ref, h_ref, xs_ref, zero_scr, sem):
    tm = h_ref.shape[0]
    blk = zero_scr.shape[0]

    @pl.when(pl.program_id(0) == 0)
    def _():
        zero_scr[...] = jnp.zeros_like(zero_scr)

        def zero_copy(e):
            first = pl.multiple_of(pad_end_ref[e] - blk, blk)
            return pltpu.make_async_copy(zero_scr, xs_ref.at[pl.ds(first, blk)], sem)

        for e in range(N_EXPERTS):
            @pl.when(padded_ref[e] > 0)
            def _():
                zero_copy(e).start()
        for e in range(N_EXPERTS):
            @pl.when(padded_ref[e] > 0)
            def _():
                zero_copy(e).wait()

        def tail_copy(b):
            return pltpu.make_async_copy(zero_scr, xs_ref.at[pl.ds(pl.multiple_of(b * blk, blk), blk)], sem)

        def tail_start(b, carry):
            tail_copy(b).start()
            return carry

        def tail_wait(b, carry):
            tail_copy(b).wait()
            return carry

        first_unused = pad_end_ref[N_EXPERTS - 1] // blk
        lax.fori_loop(first_unused, xs_ref.shape[0] // blk, tail_start, 0)
        lax.fori_loop(first_unused, xs_ref.shape[0] // blk, tail_wait, 0)

    tok_per_row = LANES // TOP_K

    def row_copy(r, c):
        return pltpu.make_async_copy(h_ref.at[pl.ds(r * tok_per_row + c // TOP_K, 1)],
                                     xs_ref.at[pl.ds(dest_ref[r, c], 1)], sem)

    def start(r, carry):
        for c in range(LANES):
            row_copy(r, c).start()
        return carry

    def wait(r, carry):
        for c in range(LANES):
            row_copy(r, c).wait()
        return carry

    lax.fori_loop(0, tm // tok_per_row, start, 0)
    lax.fori_loop(0, tm // tok_per_row, wait, 0)


def _dispatch(pad_end, padded, dest2, h2, n_rows):
    n = h2.shape[0]
    tm = MOVE_ROWS
    idx_rows = tm * TOP_K // LANES
    row_tile = h2.shape[1:]
    grid_spec = pltpu.PrefetchScalarGridSpec(
        num_scalar_prefetch=2,
        grid=(n // tm,),
        in_specs=[pl.BlockSpec((idx_rows, LANES), lambda i, pe, pd: (i, 0), memory_space=pltpu.SMEM),
                  pl.BlockSpec((tm,) + row_tile, lambda i, pe, pd: (i, 0))],
        out_specs=pl.BlockSpec(memory_space=pl.ANY),
        scratch_shapes=[pltpu.VMEM((EXPERT_ROWS,) + row_tile, h2.dtype), pltpu.SemaphoreType.DMA(())],
    )
    return pl.pallas_call(
        _dispatch_kernel,
        grid_spec=grid_spec,
        out_shape=jax.ShapeDtypeStruct((n_rows,) + row_tile, h2.dtype),
        compiler_params=_cparams("arbitrary"),
        name="moe_dispatch",
    )(pad_end, padded, dest2, h2)


def _expert_kernel(blk_e_ref, n_used_ref, x_ref, wg_ref, bg_ref, wl_ref, bl_ref, wd_ref, bd_ref, y_ref,
                   wg_s, wl_s, wd_s):
    i = pl.program_id(0)
    prev = blk_e_ref[jnp.maximum(i - 1, 0)]
    changed = jnp.logical_or(i == 0, blk_e_ref[i] != prev)
    used = i < n_used_ref[0]

    @pl.when(jnp.logical_and(used, changed))
    def _():
        wg_s[...] = wg_ref[...].astype(BF16)
        wl_s[...] = wl_ref[...].astype(BF16)
        wd_s[...] = wd_ref[...].astype(BF16)

    @pl.when(used)
    def _():
        xb = x_ref[...].astype(BF16)
        g = jnp.dot(xb, wg_s[...], preferred_element_type=F32) + bg_ref[...]
        l = jnp.dot(xb, wl_s[...], preferred_element_type=F32) + bl_ref[...]
        g = jnp.minimum(g, SWIGLU_LIMIT)
        l = jnp.clip(l, -SWIGLU_LIMIT, SWIGLU_LIMIT)
        a = g * jax.nn.sigmoid(SWIGLU_ALPHA * g) * (l + 1.0)
        y_ref[...] = jnp.dot(a.astype(BF16), wd_s[...], preferred_element_type=F32) + bd_ref[...]

    @pl.when(jnp.logical_not(used))
    def _():
        y_ref[...] = jnp.zeros_like(y_ref)


def _experts(blk_e, n_used, xs, w_gate, b_gate, w_lin, b_lin, w_down, b_down):
    n_rows = xs.shape[0]
    row_tile = xs.shape[1:]
    e, d, f = w_gate.shape
    tm = EXPERT_ROWS

    def w_spec(shape):
        return pl.BlockSpec((None,) + shape, lambda i, be, nu: (be[i], 0, 0))

    grid_spec = pltpu.PrefetchScalarGridSpec(
        num_scalar_prefetch=2,
        grid=(n_rows // tm,),
        in_specs=[pl.BlockSpec((tm,) + row_tile, lambda i, be, nu: (i, 0)),
                  w_spec((d, f)), w_spec((1, f)), w_spec((d, f)), w_spec((1, f)),
                  w_spec((f, d)), w_spec((1, d))],
        out_specs=pl.BlockSpec((tm,) + row_tile, lambda i, be, nu: (i, 0)),
        scratch_shapes=[pltpu.VMEM((d, f), BF16), pltpu.VMEM((d, f), BF16), pltpu.VMEM((f, d), BF16)],
    )
    return pl.pallas_call(
        _expert_kernel,
        grid_spec=grid_spec,
        out_shape=jax.ShapeDtypeStruct((n_rows,) + row_tile, F32),
        compiler_params=_cparams("arbitrary"),
        name="moe_experts",
    )(blk_e, n_used, xs, w_gate, b_gate.reshape(e, 1, f), w_lin, b_lin.reshape(e, 1, f),
      w_down, b_down.reshape(e, 1, d))


def _combine_kernel(dest_ref, x1_ref, w_ref, gtf_ref, gpost_ref, y_ref, o_ref, buf, sem):
    tm = x1_ref.shape[0]

    tok_per_row = LANES // TOP_K

    def row_copy(r, c):
        return pltpu.make_async_copy(y_ref.at[pl.ds(dest_ref[r, c], 1)],
                                     buf.at[c % TOP_K, pl.ds(r * tok_per_row + c // TOP_K, 1)], sem)

    def start(r, carry):
        for c in range(LANES):
            row_copy(r, c).start()
        return carry

    def wait(r, carry):
        for c in range(LANES):
            row_copy(r, c).wait()
        return carry

    lax.fori_loop(0, tm // tok_per_row, start, 0)
    lax.fori_loop(0, tm // tok_per_row, wait, 0)
    w = w_ref[...]
    ffn = (buf[0] * w[:, 0:1] + buf[1] * w[:, 1:2]) + (buf[2] * w[:, 2:3] + buf[3] * w[:, 3:4])
    o_ref[...] = x1_ref[...] + gtf_ref[...] * (_rms(ffn) * gpost_ref[...])


def _combine(dest2, x1, top_w, mod4, seq, g_post, y):
    n, d = x1.shape
    tm = MOVE_ROWS
    per_seq = seq // tm
    idx_rows = tm * TOP_K // LANES
    return pl.pallas_call(
        _combine_kernel,
        grid=(n // tm,),
        in_specs=[pl.BlockSpec((idx_rows, LANES), lambda i: (i, 0), memory_space=pltpu.SMEM),
                  pl.BlockSpec((tm, d), lambda i: (i, 0)),
                  pl.BlockSpec((tm, LANES), lambda i: (i, 0)),
                  pl.BlockSpec((None, None, 1, d), lambda i: (i // per_seq, 5, 0, 0)),
                  _const_spec((1, d)),
                  pl.BlockSpec(memory_space=pl.ANY)],
        out_specs=pl.BlockSpec((tm, d), lambda i: (i, 0)),
        out_shape=jax.ShapeDtypeStruct((n, d), F32),
        scratch_shapes=[pltpu.VMEM((TOP_K, tm) + y.shape[1:], F32), pltpu.SemaphoreType.DMA(())],
        compiler_params=_cparams("arbitrary"),
        name="moe_combine",
    )(dest2, x1, top_w, mod4, g_post.reshape(1, d), y)


def _rope_partner(w):
    n_freq = ML_QK_DIM // 4
    d = w.shape[0]
    w4 = w.reshape(d, -1, 2, n_freq)
    return jnp.stack([-w4[:, :, 1], w4[:, :, 0]], axis=2).reshape(w.shape)


def _rope_tables(seq):
    n_freq = ML_QK_DIM // 4
    t = jnp.arange(seq)
    row = (t // GRID_W).astype(F32)
    col = (t % GRID_W).astype(F32)
    inv_freq = ROPE_BASE ** (-jnp.arange(n_freq, dtype=F32) / n_freq)
    ang = jnp.concatenate([row[:, None] * inv_freq] * 2 + [col[:, None] * inv_freq] * 2, axis=-1)
    cos = jnp.tile(jnp.cos(ang), (1, ML_HEADS))
    sin = jnp.tile(jnp.sin(ang), (1, ML_HEADS))
    return cos, sin


def _layer(x, ctx, mod4, g_mix_pre, g_mix_post, g_ffn_pre, g_ffn_post, w_in, b_gates, rpb, g_head,
           w_branch_na, w_branch_ml, w_out, w_router, b_router, w_gate, b_gate, w_lin, b_lin, w_down, b_down):
    b, s, d = x.shape
    n = b * s
    n_ctx = ctx.shape[1]
    x2 = x.reshape(n, d)

    ctx_cols = (NA_WIDTH, NA_WIDTH, ML_QK_WIDTH, ML_WIDTH, N_GATES)
    lat_cols = (NA_WIDTH, ML_QK_WIDTH, ML_WIDTH, d, d)
    bounds = np.cumsum(ctx_cols + lat_cols)[:-1].tolist()
    (w_nak, w_nav, w_mk, w_mv, w_g, w_naq, w_mq, w_mo, w_gna, w_gml) = jnp.split(w_in, bounds, axis=-1)
    w_naq = w_naq * NA_HEAD_DIM ** -0.5
    w_mk = w_mk * ML_QK_DIM ** -0.5
    bf = lambda a: a.astype(BF16)
    lat_w = [bf(w_naq), bf(w_nak), bf(w_nav), bf(w_mq), bf(_rope_partner(w_mq)), bf(w_mk),
             bf(_rope_partner(w_mk)), bf(w_mv), bf(w_mo), bf(w_gna), bf(w_gml), bf(w_g), bf(w_g.T)]
    ctx_w = [bf(w_nak), bf(w_nav), bf(w_mk), bf(w_mv), bf(w_g), bf(w_g.T)]
    bg_col = b_gates.reshape(1, N_GATES).astype(F32)
    bg_row = b_gates.reshape(N_GATES, 1).astype(F32)
    cos, sin = _rope_tables(s)
    per_seq = s // PROJ_ROWS

    (na_q, na_k, na_v, ml_q, ml_k, ml_v, ml_o, gate_na, gate_ml, g_col, g_row) = _project(
        x2, mod4, lambda i: i // per_seq, g_mix_pre, bg_col, bg_row, lat_w, (cos, sin, per_seq), True)
    (na_kc, na_vc, ml_kc, ml_vc, gc_col, gc_row) = _project(
        ctx.reshape(b * n_ctx, d), mod4, lambda i: b, g_mix_pre, bg_col, bg_row, ctx_w, None, False)

    def seq3(a, length):
        return a.reshape(b, length, a.shape[-1])

    o_na = _neighbourhood_attention(seq3(na_q, s), seq3(na_k, s), seq3(na_v, s),
                                    seq3(na_kc, n_ctx), seq3(na_vc, n_ctx), rpb)
    ml_args = (seq3(ml_q, s), seq3(ml_k, s), seq3(ml_v, s), seq3(g_col, s), g_row,
               seq3(ml_kc, n_ctx), seq3(ml_vc, n_ctx), seq3(gc_col, n_ctx), gc_row)
    h_f = _mlstm_direction(*ml_args, reverse=False)
    h_b = _mlstm_direction(*ml_args, reverse=True)

    x1, h2, top_e, top_w, rank, counts = _merge_and_route(
        x2, o_na.reshape(n, NA_WIDTH), h_f.reshape(n, ML_WIDTH), h_b.reshape(n, ML_WIDTH), ml_o,
        gate_na, gate_ml, mod4, s, g_head, bf(w_branch_na), bf(w_branch_ml), bf(w_out),
        g_mix_post, g_ffn_pre, w_router, b_router)

    tm = EXPERT_ROWS
    counts = counts.reshape(N_EXPERTS).astype(jnp.int32)
    padded = (counts + tm - 1) // tm * tm
    pad_end = jnp.cumsum(padded)
    pad_start = pad_end - padded
    n_rows = n * TOP_K + N_EXPERTS * tm
    n_blocks = n_rows // tm
    e_sel = top_e[:, :TOP_K, None] == jnp.arange(N_EXPERTS, dtype=jnp.int32)
    dest = jnp.sum(jnp.where(e_sel, pad_start, 0), axis=-1) + rank[:, :TOP_K]
    dest2 = dest.reshape(n * TOP_K // LANES, LANES).astype(jnp.int32)
    blk_start = jnp.arange(n_blocks, dtype=jnp.int32) * tm
    blk_e = jnp.minimum(jnp.sum(blk_start[:, None] >= pad_end[None, :], axis=1), N_EXPERTS - 1).astype(jnp.int32)
    n_used = (pad_end[-1:] // tm).astype(jnp.int32)

    xs = _dispatch(pad_end.astype(jnp.int32), padded.astype(jnp.int32), dest2, h2, n_rows)
    y = _experts(blk_e, n_used, xs, w_gate, b_gate, w_lin, b_lin, w_down, b_down)
    out = _combine(dest2, x1, top_w, mod4, s, g_ffn_post, y)
    return out.reshape(b, s, d)


def kernel(x, c, ctx, c_ctx, w_ada, b_ada, g_mix_pre, g_mix_post, g_ffn_pre, g_ffn_post, w_in, b_mlstm_gates,
           rpb, g_mlstm_head, w_branch_na, w_branch_ml, w_out, w_router, b_router, w_gate, b_gate, w_lin,
           b_lin, w_down, b_down):
    b, s, d = x.shape
    depth = w_ada.shape[0]
    pad = (-(b + 1)) % 8
    c_all = jnp.concatenate([c, c_ctx[None, :], jnp.zeros((pad, d), c.dtype)], axis=0)
    for layer in range(depth):
        mod = _ada(c_all, w_ada[layer], b_ada[layer])
        mod4 = mod.reshape(mod.shape[0], 6, 1, d)
        x = _layer(x, ctx, mod4, g_mix_pre[layer], g_mix_post[layer], g_ffn_pre[layer], g_ffn_post[layer],
                   w_in[layer], b_mlstm_gates[layer], rpb[layer], g_mlstm_head[layer], w_branch_na[layer],
                   w_branch_ml[layer], w_out[layer], w_router[layer], b_router[layer], w_gate[layer],
                   b_gate[layer], w_lin[layer], b_lin[layer], w_down[layer], b_down[layer])
    return x
```

```python
import functools

import numpy as np
import jax
import jax.numpy as jnp
from jax import lax
from jax.experimental import pallas as pl
from jax.experimental.pallas import tpu as pltpu

F32 = jnp.float32
BF16 = jnp.bfloat16
HIGHEST = lax.Precision.HIGHEST

GRID_W = 64
NA_HEADS = 8
NA_HEAD_DIM = 64
NA_WIDTH = NA_HEADS * NA_HEAD_DIM
WIN_H = 8
WIN_W = 16
ML_HEADS = 4
ML_QK_DIM = 64
ML_V_DIM = 128
ML_QK_WIDTH = ML_HEADS * ML_QK_DIM
ML_WIDTH = ML_HEADS * ML_V_DIM
N_GATES = 4 * ML_HEADS
GATE_SOFTCAP = 15.0
ROPE_BASE = 10000.0
N_EXPERTS = 32
TOP_K = 4
SWIGLU_ALPHA = 1.702
SWIGLU_LIMIT = 7.0
NORM_EPS = 1e-6
NEG_INF = -1e30
LOG2_E = 1.4426950408889634

LANES = 128
NA_QROWS = 4
NA_WROWS = 12
ML_CHUNK = 256
PROJ_ROWS = 512
MERGE_ROWS = 256
EXPERT_ROWS = 512
MOVE_ROWS = 256
VMEM_LIMIT = 56 * 1024 * 1024

NT_DIMS = (((1,), (1,)), ((), ()))
TN_DIMS = (((0,), (0,)), ((), ()))


def _cparams(*sem):
    return pltpu.CompilerParams(dimension_semantics=sem, vmem_limit_bytes=VMEM_LIMIT)


def _rms(x):
    return x * lax.rsqrt(jnp.mean(x * x, axis=-1, keepdims=True) + NORM_EPS)


def _ada_kernel(c_ref, w_ref, b_ref, o_ref):
    c = c_ref[...]
    s = c * jax.nn.sigmoid(c)
    o_ref[...] = jnp.dot(s, w_ref[...], preferred_element_type=F32, precision=HIGHEST) + b_ref[...]


def _ada(c_all, w_ada, b_ada):
    rows, d = c_all.shape
    n_out = w_ada.shape[1]
    tn = 1536
    return pl.pallas_call(
        _ada_kernel,
        grid=(n_out // tn,),
        in_specs=[pl.BlockSpec((rows, d), lambda j: (0, 0)),
                  pl.BlockSpec((d, tn), lambda j: (0, j)),
                  pl.BlockSpec((1, tn), lambda j: (0, j))],
        out_specs=pl.BlockSpec((rows, tn), lambda j: (0, j)),
        out_shape=jax.ShapeDtypeStruct((rows, n_out), F32),
        compiler_params=_cparams("arbitrary"),
        name="ada_mod",
    )(c_all, w_ada, b_ada.reshape(1, n_out))


def _gate_logs(g, is_forget):
    g = GATE_SOFTCAP * jnp.tanh(g / GATE_SOFTCAP)
    log_sig = jnp.minimum(g, 0.0) - jnp.log(1.0 + jnp.exp(-jnp.abs(g)))
    return jnp.where(is_forget, log_sig, g)


def _proj_kernel(*refs, latent):
    if latent:
        (x_ref, g_ref, sc_ref, sh_ref, cos_ref, sin_ref, bgc_ref, bgr_ref,
         w_naq, w_nak, w_nav, w_mq, w_mqp, w_mk, w_mkp, w_mv, w_mo, w_gna, w_gml, w_gc, w_gr,
         o_naq, o_nak, o_nav, o_mq, o_mk, o_mv, o_mo, o_gna, o_gml, o_gc, o_gr) = refs
    else:
        (x_ref, g_ref, sc_ref, sh_ref, bgc_ref, bgr_ref,
         w_nak, w_nav, w_mk, w_mv, w_gc, w_gr,
         o_nak, o_nav, o_mk, o_mv, o_gc, o_gr) = refs
    x = x_ref[...]
    h = _rms(x) * g_ref[...]
    h = h * (1.0 + sc_ref[...]) + sh_ref[...]
    hb = h.astype(BF16)

    def mm(w_ref):
        return jnp.dot(hb, w_ref[...], preferred_element_type=F32)

    o_nak[...] = mm(w_nak).astype(BF16)
    o_nav[...] = mm(w_nav).astype(BF16)
    o_mv[...] = mm(w_mv).astype(BF16)
    if latent:
        cos = cos_ref[...]
        sin = sin_ref[...]
        o_naq[...] = mm(w_naq).astype(BF16)
        o_mq[...] = (mm(w_mq) * cos + mm(w_mqp) * sin).astype(BF16)
        o_mk[...] = (mm(w_mk) * cos + mm(w_mkp) * sin).astype(BF16)
        o_mo[...] = mm(w_mo).astype(BF16)
        o_gna[...] = mm(w_gna).astype(BF16)
        o_gml[...] = mm(w_gml).astype(BF16)
    else:
        o_mk[...] = mm(w_mk).astype(BF16)
    gc = mm(w_gc) + bgc_ref[...]
    col_id = lax.broadcasted_iota(jnp.int32, gc.shape, 1)
    o_gc[...] = _gate_logs(gc, (col_id // ML_HEADS) % 2 == 1)
    gr = lax.dot_general(w_gr[...], hb, NT_DIMS, preferred_element_type=F32) + bgr_ref[...]
    row_id = lax.broadcasted_iota(jnp.int32, gr.shape, 0)
    o_gr[...] = _gate_logs(gr, (row_id // ML_HEADS) % 2 == 1)


def _const_spec(shape):
    nd = len(shape)
    return pl.BlockSpec(shape, lambda i, _nd=nd: (0,) * _nd)


def _project(x2, mod4, mod_row_fn, g_pre, bg_col, bg_row, weights, tables, latent):
    n, d = x2.shape
    tm = PROJ_ROWS
    grid = (n // tm,)

    def mod_spec(j):
        return pl.BlockSpec((None, None, 1, d), lambda i, _j=j: (mod_row_fn(i), _j, 0, 0))

    in_specs = [pl.BlockSpec((tm, d), lambda i: (i, 0)), _const_spec((1, d)), mod_spec(1), mod_spec(0)]
    args = [x2, g_pre.reshape(1, d), mod4, mod4]
    if latent:
        cos, sin, tiles_per_seq = tables
        in_specs += [pl.BlockSpec((tm, ML_QK_WIDTH), lambda i: (i % tiles_per_seq, 0))] * 2
        args += [cos, sin]
    in_specs += [_const_spec(bg_col.shape), _const_spec(bg_row.shape)]
    args += [bg_col, bg_row]
    for w in weights:
        in_specs.append(_const_spec(w.shape))
        args.append(w)

    def out(width, dtype=BF16):
        return (jax.ShapeDtypeStruct((n, width), dtype), pl.BlockSpec((tm, width), lambda i: (i, 0)))

    if latent:
        outs = [out(NA_WIDTH), out(NA_WIDTH), out(NA_WIDTH), out(ML_QK_WIDTH), out(ML_QK_WIDTH),
                out(ML_WIDTH), out(ML_WIDTH), out(d), out(d), out(N_GATES, F32)]
    else:
        outs = [out(NA_WIDTH), out(NA_WIDTH), out(ML_QK_WIDTH), out(ML_WIDTH), out(N_GATES, F32)]
    outs.append((jax.ShapeDtypeStruct((N_GATES, n), F32), pl.BlockSpec((N_GATES, tm), lambda i: (0, i))))
    return pl.pallas_call(
        functools.partial(_proj_kernel, latent=latent),
        grid=grid,
        in_specs=in_specs,
        out_specs=[o[1] for o in outs],
        out_shape=[o[0] for o in outs],
        compiler_params=_cparams("arbitrary"),
        name="in_proj_latent" if latent else "in_proj_ctx",
    )(*args)


def _na_window_start(r0, rows):
    return jnp.clip(r0 - WIN_H // 2, 0, rows - NA_WROWS)


def _na_classes(rows):
    keys, group_class = [], []
    for r0 in range(0, rows, NA_QROWS):
        start = min(max(r0 - WIN_H // 2, 0), rows - NA_WROWS)
        first = tuple(min(max(r0 + i - WIN_H // 2, 0), rows - WIN_H) - start for i in range(NA_QROWS))
        assert all(0 <= f and f + WIN_H <= NA_WROWS for f in first)
        key = (r0 - start, first)
        if key not in keys:
            keys.append(key)
        group_class.append(keys.index(key))
    return keys, np.asarray(group_class, np.int32)


def _na_bias_table(rpb, rows):
    keys, group_class = _na_classes(rows)
    qc = np.arange(GRID_W)[:, None]
    kc = np.arange(GRID_W)[None, :]
    cs = np.clip(qc - WIN_W // 2, 0, GRID_W - WIN_W)
    col_ok = (kc >= cs) & (kc < cs + WIN_W)
    dc = np.clip(kc - qc, -(WIN_W - 1), WIN_W - 1) + (WIN_W - 1)
    i = np.arange(NA_QROWS)[:, None]
    j = np.arange(NA_WROWS)[None, :]
    sel_r, row_ok = [], []
    for off, first in keys:
        f = np.asarray(first)[:, None]
        ok = (j >= f) & (j < f + WIN_H)
        dr = j - off - i + (WIN_H - 1)
        sel_r.append(((dr[:, :, None] == np.arange(2 * WIN_H - 1)) & ok[:, :, None]).astype(np.float32))
        row_ok.append(ok)
    sel_r = np.stack(sel_r)
    valid = np.stack(row_ok)[:, None, :, None, :, None] & col_ok[None, None, None, :, None, :]
    sel_c = (dc[:, :, None] == np.arange(2 * WIN_W - 1)).astype(np.float32)
    t = jnp.einsum('hrc,qkc->hrqk', rpb.astype(F32), sel_c, precision=HIGHEST)
    bias = jnp.einsum('hrqk,xijr->xhiqjk', t, sel_r, precision=HIGHEST)
    bias = jnp.where(valid, bias * LOG2_E, NEG_INF)
    bias = bias.reshape(len(keys), NA_HEADS, NA_QROWS * GRID_W, NA_WROWS * GRID_W)
    return bias.astype(BF16), jnp.asarray(group_class)


def _na_kernel(cls_ref, q_ref, k_ref, v_ref, kc_ref, vc_ref, bias_ref, o_ref, *, rows):
    del cls_ref
    r0 = pl.program_id(1) * NA_QROWS
    start = pl.multiple_of(_na_window_start(r0, rows) * GRID_W, GRID_W)
    n_win = NA_WROWS * GRID_W
    low = lax.broadcasted_iota(jnp.int32, (1, LANES), 1) < NA_HEAD_DIM
    for pair in range(NA_HEADS // 2):
        sl = slice(pair * LANES, (pair + 1) * LANES)
        q2 = q_ref[:, sl]
        k2 = k_ref[pl.ds(start, n_win), sl]
        v2 = v_ref[pl.ds(start, n_win), sl]
        kc2 = kc_ref[:, sl]
        vc2 = vc_ref[:, sl]
        halves = []
        for half in range(2):
            keep = low if half == 0 else jnp.logical_not(low)
            qh = jnp.where(keep, q2, jnp.zeros_like(q2))
            s_loc = (lax.dot_general(qh, k2, NT_DIMS, preferred_element_type=F32)
                     + bias_ref[2 * pair + half].astype(F32))
            s_ctx = lax.dot_general(qh, kc2, NT_DIMS, preferred_element_type=F32)
            m = jnp.maximum(jnp.max(s_loc, axis=-1, keepdims=True), jnp.max(s_ctx, axis=-1, keepdims=True))
            p_loc = jnp.exp2(s_loc - m)
            p_ctx = jnp.exp2(s_ctx - m)
            denom = jnp.sum(p_loc, axis=-1, keepdims=True) + jnp.sum(p_ctx, axis=-1, keepdims=True)
            o = (jnp.dot(p_loc.astype(BF16), v2, preferred_element_type=F32)
                 + jnp.dot(p_ctx.astype(BF16), vc2, preferred_element_type=F32))
            halves.append(o / denom)
        o_ref[:, sl] = jnp.where(low, halves[0], halves[1]).astype(BF16)


def _neighbourhood_attention(q, k, v, kc, vc, rpb):
    b, s, w = q.shape
    rows = s // GRID_W
    n_ctx = kc.shape[1]
    bias, group_class = _na_bias_table(rpb, rows)
    nq = NA_QROWS * GRID_W
    grid_spec = pltpu.PrefetchScalarGridSpec(
        num_scalar_prefetch=1,
        grid=(b, rows // NA_QROWS),
        in_specs=[pl.BlockSpec((None, nq, w), lambda bi, g, cls: (bi, g, 0)),
                  pl.BlockSpec((None, s, w), lambda bi, g, cls: (bi, 0, 0)),
                  pl.BlockSpec((None, s, w), lambda bi, g, cls: (bi, 0, 0)),
                  pl.BlockSpec((None, n_ctx, w), lambda bi, g, cls: (bi, 0, 0)),
                  pl.BlockSpec((None, n_ctx, w), lambda bi, g, cls: (bi, 0, 0)),
                  pl.BlockSpec((None,) + bias.shape[1:], lambda bi, g, cls: (cls[g], 0, 0, 0))],
        out_specs=pl.BlockSpec((None, nq, w), lambda bi, g, cls: (bi, g, 0)),
    )
    return pl.pallas_call(
        functools.partial(_na_kernel, rows=rows),
        grid_spec=grid_spec,
        out_shape=jax.ShapeDtypeStruct((b, s, w), BF16),
        compiler_params=_cparams("arbitrary", "arbitrary"),
        name="na_attention",
    )(group_class, q, k, v, kc, vc, bias)


def _mlstm_kernel(q_ref, k_ref, v_ref, gc_ref, gr_ref, kc_ref, vc_ref, gcc_ref, gcr_ref,
                  o_ref, c_scr, n_scr, m_scr, *, reverse):
    step = pl.program_id(1)
    li_base = 2 * ML_HEADS if reverse else 0
    lf_base = li_base + ML_HEADS

    def tri(length):
        t = lax.broadcasted_iota(jnp.int32, (length, length), 0)
        s = lax.broadcasted_iota(jnp.int32, (length, length), 1)
        valid = (s >= t) if reverse else (s <= t)
        valid_t = (t >= s) if reverse else (t <= s)
        return valid, valid_t

    def cumulative(gc, gr, valid, valid_t):
        b_col = jnp.dot(valid.astype(F32), gc, preferred_element_type=F32, precision=HIGHEST)
        b_row = jnp.dot(gr, valid_t.astype(F32), preferred_element_type=F32, precision=HIGHEST)
        return b_col, b_row

    def update_state(h, k, v, gc, gr, b_col, b_row):
        li_c = gc[:, li_base + h:li_base + h + 1]
        li_r = gr[li_base + h:li_base + h + 1, :]
        lf_r = gr[lf_base + h:lf_base + h + 1, :]
        bc = b_col[:, lf_base + h:lf_base + h + 1]
        br = b_row[lf_base + h:lf_base + h + 1, :]
        total = jnp.sum(lf_r, axis=-1, keepdims=True)
        m_prev = m_scr[h]
        m_new = jnp.maximum(total + m_prev, jnp.max(total - br + li_r, axis=-1, keepdims=True))
        w_src = jnp.exp(total - bc + li_c - m_new)
        w_carry = jnp.exp(total + m_prev - m_new)
        kh = k[:, h * ML_QK_DIM:(h + 1) * ML_QK_DIM].astype(F32) * w_src
        vh = v[:, h * ML_V_DIM:(h + 1) * ML_V_DIM]
        c_scr[h] = w_carry * c_scr[h] + lax.dot_general(kh.astype(BF16), vh, TN_DIMS,
                                                        preferred_element_type=F32)
        n_scr[h] = w_carry * n_scr[h] + jnp.sum(kh, axis=0, keepdims=True)
        m_scr[h] = m_new

    @pl.when(step == 0)
    def _():
        c_scr[...] = jnp.zeros_like(c_scr)
        n_scr[...] = jnp.zeros_like(n_scr)
        m_scr[...] = jnp.zeros_like(m_scr)
        k = kc_ref[...]
        v = vc_ref[...]
        gc = gcc_ref[...]
        gr = gcr_ref[...]
        valid, valid_t = tri(k.shape[0])
        b_col, b_row = cumulative(gc, gr, valid, valid_t)
        for h in range(ML_HEADS):
            update_state(h, k, v, gc, gr, b_col, b_row)

    @pl.when(step > 0)
    def _():
        q = q_ref[...]
        k = k_ref[...]
        v = v_ref[...]
        gc = gc_ref[...]
        gr = gr_ref[...]
        valid, valid_t = tri(k.shape[0])
        b_col, b_row = cumulative(gc, gr, valid, valid_t)
        outs = []
        for h in range(ML_HEADS):
            li_r = gr[li_base + h:li_base + h + 1, :]
            bc = b_col[:, lf_base + h:lf_base + h + 1]
            br = b_row[lf_base + h:lf_base + h + 1, :]
            m_prev = m_scr[h]
            qh = q[:, h * ML_QK_DIM:(h + 1) * ML_QK_DIM]
            kh = k[:, h * ML_QK_DIM:(h + 1) * ML_QK_DIM]
            vh = v[:, h * ML_V_DIM:(h + 1) * ML_V_DIM]
            d_mat = jnp.where(valid, bc - br + li_r, NEG_INF)
            m_inter = bc + m_prev
            m_t = jnp.maximum(m_inter, jnp.max(d_mat, axis=-1, keepdims=True))
            w_intra = jnp.exp(d_mat - m_t)
            w_inter = jnp.exp(m_inter - m_t)
            s = lax.dot_general(qh, kh, NT_DIMS, preferred_element_type=F32) * w_intra
            num = (w_inter * jnp.dot(qh, c_scr[h].astype(BF16), preferred_element_type=F32)
                   + jnp.dot(s.astype(BF16), vh, preferred_element_type=F32))
            den = (w_inter * jnp.sum(qh.astype(F32) * n_scr[h], axis=-1, keepdims=True)
                   + jnp.sum(s, axis=-1, keepdims=True))
            outs.append(num / jnp.maximum(jnp.abs(den), jnp.exp(-m_t)))
        o_ref[...] = jnp.concatenate(outs, axis=-1).astype(BF16)
        for h in range(ML_HEADS):
            update_state(h, k, v, gc, gr, b_col, b_row)


def _mlstm_direction(q, k, v, gc, gr, kc, vc, gcc, gcr, reverse):
    b, s, _ = q.shape
    n_ctx = kc.shape[1]
    length = ML_CHUNK
    n_chunks = s // length

    def chunk(step):
        c = jnp.maximum(step - 1, 0)
        return (n_chunks - 1 - c) if reverse else c

    return pl.pallas_call(
        functools.partial(_mlstm_kernel, reverse=reverse),
        grid=(b, n_chunks + 1),
        in_specs=[pl.BlockSpec((None, length, ML_QK_WIDTH), lambda bi, st: (bi, chunk(st), 0)),
                  pl.BlockSpec((None, length, ML_QK_WIDTH), lambda bi, st: (bi, chunk(st), 0)),
                  pl.BlockSpec((None, length, ML_WIDTH), lambda bi, st: (bi, chunk(st), 0)),
                  pl.BlockSpec((None, length, N_GATES), lambda bi, st: (bi, chunk(st), 0)),
                  pl.BlockSpec((N_GATES, length), lambda bi, st: (0, bi * n_chunks + chunk(st))),
                  pl.BlockSpec((None, n_ctx, ML_QK_WIDTH), lambda bi, st: (bi, 0, 0)),
                  pl.BlockSpec((None, n_ctx, ML_WIDTH), lambda bi, st: (bi, 0, 0)),
                  pl.BlockSpec((None, n_ctx, N_GATES), lambda bi, st: (bi, 0, 0)),
                  pl.BlockSpec((N_GATES, n_ctx), lambda bi, st: (0, bi))],
        out_specs=pl.BlockSpec((None, length, ML_WIDTH), lambda bi, st: (bi, chunk(st), 0)),
        out_shape=jax.ShapeDtypeStruct((b, s, ML_WIDTH), BF16),
        scratch_shapes=[pltpu.VMEM((ML_HEADS, ML_QK_DIM, ML_V_DIM), F32),
                        pltpu.VMEM((ML_HEADS, 1, ML_QK_DIM), F32),
                        pltpu.VMEM((ML_HEADS, 1, 1), F32)],
        compiler_params=_cparams("arbitrary", "arbitrary"),
        name="mlstm_bwd" if reverse else "mlstm_fwd",
    )(q, k, v, gc, gr, kc, vc, gcc, gcr)


def _merge_kernel(x_ref, ona_ref, hf_ref, hb_ref, opre_ref, gna_ref, gml_ref, gtm_ref, scf_ref, shf_ref,
                  ghead_ref, wbna_ref, wbml_ref, wout_ref, gpost_ref, gpre_ref, wrh_ref, wrl_ref, br_ref,
                  x1_ref, h2_ref, tope_ref, topw_ref, rank_ref, cnt_ref):
    step = pl.program_id(0)
    tm = x_ref.shape[0]

    @pl.when(step == 0)
    def _():
        cnt_ref[...] = jnp.zeros_like(cnt_ref)

    hsum = hf_ref[...].astype(F32) + hb_ref[...].astype(F32)
    heads = [_rms(hsum[:, h * ML_V_DIM:(h + 1) * ML_V_DIM]) for h in range(ML_HEADS)]
    hn = jnp.concatenate(heads, axis=-1) * ghead_ref[...]
    o_ml = jax.nn.sigmoid(opre_ref[...].astype(F32)) * hn
    merged = (jax.nn.sigmoid(gna_ref[...].astype(F32))
              * jnp.dot(ona_ref[...], wbna_ref[...], preferred_element_type=F32)
              + jax.nn.sigmoid(gml_ref[...].astype(F32))
              * jnp.dot(o_ml.astype(BF16), wbml_ref[...], preferred_element_type=F32))
    mixed = jnp.dot(merged.astype(BF16), wout_ref[...], preferred_element_type=F32)
    x1 = x_ref[...] + gtm_ref[...] * (_rms(mixed) * gpost_ref[...])
    x1_ref[...] = x1
    h2 = _rms(x1) * gpre_ref[...] * (1.0 + scf_ref[...]) + shf_ref[...]
    h2_ref[...] = h2
    h2_hi = h2.astype(BF16)
    h2_lo = (h2 - h2_hi.astype(F32)).astype(BF16)
    logits = (jnp.dot(h2_hi, wrh_ref[...], preferred_element_type=F32)
              + (jnp.dot(h2_hi, wrl_ref[...], preferred_element_type=F32)
                 + jnp.dot(h2_lo, wrh_ref[...], preferred_element_type=F32))) + br_ref[...]

    lane = lax.broadcasted_iota(jnp.int32, logits.shape, 1)
    onehots, top_e, top_l = [], [], []
    for _ in range(TOP_K):
        best = jnp.max(logits, axis=-1, keepdims=True)
        e = jnp.min(jnp.where(logits == best, lane, N_EXPERTS), axis=-1, keepdims=True)
        hit = lane == e
        onehots.append(hit)
        top_e.append(e)
        top_l.append(best)
        logits = jnp.where(hit, -jnp.inf, logits)
    exps = [jnp.exp(l - top_l[0]) for l in top_l]
    total = exps[0] + exps[1] + exps[2] + exps[3]

    counts = (onehots[0].astype(F32) + onehots[1].astype(F32)
              + onehots[2].astype(F32) + onehots[3].astype(F32))
    t = lax.broadcasted_iota(jnp.int32, (tm, tm), 0)
    s = lax.broadcasted_iota(jnp.int32, (tm, tm), 1)
    before = jnp.dot((s < t).astype(BF16), counts.astype(BF16), preferred_element_type=F32) + cnt_ref[...]
    out_lane = lax.broadcasted_iota(jnp.int32, (tm, LANES), 1)
    e_out = jnp.zeros((tm, LANES), jnp.int32)
    w_out = jnp.zeros((tm, LANES), F32)
    r_out = jnp.zeros((tm, LANES), jnp.int32)
    for j in range(TOP_K):
        rank = jnp.sum(jnp.where(onehots[j], before, 0.0), axis=-1, keepdims=True).astype(jnp.int32)
        e_out = jnp.where(out_lane == j, top_e[j], e_out)
        w_out = jnp.where(out_lane == j, exps[j] / total, w_out)
        r_out = jnp.where(out_lane == j, rank, r_out)
    tope_ref[...] = e_out
    topw_ref[...] = w_out
    rank_ref[...] = r_out
    cnt_ref[...] += jnp.sum(counts, axis=0, keepdims=True)


def _merge_and_route(x2, o_na, h_f, h_b, o_pre, g_na, g_ml, mod4, seq, g_head, wbna, wbml, wout,
                     g_post, g_pre, w_router, b_router):
    n, d = x2.shape
    tm = MERGE_ROWS
    per_seq = seq // tm

    def rows(width):
        return pl.BlockSpec((tm, width), lambda i: (i, 0))

    def mod_spec(j):
        return pl.BlockSpec((None, None, 1, d), lambda i, _j=j: (i // per_seq, _j, 0, 0))

    w_router_hi = w_router.astype(BF16)
    return pl.pallas_call(
        _merge_kernel,
        grid=(n // tm,),
        in_specs=[rows(d), rows(NA_WIDTH), rows(ML_WIDTH), rows(ML_WIDTH), rows(ML_WIDTH), rows(d), rows(d),
                  mod_spec(2), mod_spec(4), mod_spec(3),
                  _const_spec((1, ML_WIDTH)), _const_spec(wbna.shape), _const_spec(wbml.shape),
                  _const_spec(wout.shape), _const_spec((1, d)), _const_spec((1, d)),
                  _const_spec(w_router.shape), _const_spec(w_router.shape), _const_spec((1, N_EXPERTS))],
        out_specs=[rows(d), rows(d), rows(LANES), rows(LANES), rows(LANES),
                   pl.BlockSpec((1, N_EXPERTS), lambda i: (0, 0))],
        out_shape=[jax.ShapeDtypeStruct((n, d), F32), jax.ShapeDtypeStruct((n, d), F32),
                   jax.ShapeDtypeStruct((n, LANES), jnp.int32), jax.ShapeDtypeStruct((n, LANES), F32),
                   jax.ShapeDtypeStruct((n, LANES), jnp.int32),
                   jax.ShapeDtypeStruct((1, N_EXPERTS), F32)],
        compiler_params=_cparams("arbitrary"),
        name="merge_route",
    )(x2, o_na, h_f, h_b, o_pre, g_na, g_ml, mod4, mod4, mod4,
      g_head.reshape(1, ML_WIDTH), wbna, wbml, wout, g_post.reshape(1, d), g_pre.reshape(1, d),
      w_router_hi, (w_router - w_router_hi.astype(F32)).astype(BF16), b_router.reshape(1, N_EXPERTS))


def _dispatch_kernel(pad_end_ref, padded_ref, dest_ref, h_ref, xs_ref, zero_scr, sem):
    tm = h_ref.shape[0]
    blk = zero_scr.shape[0]

    @pl.when(pl.program_id(0) == 0)
    def _():
        zero_scr[...] = jnp.zeros_like(zero_scr)

        def zero_copy(e):
            first = pl.multiple_of(pad_end_ref[e] - blk, blk)
            return pltpu.make_async_copy(zero_scr, xs_ref.at[pl.ds(first, blk)], sem)

        for e in range(N_EXPERTS):
            @pl.when(padded_ref[e] > 0)
            def _():
                zero_copy(e).start()
        for e in range(N_EXPERTS):
            @pl.when(padded_ref[e] > 0)
            def _():
                zero_copy(e).wait()

        def tail_copy(b):
            return pltpu.make_async_copy(zero_scr, xs_ref.at[pl.ds(pl.multiple_of(b * blk, blk), blk)], sem)

        def tail_start(b, carry):
            tail_copy(b).start()
            return carry

        def tail_wait(b, carry):
            tail_copy(b).wait()
            return carry

        first_unused = pad_end_ref[N_EXPERTS - 1] // blk
        lax.fori_loop(first_unused, xs_ref.shape[0] // blk, tail_start, 0)
        lax.fori_loop(first_unused, xs_ref.shape[0] // blk, tail_wait, 0)

    tok_per_row = LANES // TOP_K

    def row_copy(r, c):
        return pltpu.make_async_copy(h_ref.at[pl.ds(r * tok_per_row + c // TOP_K, 1)],
                                     xs_ref.at[pl.ds(dest_ref[r, c], 1)], sem)

    def start(r, carry):
        for c in range(LANES):
            row_copy(r, c).start()
        return carry

    def wait(r, carry):
        for c in range(LANES):
            row_copy(r, c).wait()
        return carry

    lax.fori_loop(0, tm // tok_per_row, start, 0)
    lax.fori_loop(0, tm // tok_per_row, wait, 0)


def _dispatch(pad_end, padded, dest2, h2, n_rows):
    n = h2.shape[0]
    tm = MOVE_ROWS
    idx_rows = tm * TOP_K // LANES
    row_tile = h2.shape[1:]
    grid_spec = pltpu.PrefetchScalarGridSpec(
        num_scalar_prefetch=2,
        grid=(n // tm,),
        in_specs=[pl.BlockSpec((idx_rows, LANES), lambda i, pe, pd: (i, 0), memory_space=pltpu.SMEM),
                  pl.BlockSpec((tm,) + row_tile, lambda i, pe, pd: (i, 0))],
        out_specs=pl.BlockSpec(memory_space=pl.ANY),
        scratch_shapes=[pltpu.VMEM((EXPERT_ROWS,) + row_tile, h2.dtype), pltpu.SemaphoreType.DMA(())],
    )
    return pl.pallas_call(
        _dispatch_kernel,
        grid_spec=grid_spec,
        out_shape=jax.ShapeDtypeStruct((n_rows,) + row_tile, h2.dtype),
        compiler_params=_cparams("arbitrary"),
        name="moe_dispatch",
    )(pad_end, padded, dest2, h2)


def _expert_kernel(blk_e_ref, n_used_ref, x_ref, wg_ref, bg_ref, wl_ref, bl_ref, wd_ref, bd_ref, y_ref,
                   wg_s, wl_s, wd_s):
    i = pl.program_id(0)
    prev = blk_e_ref[jnp.maximum(i - 1, 0)]
    changed = jnp.logical_or(i == 0, blk_e_ref[i] != prev)
    used = i < n_used_ref[0]

    @pl.when(jnp.logical_and(used, changed))
    def _():
        wg_s[...] = wg_ref[...].astype(BF16)
        wl_s[...] = wl_ref[...].astype(BF16)
        wd_s[...] = wd_ref[...].astype(BF16)

    @pl.when(used)
    def _():
        xb = x_ref[...].astype(BF16)
        g = jnp.dot(xb, wg_s[...], preferred_element_type=F32) + bg_ref[...]
        l = jnp.dot(xb, wl_s[...], preferred_element_type=F32) + bl_ref[...]
        g = jnp.minimum(g, SWIGLU_LIMIT)
        l = jnp.clip(l, -SWIGLU_LIMIT, SWIGLU_LIMIT)
        a = g * jax.nn.sigmoid(SWIGLU_ALPHA * g) * (l + 1.0)
        y_ref[...] = jnp.dot(a.astype(BF16), wd_s[...], preferred_element_type=F32) + bd_ref[...]

    @pl.when(jnp.logical_not(used))
    def _():
        y_ref[...] = jnp.zeros_like(y_ref)


def _experts(blk_e, n_used, xs, w_gate, b_gate, w_lin, b_lin, w_down, b_down):
    n_rows = xs.shape[0]
    row_tile = xs.shape[1:]
    e, d, f = w_gate.shape
    tm = EXPERT_ROWS

    def w_spec(shape):
        return pl.BlockSpec((None,) + shape, lambda i, be, nu: (be[i], 0, 0))

    grid_spec = pltpu.PrefetchScalarGridSpec(
        num_scalar_prefetch=2,
        grid=(n_rows // tm,),
        in_specs=[pl.BlockSpec((tm,) + row_tile, lambda i, be, nu: (i, 0)),
                  w_spec((d, f)), w_spec((1, f)), w_spec((d, f)), w_spec((1, f)),
                  w_spec((f, d)), w_spec((1, d))],
        out_specs=pl.BlockSpec((tm,) + row_tile, lambda i, be, nu: (i, 0)),
        scratch_shapes=[pltpu.VMEM((d, f), BF16), pltpu.VMEM((d, f), BF16), pltpu.VMEM((f, d), BF16)],
    )
    return pl.pallas_call(
        _expert_kernel,
        grid_spec=grid_spec,
        out_shape=jax.ShapeDtypeStruct((n_rows,) + row_tile, F32),
        compiler_params=_cparams("arbitrary"),
        name="moe_experts",
    )(blk_e, n_used, xs, w_gate, b_gate.reshape(e, 1, f), w_lin, b_lin.reshape(e, 1, f),
      w_down, b_down.reshape(e, 1, d))


def _combine_kernel(dest_ref, x1_ref, w_ref, gtf_ref, gpost_ref, y_ref, o_ref, buf, sem):
    tm = x1_ref.shape[0]

    tok_per_row = LANES // TOP_K

    def row_copy(r, c):
        return pltpu.make_async_copy(y_ref.at[pl.ds(dest_ref[r, c], 1)],
                                     buf.at[c % TOP_K, pl.ds(r * tok_per_row + c // TOP_K, 1)], sem)

    def start(r, carry):
        for c in range(LANES):
            row_copy(r, c).start()
        return carry

    def wait(r, carry):
        for c in range(LANES):
            row_copy(r, c).wait()
        return carry

    lax.fori_loop(0, tm // tok_per_row, start, 0)
    lax.fori_loop(0, tm // tok_per_row, wait, 0)
    w = w_ref[...]
    ffn = (buf[0] * w[:, 0:1] + buf[1] * w[:, 1:2]) + (buf[2] * w[:, 2:3] + buf[3] * w[:, 3:4])
    o_ref[...] = x1_ref[...] + gtf_ref[...] * (_rms(ffn) * gpost_ref[...])


def _combine(dest2, x1, top_w, mod4, seq, g_post, y):
    n, d = x1.shape
    tm = MOVE_ROWS
    per_seq = seq // tm
    idx_rows = tm * TOP_K // LANES
    return pl.pallas_call(
        _combine_kernel,
        grid=(n // tm,),
        in_specs=[pl.BlockSpec((idx_rows, LANES), lambda i: (i, 0), memory_space=pltpu.SMEM),
                  pl.BlockSpec((tm, d), lambda i: (i, 0)),
                  pl.BlockSpec((tm, LANES), lambda i: (i, 0)),
                  pl.BlockSpec((None, None, 1, d), lambda i: (i // per_seq, 5, 0, 0)),
                  _const_spec((1, d)),
                  pl.BlockSpec(memory_space=pl.ANY)],
        out_specs=pl.BlockSpec((tm, d), lambda i: (i, 0)),
        out_shape=jax.ShapeDtypeStruct((n, d), F32),
        scratch_shapes=[pltpu.VMEM((TOP_K, tm) + y.shape[1:], F32), pltpu.SemaphoreType.DMA(())],
        compiler_params=_cparams("arbitrary"),
        name="moe_combine",
    )(dest2, x1, top_w, mod4, g_post.reshape(1, d), y)


def _rope_partner(w):
    n_freq = ML_QK_DIM // 4
    d = w.shape[0]
    w4 = w.reshape(d, -1, 2, n_freq)
    return jnp.stack([-w4[:, :, 1], w4[:, :, 0]], axis=2).reshape(w.shape)


def _rope_tables(seq):
    n_freq = ML_QK_DIM // 4
    t = jnp.arange(seq)
    row = (t // GRID_W).astype(F32)
    col = (t % GRID_W).astype(F32)
    inv_freq = ROPE_BASE ** (-jnp.arange(n_freq, dtype=F32) / n_freq)
    ang = jnp.concatenate([row[:, None] * inv_freq] * 2 + [col[:, None] * inv_freq] * 2, axis=-1)
    cos = jnp.tile(jnp.cos(ang), (1, ML_HEADS))
    sin = jnp.tile(jnp.sin(ang), (1, ML_HEADS))
    return cos, sin


def _layer(x, ctx, mod4, g_mix_pre, g_mix_post, g_ffn_pre, g_ffn_post, w_in, b_gates, rpb, g_head,
           w_branch_na, w_branch_ml, w_out, w_router, b_router, w_gate, b_gate, w_lin, b_lin, w_down, b_down):
    b, s, d = x.shape
    n = b * s
    n_ctx = ctx.shape[1]
    x2 = x.reshape(n, d)

    ctx_cols = (NA_WIDTH, NA_WIDTH, ML_QK_WIDTH, ML_WIDTH, N_GATES)
    lat_cols = (NA_WIDTH, ML_QK_WIDTH, ML_WIDTH, d, d)
    bounds = np.cumsum(ctx_cols + lat_cols)[:-1].tolist()
    (w_nak, w_nav, w_mk, w_mv, w_g, w_naq, w_mq, w_mo, w_gna, w_gml) = jnp.split(w_in, bounds, axis=-1)
    w_naq = w_naq * (NA_HEAD_DIM ** -0.5 * LOG2_E)
    w_mk = w_mk * ML_QK_DIM ** -0.5
    bf = lambda a: a.astype(BF16)
    lat_w = [bf(w_naq), bf(w_nak), bf(w_nav), bf(w_mq), bf(_rope_partner(w_mq)), bf(w_mk),
             bf(_rope_partner(w_mk)), bf(w_mv), bf(w_mo), bf(w_gna), bf(w_gml), bf(w_g), bf(w_g.T)]
    ctx_w = [bf(w_nak), bf(w_nav), bf(w_mk), bf(w_mv), bf(w_g), bf(w_g.T)]
    bg_col = b_gates.reshape(1, N_GATES).astype(F32)
    bg_row = b_gates.reshape(N_GATES, 1).astype(F32)
    cos, sin = _rope_tables(s)
    per_seq = s // PROJ_ROWS

    (na_q, na_k, na_v, ml_q, ml_k, ml_v, ml_o, gate_na, gate_ml, g_col, g_row) = _project(
        x2, mod4, lambda i: i // per_seq, g_mix_pre, bg_col, bg_row, lat_w, (cos, sin, per_seq), True)
    (na_kc, na_vc, ml_kc, ml_vc, gc_col, gc_row) = _project(
        ctx.reshape(b * n_ctx, d), mod4, lambda i: b, g_mix_pre, bg_col, bg_row, ctx_w, None, False)

    def seq3(a, length):
        return a.reshape(b, length, a.shape[-1])

    o_na = _neighbourhood_attention(seq3(na_q, s), seq3(na_k, s), seq3(na_v, s),
                                    seq3(na_kc, n_ctx), seq3(na_vc, n_ctx), rpb)
    ml_args = (seq3(ml_q, s), seq3(ml_k, s), seq3(ml_v, s), seq3(g_col, s), g_row,
               seq3(ml_kc, n_ctx), seq3(ml_vc, n_ctx), seq3(gc_col, n_ctx), gc_row)
    h_f = _mlstm_direction(*ml_args, reverse=False)
    h_b = _mlstm_direction(*ml_args, reverse=True)

    x1, h2, top_e, top_w, rank, counts = _merge_and_route(
        x2, o_na.reshape(n, NA_WIDTH), h_f.reshape(n, ML_WIDTH), h_b.reshape(n, ML_WIDTH), ml_o,
        gate_na, gate_ml, mod4, s, g_head, bf(w_branch_na), bf(w_branch_ml), bf(w_out),
        g_mix_post, g_ffn_pre, w_router, b_router)

    tm = EXPERT_ROWS
    counts = counts.reshape(N_EXPERTS).astype(jnp.int32)
    padded = (counts + tm - 1) // tm * tm
    pad_end = jnp.cumsum(padded)
    pad_start = pad_end - padded
    n_rows = n * TOP_K + N_EXPERTS * tm
    n_blocks = n_rows // tm
    e_sel = top_e[:, :TOP_K, None] == jnp.arange(N_EXPERTS, dtype=jnp.int32)
    dest = jnp.sum(jnp.where(e_sel, pad_start, 0), axis=-1) + rank[:, :TOP_K]
    dest2 = dest.reshape(n * TOP_K // LANES, LANES).astype(jnp.int32)
    blk_start = jnp.arange(n_blocks, dtype=jnp.int32) * tm
    blk_e = jnp.minimum(jnp.sum(blk_start[:, None] >= pad_end[None, :], axis=1), N_EXPERTS - 1).astype(jnp.int32)
    n_used = (pad_end[-1:] // tm).astype(jnp.int32)

    xs = _dispatch(pad_end.astype(jnp.int32), padded.astype(jnp.int32), dest2, h2, n_rows)
    y = _experts(blk_e, n_used, xs, w_gate, b_gate, w_lin, b_lin, w_down, b_down)
    out = _combine(dest2, x1, top_w, mod4, s, g_ffn_post, y)
    return out.reshape(b, s, d)


def kernel(x, c, ctx, c_ctx, w_ada, b_ada, g_mix_pre, g_mix_post, g_ffn_pre, g_ffn_post, w_in, b_mlstm_gates,
           rpb, g_mlstm_head, w_branch_na, w_branch_ml, w_out, w_router, b_router, w_gate, b_gate, w_lin,
           b_lin, w_down, b_down):
    b, s, d = x.shape
    depth = w_ada.shape[0]
    pad = (-(b + 1)) % 8
    c_all = jnp.concatenate([c, c_ctx[None, :], jnp.zeros((pad, d), c.dtype)], axis=0)
    for layer in range(depth):
        mod = _ada(c_all, w_ada[layer], b_ada[layer])
        mod4 = mod.reshape(mod.shape[0], 6, 1, d)
        x = _layer(x, ctx, mod4, g_mix_pre[layer], g_mix_post[layer], g_ffn_pre[layer], g_ffn_post[layer],
                   w_in[layer], b_mlstm_gates[layer], rpb[layer], g_mlstm_head[layer], w_branch_na[layer],
                   w_branch_ml[layer], w_out[layer], w_router[layer], b_router[layer], w_gate[layer],
                   b_gate[layer], w_lin[layer], b_lin[layer], w_down[layer], b_down[layer])
    return x
```

```python
import functools

import numpy as np
import jax
import jax.numpy as jnp
from jax import lax
from jax.experimental import pallas as pl
from jax.experimental.pallas import tpu as pltpu

F32 = jnp.float32
BF16 = jnp.bfloat16
HIGHEST = lax.Precision.HIGHEST

GRID_W = 64
NA_HEADS = 8
NA_HEAD_DIM = 64
NA_WIDTH = NA_HEADS * NA_HEAD_DIM
WIN_H = 8
WIN_W = 16
ML_HEADS = 4
ML_QK_DIM = 64
ML_V_DIM = 128
ML_QK_WIDTH = ML_HEADS * ML_QK_DIM
ML_WIDTH = ML_HEADS * ML_V_DIM
N_GATES = 4 * ML_HEADS
GATE_SOFTCAP = 15.0
ROPE_BASE = 10000.0
N_EXPERTS = 32
TOP_K = 4
SWIGLU_ALPHA = 1.702
SWIGLU_LIMIT = 7.0
NORM_EPS = 1e-6
NEG_INF = -1e30
LOG2_E = 1.4426950408889634

LANES = 128
NA_QROWS = 4
NA_WROWS = 12
ML_CHUNK = 256
PROJ_ROWS = 512
MERGE_ROWS = 256
EXPERT_ROWS = 512
MOVE_ROWS = 256
VMEM_LIMIT = 56 * 1024 * 1024

NT_DIMS = (((1,), (1,)), ((), ()))
TN_DIMS = (((0,), (0,)), ((), ()))


def _cparams(*sem):
    return pltpu.CompilerParams(dimension_semantics=sem, vmem_limit_bytes=VMEM_LIMIT)


def _rms(x):
    return x * lax.rsqrt(jnp.mean(x * x, axis=-1, keepdims=True) + NORM_EPS)


def _ada_kernel(c_ref, w_ref, b_ref, o_ref):
    c = c_ref[...]
    s = c * jax.nn.sigmoid(c)
    o_ref[...] = jnp.dot(s, w_ref[...], preferred_element_type=F32, precision=HIGHEST) + b_ref[...]


def _ada(c_all, w_ada, b_ada):
    rows, d = c_all.shape
    n_out = w_ada.shape[1]
    tn = 1536
    return pl.pallas_call(
        _ada_kernel,
        grid=(n_out // tn,),
        in_specs=[pl.BlockSpec((rows, d), lambda j: (0, 0)),
                  pl.BlockSpec((d, tn), lambda j: (0, j)),
                  pl.BlockSpec((1, tn), lambda j: (0, j))],
        out_specs=pl.BlockSpec((rows, tn), lambda j: (0, j)),
        out_shape=jax.ShapeDtypeStruct((rows, n_out), F32),
        compiler_params=_cparams("arbitrary"),
        name="ada_mod",
    )(c_all, w_ada, b_ada.reshape(1, n_out))


def _gate_logs(g, is_forget):
    g = GATE_SOFTCAP * jnp.tanh(g / GATE_SOFTCAP)
    log_sig = jnp.minimum(g, 0.0) - jnp.log(1.0 + jnp.exp(-jnp.abs(g)))
    return jnp.where(is_forget, log_sig, g)


def _proj_kernel(*refs, latent):
    if latent:
        (x_ref, g_ref, sc_ref, sh_ref, cos_ref, sin_ref, bgc_ref, bgr_ref,
         w_naq, w_nak, w_nav, w_mq, w_mqp, w_mk, w_mkp, w_mv, w_mo, w_gna, w_gml, w_gc, w_gr,
         o_naq, o_nak, o_nav, o_mq, o_mk, o_mv, o_mo, o_gna, o_gml, o_gc, o_gr) = refs
    else:
        (x_ref, g_ref, sc_ref, sh_ref, bgc_ref, bgr_ref,
         w_nak, w_nav, w_mk, w_mv, w_gc, w_gr,
         o_nak, o_nav, o_mk, o_mv, o_gc, o_gr) = refs
    x = x_ref[...]
    h = _rms(x) * g_ref[...]
    h = h * (1.0 + sc_ref[...]) + sh_ref[...]
    hb = h.astype(BF16)

    def mm(w_ref):
        return jnp.dot(hb, w_ref[...], preferred_element_type=F32)

    o_nak[...] = mm(w_nak).astype(BF16)
    o_nav[...] = mm(w_nav).astype(BF16)
    o_mv[...] = mm(w_mv).astype(BF16)
    if latent:
        cos = cos_ref[...]
        sin = sin_ref[...]
        o_naq[...] = mm(w_naq).astype(BF16)
        o_mq[...] = (mm(w_mq) * cos + mm(w_mqp) * sin).astype(BF16)
        o_mk[...] = (mm(w_mk) * cos + mm(w_mkp) * sin).astype(BF16)
        o_mo[...] = mm(w_mo).astype(BF16)
        o_gna[...] = mm(w_gna).astype(BF16)
        o_gml[...] = mm(w_gml).astype(BF16)
    else:
        o_mk[...] = mm(w_mk).astype(BF16)
    gc = mm(w_gc) + bgc_ref[...]
    col_id = lax.broadcasted_iota(jnp.int32, gc.shape, 1)
    o_gc[...] = _gate_logs(gc, (col_id // ML_HEADS) % 2 == 1)
    gr = lax.dot_general(w_gr[...], hb, NT_DIMS, preferred_element_type=F32) + bgr_ref[...]
    row_id = lax.broadcasted_iota(jnp.int32, gr.shape, 0)
    o_gr[...] = _gate_logs(gr, (row_id // ML_HEADS) % 2 == 1)


def _const_spec(shape):
    nd = len(shape)
    return pl.BlockSpec(shape, lambda i, _nd=nd: (0,) * _nd)


def _project(x2, mod4, mod_row_fn, g_pre, bg_col, bg_row, weights, tables, latent):
    n, d = x2.shape
    tm = PROJ_ROWS
    grid = (n // tm,)

    def mod_spec(j):
        return pl.BlockSpec((None, None, 1, d), lambda i, _j=j: (mod_row_fn(i), _j, 0, 0))

    in_specs = [pl.BlockSpec((tm, d), lambda i: (i, 0)), _const_spec((1, d)), mod_spec(1), mod_spec(0)]
    args = [x2, g_pre.reshape(1, d), mod4, mod4]
    if latent:
        cos, sin, tiles_per_seq = tables
        in_specs += [pl.BlockSpec((tm, ML_QK_WIDTH), lambda i: (i % tiles_per_seq, 0))] * 2
        args += [cos, sin]
    in_specs += [_const_spec(bg_col.shape), _const_spec(bg_row.shape)]
    args += [bg_col, bg_row]
    for w in weights:
        in_specs.append(_const_spec(w.shape))
        args.append(w)

    def out(width, dtype=BF16):
        return (jax.ShapeDtypeStruct((n, width), dtype), pl.BlockSpec((tm, width), lambda i: (i, 0)))

    if latent:
        outs = [out(NA_WIDTH), out(NA_WIDTH), out(NA_WIDTH), out(ML_QK_WIDTH), out(ML_QK_WIDTH),
                out(ML_WIDTH), out(ML_WIDTH), out(d), out(d), out(N_GATES, F32)]
    else:
        outs = [out(NA_WIDTH), out(NA_WIDTH), out(ML_QK_WIDTH), out(ML_WIDTH), out(N_GATES, F32)]
    outs.append((jax.ShapeDtypeStruct((N_GATES, n), F32), pl.BlockSpec((N_GATES, tm), lambda i: (0, i))))
    return pl.pallas_call(
        functools.partial(_proj_kernel, latent=latent),
        grid=grid,
        in_specs=in_specs,
        out_specs=[o[1] for o in outs],
        out_shape=[o[0] for o in outs],
        compiler_params=_cparams("arbitrary"),
        name="in_proj_latent" if latent else "in_proj_ctx",
    )(*args)


def _na_window_start(r0, rows):
    return jnp.clip(r0 - WIN_H // 2, 0, rows - NA_WROWS)


def _na_classes(rows):
    keys, group_class = [], []
    for r0 in range(0, rows, NA_QROWS):
        start = min(max(r0 - WIN_H // 2, 0), rows - NA_WROWS)
        first = tuple(min(max(r0 + i - WIN_H // 2, 0), rows - WIN_H) - start for i in range(NA_QROWS))
        assert all(0 <= f and f + WIN_H <= NA_WROWS for f in first)
        key = (r0 - start, first)
        if key not in keys:
            keys.append(key)
        group_class.append(keys.index(key))
    return keys, np.asarray(group_class, np.int32)


def _na_bias_table(rpb, rows):
    keys, group_class = _na_classes(rows)
    qc = np.arange(GRID_W)[:, None]
    kc = np.arange(GRID_W)[None, :]
    cs = np.clip(qc - WIN_W // 2, 0, GRID_W - WIN_W)
    col_ok = (kc >= cs) & (kc < cs + WIN_W)
    dc = np.clip(kc - qc, -(WIN_W - 1), WIN_W - 1) + (WIN_W - 1)
    i = np.arange(NA_QROWS)[:, None]
    j = np.arange(NA_WROWS)[None, :]
    sel_r, row_ok = [], []
    for off, first in keys:
        f = np.asarray(first)[:, None]
        ok = (j >= f) & (j < f + WIN_H)
        dr = j - off - i + (WIN_H - 1)
        sel_r.append(((dr[:, :, None] == np.arange(2 * WIN_H - 1)) & ok[:, :, None]).astype(np.float32))
        row_ok.append(ok)
    sel_r = np.stack(sel_r)
    valid = np.stack(row_ok)[:, None, :, None, :, None] & col_ok[None, None, None, :, None, :]
    sel_c = (dc[:, :, None] == np.arange(2 * WIN_W - 1)).astype(np.float32)
    t = jnp.einsum('hrc,qkc->hrqk', rpb.astype(F32), sel_c, precision=HIGHEST)
    bias = jnp.einsum('hrqk,xijr->xhiqjk', t, sel_r, precision=HIGHEST)
    bias = jnp.where(valid, bias * LOG2_E, NEG_INF)
    bias = bias.reshape(len(keys), NA_HEADS, NA_QROWS * GRID_W, NA_WROWS * GRID_W)
    return bias.astype(BF16), jnp.asarray(group_class)


def _na_kernel(cls_ref, q_ref, k_ref, v_ref, kc_ref, vc_ref, bias_ref, o_ref, *, rows):
    del cls_ref
    r0 = pl.program_id(1) * NA_QROWS
    start = pl.multiple_of(_na_window_start(r0, rows) * GRID_W, GRID_W)
    n_win = NA_WROWS * GRID_W
    low = lax.broadcasted_iota(jnp.int32, (1, LANES), 1) < NA_HEAD_DIM
    for pair in range(NA_HEADS // 2):
        sl = slice(pair * LANES, (pair + 1) * LANES)
        q2 = q_ref[:, sl]
        k2 = k_ref[pl.ds(start, n_win), sl]
        v2 = v_ref[pl.ds(start, n_win), sl]
        kc2 = kc_ref[:, sl]
        vc2 = vc_ref[:, sl]
        halves = []
        for half in range(2):
            keep = low if half == 0 else jnp.logical_not(low)
            qh = jnp.where(keep, q2, jnp.zeros_like(q2))
            s_loc = (lax.dot_general(qh, k2, NT_DIMS, preferred_element_type=F32)
                     + bias_ref[2 * pair + half].astype(F32))
            s_ctx = lax.dot_general(qh, kc2, NT_DIMS, preferred_element_type=F32)
            m = jnp.maximum(jnp.max(s_loc, axis=-1, keepdims=True), jnp.max(s_ctx, axis=-1, keepdims=True))
            p_loc = jnp.exp2(s_loc - m)
            p_ctx = jnp.exp2(s_ctx - m)
            denom = jnp.sum(p_loc, axis=-1, keepdims=True) + jnp.sum(p_ctx, axis=-1, keepdims=True)
            o = (jnp.dot(p_loc.astype(BF16), v2, preferred_element_type=F32)
                 + jnp.dot(p_ctx.astype(BF16), vc2, preferred_element_type=F32))
            halves.append(o / denom)
        o_ref[:, sl] = jnp.where(low, halves[0], halves[1]).astype(BF16)


def _neighbourhood_attention(q, k, v, kc, vc, rpb):
    b, s, w = q.shape
    rows = s // GRID_W
    n_ctx = kc.shape[1]
    bias, group_class = _na_bias_table(rpb, rows)
    nq = NA_QROWS * GRID_W
    grid_spec = pltpu.PrefetchScalarGridSpec(
        num_scalar_prefetch=1,
        grid=(b, rows // NA_QROWS),
        in_specs=[pl.BlockSpec((None, nq, w), lambda bi, g, cls: (bi, g, 0)),
                  pl.BlockSpec((None, s, w), lambda bi, g, cls: (bi, 0, 0)),
                  pl.BlockSpec((None, s, w), lambda bi, g, cls: (bi, 0, 0)),
                  pl.BlockSpec((None, n_ctx, w), lambda bi, g, cls: (bi, 0, 0)),
                  pl.BlockSpec((None, n_ctx, w), lambda bi, g, cls: (bi, 0, 0)),
                  pl.BlockSpec((None,) + bias.shape[1:], lambda bi, g, cls: (cls[g], 0, 0, 0))],
        out_specs=pl.BlockSpec((None, nq, w), lambda bi, g, cls: (bi, g, 0)),
    )
    return pl.pallas_call(
        functools.partial(_na_kernel, rows=rows),
        grid_spec=grid_spec,
        out_shape=jax.ShapeDtypeStruct((b, s, w), BF16),
        compiler_params=_cparams("arbitrary", "arbitrary"),
        name="na_attention",
    )(group_class, q, k, v, kc, vc, bias)


def _mlstm_kernel(q_ref, k_ref, v_ref, gc_ref, gr_ref, kc_ref, vc_ref, gcc_ref, gcr_ref,
                  o_ref, c_scr, n_scr, m_scr, *, reverse):
    step = pl.program_id(1)
    li_base = 2 * ML_HEADS if reverse else 0
    lf_base = li_base + ML_HEADS

    def tri(length):
        t = lax.broadcasted_iota(jnp.int32, (length, length), 0)
        s = lax.broadcasted_iota(jnp.int32, (length, length), 1)
        valid = (s >= t) if reverse else (s <= t)
        valid_t = (t >= s) if reverse else (t <= s)
        return valid, valid_t

    def cumulative(gc, gr, valid, valid_t):
        b_col = jnp.dot(valid.astype(F32), gc, preferred_element_type=F32, precision=HIGHEST)
        b_row = jnp.dot(gr, valid_t.astype(F32), preferred_element_type=F32, precision=HIGHEST)
        return b_col, b_row

    def update_state(h, k, v, gc, gr, b_col, b_row):
        li_c = gc[:, li_base + h:li_base + h + 1]
        li_r = gr[li_base + h:li_base + h + 1, :]
        lf_r = gr[lf_base + h:lf_base + h + 1, :]
        bc = b_col[:, lf_base + h:lf_base + h + 1]
        br = b_row[lf_base + h:lf_base + h + 1, :]
        total = jnp.sum(lf_r, axis=-1, keepdims=True)
        m_prev = m_scr[h]
        m_new = jnp.maximum(total + m_prev, jnp.max(total - br + li_r, axis=-1, keepdims=True))
        w_src = jnp.exp(total - bc + li_c - m_new)
        w_carry = jnp.exp(total + m_prev - m_new)
        kh = k[:, h * ML_QK_DIM:(h + 1) * ML_QK_DIM].astype(F32) * w_src
        vh = v[:, h * ML_V_DIM:(h + 1) * ML_V_DIM]
        c_scr[h] = w_carry * c_scr[h] + lax.dot_general(kh.astype(BF16), vh, TN_DIMS,
                                                        preferred_element_type=F32)
        n_scr[h] = w_carry * n_scr[h] + jnp.sum(kh, axis=0, keepdims=True)
        m_scr[h] = m_new

    @pl.when(step == 0)
    def _():
        c_scr[...] = jnp.zeros_like(c_scr)
        n_scr[...] = jnp.zeros_like(n_scr)
        m_scr[...] = jnp.zeros_like(m_scr)
        k = kc_ref[...]
        v = vc_ref[...]
        gc = gcc_ref[...]
        gr = gcr_ref[...]
        valid, valid_t = tri(k.shape[0])
        b_col, b_row = cumulative(gc, gr, valid, valid_t)
        for h in range(ML_HEADS):
            update_state(h, k, v, gc, gr, b_col, b_row)

    @pl.when(step > 0)
    def _():
        q = q_ref[...]
        k = k_ref[...]
        v = v_ref[...]
        gc = gc_ref[...]
        gr = gr_ref[...]
        valid, valid_t = tri(k.shape[0])
        b_col, b_row = cumulative(gc, gr, valid, valid_t)
        outs = []
        for h in range(ML_HEADS):
            li_r = gr[li_base + h:li_base + h + 1, :]
            bc = b_col[:, lf_base + h:lf_base + h + 1]
            br = b_row[lf_base + h:lf_base + h + 1, :]
            m_prev = m_scr[h]
            qh = q[:, h * ML_QK_DIM:(h + 1) * ML_QK_DIM]
            kh = k[:, h * ML_QK_DIM:(h + 1) * ML_QK_DIM]
            vh = v[:, h * ML_V_DIM:(h + 1) * ML_V_DIM]
            d_mat = jnp.where(valid, bc - br + li_r, NEG_INF)
            m_inter = bc + m_prev
            m_t = jnp.maximum(m_inter, jnp.max(d_mat, axis=-1, keepdims=True))
            w_intra = jnp.exp(d_mat - m_t)
            w_inter = jnp.exp(m_inter - m_t)
            s = lax.dot_general(qh, kh, NT_DIMS, preferred_element_type=F32) * w_intra
            num = (w_inter * jnp.dot(qh, c_scr[h].astype(BF16), preferred_element_type=F32)
                   + jnp.dot(s.astype(BF16), vh, preferred_element_type=F32))
            den = (w_inter * jnp.sum(qh.astype(F32) * n_scr[h], axis=-1, keepdims=True)
                   + jnp.sum(s, axis=-1, keepdims=True))
            outs.append(num / jnp.maximum(jnp.abs(den), jnp.exp(-m_t)))
        o_ref[...] = jnp.concatenate(outs, axis=-1).astype(BF16)
        for h in range(ML_HEADS):
            update_state(h, k, v, gc, gr, b_col, b_row)


def _mlstm_direction(q, k, v, gc, gr, kc, vc, gcc, gcr, reverse):
    b, s, _ = q.shape
    n_ctx = kc.shape[1]
    length = ML_CHUNK
    n_chunks = s // length

    def chunk(step):
        c = jnp.maximum(step - 1, 0)
        return (n_chunks - 1 - c) if reverse else c

    return pl.pallas_call(
        functools.partial(_mlstm_kernel, reverse=reverse),
        grid=(b, n_chunks + 1),
        in_specs=[pl.BlockSpec((None, length, ML_QK_WIDTH), lambda bi, st: (bi, chunk(st), 0)),
                  pl.BlockSpec((None, length, ML_QK_WIDTH), lambda bi, st: (bi, chunk(st), 0)),
                  pl.BlockSpec((None, length, ML_WIDTH), lambda bi, st: (bi, chunk(st), 0)),
                  pl.BlockSpec((None, length, N_GATES), lambda bi, st: (bi, chunk(st), 0)),
                  pl.BlockSpec((N_GATES, length), lambda bi, st: (0, bi * n_chunks + chunk(st))),
                  pl.BlockSpec((None, n_ctx, ML_QK_WIDTH), lambda bi, st: (bi, 0, 0)),
                  pl.BlockSpec((None, n_ctx, ML_WIDTH), lambda bi, st: (bi, 0, 0)),
                  pl.BlockSpec((None, n_ctx, N_GATES), lambda bi, st: (bi, 0, 0)),
                  pl.BlockSpec((N_GATES, n_ctx), lambda bi, st: (0, bi))],
        out_specs=pl.BlockSpec((None, length, ML_WIDTH), lambda bi, st: (bi, chunk(st), 0)),
        out_shape=jax.ShapeDtypeStruct((b, s, ML_WIDTH), BF16),
        scratch_shapes=[pltpu.VMEM((ML_HEADS, ML_QK_DIM, ML_V_DIM), F32),
                        pltpu.VMEM((ML_HEADS, 1, ML_QK_DIM), F32),
                        pltpu.VMEM((ML_HEADS, 1, 1), F32)],
        compiler_params=_cparams("arbitrary", "arbitrary"),
        name="mlstm_bwd" if reverse else "mlstm_fwd",
    )(q, k, v, gc, gr, kc, vc, gcc, gcr)


def _merge_kernel(x_ref, ona_ref, hf_ref, hb_ref, opre_ref, gna_ref, gml_ref, gtm_ref, scf_ref, shf_ref,
                  ghead_ref, wbna_ref, wbml_ref, wout_ref, gpost_ref, gpre_ref, wrh_ref, wrl_ref, br_ref,
                  x1_ref, h2_ref, tope_ref, topw_ref, rank_ref, cnt_ref):
    step = pl.program_id(0)
    tm = x_ref.shape[0]

    @pl.when(step == 0)
    def _():
        cnt_ref[...] = jnp.zeros_like(cnt_ref)

    hsum = hf_ref[...].astype(F32) + hb_ref[...].astype(F32)
    heads = [_rms(hsum[:, h * ML_V_DIM:(h + 1) * ML_V_DIM]) for h in range(ML_HEADS)]
    hn = jnp.concatenate(heads, axis=-1) * ghead_ref[...]
    o_ml = jax.nn.sigmoid(opre_ref[...].astype(F32)) * hn
    merged = (jax.nn.sigmoid(gna_ref[...].astype(F32))
              * jnp.dot(ona_ref[...], wbna_ref[...], preferred_element_type=F32)
              + jax.nn.sigmoid(gml_ref[...].astype(F32))
              * jnp.dot(o_ml.astype(BF16), wbml_ref[...], preferred_element_type=F32))
    mixed = jnp.dot(merged.astype(BF16), wout_ref[...], preferred_element_type=F32)
    x1 = x_ref[...] + gtm_ref[...] * (_rms(mixed) * gpost_ref[...])
    x1_ref[...] = x1
    h2 = _rms(x1) * gpre_ref[...] * (1.0 + scf_ref[...]) + shf_ref[...]
    h2_ref[...] = h2
    h2_hi = h2.astype(BF16)
    h2_lo = (h2 - h2_hi.astype(F32)).astype(BF16)
    logits = (jnp.dot(h2_hi, wrh_ref[...], preferred_element_type=F32)
              + (jnp.dot(h2_hi, wrl_ref[...], preferred_element_type=F32)
                 + jnp.dot(h2_lo, wrh_ref[...], preferred_element_type=F32))) + br_ref[...]

    lane = lax.broadcasted_iota(jnp.int32, logits.shape, 1)
    onehots, top_e, top_l = [], [], []
    for _ in range(TOP_K):
        best = jnp.max(logits, axis=-1, keepdims=True)
        e = jnp.min(jnp.where(logits == best, lane, N_EXPERTS), axis=-1, keepdims=True)
        hit = lane == e
        onehots.append(hit)
        top_e.append(e)
        top_l.append(best)
        logits = jnp.where(hit, -jnp.inf, logits)
    exps = [jnp.exp(l - top_l[0]) for l in top_l]
    total = exps[0] + exps[1] + exps[2] + exps[3]

    counts = (onehots[0].astype(F32) + onehots[1].astype(F32)
              + onehots[2].astype(F32) + onehots[3].astype(F32))
    t = lax.broadcasted_iota(jnp.int32, (tm, tm), 0)
    s = lax.broadcasted_iota(jnp.int32, (tm, tm), 1)
    before = jnp.dot((s < t).astype(BF16), counts.astype(BF16), preferred_element_type=F32) + cnt_ref[...]
    out_lane = lax.broadcasted_iota(jnp.int32, (tm, LANES), 1)
    e_out = jnp.zeros((tm, LANES), jnp.int32)
    w_out = jnp.zeros((tm, LANES), F32)
    r_out = jnp.zeros((tm, LANES), jnp.int32)
    for j in range(TOP_K):
        rank = jnp.sum(jnp.where(onehots[j], before, 0.0), axis=-1, keepdims=True).astype(jnp.int32)
        e_out = jnp.where(out_lane == j, top_e[j], e_out)
        w_out = jnp.where(out_lane == j, exps[j] / total, w_out)
        r_out = jnp.where(out_lane == j, rank, r_out)
    tope_ref[...] = e_out
    topw_ref[...] = w_out
    rank_ref[...] = r_out
    cnt_ref[...] += jnp.sum(counts, axis=0, keepdims=True)


def _merge_and_route(x2, o_na, h_f, h_b, o_pre, g_na, g_ml, mod4, seq, g_head, wbna, wbml, wout,
                     g_post, g_pre, w_router, b_router):
    n, d = x2.shape
    tm = MERGE_ROWS
    per_seq = seq // tm

    def rows(width):
        return pl.BlockSpec((tm, width), lambda i: (i, 0))

    def mod_spec(j):
        return pl.BlockSpec((None, None, 1, d), lambda i, _j=j: (i // per_seq, _j, 0, 0))

    w_router_hi = w_router.astype(BF16)
    return pl.pallas_call(
        _merge_kernel,
        grid=(n // tm,),
        in_specs=[rows(d), rows(NA_WIDTH), rows(ML_WIDTH), rows(ML_WIDTH), rows(ML_WIDTH), rows(d), rows(d),
                  mod_spec(2), mod_spec(4), mod_spec(3),
                  _const_spec((1, ML_WIDTH)), _const_spec(wbna.shape), _const_spec(wbml.shape),
                  _const_spec(wout.shape), _const_spec((1, d)), _const_spec((1, d)),
                  _const_spec(w_router.shape), _const_spec(w_router.shape), _const_spec((1, N_EXPERTS))],
        out_specs=[rows(d), rows(d), rows(LANES), rows(LANES), rows(LANES),
                   pl.BlockSpec((1, N_EXPERTS), lambda i: (0, 0))],
        out_shape=[jax.ShapeDtypeStruct((n, d), F32), jax.ShapeDtypeStruct((n, d), F32),
                   jax.ShapeDtypeStruct((n, LANES), jnp.int32), jax.ShapeDtypeStruct((n, LANES), F32),
                   jax.ShapeDtypeStruct((n, LANES), jnp.int32),
                   jax.ShapeDtypeStruct((1, N_EXPERTS), F32)],
        compiler_params=_cparams("arbitrary"),
        name="merge_route",
    )(x2, o_na, h_f, h_b, o_pre, g_na, g_ml, mod4, mod4, mod4,
      g_head.reshape(1, ML_WIDTH), wbna, wbml, wout, g_post.reshape(1, d), g_pre.reshape(1, d),
      w_router_hi, (w_router - w_router_hi.astype(F32)).astype(BF16), b_router.reshape(1, N_EXPERTS))


def _dispatch_kernel(pad_end_ref, padded_ref, dest_ref, h_ref, xs_ref, zero_scr, sem):
    tm = h_ref.shape[0]
    blk = zero_scr.shape[0]

    @pl.when(pl.program_id(0) == 0)
    def _():
        zero_scr[...] = jnp.zeros_like(zero_scr)

        def zero_copy(e):
            first = pl.multiple_of(pad_end_ref[e] - blk, blk)
            return pltpu.make_async_copy(zero_scr, xs_ref.at[pl.ds(first, blk)], sem)

        for e in range(N_EXPERTS):
            @pl.when(padded_ref[e] > 0)
            def _():
                zero_copy(e).start()
        for e in range(N_EXPERTS):
            @pl.when(padded_ref[e] > 0)
            def _():
                zero_copy(e).wait()

        def tail_copy(b):
            return pltpu.make_async_copy(zero_scr, xs_ref.at[pl.ds(pl.multiple_of(b * blk, blk), blk)], sem)

        def tail_start(b, carry):
            tail_copy(b).start()
            return carry

        def tail_wait(b, carry):
            tail_copy(b).wait()
            return carry

        first_unused = pad_end_ref[N_EXPERTS - 1] // blk
        lax.fori_loop(first_unused, xs_ref.shape[0] // blk, tail_start, 0)
        lax.fori_loop(first_unused, xs_ref.shape[0] // blk, tail_wait, 0)

    tok_per_row = LANES // TOP_K

    def row_copy(r, c):
        return pltpu.make_async_copy(h_ref.at[pl.ds(r * tok_per_row + c // TOP_K, 1)],
                                     xs_ref.at[pl.ds(dest_ref[r, c], 1)], sem)

    for r in range(tm // tok_per_row):
        for c in range(LANES):
            row_copy(r, c).start(priority=c % 2)
    for r in range(tm // tok_per_row):
        for c in range(LANES):
            row_copy(r, c).wait()


def _dispatch(pad_end, padded, dest2, h2, n_rows):
    n = h2.shape[0]
    tm = MOVE_ROWS
    idx_rows = tm * TOP_K // LANES
    row_tile = h2.shape[1:]
    grid_spec = pltpu.PrefetchScalarGridSpec(
        num_scalar_prefetch=2,
        grid=(n // tm,),
        in_specs=[pl.BlockSpec((idx_rows, LANES), lambda i, pe, pd: (i, 0), memory_space=pltpu.SMEM),
                  pl.BlockSpec((tm,) + row_tile, lambda i, pe, pd: (i, 0))],
        out_specs=pl.BlockSpec(memory_space=pl.ANY),
        scratch_shapes=[pltpu.VMEM((EXPERT_ROWS,) + row_tile, h2.dtype), pltpu.SemaphoreType.DMA(())],
    )
    return pl.pallas_call(
        _dispatch_kernel,
        grid_spec=grid_spec,
        out_shape=jax.ShapeDtypeStruct((n_rows,) + row_tile, h2.dtype),
        compiler_params=_cparams("arbitrary"),
        name="moe_dispatch",
    )(pad_end, padded, dest2, h2)


def _expert_kernel(blk_e_ref, n_used_ref, x_ref, wg_ref, bg_ref, wl_ref, bl_ref, wd_ref, bd_ref, y_ref,
                   wg_s, wl_s, wd_s):
    i = pl.program_id(0)
    prev = blk_e_ref[jnp.maximum(i - 1, 0)]
    changed = jnp.logical_or(i == 0, blk_e_ref[i] != prev)
    used = i < n_used_ref[0]

    @pl.when(jnp.logical_and(used, changed))
    def _():
        wg_s[...] = wg_ref[...].astype(BF16)
        wl_s[...] = wl_ref[...].astype(BF16)
        wd_s[...] = wd_ref[...].astype(BF16)

    @pl.when(used)
    def _():
        xb = x_ref[...].astype(BF16)
        g = jnp.dot(xb, wg_s[...], preferred_element_type=F32) + bg_ref[...]
        l = jnp.dot(xb, wl_s[...], preferred_element_type=F32) + bl_ref[...]
        g = jnp.minimum(g, SWIGLU_LIMIT)
        l = jnp.clip(l, -SWIGLU_LIMIT, SWIGLU_LIMIT)
        a = g * jax.nn.sigmoid(SWIGLU_ALPHA * g) * (l + 1.0)
        y_ref[...] = jnp.dot(a.astype(BF16), wd_s[...], preferred_element_type=F32) + bd_ref[...]

    @pl.when(jnp.logical_not(used))
    def _():
        y_ref[...] = jnp.zeros_like(y_ref)


def _experts(blk_e, n_used, xs, w_gate, b_gate, w_lin, b_lin, w_down, b_down):
    n_rows = xs.shape[0]
    row_tile = xs.shape[1:]
    e, d, f = w_gate.shape
    tm = EXPERT_ROWS

    def w_spec(shape):
        return pl.BlockSpec((None,) + shape, lambda i, be, nu: (be[i], 0, 0))

    grid_spec = pltpu.PrefetchScalarGridSpec(
        num_scalar_prefetch=2,
        grid=(n_rows // tm,),
        in_specs=[pl.BlockSpec((tm,) + row_tile, lambda i, be, nu: (i, 0)),
                  w_spec((d, f)), w_spec((1, f)), w_spec((d, f)), w_spec((1, f)),
                  w_spec((f, d)), w_spec((1, d))],
        out_specs=pl.BlockSpec((tm,) + row_tile, lambda i, be, nu: (i, 0)),
        scratch_shapes=[pltpu.VMEM((d, f), BF16), pltpu.VMEM((d, f), BF16), pltpu.VMEM((f, d), BF16)],
    )
    return pl.pallas_call(
        _expert_kernel,
        grid_spec=grid_spec,
        out_shape=jax.ShapeDtypeStruct((n_rows,) + row_tile, F32),
        compiler_params=_cparams("arbitrary"),
        name="moe_experts",
    )(blk_e, n_used, xs, w_gate, b_gate.reshape(e, 1, f), w_lin, b_lin.reshape(e, 1, f),
      w_down, b_down.reshape(e, 1, d))


def _combine_kernel(dest_ref, x1_ref, w_ref, gtf_ref, gpost_ref, y_ref, o_ref, buf, sem):
    tm = x1_ref.shape[0]

    tok_per_row = LANES // TOP_K

    def row_copy(r, c):
        return pltpu.make_async_copy(y_ref.at[pl.ds(dest_ref[r, c], 1)],
                                     buf.at[c % TOP_K, pl.ds(r * tok_per_row + c // TOP_K, 1)], sem)

    for r in range(tm // tok_per_row):
        for c in range(LANES):
            row_copy(r, c).start(priority=c % 2)
    for r in range(tm // tok_per_row):
        for c in range(LANES):
            row_copy(r, c).wait()
    w = w_ref[...]
    ffn = (buf[0] * w[:, 0:1] + buf[1] * w[:, 1:2]) + (buf[2] * w[:, 2:3] + buf[3] * w[:, 3:4])
    o_ref[...] = x1_ref[...] + gtf_ref[...] * (_rms(ffn) * gpost_ref[...])


def _combine(dest2, x1, top_w, mod4, seq, g_post, y):
    n, d = x1.shape
    tm = MOVE_ROWS
    per_seq = seq // tm
    idx_rows = tm * TOP_K // LANES
    return pl.pallas_call(
        _combine_kernel,
        grid=(n // tm,),
        in_specs=[pl.BlockSpec((idx_rows, LANES), lambda i: (i, 0), memory_space=pltpu.SMEM),
                  pl.BlockSpec((tm, d), lambda i: (i, 0)),
                  pl.BlockSpec((tm, LANES), lambda i: (i, 0)),
                  pl.BlockSpec((None, None, 1, d), lambda i: (i // per_seq, 5, 0, 0)),
                  _const_spec((1, d)),
                  pl.BlockSpec(memory_space=pl.ANY)],
        out_specs=pl.BlockSpec((tm, d), lambda i: (i, 0)),
        out_shape=jax.ShapeDtypeStruct((n, d), F32),
        scratch_shapes=[pltpu.VMEM((TOP_K, tm) + y.shape[1:], F32), pltpu.SemaphoreType.DMA(())],
        compiler_params=_cparams("arbitrary"),
        name="moe_combine",
    )(dest2, x1, top_w, mod4, g_post.reshape(1, d), y)


def _rope_partner(w):
    n_freq = ML_QK_DIM // 4
    d = w.shape[0]
    w4 = w.reshape(d, -1, 2, n_freq)
    return jnp.stack([-w4[:, :, 1], w4[:, :, 0]], axis=2).reshape(w.shape)


def _rope_tables(seq):
    n_freq = ML_QK_DIM // 4
    t = jnp.arange(seq)
    row = (t // GRID_W).astype(F32)
    col = (t % GRID_W).astype(F32)
    inv_freq = ROPE_BASE ** (-jnp.arange(n_freq, dtype=F32) / n_freq)
    ang = jnp.concatenate([row[:, None] * inv_freq] * 2 + [col[:, None] * inv_freq] * 2, axis=-1)
    cos = jnp.tile(jnp.cos(ang), (1, ML_HEADS))
    sin = jnp.tile(jnp.sin(ang), (1, ML_HEADS))
    return cos, sin


def _layer(x, ctx, mod4, g_mix_pre, g_mix_post, g_ffn_pre, g_ffn_post, w_in, b_gates, rpb, g_head,
           w_branch_na, w_branch_ml, w_out, w_router, b_router, w_gate, b_gate, w_lin, b_lin, w_down, b_down):
    b, s, d = x.shape
    n = b * s
    n_ctx = ctx.shape[1]
    x2 = x.reshape(n, d)

    ctx_cols = (NA_WIDTH, NA_WIDTH, ML_QK_WIDTH, ML_WIDTH, N_GATES)
    lat_cols = (NA_WIDTH, ML_QK_WIDTH, ML_WIDTH, d, d)
    bounds = np.cumsum(ctx_cols + lat_cols)[:-1].tolist()
    (w_nak, w_nav, w_mk, w_mv, w_g, w_naq, w_mq, w_mo, w_gna, w_gml) = jnp.split(w_in, bounds, axis=-1)
    w_naq = w_naq * (NA_HEAD_DIM ** -0.5 * LOG2_E)
    w_mk = w_mk * ML_QK_DIM ** -0.5
    bf = lambda a: a.astype(BF16)
    lat_w = [bf(w_naq), bf(w_nak), bf(w_nav), bf(w_mq), bf(_rope_partner(w_mq)), bf(w_mk),
             bf(_rope_partner(w_mk)), bf(w_mv), bf(w_mo), bf(w_gna), bf(w_gml), bf(w_g), bf(w_g.T)]
    ctx_w = [bf(w_nak), bf(w_nav), bf(w_mk), bf(w_mv), bf(w_g), bf(w_g.T)]
    bg_col = b_gates.reshape(1, N_GATES).astype(F32)
    bg_row = b_gates.reshape(N_GATES, 1).astype(F32)
    cos, sin = _rope_tables(s)
    per_seq = s // PROJ_ROWS

    (na_q, na_k, na_v, ml_q, ml_k, ml_v, ml_o, gate_na, gate_ml, g_col, g_row) = _project(
        x2, mod4, lambda i: i // per_seq, g_mix_pre, bg_col, bg_row, lat_w, (cos, sin, per_seq), True)
    (na_kc, na_vc, ml_kc, ml_vc, gc_col, gc_row) = _project(
        ctx.reshape(b * n_ctx, d), mod4, lambda i: b, g_mix_pre, bg_col, bg_row, ctx_w, None, False)

    def seq3(a, length):
        return a.reshape(b, length, a.shape[-1])

    o_na = _neighbourhood_attention(seq3(na_q, s), seq3(na_k, s), seq3(na_v, s),
                                    seq3(na_kc, n_ctx), seq3(na_vc, n_ctx), rpb)
    ml_args = (seq3(ml_q, s), seq3(ml_k, s), seq3(ml_v, s), seq3(g_col, s), g_row,
               seq3(ml_kc, n_ctx), seq3(ml_vc, n_ctx), seq3(gc_col, n_ctx), gc_row)
    h_f = _mlstm_direction(*ml_args, reverse=False)
    h_b = _mlstm_direction(*ml_args, reverse=True)

    x1, h2, top_e, top_w, rank, counts = _merge_and_route(
        x2, o_na.reshape(n, NA_WIDTH), h_f.reshape(n, ML_WIDTH), h_b.reshape(n, ML_WIDTH), ml_o,
        gate_na, gate_ml, mod4, s, g_head, bf(w_branch_na), bf(w_branch_ml), bf(w_out),
        g_mix_post, g_ffn_pre, w_router, b_router)

    tm = EXPERT_ROWS
    counts = counts.reshape(N_EXPERTS).astype(jnp.int32)
    padded = (counts + tm - 1) // tm * tm
    pad_end = jnp.cumsum(padded)
    pad_start = pad_end - padded
    n_rows = n * TOP_K + N_EXPERTS * tm
    n_blocks = n_rows // tm
    e_sel = top_e[:, :TOP_K, None] == jnp.arange(N_EXPERTS, dtype=jnp.int32)
    dest = jnp.sum(jnp.where(e_sel, pad_start, 0), axis=-1) + rank[:, :TOP_K]
    dest2 = dest.reshape(n * TOP_K // LANES, LANES).astype(jnp.int32)
    blk_start = jnp.arange(n_blocks, dtype=jnp.int32) * tm
    blk_e = jnp.minimum(jnp.sum(blk_start[:, None] >= pad_end[None, :], axis=1), N_EXPERTS - 1).astype(jnp.int32)
    n_used = (pad_end[-1:] // tm).astype(jnp.int32)

    xs = _dispatch(pad_end.astype(jnp.int32), padded.astype(jnp.int32), dest2, h2, n_rows)
    y = _experts(blk_e, n_used, xs, w_gate, b_gate, w_lin, b_lin, w_down, b_down)
    out = _combine(dest2, x1, top_w, mod4, s, g_ffn_post, y)
    return out.reshape(b, s, d)


def kernel(x, c, ctx, c_ctx, w_ada, b_ada, g_mix_pre, g_mix_post, g_ffn_pre, g_ffn_post, w_in, b_mlstm_gates,
           rpb, g_mlstm_head, w_branch_na, w_branch_ml, w_out, w_router, b_router, w_gate, b_gate, w_lin,
           b_lin, w_down, b_down):
    b, s, d = x.shape
    depth = w_ada.shape[0]
    pad = (-(b + 1)) % 8
    c_all = jnp.concatenate([c, c_ctx[None, :], jnp.zeros((pad, d), c.dtype)], axis=0)
    for layer in range(depth):
        mod = _ada(c_all, w_ada[layer], b_ada[layer])
        mod4 = mod.reshape(mod.shape[0], 6, 1, d)
        x = _layer(x, ctx, mod4, g_mix_pre[layer], g_mix_post[layer], g_ffn_pre[layer], g_ffn_post[layer],
                   w_in[layer], b_mlstm_gates[layer], rpb[layer], g_mlstm_head[layer], w_branch_na[layer],
                   w_branch_ml[layer], w_out[layer], w_router[layer], b_router[layer], w_gate[layer],
                   b_gate[layer], w_lin[layer], b_lin[layer], w_down[layer], b_down[layer])
    return x
```

```python
import functools

import numpy as np
import jax
import jax.numpy as jnp
from jax import lax
from jax.experimental import pallas as pl
from jax.experimental.pallas import tpu as pltpu

F32 = jnp.float32
BF16 = jnp.bfloat16
HIGHEST = lax.Precision.HIGHEST

GRID_W = 64
NA_HEADS = 8
NA_HEAD_DIM = 64
NA_WIDTH = NA_HEADS * NA_HEAD_DIM
WIN_H = 8
WIN_W = 16
ML_HEADS = 4
ML_QK_DIM = 64
ML_V_DIM = 128
ML_QK_WIDTH = ML_HEADS * ML_QK_DIM
ML_WIDTH = ML_HEADS * ML_V_DIM
N_GATES = 4 * ML_HEADS
GATE_SOFTCAP = 15.0
ROPE_BASE = 10000.0
N_EXPERTS = 32
TOP_K = 4
SWIGLU_ALPHA = 1.702
SWIGLU_LIMIT = 7.0
NORM_EPS = 1e-6
NEG_INF = -1e30
LOG2_E = 1.4426950408889634

LANES = 128
NA_QROWS = 4
NA_WROWS = 12
ML_CHUNK = 256
PROJ_ROWS = 512
MERGE_ROWS = 256
EXPERT_ROWS = 512
MOVE_ROWS = 256
VMEM_LIMIT = 56 * 1024 * 1024

NT_DIMS = (((1,), (1,)), ((), ()))
TN_DIMS = (((0,), (0,)), ((), ()))


def _cparams(*sem):
    return pltpu.CompilerParams(dimension_semantics=sem, vmem_limit_bytes=VMEM_LIMIT)


def _rms(x):
    return x * lax.rsqrt(jnp.mean(x * x, axis=-1, keepdims=True) + NORM_EPS)


def _ada_kernel(c_ref, w_ref, b_ref, o_ref):
    c = c_ref[...]
    s = c * jax.nn.sigmoid(c)
    o_ref[...] = jnp.dot(s, w_ref[...], preferred_element_type=F32, precision=HIGHEST) + b_ref[...]


def _ada(c_all, w_ada, b_ada):
    rows, d = c_all.shape
    n_out = w_ada.shape[1]
    tn = 1536
    return pl.pallas_call(
        _ada_kernel,
        grid=(n_out // tn,),
        in_specs=[pl.BlockSpec((rows, d), lambda j: (0, 0)),
                  pl.BlockSpec((d, tn), lambda j: (0, j)),
                  pl.BlockSpec((1, tn), lambda j: (0, j))],
        out_specs=pl.BlockSpec((rows, tn), lambda j: (0, j)),
        out_shape=jax.ShapeDtypeStruct((rows, n_out), F32),
        compiler_params=_cparams("arbitrary"),
        name="ada_mod",
    )(c_all, w_ada, b_ada.reshape(1, n_out))


def _gate_logs(g, is_forget):
    g = GATE_SOFTCAP * jnp.tanh(g / GATE_SOFTCAP)
    log_sig = jnp.minimum(g, 0.0) - jnp.log(1.0 + jnp.exp(-jnp.abs(g)))
    return jnp.where(is_forget, log_sig, g)


def _proj_kernel(*refs, latent):
    if latent:
        (x_ref, g_ref, sc_ref, sh_ref, cos_ref, sin_ref, cos_t_ref, sin_t_ref, bgc_ref, bgr_ref,
         w_naq, w_nak, w_nav, w_mq, w_mqp, w_mk, w_mkp, w_mv, w_mo, w_gna, w_gml, w_gc, w_gr,
         o_naq, o_nak, o_nav, o_mq, o_mk, o_mv, o_mo, o_gna, o_gml, o_gc, o_gr) = refs
    else:
        (x_ref, g_ref, sc_ref, sh_ref, bgc_ref, bgr_ref,
         w_nak, w_nav, w_mk, w_mv, w_gc, w_gr,
         o_nak, o_nav, o_mk, o_mv, o_gc, o_gr) = refs
    x = x_ref[...]
    h = _rms(x) * g_ref[...]
    h = h * (1.0 + sc_ref[...]) + sh_ref[...]
    hb = h.astype(BF16)

    def mm(w_ref):
        return jnp.dot(hb, w_ref[...], preferred_element_type=F32)

    def mm_t(w_ref):
        return lax.dot_general(w_ref[...], hb, NT_DIMS, preferred_element_type=F32)

    o_nak[...] = mm(w_nak).astype(BF16)
    o_nav[...] = mm(w_nav).astype(BF16)
    o_mv[...] = mm(w_mv).astype(BF16)
    if latent:
        cos = cos_ref[...]
        sin = sin_ref[...]
        o_naq[...] = mm(w_naq).astype(BF16)
        o_mq[...] = (mm(w_mq) * cos + mm(w_mqp) * sin).astype(BF16)
        o_mk[...] = (mm_t(w_mk) * cos_t_ref[...] + mm_t(w_mkp) * sin_t_ref[...]).astype(BF16)
        o_mo[...] = mm(w_mo).astype(BF16)
        o_gna[...] = mm(w_gna).astype(BF16)
        o_gml[...] = mm(w_gml).astype(BF16)
    else:
        o_mk[...] = mm_t(w_mk).astype(BF16)
    gc = mm(w_gc) + bgc_ref[...]
    col_id = lax.broadcasted_iota(jnp.int32, gc.shape, 1)
    o_gc[...] = _gate_logs(gc, (col_id // ML_HEADS) % 2 == 1)
    gr = mm_t(w_gr) + bgr_ref[...]
    row_id = lax.broadcasted_iota(jnp.int32, gr.shape, 0)
    o_gr[...] = _gate_logs(gr, (row_id // ML_HEADS) % 2 == 1)


def _const_spec(shape):
    nd = len(shape)
    return pl.BlockSpec(shape, lambda i, _nd=nd: (0,) * _nd)


def _project(x2, mod4, mod_row_fn, g_pre, bg_col, bg_row, weights, tables, latent):
    n, d = x2.shape
    tm = PROJ_ROWS
    grid = (n // tm,)

    def mod_spec(j):
        return pl.BlockSpec((None, None, 1, d), lambda i, _j=j: (mod_row_fn(i), _j, 0, 0))

    in_specs = [pl.BlockSpec((tm, d), lambda i: (i, 0)), _const_spec((1, d)), mod_spec(1), mod_spec(0)]
    args = [x2, g_pre.reshape(1, d), mod4, mod4]
    if latent:
        cos, sin, tiles_per_seq = tables
        in_specs += [pl.BlockSpec((tm, ML_QK_WIDTH), lambda i: (i % tiles_per_seq, 0))] * 2
        in_specs += [pl.BlockSpec((ML_QK_WIDTH, tm), lambda i: (0, i % tiles_per_seq))] * 2
        args += [cos, sin, cos.T, sin.T]
    in_specs += [_const_spec(bg_col.shape), _const_spec(bg_row.shape)]
    args += [bg_col, bg_row]
    for w in weights:
        in_specs.append(_const_spec(w.shape))
        args.append(w)

    def out(width, dtype=BF16):
        return (jax.ShapeDtypeStruct((n, width), dtype), pl.BlockSpec((tm, width), lambda i: (i, 0)))

    def out_t(width, dtype=BF16):
        return (jax.ShapeDtypeStruct((width, n), dtype), pl.BlockSpec((width, tm), lambda i: (0, i)))

    if latent:
        outs = [out(NA_WIDTH), out(NA_WIDTH), out(NA_WIDTH), out(ML_QK_WIDTH), out_t(ML_QK_WIDTH),
                out(ML_WIDTH), out(ML_WIDTH), out(d), out(d), out(N_GATES, F32)]
    else:
        outs = [out(NA_WIDTH), out(NA_WIDTH), out_t(ML_QK_WIDTH), out(ML_WIDTH), out(N_GATES, F32)]
    outs.append(out_t(N_GATES, F32))
    return pl.pallas_call(
        functools.partial(_proj_kernel, latent=latent),
        grid=grid,
        in_specs=in_specs,
        out_specs=[o[1] for o in outs],
        out_shape=[o[0] for o in outs],
        compiler_params=_cparams("arbitrary"),
        name="in_proj_latent" if latent else "in_proj_ctx",
    )(*args)


def _na_window_start(r0, rows):
    return jnp.clip(r0 - WIN_H // 2, 0, rows - NA_WROWS)


def _na_classes(rows):
    keys, group_class = [], []
    for r0 in range(0, rows, NA_QROWS):
        start = min(max(r0 - WIN_H // 2, 0), rows - NA_WROWS)
        first = tuple(min(max(r0 + i - WIN_H // 2, 0), rows - WIN_H) - start for i in range(NA_QROWS))
        assert all(0 <= f and f + WIN_H <= NA_WROWS for f in first)
        key = (r0 - start, first)
        if key not in keys:
            keys.append(key)
        group_class.append(keys.index(key))
    return keys, np.asarray(group_class, np.int32)


def _na_bias_table(rpb, rows):
    keys, group_class = _na_classes(rows)
    qc = np.arange(GRID_W)[:, None]
    kc = np.arange(GRID_W)[None, :]
    cs = np.clip(qc - WIN_W // 2, 0, GRID_W - WIN_W)
    col_ok = (kc >= cs) & (kc < cs + WIN_W)
    dc = np.clip(kc - qc, -(WIN_W - 1), WIN_W - 1) + (WIN_W - 1)
    i = np.arange(NA_QROWS)[:, None]
    j = np.arange(NA_WROWS)[None, :]
    sel_r, row_ok = [], []
    for off, first in keys:
        f = np.asarray(first)[:, None]
        ok = (j >= f) & (j < f + WIN_H)
        dr = j - off - i + (WIN_H - 1)
        sel_r.append(((dr[:, :, None] == np.arange(2 * WIN_H - 1)) & ok[:, :, None]).astype(np.float32))
        row_ok.append(ok)
    sel_r = np.stack(sel_r)
    valid = np.stack(row_ok)[:, None, :, None, :, None] & col_ok[None, None, None, :, None, :]
    sel_c = (dc[:, :, None] == np.arange(2 * WIN_W - 1)).astype(np.float32)
    t = jnp.einsum('hrc,qkc->hrqk', rpb.astype(F32), sel_c, precision=HIGHEST)
    bias = jnp.einsum('hrqk,xijr->xhiqjk', t, sel_r, precision=HIGHEST)
    bias = jnp.where(valid, bias * LOG2_E, NEG_INF)
    bias = bias.reshape(len(keys), NA_HEADS, NA_QROWS * GRID_W, NA_WROWS * GRID_W)
    return bias.astype(BF16), jnp.asarray(group_class)


def _na_kernel(cls_ref, q_ref, k_ref, v_ref, kc_ref, vc_ref, bias_ref, o_ref, *, rows):
    del cls_ref
    r0 = pl.program_id(1) * NA_QROWS
    start = pl.multiple_of(_na_window_start(r0, rows) * GRID_W, GRID_W)
    n_win = NA_WROWS * GRID_W
    low = lax.broadcasted_iota(jnp.int32, (1, LANES), 1) < NA_HEAD_DIM
    for pair in range(NA_HEADS // 2):
        sl = slice(pair * LANES, (pair + 1) * LANES)
        q2 = q_ref[:, sl]
        k2 = k_ref[pl.ds(start, n_win), sl]
        v2 = v_ref[pl.ds(start, n_win), sl]
        kc2 = kc_ref[:, sl]
        vc2 = vc_ref[:, sl]
        halves = []
        for half in range(2):
            keep = low if half == 0 else jnp.logical_not(low)
            qh = jnp.where(keep, q2, jnp.zeros_like(q2))
            s_loc = (lax.dot_general(qh, k2, NT_DIMS, preferred_element_type=F32)
                     + bias_ref[2 * pair + half].astype(F32))
            s_ctx = lax.dot_general(qh, kc2, NT_DIMS, preferred_element_type=F32)
            m = jnp.maximum(jnp.max(s_loc, axis=-1, keepdims=True), jnp.max(s_ctx, axis=-1, keepdims=True))
            p_loc = jnp.exp2(s_loc - m)
            p_ctx = jnp.exp2(s_ctx - m)
            denom = jnp.sum(p_loc, axis=-1, keepdims=True) + jnp.sum(p_ctx, axis=-1, keepdims=True)
            o = (jnp.dot(p_loc.astype(BF16), v2, preferred_element_type=F32)
                 + jnp.dot(p_ctx.astype(BF16), vc2, preferred_element_type=F32))
            halves.append(o / denom)
        o_ref[:, sl] = jnp.where(low, halves[0], halves[1]).astype(BF16)


def _neighbourhood_attention(q, k, v, kc, vc, rpb):
    b, s, w = q.shape
    rows = s // GRID_W
    n_ctx = kc.shape[1]
    bias, group_class = _na_bias_table(rpb, rows)
    nq = NA_QROWS * GRID_W
    grid_spec = pltpu.PrefetchScalarGridSpec(
        num_scalar_prefetch=1,
        grid=(b, rows // NA_QROWS),
        in_specs=[pl.BlockSpec((None, nq, w), lambda bi, g, cls: (bi, g, 0)),
                  pl.BlockSpec((None, s, w), lambda bi, g, cls: (bi, 0, 0)),
                  pl.BlockSpec((None, s, w), lambda bi, g, cls: (bi, 0, 0)),
                  pl.BlockSpec((None, n_ctx, w), lambda bi, g, cls: (bi, 0, 0)),
                  pl.BlockSpec((None, n_ctx, w), lambda bi, g, cls: (bi, 0, 0)),
                  pl.BlockSpec((None,) + bias.shape[1:], lambda bi, g, cls: (cls[g], 0, 0, 0))],
        out_specs=pl.BlockSpec((None, nq, w), lambda bi, g, cls: (bi, g, 0)),
    )
    return pl.pallas_call(
        functools.partial(_na_kernel, rows=rows),
        grid_spec=grid_spec,
        out_shape=jax.ShapeDtypeStruct((b, s, w), BF16),
        compiler_params=_cparams("arbitrary", "arbitrary"),
        name="na_attention",
    )(group_class, q, k, v, kc, vc, bias)


def _split3(x):
    hi = x.astype(BF16)
    r1 = x - hi.astype(F32)
    mid = r1.astype(BF16)
    lo = (r1 - mid.astype(F32)).astype(BF16)
    return hi, mid, lo


def _mlstm_direction_step(reverse, is_ctx, q, kt, v, gc, gr, o_ref, c_scr, m_scr):
    li_base = 2 * ML_HEADS if reverse else 0
    lf_base = li_base + ML_HEADS
    length = kt.shape[1]
    t = lax.broadcasted_iota(jnp.int32, (length, length), 0)
    s = lax.broadcasted_iota(jnp.int32, (length, length), 1)
    valid = (s >= t) if reverse else (s <= t)
    valid_t = (t >= s) if reverse else (t <= s)
    b_col = sum(jnp.dot(valid.astype(BF16), p, preferred_element_type=F32) for p in _split3(gc))
    b_row = sum(jnp.dot(p, valid_t.astype(BF16), preferred_element_type=F32) for p in _split3(gr))
    low = lax.broadcasted_iota(jnp.int32, (1, LANES), 1) < ML_QK_DIM
    ones = jnp.ones((length, ML_V_DIM), BF16)

    for h in range(ML_HEADS):
        pair, half = divmod(h, 2)
        rows = slice(half * ML_QK_DIM, (half + 1) * ML_QK_DIM)
        li_r = gr[li_base + h:li_base + h + 1, :]
        lf_r = gr[lf_base + h:lf_base + h + 1, :]
        br = b_row[lf_base + h:lf_base + h + 1, :]
        m_prev = m_scr[h]
        v_ext = jnp.concatenate([v[:, h * ML_V_DIM:(h + 1) * ML_V_DIM], ones], axis=-1)
        kt_pair = kt[pair * LANES:(pair + 1) * LANES, :]

        if not is_ctx:
            bc = b_col[:, lf_base + h:lf_base + h + 1]
            q2 = q[:, pair * LANES:(pair + 1) * LANES]
            qh = jnp.where(low if half == 0 else jnp.logical_not(low), q2, jnp.zeros_like(q2))
            d_mat = jnp.where(valid, bc - br + li_r, NEG_INF)
            m_inter = bc + m_prev
            m_t = jnp.maximum(m_inter, jnp.max(d_mat, axis=-1, keepdims=True))
            w_intra = jnp.exp(d_mat - m_t)
            w_inter = jnp.exp(m_inter - m_t)
            sc = jnp.dot(qh, kt_pair, preferred_element_type=F32) * w_intra
            both = (w_inter * jnp.dot(qh, c_scr[pair].astype(BF16), preferred_element_type=F32)
                    + jnp.dot(sc.astype(BF16), v_ext, preferred_element_type=F32))
            num = both[:, :ML_V_DIM]
            den = both[:, ML_V_DIM:]
            o_ref[:, h * ML_V_DIM:(h + 1) * ML_V_DIM] = (
                num / jnp.maximum(jnp.abs(den), jnp.exp(-m_t))).astype(BF16)

        total = jnp.sum(lf_r, axis=-1, keepdims=True)
        m_new = jnp.maximum(total + m_prev, jnp.max(total - br + li_r, axis=-1, keepdims=True))
        w_src = jnp.exp(total - br + li_r - m_new)
        w_carry = jnp.exp(total + m_prev - m_new)
        kw = (kt_pair[rows, :].astype(F32) * w_src).astype(BF16)
        c_scr[pair, rows, :] = w_carry * c_scr[pair, rows, :] + jnp.dot(kw, v_ext, preferred_element_type=F32)
        m_scr[h] = m_new


def _mlstm_kernel(qf_ref, kf_ref, vf_ref, gcf_ref, grf_ref, qb_ref, kb_ref, vb_ref, gcb_ref, grb_ref,
                  kc_ref, vc_ref, gcc_ref, gcr_ref, of_ref, ob_ref, cf_scr, mf_scr, cb_scr, mb_scr):
    step = pl.program_id(1)

    @pl.when(step == 0)
    def _():
        for ref in (cf_scr, mf_scr, cb_scr, mb_scr):
            ref[...] = jnp.zeros_like(ref)
        kt = kc_ref[...]
        v = vc_ref[...]
        gc = gcc_ref[...]
        gr = gcr_ref[...]
        _mlstm_direction_step(False, True, None, kt, v, gc, gr, None, cf_scr, mf_scr)
        _mlstm_direction_step(True, True, None, kt, v, gc, gr, None, cb_scr, mb_scr)

    @pl.when(step > 0)
    def _():
        _mlstm_direction_step(False, False, qf_ref[...], kf_ref[...], vf_ref[...], gcf_ref[...], grf_ref[...],
                              of_ref, cf_scr, mf_scr)
        _mlstm_direction_step(True, False, qb_ref[...], kb_ref[...], vb_ref[...], gcb_ref[...], grb_ref[...],
                              ob_ref, cb_scr, mb_scr)


def _mlstm(q, k, v, gc, gr, kc, vc, gcc, gcr):
    b, s, _ = q.shape
    n_ctx = vc.shape[1]
    length = ML_CHUNK
    n_chunks = s // length

    def chunk(step, reverse):
        c = jnp.maximum(step - 1, 0)
        return (n_chunks - 1 - c) if reverse else c

    def stream_specs(reverse):
        def seq(width):
            return pl.BlockSpec((None, length, width), lambda bi, st: (bi, chunk(st, reverse), 0))

        def seq_t(width):
            return pl.BlockSpec((width, length), lambda bi, st: (0, bi * n_chunks + chunk(st, reverse)))
        return [seq(ML_QK_WIDTH), seq_t(ML_QK_WIDTH), seq(ML_WIDTH), seq(N_GATES), seq_t(N_GATES)]

    def out_spec(reverse):
        return pl.BlockSpec((None, length, ML_WIDTH), lambda bi, st: (bi, chunk(st, reverse), 0))

    state = [pltpu.VMEM((ML_HEADS // 2, 2 * ML_QK_DIM, 2 * ML_V_DIM), F32),
             pltpu.VMEM((ML_HEADS, 1, 1), F32)]
    return pl.pallas_call(
        _mlstm_kernel,
        grid=(b, n_chunks + 1),
        in_specs=stream_specs(False) + stream_specs(True) + [
            pl.BlockSpec((ML_QK_WIDTH, n_ctx), lambda bi, st: (0, bi)),
            pl.BlockSpec((None, n_ctx, ML_WIDTH), lambda bi, st: (bi, 0, 0)),
            pl.BlockSpec((None, n_ctx, N_GATES), lambda bi, st: (bi, 0, 0)),
            pl.BlockSpec((N_GATES, n_ctx), lambda bi, st: (0, bi))],
        out_specs=[out_spec(False), out_spec(True)],
        out_shape=[jax.ShapeDtypeStruct((b, s, ML_WIDTH), BF16)] * 2,
        scratch_shapes=state + state,
        compiler_params=_cparams("arbitrary", "arbitrary"),
        name="mlstm",
    )(q, k, v, gc, gr, q, k, v, gc, gr, kc, vc, gcc, gcr)


def _merge_kernel(x_ref, ona_ref, hf_ref, hb_ref, opre_ref, gna_ref, gml_ref, gtm_ref, scf_ref, shf_ref,
                  ghead_ref, wbna_ref, wbml_ref, wout_ref, gpost_ref, gpre_ref, wrh_ref, wrl_ref, br_ref,
                  x1_ref, h2_ref, tope_ref, topw_ref, rank_ref, cnt_ref):
    step = pl.program_id(0)
    tm = x_ref.shape[0]

    @pl.when(step == 0)
    def _():
        cnt_ref[...] = jnp.zeros_like(cnt_ref)

    hsum = hf_ref[...].astype(F32) + hb_ref[...].astype(F32)
    heads = [_rms(hsum[:, h * ML_V_DIM:(h + 1) * ML_V_DIM]) for h in range(ML_HEADS)]
    hn = jnp.concatenate(heads, axis=-1) * ghead_ref[...]
    o_ml = jax.nn.sigmoid(opre_ref[...].astype(F32)) * hn
    merged = (jax.nn.sigmoid(gna_ref[...].astype(F32))
              * jnp.dot(ona_ref[...], wbna_ref[...], preferred_element_type=F32)
              + jax.nn.sigmoid(gml_ref[...].astype(F32))
              * jnp.dot(o_ml.astype(BF16), wbml_ref[...], preferred_element_type=F32))
    mixed = jnp.dot(merged.astype(BF16), wout_ref[...], preferred_element_type=F32)
    x1 = x_ref[...] + gtm_ref[...] * (_rms(mixed) * gpost_ref[...])
    x1_ref[...] = x1
    h2 = _rms(x1) * gpre_ref[...] * (1.0 + scf_ref[...]) + shf_ref[...]
    h2_ref[...] = h2
    h2_hi = h2.astype(BF16)
    h2_lo = (h2 - h2_hi.astype(F32)).astype(BF16)
    logits = (jnp.dot(h2_hi, wrh_ref[...], preferred_element_type=F32)
              + (jnp.dot(h2_hi, wrl_ref[...], preferred_element_type=F32)
                 + jnp.dot(h2_lo, wrh_ref[...], preferred_element_type=F32))) + br_ref[...]

    lane = lax.broadcasted_iota(jnp.int32, logits.shape, 1)
    onehots, top_e, top_l = [], [], []
    for _ in range(TOP_K):
        best = jnp.max(logits, axis=-1, keepdims=True)
        e = jnp.min(jnp.where(logits == best, lane, N_EXPERTS), axis=-1, keepdims=True)
        hit = lane == e
        onehots.append(hit)
        top_e.append(e)
        top_l.append(best)
        logits = jnp.where(hit, -jnp.inf, logits)
    exps = [jnp.exp(l - top_l[0]) for l in top_l]
    total = exps[0] + exps[1] + exps[2] + exps[3]

    counts = (onehots[0].astype(F32) + onehots[1].astype(F32)
              + onehots[2].astype(F32) + onehots[3].astype(F32))
    t = lax.broadcasted_iota(jnp.int32, (tm, tm), 0)
    s = lax.broadcasted_iota(jnp.int32, (tm, tm), 1)
    before = jnp.dot((s < t).astype(BF16), counts.astype(BF16), preferred_element_type=F32) + cnt_ref[...]
    out_lane = lax.broadcasted_iota(jnp.int32, (tm, LANES), 1)
    e_out = jnp.zeros((tm, LANES), jnp.int32)
    w_out = jnp.zeros((tm, LANES), F32)
    r_out = jnp.zeros((tm, LANES), jnp.int32)
    for j in range(TOP_K):
        rank = jnp.sum(jnp.where(onehots[j], before, 0.0), axis=-1, keepdims=True).astype(jnp.int32)
        e_out = jnp.where(out_lane == j, top_e[j], e_out)
        w_out = jnp.where(out_lane == j, exps[j] / total, w_out)
        r_out = jnp.where(out_lane == j, rank, r_out)
    tope_ref[...] = e_out
    topw_ref[...] = w_out
    rank_ref[...] = r_out
    cnt_ref[...] += jnp.sum(counts, axis=0, keepdims=True)


def _merge_and_route(x2, o_na, h_f, h_b, o_pre, g_na, g_ml, mod4, seq, g_head, wbna, wbml, wout,
                     g_post, g_pre, w_router, b_router):
    n, d = x2.shape
    tm = MERGE_ROWS
    per_seq = seq // tm

    def rows(width):
        return pl.BlockSpec((tm, width), lambda i: (i, 0))

    def mod_spec(j):
        return pl.BlockSpec((None, None, 1, d), lambda i, _j=j: (i // per_seq, _j, 0, 0))

    w_router_hi = w_router.astype(BF16)
    return pl.pallas_call(
        _merge_kernel,
        grid=(n // tm,),
        in_specs=[rows(d), rows(NA_WIDTH), rows(ML_WIDTH), rows(ML_WIDTH), rows(ML_WIDTH), rows(d), rows(d),
                  mod_spec(2), mod_spec(4), mod_spec(3),
                  _const_spec((1, ML_WIDTH)), _const_spec(wbna.shape), _const_spec(wbml.shape),
                  _const_spec(wout.shape), _const_spec((1, d)), _const_spec((1, d)),
                  _const_spec(w_router.shape), _const_spec(w_router.shape), _const_spec((1, N_EXPERTS))],
        out_specs=[rows(d), rows(d), rows(LANES), rows(LANES), rows(LANES),
                   pl.BlockSpec((1, N_EXPERTS), lambda i: (0, 0))],
        out_shape=[jax.ShapeDtypeStruct((n, d), F32), jax.ShapeDtypeStruct((n, d), F32),
                   jax.ShapeDtypeStruct((n, LANES), jnp.int32), jax.ShapeDtypeStruct((n, LANES), F32),
                   jax.ShapeDtypeStruct((n, LANES), jnp.int32),
                   jax.ShapeDtypeStruct((1, N_EXPERTS), F32)],
        compiler_params=_cparams("arbitrary"),
        name="merge_route",
    )(x2, o_na, h_f, h_b, o_pre, g_na, g_ml, mod4, mod4, mod4,
      g_head.reshape(1, ML_WIDTH), wbna, wbml, wout, g_post.reshape(1, d), g_pre.reshape(1, d),
      w_router_hi, (w_router - w_router_hi.astype(F32)).astype(BF16), b_router.reshape(1, N_EXPERTS))


def _dispatch_kernel(pad_end_ref, padded_ref, dest_ref, h_ref, xs_ref, zero_scr, sem):
    tm = h_ref.shape[0]
    blk = zero_scr.shape[0]

    @pl.when(pl.program_id(0) == 0)
    def _():
        zero_scr[...] = jnp.zeros_like(zero_scr)

        def zero_copy(e):
            first = pl.multiple_of(pad_end_ref[e] - blk, blk)
            return pltpu.make_async_copy(zero_scr, xs_ref.at[pl.ds(first, blk)], sem)

        for e in range(N_EXPERTS):
            @pl.when(padded_ref[e] > 0)
            def _():
                zero_copy(e).start()
        for e in range(N_EXPERTS):
            @pl.when(padded_ref[e] > 0)
            def _():
                zero_copy(e).wait()

        def tail_copy(b):
            return pltpu.make_async_copy(zero_scr, xs_ref.at[pl.ds(pl.multiple_of(b * blk, blk), blk)], sem)

        def tail_start(b, carry):
            tail_copy(b).start()
            return carry

        def tail_wait(b, carry):
            tail_copy(b).wait()
            return carry

        first_unused = pad_end_ref[N_EXPERTS - 1] // blk
        lax.fori_loop(first_unused, xs_ref.shape[0] // blk, tail_start, 0)
        lax.fori_loop(first_unused, xs_ref.shape[0] // blk, tail_wait, 0)

    tok_per_row = LANES // TOP_K

    def row_copy(r, c):
        return pltpu.make_async_copy(h_ref.at[pl.ds(r * tok_per_row + c // TOP_K, 1)],
                                     xs_ref.at[pl.ds(dest_ref[r, c], 1)], sem)

    for r in range(tm // tok_per_row):
        for c in range(LANES):
            row_copy(r, c).start(priority=c % 2)
    for r in range(tm // tok_per_row):
        for c in range(LANES):
            row_copy(r, c).wait()


def _dispatch(pad_end, padded, dest2, h2, n_rows):
    n = h2.shape[0]
    tm = MOVE_ROWS
    idx_rows = tm * TOP_K // LANES
    row_tile = h2.shape[1:]
    grid_spec = pltpu.PrefetchScalarGridSpec(
        num_scalar_prefetch=2,
        grid=(n // tm,),
        in_specs=[pl.BlockSpec((idx_rows, LANES), lambda i, pe, pd: (i, 0), memory_space=pltpu.SMEM),
                  pl.BlockSpec((tm,) + row_tile, lambda i, pe, pd: (i, 0))],
        out_specs=pl.BlockSpec(memory_space=pl.ANY),
        scratch_shapes=[pltpu.VMEM((EXPERT_ROWS,) + row_tile, h2.dtype), pltpu.SemaphoreType.DMA(())],
    )
    return pl.pallas_call(
        _dispatch_kernel,
        grid_spec=grid_spec,
        out_shape=jax.ShapeDtypeStruct((n_rows,) + row_tile, h2.dtype),
        compiler_params=_cparams("arbitrary"),
        name="moe_dispatch",
    )(pad_end, padded, dest2, h2)


def _expert_kernel(blk_e_ref, n_used_ref, x_ref, wg_ref, bg_ref, wl_ref, bl_ref, wd_ref, bd_ref, y_ref,
                   wg_s, wl_s, wd_s):
    i = pl.program_id(0)
    prev = blk_e_ref[jnp.maximum(i - 1, 0)]
    changed = jnp.logical_or(i == 0, blk_e_ref[i] != prev)
    used = i < n_used_ref[0]

    @pl.when(jnp.logical_and(used, changed))
    def _():
        wg_s[...] = wg_ref[...].astype(BF16)
        wl_s[...] = wl_ref[...].astype(BF16)
        wd_s[...] = wd_ref[...].astype(BF16)

    @pl.when(used)
    def _():
        xb = x_ref[...].astype(BF16)
        g = jnp.dot(xb, wg_s[...], preferred_element_type=F32) + bg_ref[...]
        l = jnp.dot(xb, wl_s[...], preferred_element_type=F32) + bl_ref[...]
        g = jnp.minimum(g, SWIGLU_LIMIT)
        l = jnp.clip(l, -SWIGLU_LIMIT, SWIGLU_LIMIT)
        a = g * jax.nn.sigmoid(SWIGLU_ALPHA * g) * (l + 1.0)
        y_ref[...] = jnp.dot(a.astype(BF16), wd_s[...], preferred_element_type=F32) + bd_ref[...]

    @pl.when(jnp.logical_not(used))
    def _():
        y_ref[...] = jnp.zeros_like(y_ref)


def _experts(blk_e, n_used, xs, w_gate, b_gate, w_lin, b_lin, w_down, b_down):
    n_rows = xs.shape[0]
    row_tile = xs.shape[1:]
    e, d, f = w_gate.shape
    tm = EXPERT_ROWS

    def w_spec(shape):
        return pl.BlockSpec((None,) + shape, lambda i, be, nu: (be[i], 0, 0))

    grid_spec = pltpu.PrefetchScalarGridSpec(
        num_scalar_prefetch=2,
        grid=(n_rows // tm,),
        in_specs=[pl.BlockSpec((tm,) + row_tile, lambda i, be, nu: (i, 0)),
                  w_spec((d, f)), w_spec((1, f)), w_spec((d, f)), w_spec((1, f)),
                  w_spec((f, d)), w_spec((1, d))],
        out_specs=pl.BlockSpec((tm,) + row_tile, lambda i, be, nu: (i, 0)),
        scratch_shapes=[pltpu.VMEM((d, f), BF16), pltpu.VMEM((d, f), BF16), pltpu.VMEM((f, d), BF16)],
    )
    return pl.pallas_call(
        _expert_kernel,
        grid_spec=grid_spec,
        out_shape=jax.ShapeDtypeStruct((n_rows,) + row_tile, F32),
        compiler_params=_cparams("arbitrary"),
        name="moe_experts",
    )(blk_e, n_used, xs, w_gate, b_gate.reshape(e, 1, f), w_lin, b_lin.reshape(e, 1, f),
      w_down, b_down.reshape(e, 1, d))


def _combine_kernel(dest_ref, x1_ref, w_ref, gtf_ref, gpost_ref, y_ref, o_ref, buf, sem):
    tm = x1_ref.shape[0]

    tok_per_row = LANES // TOP_K

    def row_copy(r, c):
        return pltpu.make_async_copy(y_ref.at[pl.ds(dest_ref[r, c], 1)],
                                     buf.at[c % TOP_K, pl.ds(r * tok_per_row + c // TOP_K, 1)], sem)

    for r in range(tm // tok_per_row):
        for c in range(LANES):
            row_copy(r, c).start(priority=c % 2)
    for r in range(tm // tok_per_row):
        for c in range(LANES):
            row_copy(r, c).wait()
    w = w_ref[...]
    ffn = (buf[0] * w[:, 0:1] + buf[1] * w[:, 1:2]) + (buf[2] * w[:, 2:3] + buf[3] * w[:, 3:4])
    o_ref[...] = x1_ref[...] + gtf_ref[...] * (_rms(ffn) * gpost_ref[...])


def _combine(dest2, x1, top_w, mod4, seq, g_post, y):
    n, d = x1.shape
    tm = MOVE_ROWS
    per_seq = seq // tm
    idx_rows = tm * TOP_K // LANES
    return pl.pallas_call(
        _combine_kernel,
        grid=(n // tm,),
        in_specs=[pl.BlockSpec((idx_rows, LANES), lambda i: (i, 0), memory_space=pltpu.SMEM),
                  pl.BlockSpec((tm, d), lambda i: (i, 0)),
                  pl.BlockSpec((tm, LANES), lambda i: (i, 0)),
                  pl.BlockSpec((None, None, 1, d), lambda i: (i // per_seq, 5, 0, 0)),
                  _const_spec((1, d)),
                  pl.BlockSpec(memory_space=pl.ANY)],
        out_specs=pl.BlockSpec((tm, d), lambda i: (i, 0)),
        out_shape=jax.ShapeDtypeStruct((n, d), F32),
        scratch_shapes=[pltpu.VMEM((TOP_K, tm) + y.shape[1:], F32), pltpu.SemaphoreType.DMA(())],
        compiler_params=_cparams("arbitrary"),
        name="moe_combine",
    )(dest2, x1, top_w, mod4, g_post.reshape(1, d), y)


def _rope_partner(w):
    n_freq = ML_QK_DIM // 4
    d = w.shape[0]
    w4 = w.reshape(d, -1, 2, n_freq)
    return jnp.stack([-w4[:, :, 1], w4[:, :, 0]], axis=2).reshape(w.shape)


def _rope_tables(seq):
    n_freq = ML_QK_DIM // 4
    t = jnp.arange(seq)
    row = (t // GRID_W).astype(F32)
    col = (t % GRID_W).astype(F32)
    inv_freq = ROPE_BASE ** (-jnp.arange(n_freq, dtype=F32) / n_freq)
    ang = jnp.concatenate([row[:, None] * inv_freq] * 2 + [col[:, None] * inv_freq] * 2, axis=-1)
    cos = jnp.tile(jnp.cos(ang), (1, ML_HEADS))
    sin = jnp.tile(jnp.sin(ang), (1, ML_HEADS))
    return cos, sin


def _layer(x, ctx, mod4, g_mix_pre, g_mix_post, g_ffn_pre, g_ffn_post, w_in, b_gates, rpb, g_head,
           w_branch_na, w_branch_ml, w_out, w_router, b_router, w_gate, b_gate, w_lin, b_lin, w_down, b_down):
    b, s, d = x.shape
    n = b * s
    n_ctx = ctx.shape[1]
    x2 = x.reshape(n, d)

    ctx_cols = (NA_WIDTH, NA_WIDTH, ML_QK_WIDTH, ML_WIDTH, N_GATES)
    lat_cols = (NA_WIDTH, ML_QK_WIDTH, ML_WIDTH, d, d)
    bounds = np.cumsum(ctx_cols + lat_cols)[:-1].tolist()
    (w_nak, w_nav, w_mk, w_mv, w_g, w_naq, w_mq, w_mo, w_gna, w_gml) = jnp.split(w_in, bounds, axis=-1)
    w_naq = w_naq * (NA_HEAD_DIM ** -0.5 * LOG2_E)
    w_mk = w_mk * ML_QK_DIM ** -0.5
    bf = lambda a: a.astype(BF16)
    lat_w = [bf(w_naq), bf(w_nak), bf(w_nav), bf(w_mq), bf(_rope_partner(w_mq)), bf(w_mk.T),
             bf(_rope_partner(w_mk).T), bf(w_mv), bf(w_mo), bf(w_gna), bf(w_gml), bf(w_g), bf(w_g.T)]
    ctx_w = [bf(w_nak), bf(w_nav), bf(w_mk.T), bf(w_mv), bf(w_g), bf(w_g.T)]
    bg_col = b_gates.reshape(1, N_GATES).astype(F32)
    bg_row = b_gates.reshape(N_GATES, 1).astype(F32)
    cos, sin = _rope_tables(s)
    per_seq = s // PROJ_ROWS

    (na_q, na_k, na_v, ml_q, ml_k, ml_v, ml_o, gate_na, gate_ml, g_col, g_row) = _project(
        x2, mod4, lambda i: i // per_seq, g_mix_pre, bg_col, bg_row, lat_w, (cos, sin, per_seq), True)
    (na_kc, na_vc, ml_kc, ml_vc, gc_col, gc_row) = _project(
        ctx.reshape(b * n_ctx, d), mod4, lambda i: b, g_mix_pre, bg_col, bg_row, ctx_w, None, False)

    def seq3(a, length):
        return a.reshape(b, length, a.shape[-1])

    o_na = _neighbourhood_attention(seq3(na_q, s), seq3(na_k, s), seq3(na_v, s),
                                    seq3(na_kc, n_ctx), seq3(na_vc, n_ctx), rpb)
    ml_args = (seq3(ml_q, s), ml_k, seq3(ml_v, s), seq3(g_col, s), g_row,
               ml_kc, seq3(ml_vc, n_ctx), seq3(gc_col, n_ctx), gc_row)
    h_f, h_b = _mlstm(*ml_args)

    x1, h2, top_e, top_w, rank, counts = _merge_and_route(
        x2, o_na.reshape(n, NA_WIDTH), h_f.reshape(n, ML_WIDTH), h_b.reshape(n, ML_WIDTH), ml_o,
        gate_na, gate_ml, mod4, s, g_head, bf(w_branch_na), bf(w_branch_ml), bf(w_out),
        g_mix_post, g_ffn_pre, w_router, b_router)

    tm = EXPERT_ROWS
    counts = counts.reshape(N_EXPERTS).astype(jnp.int32)
    padded = (counts + tm - 1) // tm * tm
    pad_end = jnp.cumsum(padded)
    pad_start = pad_end - padded
    n_rows = n * TOP_K + N_EXPERTS * tm
    n_blocks = n_rows // tm
    e_sel = top_e[:, :TOP_K, None] == jnp.arange(N_EXPERTS, dtype=jnp.int32)
    dest = jnp.sum(jnp.where(e_sel, pad_start, 0), axis=-1) + rank[:, :TOP_K]
    dest2 = dest.reshape(n * TOP_K // LANES, LANES).astype(jnp.int32)
    blk_start = jnp.arange(n_blocks, dtype=jnp.int32) * tm
    blk_e = jnp.minimum(jnp.sum(blk_start[:, None] >= pad_end[None, :], axis=1), N_EXPERTS - 1).astype(jnp.int32)
    n_used = (pad_end[-1:] // tm).astype(jnp.int32)

    xs = _dispatch(pad_end.astype(jnp.int32), padded.astype(jnp.int32), dest2, h2, n_rows)
    y = _experts(blk_e, n_used, xs, w_gate, b_gate, w_lin, b_lin, w_down, b_down)
    out = _combine(dest2, x1, top_w, mod4, s, g_ffn_post, y)
    return out.reshape(b, s, d)


def kernel(x, c, ctx, c_ctx, w_ada, b_ada, g_mix_pre, g_mix_post, g_ffn_pre, g_ffn_post, w_in, b_mlstm_gates,
           rpb, g_mlstm_head, w_branch_na, w_branch_ml, w_out, w_router, b_router, w_gate, b_gate, w_lin,
           b_lin, w_down, b_down):
    b, s, d = x.shape
    depth = w_ada.shape[0]
    pad = (-(b + 1)) % 8
    c_all = jnp.concatenate([c, c_ctx[None, :], jnp.zeros((pad, d), c.dtype)], axis=0)
    for layer in range(depth):
        mod = _ada(c_all, w_ada[layer], b_ada[layer])
        mod4 = mod.reshape(mod.shape[0], 6, 1, d)
        x = _layer(x, ctx, mod4, g_mix_pre[layer], g_mix_post[layer], g_ffn_pre[layer], g_ffn_post[layer],
                   w_in[layer], b_mlstm_gates[layer], rpb[layer], g_mlstm_head[layer], w_branch_na[layer],
                   w_branch_ml[layer], w_out[layer], w_router[layer], b_router[layer], w_gate[layer],
                   b_gate[layer], w_lin[layer], b_lin[layer], w_down[layer], b_down[layer])
    return x
```

```python
import functools

import numpy as np
import jax
import jax.numpy as jnp
from jax import lax
from jax.experimental import pallas as pl
from jax.experimental.pallas import tpu as pltpu

F32 = jnp.float32
BF16 = jnp.bfloat16
HIGHEST = lax.Precision.HIGHEST

GRID_W = 64
NA_HEADS = 8
NA_HEAD_DIM = 64
NA_WIDTH = NA_HEADS * NA_HEAD_DIM
WIN_H = 8
WIN_W = 16
ML_HEADS = 4
ML_QK_DIM = 64
ML_V_DIM = 128
ML_QK_WIDTH = ML_HEADS * ML_QK_DIM
ML_WIDTH = ML_HEADS * ML_V_DIM
N_GATES = 4 * ML_HEADS
GATE_SOFTCAP = 15.0
ROPE_BASE = 10000.0
N_EXPERTS = 32
TOP_K = 4
SWIGLU_ALPHA = 1.702
SWIGLU_LIMIT = 7.0
NORM_EPS = 1e-6
NEG_INF = -1e30
LOG2_E = 1.4426950408889634

LANES = 128
NA_QROWS = 4
NA_WROWS = 12
ML_CHUNK = 256
PROJ_ROWS = 512
MERGE_ROWS = 256
EXPERT_ROWS = 512
MOVE_ROWS = 256
VMEM_LIMIT = 56 * 1024 * 1024

NT_DIMS = (((1,), (1,)), ((), ()))
TN_DIMS = (((0,), (0,)), ((), ()))


def _cparams(*sem):
    return pltpu.CompilerParams(dimension_semantics=sem, vmem_limit_bytes=VMEM_LIMIT)


def _rms(x):
    return x * lax.rsqrt(jnp.mean(x * x, axis=-1, keepdims=True) + NORM_EPS)


def _ada_kernel(c_ref, w_ref, b_ref, o_ref):
    c = c_ref[...]
    s = c * jax.nn.sigmoid(c)
    o_ref[...] = jnp.dot(s, w_ref[...], preferred_element_type=F32, precision=HIGHEST) + b_ref[...]


def _ada(c_all, w_ada, b_ada):
    rows, d = c_all.shape
    n_out = w_ada.shape[1]
    tn = 1536
    return pl.pallas_call(
        _ada_kernel,
        grid=(n_out // tn,),
        in_specs=[pl.BlockSpec((rows, d), lambda j: (0, 0)),
                  pl.BlockSpec((d, tn), lambda j: (0, j)),
                  pl.BlockSpec((1, tn), lambda j: (0, j))],
        out_specs=pl.BlockSpec((rows, tn), lambda j: (0, j)),
        out_shape=jax.ShapeDtypeStruct((rows, n_out), F32),
        compiler_params=_cparams("arbitrary"),
        name="ada_mod",
    )(c_all, w_ada, b_ada.reshape(1, n_out))


def _gate_logs(g, is_forget):
    g = GATE_SOFTCAP * jnp.tanh(g / GATE_SOFTCAP)
    log_sig = jnp.minimum(g, 0.0) - jnp.log(1.0 + jnp.exp(-jnp.abs(g)))
    return jnp.where(is_forget, log_sig, g)


def _proj_kernel(*refs, latent):
    if latent:
        (x_ref, g_ref, sc_ref, sh_ref, cos_ref, sin_ref, cos_t_ref, sin_t_ref, bgc_ref, bgr_ref,
         w_naq, w_nak, w_nav, w_mq, w_mqp, w_mk, w_mkp, w_mv, w_mo, w_gna, w_gml, w_gc, w_gr,
         o_naq, o_nak, o_nav, o_mq, o_mk, o_mv, o_mo, o_gna, o_gml, o_gc, o_gr) = refs
    else:
        (x_ref, g_ref, sc_ref, sh_ref, bgc_ref, bgr_ref,
         w_nak, w_nav, w_mk, w_mv, w_gc, w_gr,
         o_nak, o_nav, o_mk, o_mv, o_gc, o_gr) = refs
    x = x_ref[...]
    h = _rms(x) * g_ref[...]
    h = h * (1.0 + sc_ref[...]) + sh_ref[...]
    hb = h.astype(BF16)

    def mm(w_ref):
        return jnp.dot(hb, w_ref[...], preferred_element_type=F32)

    def mm_t(w_ref):
        return lax.dot_general(w_ref[...], hb, NT_DIMS, preferred_element_type=F32)

    o_nak[...] = mm(w_nak).astype(BF16)
    o_nav[...] = mm(w_nav).astype(BF16)
    o_mv[...] = mm(w_mv).astype(BF16)
    if latent:
        cos = cos_ref[...]
        sin = sin_ref[...]
        o_naq[...] = mm(w_naq).astype(BF16)
        o_mq[...] = (mm(w_mq) * cos + mm(w_mqp) * sin).astype(BF16)
        o_mk[...] = (mm_t(w_mk) * cos_t_ref[...] + mm_t(w_mkp) * sin_t_ref[...]).astype(BF16)
        o_mo[...] = mm(w_mo).astype(BF16)
        o_gna[...] = mm(w_gna).astype(BF16)
        o_gml[...] = mm(w_gml).astype(BF16)
    else:
        o_mk[...] = mm_t(w_mk).astype(BF16)
    gc = mm(w_gc) + bgc_ref[...]
    col_id = lax.broadcasted_iota(jnp.int32, gc.shape, 1)
    o_gc[...] = _gate_logs(gc, (col_id // ML_HEADS) % 2 == 1)
    gr = mm_t(w_gr) + bgr_ref[...]
    row_id = lax.broadcasted_iota(jnp.int32, gr.shape, 0)
    o_gr[...] = _gate_logs(gr, (row_id // ML_HEADS) % 2 == 1)


def _const_spec(shape):
    nd = len(shape)
    return pl.BlockSpec(shape, lambda i, _nd=nd: (0,) * _nd)


def _project(x2, mod4, mod_row_fn, g_pre, bg_col, bg_row, weights, tables, latent):
    n, d = x2.shape
    tm = PROJ_ROWS
    grid = (n // tm,)

    def mod_spec(j):
        return pl.BlockSpec((None, None, 1, d), lambda i, _j=j: (mod_row_fn(i), _j, 0, 0))

    in_specs = [pl.BlockSpec((tm, d), lambda i: (i, 0)), _const_spec((1, d)), mod_spec(1), mod_spec(0)]
    args = [x2, g_pre.reshape(1, d), mod4, mod4]
    if latent:
        cos, sin, tiles_per_seq = tables
        in_specs += [pl.BlockSpec((tm, ML_QK_WIDTH), lambda i: (i % tiles_per_seq, 0))] * 2
        in_specs += [pl.BlockSpec((ML_QK_WIDTH, tm), lambda i: (0, i % tiles_per_seq))] * 2
        args += [cos, sin, cos.T, sin.T]
    in_specs += [_const_spec(bg_col.shape), _const_spec(bg_row.shape)]
    args += [bg_col, bg_row]
    for w in weights:
        in_specs.append(_const_spec(w.shape))
        args.append(w)

    def out(width, dtype=BF16):
        return (jax.ShapeDtypeStruct((n, width), dtype), pl.BlockSpec((tm, width), lambda i: (i, 0)))

    def out_t(width, dtype=BF16):
        return (jax.ShapeDtypeStruct((width, n), dtype), pl.BlockSpec((width, tm), lambda i: (0, i)))

    if latent:
        outs = [out(NA_WIDTH), out(NA_WIDTH), out(NA_WIDTH), out(ML_QK_WIDTH), out_t(ML_QK_WIDTH),
                out(ML_WIDTH), out(ML_WIDTH), out(d), out(d), out(N_GATES, F32)]
    else:
        outs = [out(NA_WIDTH), out(NA_WIDTH), out_t(ML_QK_WIDTH), out(ML_WIDTH), out(N_GATES, F32)]
    outs.append(out_t(N_GATES, F32))
    return pl.pallas_call(
        functools.partial(_proj_kernel, latent=latent),
        grid=grid,
        in_specs=in_specs,
        out_specs=[o[1] for o in outs],
        out_shape=[o[0] for o in outs],
        compiler_params=_cparams("arbitrary"),
        name="in_proj_latent" if latent else "in_proj_ctx",
    )(*args)


def _na_window_start(r0, rows):
    return jnp.clip(r0 - WIN_H // 2, 0, rows - NA_WROWS)


def _na_classes(rows):
    keys, group_class = [], []
    for r0 in range(0, rows, NA_QROWS):
        start = min(max(r0 - WIN_H // 2, 0), rows - NA_WROWS)
        first = tuple(min(max(r0 + i - WIN_H // 2, 0), rows - WIN_H) - start for i in range(NA_QROWS))
        assert all(0 <= f and f + WIN_H <= NA_WROWS for f in first)
        key = (r0 - start, first)
        if key not in keys:
            keys.append(key)
        group_class.append(keys.index(key))
    return keys, np.asarray(group_class, np.int32)


def _na_bias_table(rpb, rows):
    keys, group_class = _na_classes(rows)
    qc = np.arange(GRID_W)[:, None]
    kc = np.arange(GRID_W)[None, :]
    cs = np.clip(qc - WIN_W // 2, 0, GRID_W - WIN_W)
    col_ok = (kc >= cs) & (kc < cs + WIN_W)
    dc = np.clip(kc - qc, -(WIN_W - 1), WIN_W - 1) + (WIN_W - 1)
    i = np.arange(NA_QROWS)[:, None]
    j = np.arange(NA_WROWS)[None, :]
    sel_r, row_ok = [], []
    for off, first in keys:
        f = np.asarray(first)[:, None]
        ok = (j >= f) & (j < f + WIN_H)
        dr = j - off - i + (WIN_H - 1)
        sel_r.append(((dr[:, :, None] == np.arange(2 * WIN_H - 1)) & ok[:, :, None]).astype(np.float32))
        row_ok.append(ok)
    sel_r = np.stack(sel_r)
    valid = np.stack(row_ok)[:, None, :, None, :, None] & col_ok[None, None, None, :, None, :]
    sel_c = (dc[:, :, None] == np.arange(2 * WIN_W - 1)).astype(np.float32)
    t = jnp.einsum('hrc,qkc->hrqk', rpb.astype(F32), sel_c, precision=HIGHEST)
    bias = jnp.einsum('hrqk,xijr->xhiqjk', t, sel_r, precision=HIGHEST)
    bias = jnp.where(valid, bias * LOG2_E, NEG_INF)
    bias = bias.reshape(len(keys), NA_HEADS, NA_QROWS * GRID_W, NA_WROWS * GRID_W)
    return bias.astype(BF16), jnp.asarray(group_class)


def _na_kernel(cls_ref, q_ref, k_ref, v_ref, kc_ref, vc_ref, bias_ref, o_ref, *, rows):
    del cls_ref
    r0 = pl.program_id(1) * NA_QROWS
    start = pl.multiple_of(_na_window_start(r0, rows) * GRID_W, GRID_W)
    n_win = NA_WROWS * GRID_W
    low = lax.broadcasted_iota(jnp.int32, (1, LANES), 1) < NA_HEAD_DIM
    for pair in range(NA_HEADS // 2):
        sl = slice(pair * LANES, (pair + 1) * LANES)
        q2 = q_ref[:, sl]
        k2 = k_ref[pl.ds(start, n_win), sl]
        v2 = v_ref[pl.ds(start, n_win), sl]
        kc2 = kc_ref[:, sl]
        vc2 = vc_ref[:, sl]
        halves = []
        for half in range(2):
            keep = low if half == 0 else jnp.logical_not(low)
            qh = jnp.where(keep, q2, jnp.zeros_like(q2))
            s_loc = (lax.dot_general(qh, k2, NT_DIMS, preferred_element_type=F32)
                     + bias_ref[2 * pair + half].astype(F32))
            s_ctx = lax.dot_general(qh, kc2, NT_DIMS, preferred_element_type=F32)
            m = jnp.maximum(jnp.max(s_loc, axis=-1, keepdims=True), jnp.max(s_ctx, axis=-1, keepdims=True))
            p_loc = jnp.exp2(s_loc - m)
            p_ctx = jnp.exp2(s_ctx - m)
            denom = jnp.sum(p_loc, axis=-1, keepdims=True) + jnp.sum(p_ctx, axis=-1, keepdims=True)
            o = (jnp.dot(p_loc.astype(BF16), v2, preferred_element_type=F32)
                 + jnp.dot(p_ctx.astype(BF16), vc2, preferred_element_type=F32))
            halves.append(o / denom)
        o_ref[:, sl] = jnp.where(low, halves[0], halves[1]).astype(BF16)


def _neighbourhood_attention(q, k, v, kc, vc, rpb):
    b, s, w = q.shape
    rows = s // GRID_W
    n_ctx = kc.shape[1]
    bias, group_class = _na_bias_table(rpb, rows)
    nq = NA_QROWS * GRID_W
    grid_spec = pltpu.PrefetchScalarGridSpec(
        num_scalar_prefetch=1,
        grid=(b, rows // NA_QROWS),
        in_specs=[pl.BlockSpec((None, nq, w), lambda bi, g, cls: (bi, g, 0)),
                  pl.BlockSpec((None, s, w), lambda bi, g, cls: (bi, 0, 0)),
                  pl.BlockSpec((None, s, w), lambda bi, g, cls: (bi, 0, 0)),
                  pl.BlockSpec((None, n_ctx, w), lambda bi, g, cls: (bi, 0, 0)),
                  pl.BlockSpec((None, n_ctx, w), lambda bi, g, cls: (bi, 0, 0)),
                  pl.BlockSpec((None,) + bias.shape[1:], lambda bi, g, cls: (cls[g], 0, 0, 0))],
        out_specs=pl.BlockSpec((None, nq, w), lambda bi, g, cls: (bi, g, 0)),
    )
    return pl.pallas_call(
        functools.partial(_na_kernel, rows=rows),
        grid_spec=grid_spec,
        out_shape=jax.ShapeDtypeStruct((b, s, w), BF16),
        compiler_params=_cparams("arbitrary", "arbitrary"),
        name="na_attention",
    )(group_class, q, k, v, kc, vc, bias)


def _split3(x):
    hi = x.astype(BF16)
    r1 = x - hi.astype(F32)
    mid = r1.astype(BF16)
    lo = (r1 - mid.astype(F32)).astype(BF16)
    return hi, mid, lo


def _mlstm_direction_step(reverse, is_ctx, q, kt, v, gc, gr, o_ref, c_scr, m_scr):
    li_base = 2 * ML_HEADS if reverse else 0
    lf_base = li_base + ML_HEADS
    length = kt.shape[1]
    t = lax.broadcasted_iota(jnp.int32, (length, length), 0)
    s = lax.broadcasted_iota(jnp.int32, (length, length), 1)
    valid = (s >= t) if reverse else (s <= t)
    valid_t = (t >= s) if reverse else (t <= s)
    b_col = sum(jnp.dot(valid.astype(BF16), p, preferred_element_type=F32) for p in _split3(gc))
    b_row = sum(jnp.dot(p, valid_t.astype(BF16), preferred_element_type=F32) for p in _split3(gr))
    low = lax.broadcasted_iota(jnp.int32, (1, LANES), 1) < ML_QK_DIM
    ones = jnp.ones((length, ML_V_DIM), BF16)

    for h in range(ML_HEADS):
        pair, half = divmod(h, 2)
        rows = slice(half * ML_QK_DIM, (half + 1) * ML_QK_DIM)
        li_r = gr[li_base + h:li_base + h + 1, :]
        lf_r = gr[lf_base + h:lf_base + h + 1, :]
        br = b_row[lf_base + h:lf_base + h + 1, :]
        m_prev = m_scr[h]
        v_ext = jnp.concatenate([v[:, h * ML_V_DIM:(h + 1) * ML_V_DIM], ones], axis=-1)
        kt_pair = kt[pair * LANES:(pair + 1) * LANES, :]

        if not is_ctx:
            bc = b_col[:, lf_base + h:lf_base + h + 1]
            q2 = q[:, pair * LANES:(pair + 1) * LANES]
            qh = jnp.where(low if half == 0 else jnp.logical_not(low), q2, jnp.zeros_like(q2))
            d_mat = jnp.where(valid, bc - br + li_r, NEG_INF)
            m_inter = bc + m_prev
            m_t = jnp.maximum(m_inter, jnp.max(d_mat, axis=-1, keepdims=True))
            w_intra = jnp.exp(d_mat - m_t)
            w_inter = jnp.exp(m_inter - m_t)
            sc = jnp.dot(qh, kt_pair, preferred_element_type=F32) * w_intra
            both = (w_inter * jnp.dot(qh, c_scr[pair].astype(BF16), preferred_element_type=F32)
                    + jnp.dot(sc.astype(BF16), v_ext, preferred_element_type=F32))
            num = both[:, :ML_V_DIM]
            den = both[:, ML_V_DIM:]
            o_ref[:, h * ML_V_DIM:(h + 1) * ML_V_DIM] = (
                num / jnp.maximum(jnp.abs(den), jnp.exp(-m_t))).astype(BF16)

        total = jnp.sum(lf_r, axis=-1, keepdims=True)
        m_new = jnp.maximum(total + m_prev, jnp.max(total - br + li_r, axis=-1, keepdims=True))
        w_src = jnp.exp(total - br + li_r - m_new)
        w_carry = jnp.exp(total + m_prev - m_new)
        kw = (kt_pair[rows, :].astype(F32) * w_src).astype(BF16)
        c_scr[pair, rows, :] = w_carry * c_scr[pair, rows, :] + jnp.dot(kw, v_ext, preferred_element_type=F32)
        m_scr[h] = m_new


def _mlstm_kernel(qf_ref, kf_ref, vf_ref, gcf_ref, grf_ref, qb_ref, kb_ref, vb_ref, gcb_ref, grb_ref,
                  kc_ref, vc_ref, gcc_ref, gcr_ref, of_ref, ob_ref, cf_scr, mf_scr, cb_scr, mb_scr):
    step = pl.program_id(1)

    @pl.when(step == 0)
    def _():
        for ref in (cf_scr, mf_scr, cb_scr, mb_scr):
            ref[...] = jnp.zeros_like(ref)
        kt = kc_ref[...]
        v = vc_ref[...]
        gc = gcc_ref[...]
        gr = gcr_ref[...]
        _mlstm_direction_step(False, True, None, kt, v, gc, gr, None, cf_scr, mf_scr)
        _mlstm_direction_step(True, True, None, kt, v, gc, gr, None, cb_scr, mb_scr)

    @pl.when(step > 0)
    def _():
        _mlstm_direction_step(False, False, qf_ref[...], kf_ref[...], vf_ref[...], gcf_ref[...], grf_ref[...],
                              of_ref, cf_scr, mf_scr)
        _mlstm_direction_step(True, False, qb_ref[...], kb_ref[...], vb_ref[...], gcb_ref[...], grb_ref[...],
                              ob_ref, cb_scr, mb_scr)


def _mlstm(q, k, v, gc, gr, kc, vc, gcc, gcr):
    b, s, _ = q.shape
    n_ctx = vc.shape[1]
    length = ML_CHUNK
    n_chunks = s // length

    def chunk(step, reverse):
        c = jnp.maximum(step - 1, 0)
        return (n_chunks - 1 - c) if reverse else c

    def stream_specs(reverse):
        def seq(width):
            return pl.BlockSpec((None, length, width), lambda bi, st: (bi, chunk(st, reverse), 0))

        def seq_t(width):
            return pl.BlockSpec((width, length), lambda bi, st: (0, bi * n_chunks + chunk(st, reverse)))
        return [seq(ML_QK_WIDTH), seq_t(ML_QK_WIDTH), seq(ML_WIDTH), seq(N_GATES), seq_t(N_GATES)]

    def out_spec(reverse):
        return pl.BlockSpec((None, length, ML_WIDTH), lambda bi, st: (bi, chunk(st, reverse), 0))

    state = [pltpu.VMEM((ML_HEADS // 2, 2 * ML_QK_DIM, 2 * ML_V_DIM), F32),
             pltpu.VMEM((ML_HEADS, 1, 1), F32)]
    return pl.pallas_call(
        _mlstm_kernel,
        grid=(b, n_chunks + 1),
        in_specs=stream_specs(False) + stream_specs(True) + [
            pl.BlockSpec((ML_QK_WIDTH, n_ctx), lambda bi, st: (0, bi)),
            pl.BlockSpec((None, n_ctx, ML_WIDTH), lambda bi, st: (bi, 0, 0)),
            pl.BlockSpec((None, n_ctx, N_GATES), lambda bi, st: (bi, 0, 0)),
            pl.BlockSpec((N_GATES, n_ctx), lambda bi, st: (0, bi))],
        out_specs=[out_spec(False), out_spec(True)],
        out_shape=[jax.ShapeDtypeStruct((b, s, ML_WIDTH), BF16)] * 2,
        scratch_shapes=state + state,
        compiler_params=_cparams("arbitrary", "arbitrary"),
        name="mlstm",
    )(q, k, v, gc, gr, q, k, v, gc, gr, kc, vc, gcc, gcr)


def _merge_kernel(x_ref, ona_ref, hf_ref, hb_ref, opre_ref, gna_ref, gml_ref, gtm_ref, scf_ref, shf_ref,
                  ghead_ref, wbna_ref, wbml_ref, wout_ref, gpost_ref, gpre_ref, wrh_ref, wrl_ref, br_ref,
                  x1_ref, h2_ref, tope_ref, topw_ref, rank_ref, cnt_ref):
    step = pl.program_id(0)
    tm = x_ref.shape[0]

    @pl.when(step == 0)
    def _():
        cnt_ref[...] = jnp.zeros_like(cnt_ref)

    hsum = hf_ref[...].astype(F32) + hb_ref[...].astype(F32)
    heads = [_rms(hsum[:, h * ML_V_DIM:(h + 1) * ML_V_DIM]) for h in range(ML_HEADS)]
    hn = jnp.concatenate(heads, axis=-1) * ghead_ref[...]
    o_ml = jax.nn.sigmoid(opre_ref[...].astype(F32)) * hn
    merged = (jax.nn.sigmoid(gna_ref[...].astype(F32))
              * jnp.dot(ona_ref[...], wbna_ref[...], preferred_element_type=F32)
              + jax.nn.sigmoid(gml_ref[...].astype(F32))
              * jnp.dot(o_ml.astype(BF16), wbml_ref[...], preferred_element_type=F32))
    mixed = jnp.dot(merged.astype(BF16), wout_ref[...], preferred_element_type=F32)
    x1 = x_ref[...] + gtm_ref[...] * (_rms(mixed) * gpost_ref[...])
    x1_ref[...] = x1
    h2 = _rms(x1) * gpre_ref[...] * (1.0 + scf_ref[...]) + shf_ref[...]
    h2_ref[...] = h2
    h2_hi = h2.astype(BF16)
    h2_lo = (h2 - h2_hi.astype(F32)).astype(BF16)
    logits = (jnp.dot(h2_hi, wrh_ref[...], preferred_element_type=F32)
              + (jnp.dot(h2_hi, wrl_ref[...], preferred_element_type=F32)
                 + jnp.dot(h2_lo, wrh_ref[...], preferred_element_type=F32))) + br_ref[...]

    lane = lax.broadcasted_iota(jnp.int32, logits.shape, 1)
    onehots, top_e, top_l = [], [], []
    for _ in range(TOP_K):
        best = jnp.max(logits, axis=-1, keepdims=True)
        e = jnp.min(jnp.where(logits == best, lane, N_EXPERTS), axis=-1, keepdims=True)
        hit = lane == e
        onehots.append(hit)
        top_e.append(e)
        top_l.append(best)
        logits = jnp.where(hit, -jnp.inf, logits)
    exps = [jnp.exp(l - top_l[0]) for l in top_l]
    total = exps[0] + exps[1] + exps[2] + exps[3]

    counts = (onehots[0].astype(F32) + onehots[1].astype(F32)
              + onehots[2].astype(F32) + onehots[3].astype(F32))
    t = lax.broadcasted_iota(jnp.int32, (tm, tm), 0)
    s = lax.broadcasted_iota(jnp.int32, (tm, tm), 1)
    before = jnp.dot((s < t).astype(BF16), counts.astype(BF16), preferred_element_type=F32) + cnt_ref[...]
    out_lane = lax.broadcasted_iota(jnp.int32, (tm, LANES), 1)
    e_out = jnp.zeros((tm, LANES), jnp.int32)
    w_out = jnp.zeros((tm, LANES), F32)
    r_out = jnp.zeros((tm, LANES), jnp.int32)
    for j in range(TOP_K):
        rank = jnp.sum(jnp.where(onehots[j], before, 0.0), axis=-1, keepdims=True).astype(jnp.int32)
        e_out = jnp.where(out_lane == j, top_e[j], e_out)
        w_out = jnp.where(out_lane == j, exps[j] / total, w_out)
        r_out = jnp.where(out_lane == j, rank, r_out)
    tope_ref[...] = e_out
    topw_ref[...] = w_out
    rank_ref[...] = r_out
    cnt_ref[...] += jnp.sum(counts, axis=0, keepdims=True)


def _merge_and_route(x2, o_na, h_f, h_b, o_pre, g_na, g_ml, mod4, seq, g_head, wbna, wbml, wout,
                     g_post, g_pre, w_router, b_router):
    n, d = x2.shape
    tm = MERGE_ROWS
    per_seq = seq // tm

    def rows(width):
        return pl.BlockSpec((tm, width), lambda i: (i, 0))

    def mod_spec(j):
        return pl.BlockSpec((None, None, 1, d), lambda i, _j=j: (i // per_seq, _j, 0, 0))

    w_router_hi = w_router.astype(BF16)
    return pl.pallas_call(
        _merge_kernel,
        grid=(n // tm,),
        in_specs=[rows(d), rows(NA_WIDTH), rows(ML_WIDTH), rows(ML_WIDTH), rows(ML_WIDTH), rows(d), rows(d),
                  mod_spec(2), mod_spec(4), mod_spec(3),
                  _const_spec((1, ML_WIDTH)), _const_spec(wbna.shape), _const_spec(wbml.shape),
                  _const_spec(wout.shape), _const_spec((1, d)), _const_spec((1, d)),
                  _const_spec(w_router.shape), _const_spec(w_router.shape), _const_spec((1, N_EXPERTS))],
        out_specs=[rows(d), rows(d), rows(LANES), rows(LANES), rows(LANES),
                   pl.BlockSpec((1, N_EXPERTS), lambda i: (0, 0))],
        out_shape=[jax.ShapeDtypeStruct((n, d), F32), jax.ShapeDtypeStruct((n, d), F32),
                   jax.ShapeDtypeStruct((n, LANES), jnp.int32), jax.ShapeDtypeStruct((n, LANES), F32),
                   jax.ShapeDtypeStruct((n, LANES), jnp.int32),
                   jax.ShapeDtypeStruct((1, N_EXPERTS), F32)],
        compiler_params=_cparams("arbitrary"),
        name="merge_route",
    )(x2, o_na, h_f, h_b, o_pre, g_na, g_ml, mod4, mod4, mod4,
      g_head.reshape(1, ML_WIDTH), wbna, wbml, wout, g_post.reshape(1, d), g_pre.reshape(1, d),
      w_router_hi, (w_router - w_router_hi.astype(F32)).astype(BF16), b_router.reshape(1, N_EXPERTS))


def _expert_kernel(blk_e_ref, n_used_ref, src_cur_ref, src_next_ref, dst_prev_ref, h_ref,
                   wg_ref, bg_ref, wl_ref, bl_ref, wd_ref, bd_ref, y_ref,
                   wg_s, wl_s, wd_s, xb_s, x_buf, y_buf, gather_sem, scatter_sem):
    i = pl.program_id(0)
    n_used = n_used_ref[0]
    prev = blk_e_ref[jnp.maximum(i - 1, 0)]
    changed = jnp.logical_or(i == 0, blk_e_ref[i] != prev)
    used = i < n_used
    tm = x_buf.shape[1]
    slot = i % 2
    other = 1 - slot

    def gather(idx_ref, buf_slot, r, c):
        return pltpu.make_async_copy(h_ref.at[pl.ds(idx_ref[r, c], 1)],
                                     x_buf.at[buf_slot, pl.ds(r * LANES + c, 1)], gather_sem)

    def scatter(buf_slot, r, c):
        return pltpu.make_async_copy(y_buf.at[buf_slot, pl.ds(r * LANES + c, 1)],
                                     y_ref.at[pl.ds(dst_prev_ref[r, c], 1)], scatter_sem)

    def each_row(fn):
        for r in range(tm // LANES):
            for c in range(LANES):
                fn(r, c)

    @pl.when(i == 0)
    def _():
        y_buf[...] = jnp.zeros_like(y_buf)
        each_row(lambda r, c: gather(src_cur_ref, 0, r, c).start(priority=c % 2))

    @pl.when(used)
    def _():
        each_row(lambda r, c: gather(src_cur_ref, slot, r, c).wait())

    @pl.when(jnp.logical_and(i >= 1, i <= n_used))
    def _():
        each_row(lambda r, c: scatter(slot, r, c).wait())

    @pl.when(jnp.logical_and(used, changed))
    def _():
        wg_s[...] = wg_ref[...].astype(BF16)
        wl_s[...] = wl_ref[...].astype(BF16)
        wd_s[...] = wd_ref[...].astype(BF16)

    @pl.when(used)
    def _():
        xb_s[...] = x_buf[slot].astype(BF16)
        each_row(lambda r, c: scatter(other, r, c).start(priority=c % 2))
        each_row(lambda r, c: gather(src_next_ref, other, r, c).start(priority=c % 2))
        xb = xb_s[...]
        g = jnp.dot(xb, wg_s[...], preferred_element_type=F32) + bg_ref[...]
        l = jnp.dot(xb, wl_s[...], preferred_element_type=F32) + bl_ref[...]
        g = jnp.minimum(g, SWIGLU_LIMIT)
        l = jnp.clip(l, -SWIGLU_LIMIT, SWIGLU_LIMIT)
        a = g * jax.nn.sigmoid(SWIGLU_ALPHA * g) * (l + 1.0)
        y_buf[slot] = jnp.dot(a.astype(BF16), wd_s[...], preferred_element_type=F32) + bd_ref[...]

    @pl.when(i == n_used)
    def _():
        each_row(lambda r, c: gather(src_cur_ref, slot, r, c).wait())
        each_row(lambda r, c: scatter(other, r, c).start(priority=c % 2))
        each_row(lambda r, c: scatter(other, r, c).wait())


def _experts(blk_e, n_used, src_rows, dst_rows, h2, w_gate, b_gate, w_lin, b_lin, w_down, b_down):
    n_blocks, idx_rows, _ = src_rows.shape
    e, d, f = w_gate.shape
    tm = EXPERT_ROWS
    n_out = h2.shape[0] * TOP_K + tm

    def w_spec(shape):
        return pl.BlockSpec((None,) + shape, lambda i, be, nu: (be[i], 0, 0))

    def idx_spec(index_map):
        return pl.BlockSpec((None, idx_rows, LANES), index_map, memory_space=pltpu.SMEM)

    grid_spec = pltpu.PrefetchScalarGridSpec(
        num_scalar_prefetch=2,
        grid=(n_blocks,),
        in_specs=[idx_spec(lambda i, be, nu: (i, 0, 0)),
                  idx_spec(lambda i, be, nu: (jnp.minimum(i + 1, n_blocks - 1), 0, 0)),
                  idx_spec(lambda i, be, nu: (jnp.where(i == 0, n_blocks, i - 1), 0, 0)),
                  pl.BlockSpec(memory_space=pl.ANY),
                  w_spec((d, f)), w_spec((1, f)), w_spec((d, f)), w_spec((1, f)),
                  w_spec((f, d)), w_spec((1, d))],
        out_specs=pl.BlockSpec(memory_space=pl.ANY),
        scratch_shapes=[pltpu.VMEM((d, f), BF16), pltpu.VMEM((d, f), BF16), pltpu.VMEM((f, d), BF16),
                        pltpu.VMEM((tm, d), BF16), pltpu.VMEM((2, tm, d), F32), pltpu.VMEM((2, tm, d), F32),
                        pltpu.SemaphoreType.DMA(()), pltpu.SemaphoreType.DMA(())],
    )
    return pl.pallas_call(
        _expert_kernel,
        grid_spec=grid_spec,
        out_shape=jax.ShapeDtypeStruct((n_out, d), F32),
        compiler_params=_cparams("arbitrary"),
        name="moe_experts",
    )(blk_e, n_used, src_rows, src_rows, dst_rows, h2, w_gate, b_gate.reshape(e, 1, f), w_lin,
      b_lin.reshape(e, 1, f), w_down, b_down.reshape(e, 1, d))


def _combine_kernel(x1_ref, w_ref, gtf_ref, gpost_ref, y0_ref, y1_ref, y2_ref, y3_ref, o_ref):
    w = w_ref[...]
    ffn = ((y0_ref[...] * w[:, 0:1] + y1_ref[...] * w[:, 1:2])
           + (y2_ref[...] * w[:, 2:3] + y3_ref[...] * w[:, 3:4]))
    o_ref[...] = x1_ref[...] + gtf_ref[...] * (_rms(ffn) * gpost_ref[...])


def _combine(x1, top_w, mod4, seq, g_post, y):
    n, d = x1.shape
    tm = MOVE_ROWS
    per_seq = seq // tm
    per_slot = n // tm
    return pl.pallas_call(
        _combine_kernel,
        grid=(n // tm,),
        in_specs=[pl.BlockSpec((tm, d), lambda i: (i, 0)),
                  pl.BlockSpec((tm, LANES), lambda i: (i, 0)),
                  pl.BlockSpec((None, None, 1, d), lambda i: (i // per_seq, 5, 0, 0)),
                  _const_spec((1, d))]
        + [pl.BlockSpec((tm, d), lambda i, _k=k: (_k * per_slot + i, 0)) for k in range(TOP_K)],
        out_specs=pl.BlockSpec((tm, d), lambda i: (i, 0)),
        out_shape=jax.ShapeDtypeStruct((n, d), F32),
        compiler_params=_cparams("arbitrary"),
        name="moe_combine",
    )(x1, top_w, mod4, g_post.reshape(1, d), y, y, y, y)


def _rope_partner(w):
    n_freq = ML_QK_DIM // 4
    d = w.shape[0]
    w4 = w.reshape(d, -1, 2, n_freq)
    return jnp.stack([-w4[:, :, 1], w4[:, :, 0]], axis=2).reshape(w.shape)


def _rope_tables(seq):
    n_freq = ML_QK_DIM // 4
    t = jnp.arange(seq)
    row = (t // GRID_W).astype(F32)
    col = (t % GRID_W).astype(F32)
    inv_freq = ROPE_BASE ** (-jnp.arange(n_freq, dtype=F32) / n_freq)
    ang = jnp.concatenate([row[:, None] * inv_freq] * 2 + [col[:, None] * inv_freq] * 2, axis=-1)
    cos = jnp.tile(jnp.cos(ang), (1, ML_HEADS))
    sin = jnp.tile(jnp.sin(ang), (1, ML_HEADS))
    return cos, sin


def _layer(x, ctx, mod4, g_mix_pre, g_mix_post, g_ffn_pre, g_ffn_post, w_in, b_gates, rpb, g_head,
           w_branch_na, w_branch_ml, w_out, w_router, b_router, w_gate, b_gate, w_lin, b_lin, w_down, b_down):
    b, s, d = x.shape
    n = b * s
    n_ctx = ctx.shape[1]
    x2 = x.reshape(n, d)

    ctx_cols = (NA_WIDTH, NA_WIDTH, ML_QK_WIDTH, ML_WIDTH, N_GATES)
    lat_cols = (NA_WIDTH, ML_QK_WIDTH, ML_WIDTH, d, d)
    bounds = np.cumsum(ctx_cols + lat_cols)[:-1].tolist()
    (w_nak, w_nav, w_mk, w_mv, w_g, w_naq, w_mq, w_mo, w_gna, w_gml) = jnp.split(w_in, bounds, axis=-1)
    w_naq = w_naq * (NA_HEAD_DIM ** -0.5 * LOG2_E)
    w_mk = w_mk * ML_QK_DIM ** -0.5
    bf = lambda a: a.astype(BF16)
    lat_w = [bf(w_naq), bf(w_nak), bf(w_nav), bf(w_mq), bf(_rope_partner(w_mq)), bf(w_mk.T),
             bf(_rope_partner(w_mk).T), bf(w_mv), bf(w_mo), bf(w_gna), bf(w_gml), bf(w_g), bf(w_g.T)]
    ctx_w = [bf(w_nak), bf(w_nav), bf(w_mk.T), bf(w_mv), bf(w_g), bf(w_g.T)]
    bg_col = b_gates.reshape(1, N_GATES).astype(F32)
    bg_row = b_gates.reshape(N_GATES, 1).astype(F32)
    cos, sin = _rope_tables(s)
    per_seq = s // PROJ_ROWS

    (na_q, na_k, na_v, ml_q, ml_k, ml_v, ml_o, gate_na, gate_ml, g_col, g_row) = _project(
        x2, mod4, lambda i: i // per_seq, g_mix_pre, bg_col, bg_row, lat_w, (cos, sin, per_seq), True)
    (na_kc, na_vc, ml_kc, ml_vc, gc_col, gc_row) = _project(
        ctx.reshape(b * n_ctx, d), mod4, lambda i: b, g_mix_pre, bg_col, bg_row, ctx_w, None, False)

    def seq3(a, length):
        return a.reshape(b, length, a.shape[-1])

    o_na = _neighbourhood_attention(seq3(na_q, s), seq3(na_k, s), seq3(na_v, s),
                                    seq3(na_kc, n_ctx), seq3(na_vc, n_ctx), rpb)
    ml_args = (seq3(ml_q, s), ml_k, seq3(ml_v, s), seq3(g_col, s), g_row,
               ml_kc, seq3(ml_vc, n_ctx), seq3(gc_col, n_ctx), gc_row)
    h_f, h_b = _mlstm(*ml_args)

    x1, h2, top_e, top_w, rank, counts = _merge_and_route(
        x2, o_na.reshape(n, NA_WIDTH), h_f.reshape(n, ML_WIDTH), h_b.reshape(n, ML_WIDTH), ml_o,
        gate_na, gate_ml, mod4, s, g_head, bf(w_branch_na), bf(w_branch_ml), bf(w_out),
        g_mix_post, g_ffn_pre, w_router, b_router)

    tm = EXPERT_ROWS
    counts = counts.reshape(N_EXPERTS).astype(jnp.int32)
    padded = (counts + tm - 1) // tm * tm
    pad_end = jnp.cumsum(padded)
    pad_start = pad_end - padded
    n_rows = n * TOP_K + N_EXPERTS * tm
    n_blocks = n_rows // tm
    e_sel = top_e[:, :TOP_K, None] == jnp.arange(N_EXPERTS, dtype=jnp.int32)
    dest = jnp.sum(jnp.where(e_sel, pad_start, 0), axis=-1) + rank[:, :TOP_K]
    blk_start = jnp.arange(n_blocks, dtype=jnp.int32) * tm
    blk_e = jnp.minimum(jnp.sum(blk_start[:, None] >= pad_end[None, :], axis=1), N_EXPERTS - 1).astype(jnp.int32)
    n_used = (pad_end[-1:] // tm).astype(jnp.int32)

    assign = jnp.arange(n * TOP_K, dtype=jnp.int32)
    owner = jnp.full((n_rows,), -1, jnp.int32).at[dest.reshape(-1)].set(assign, unique_indices=True)
    real = owner >= 0
    tok = owner // TOP_K
    src_rows = jnp.where(real, tok, 0)
    spare = n * TOP_K + jnp.arange(n_rows, dtype=jnp.int32) % tm
    dst_rows = jnp.where(real, (owner % TOP_K) * n + tok, spare)
    dst_rows = jnp.concatenate([dst_rows, n * TOP_K + jnp.arange(tm, dtype=jnp.int32)])
    src_rows = src_rows.reshape(n_blocks, tm // LANES, LANES)
    dst_rows = dst_rows.reshape(n_blocks + 1, tm // LANES, LANES)

    y = _experts(blk_e, n_used, src_rows, dst_rows, h2, w_gate, b_gate, w_lin, b_lin, w_down, b_down)
    out = _combine(x1, top_w, mod4, s, g_ffn_post, y)
    return out.reshape(b, s, d)


def kernel(x, c, ctx, c_ctx, w_ada, b_ada, g_mix_pre, g_mix_post, g_ffn_pre, g_ffn_post, w_in, b_mlstm_gates,
           rpb, g_mlstm_head, w_branch_na, w_branch_ml, w_out, w_router, b_router, w_gate, b_gate, w_lin,
           b_lin, w_down, b_down):
    b, s, d = x.shape
    depth = w_ada.shape[0]
    pad = (-(b + 1)) % 8
    c_all = jnp.concatenate([c, c_ctx[None, :], jnp.zeros((pad, d), c.dtype)], axis=0)
    for layer in range(depth):
        mod = _ada(c_all, w_ada[layer], b_ada[layer])
        mod4 = mod.reshape(mod.shape[0], 6, 1, d)
        x = _layer(x, ctx, mod4, g_mix_pre[layer], g_mix_post[layer], g_ffn_pre[layer], g_ffn_post[layer],
                   w_in[layer], b_mlstm_gates[layer], rpb[layer], g_mlstm_head[layer], w_branch_na[layer],
                   w_branch_ml[layer], w_out[layer], w_router[layer], b_router[layer], w_gate[layer],
                   b_gate[layer], w_lin[layer], b_lin[layer], w_down[layer], b_down[layer])
    return x
```

```python
import functools

import numpy as np
import jax
import jax.numpy as jnp
from jax import lax
from jax.experimental import pallas as pl
from jax.experimental.pallas import tpu as pltpu

F32 = jnp.float32
BF16 = jnp.bfloat16
HIGHEST = lax.Precision.HIGHEST

GRID_W = 64
NA_HEADS = 8
NA_HEAD_DIM = 64
NA_WIDTH = NA_HEADS * NA_HEAD_DIM
WIN_H = 8
WIN_W = 16
ML_HEADS = 4
ML_QK_DIM = 64
ML_V_DIM = 128
ML_QK_WIDTH = ML_HEADS * ML_QK_DIM
ML_WIDTH = ML_HEADS * ML_V_DIM
N_GATES = 4 * ML_HEADS
GATE_SOFTCAP = 15.0
ROPE_BASE = 10000.0
N_EXPERTS = 32
TOP_K = 4
SWIGLU_ALPHA = 1.702
SWIGLU_LIMIT = 7.0
NORM_EPS = 1e-6
NEG_INF = -1e30
LOG2_E = 1.4426950408889634

LANES = 128
NA_QROWS = 4
NA_WROWS = 12
ML_CHUNK = 256
PROJ_ROWS = 512
MERGE_ROWS = 512
EXPERT_ROWS = 512
MOVE_ROWS = 256
VMEM_LIMIT = 56 * 1024 * 1024

NT_DIMS = (((1,), (1,)), ((), ()))
TN_DIMS = (((0,), (0,)), ((), ()))


def _cparams(*sem):
    return pltpu.CompilerParams(dimension_semantics=sem, vmem_limit_bytes=VMEM_LIMIT)


def _rms(x):
    return x * lax.rsqrt(jnp.mean(x * x, axis=-1, keepdims=True) + NORM_EPS)


def _ada_kernel(c_ref, w_ref, b_ref, o_ref):
    c = c_ref[...]
    s = c * jax.nn.sigmoid(c)
    o_ref[...] = jnp.dot(s, w_ref[...], preferred_element_type=F32, precision=HIGHEST) + b_ref[...]


def _ada(c_all, w_ada, b_ada):
    rows, d = c_all.shape
    n_out = w_ada.shape[1]
    tn = 1536
    return pl.pallas_call(
        _ada_kernel,
        grid=(n_out // tn,),
        in_specs=[pl.BlockSpec((rows, d), lambda j: (0, 0)),
                  pl.BlockSpec((d, tn), lambda j: (0, j)),
                  pl.BlockSpec((1, tn), lambda j: (0, j))],
        out_specs=pl.BlockSpec((rows, tn), lambda j: (0, j)),
        out_shape=jax.ShapeDtypeStruct((rows, n_out), F32),
        compiler_params=_cparams("arbitrary"),
        name="ada_mod",
    )(c_all, w_ada, b_ada.reshape(1, n_out))


def _gate_logs(g, is_forget):
    g = GATE_SOFTCAP * jnp.tanh(g / GATE_SOFTCAP)
    log_sig = jnp.minimum(g, 0.0) - jnp.log(1.0 + jnp.exp(-jnp.abs(g)))
    return jnp.where(is_forget, log_sig, g)


def _proj_kernel(*refs, latent):
    if latent:
        (x_ref, g_ref, sc_ref, sh_ref, cos_ref, sin_ref, cos_t_ref, sin_t_ref, bgc_ref, bgr_ref,
         w_naq, w_nak, w_nav, w_mq, w_mqp, w_mk, w_mkp, w_mv, w_mo, w_gna, w_gml, w_gc, w_gr,
         o_naq, o_nak, o_nav, o_mq, o_mk, o_mv, o_mo, o_gna, o_gml, o_gc, o_gr) = refs
    else:
        (x_ref, g_ref, sc_ref, sh_ref, bgc_ref, bgr_ref,
         w_nak, w_nav, w_mk, w_mv, w_gc, w_gr,
         o_nak, o_nav, o_mk, o_mv, o_gc, o_gr) = refs
    x = x_ref[...]
    h = _rms(x) * g_ref[...]
    h = h * (1.0 + sc_ref[...]) + sh_ref[...]
    hb = h.astype(BF16)

    def mm(w_ref):
        return jnp.dot(hb, w_ref[...], preferred_element_type=F32)

    def mm_t(w_ref):
        return lax.dot_general(w_ref[...], hb, NT_DIMS, preferred_element_type=F32)

    o_nak[...] = mm(w_nak).astype(BF16)
    o_nav[...] = mm(w_nav).astype(BF16)
    o_mv[...] = mm(w_mv).astype(BF16)
    if latent:
        cos = cos_ref[...]
        sin = sin_ref[...]
        o_naq[...] = mm(w_naq).astype(BF16)
        o_mq[...] = (mm(w_mq) * cos + mm(w_mqp) * sin).astype(BF16)
        o_mk[...] = (mm_t(w_mk) * cos_t_ref[...] + mm_t(w_mkp) * sin_t_ref[...]).astype(BF16)
        o_mo[...] = mm(w_mo).astype(BF16)
        o_gna[...] = mm(w_gna).astype(BF16)
        o_gml[...] = mm(w_gml).astype(BF16)
    else:
        o_mk[...] = mm_t(w_mk).astype(BF16)
    gc = mm(w_gc) + bgc_ref[...]
    col_id = lax.broadcasted_iota(jnp.int32, gc.shape, 1)
    o_gc[...] = _gate_logs(gc, (col_id // ML_HEADS) % 2 == 1)
    gr = mm_t(w_gr) + bgr_ref[...]
    row_id = lax.broadcasted_iota(jnp.int32, gr.shape, 0)
    o_gr[...] = _gate_logs(gr, (row_id // ML_HEADS) % 2 == 1)


def _const_spec(shape):
    nd = len(shape)
    return pl.BlockSpec(shape, lambda i, _nd=nd: (0,) * _nd)


def _project(x2, mod4, mod_row_fn, g_pre, bg_col, bg_row, weights, tables, latent):
    n, d = x2.shape
    tm = PROJ_ROWS
    grid = (n // tm,)

    def mod_spec(j):
        return pl.BlockSpec((None, None, 1, d), lambda i, _j=j: (mod_row_fn(i), _j, 0, 0))

    in_specs = [pl.BlockSpec((tm, d), lambda i: (i, 0)), _const_spec((1, d)), mod_spec(1), mod_spec(0)]
    args = [x2, g_pre.reshape(1, d), mod4, mod4]
    if latent:
        cos, sin, tiles_per_seq = tables
        in_specs += [pl.BlockSpec((tm, ML_QK_WIDTH), lambda i: (i % tiles_per_seq, 0))] * 2
        in_specs += [pl.BlockSpec((ML_QK_WIDTH, tm), lambda i: (0, i % tiles_per_seq))] * 2
        args += [cos, sin, cos.T, sin.T]
    in_specs += [_const_spec(bg_col.shape), _const_spec(bg_row.shape)]
    args += [bg_col, bg_row]
    for w in weights:
        in_specs.append(_const_spec(w.shape))
        args.append(w)

    def out(width, dtype=BF16):
        return (jax.ShapeDtypeStruct((n, width), dtype), pl.BlockSpec((tm, width), lambda i: (i, 0)))

    def out_t(width, dtype=BF16):
        return (jax.ShapeDtypeStruct((width, n), dtype), pl.BlockSpec((width, tm), lambda i: (0, i)))

    if latent:
        outs = [out(NA_WIDTH), out(NA_WIDTH), out(NA_WIDTH), out(ML_QK_WIDTH), out_t(ML_QK_WIDTH),
                out(ML_WIDTH), out(ML_WIDTH), out(d), out(d), out(N_GATES, F32)]
    else:
        outs = [out(NA_WIDTH), out(NA_WIDTH), out_t(ML_QK_WIDTH), out(ML_WIDTH), out(N_GATES, F32)]
    outs.append(out_t(N_GATES, F32))
    return pl.pallas_call(
        functools.partial(_proj_kernel, latent=latent),
        grid=grid,
        in_specs=in_specs,
        out_specs=[o[1] for o in outs],
        out_shape=[o[0] for o in outs],
        compiler_params=_cparams("arbitrary"),
        name="in_proj_latent" if latent else "in_proj_ctx",
    )(*args)


def _na_window_start(r0, rows):
    return jnp.clip(r0 - WIN_H // 2, 0, rows - NA_WROWS)


def _na_classes(rows):
    keys, group_class = [], []
    for r0 in range(0, rows, NA_QROWS):
        start = min(max(r0 - WIN_H // 2, 0), rows - NA_WROWS)
        first = tuple(min(max(r0 + i - WIN_H // 2, 0), rows - WIN_H) - start for i in range(NA_QROWS))
        assert all(0 <= f and f + WIN_H <= NA_WROWS for f in first)
        key = (r0 - start, first)
        if key not in keys:
            keys.append(key)
        group_class.append(keys.index(key))
    return keys, np.asarray(group_class, np.int32)


def _na_bias_table(rpb, rows):
    keys, group_class = _na_classes(rows)
    qc = np.arange(GRID_W)[:, None]
    kc = np.arange(GRID_W)[None, :]
    cs = np.clip(qc - WIN_W // 2, 0, GRID_W - WIN_W)
    col_ok = (kc >= cs) & (kc < cs + WIN_W)
    dc = np.clip(kc - qc, -(WIN_W - 1), WIN_W - 1) + (WIN_W - 1)
    i = np.arange(NA_QROWS)[:, None]
    j = np.arange(NA_WROWS)[None, :]
    sel_r, row_ok = [], []
    for off, first in keys:
        f = np.asarray(first)[:, None]
        ok = (j >= f) & (j < f + WIN_H)
        dr = j - off - i + (WIN_H - 1)
        sel_r.append(((dr[:, :, None] == np.arange(2 * WIN_H - 1)) & ok[:, :, None]).astype(np.float32))
        row_ok.append(ok)
    sel_r = np.stack(sel_r)
    valid = np.stack(row_ok)[:, None, :, None, :, None] & col_ok[None, None, None, :, None, :]
    sel_c = (dc[:, :, None] == np.arange(2 * WIN_W - 1)).astype(np.float32)
    t = jnp.einsum('hrc,qkc->hrqk', rpb.astype(F32), sel_c, precision=HIGHEST)
    bias = jnp.einsum('hrqk,xijr->xhiqjk', t, sel_r, precision=HIGHEST)
    bias = jnp.where(valid, bias * LOG2_E, NEG_INF)
    bias = bias.reshape(len(keys), NA_HEADS, NA_QROWS * GRID_W, NA_WROWS * GRID_W)
    return bias.astype(BF16), jnp.asarray(group_class)


def _na_kernel(cls_ref, q_ref, k_ref, v_ref, kc_ref, vc_ref, bias_ref, o_ref, *, rows):
    del cls_ref
    r0 = pl.program_id(1) * NA_QROWS
    start = pl.multiple_of(_na_window_start(r0, rows) * GRID_W, GRID_W)
    n_win = NA_WROWS * GRID_W
    low = lax.broadcasted_iota(jnp.int32, (1, LANES), 1) < NA_HEAD_DIM
    for pair in range(NA_HEADS // 2):
        sl = slice(pair * LANES, (pair + 1) * LANES)
        q2 = q_ref[:, sl]
        k2 = k_ref[pl.ds(start, n_win), sl]
        v2 = v_ref[pl.ds(start, n_win), sl]
        kc2 = kc_ref[:, sl]
        vc2 = vc_ref[:, sl]
        halves = []
        for half in range(2):
            keep = low if half == 0 else jnp.logical_not(low)
            qh = jnp.where(keep, q2, jnp.zeros_like(q2))
            s_loc = (lax.dot_general(qh, k2, NT_DIMS, preferred_element_type=F32)
                     + bias_ref[2 * pair + half].astype(F32))
            s_ctx = lax.dot_general(qh, kc2, NT_DIMS, preferred_element_type=F32)
            m = jnp.maximum(jnp.max(s_loc, axis=-1, keepdims=True), jnp.max(s_ctx, axis=-1, keepdims=True))
            p_loc = jnp.exp2(s_loc - m)
            p_ctx = jnp.exp2(s_ctx - m)
            denom = jnp.sum(p_loc, axis=-1, keepdims=True) + jnp.sum(p_ctx, axis=-1, keepdims=True)
            o = (jnp.dot(p_loc.astype(BF16), v2, preferred_element_type=F32)
                 + jnp.dot(p_ctx.astype(BF16), vc2, preferred_element_type=F32))
            halves.append(o / denom)
        o_ref[:, sl] = jnp.where(low, halves[0], halves[1]).astype(BF16)


def _neighbourhood_attention(q, k, v, kc, vc, rpb):
    b, s, w = q.shape
    rows = s // GRID_W
    n_ctx = kc.shape[1]
    bias, group_class = _na_bias_table(rpb, rows)
    nq = NA_QROWS * GRID_W
    grid_spec = pltpu.PrefetchScalarGridSpec(
        num_scalar_prefetch=1,
        grid=(b, rows // NA_QROWS),
        in_specs=[pl.BlockSpec((None, nq, w), lambda bi, g, cls: (bi, g, 0)),
                  pl.BlockSpec((None, s, w), lambda bi, g, cls: (bi, 0, 0)),
                  pl.BlockSpec((None, s, w), lambda bi, g, cls: (bi, 0, 0)),
                  pl.BlockSpec((None, n_ctx, w), lambda bi, g, cls: (bi, 0, 0)),
                  pl.BlockSpec((None, n_ctx, w), lambda bi, g, cls: (bi, 0, 0)),
                  pl.BlockSpec((None,) + bias.shape[1:], lambda bi, g, cls: (cls[g], 0, 0, 0))],
        out_specs=pl.BlockSpec((None, nq, w), lambda bi, g, cls: (bi, g, 0)),
    )
    return pl.pallas_call(
        functools.partial(_na_kernel, rows=rows),
        grid_spec=grid_spec,
        out_shape=jax.ShapeDtypeStruct((b, s, w), BF16),
        compiler_params=_cparams("arbitrary", "arbitrary"),
        name="na_attention",
    )(group_class, q, k, v, kc, vc, bias)


def _split3(x):
    hi = x.astype(BF16)
    r1 = x - hi.astype(F32)
    mid = r1.astype(BF16)
    lo = (r1 - mid.astype(F32)).astype(BF16)
    return hi, mid, lo


def _mlstm_direction_step(reverse, is_ctx, q, kt, v, gc, gr, o_ref, c_scr, m_scr):
    li_base = 2 * ML_HEADS if reverse else 0
    lf_base = li_base + ML_HEADS
    length = kt.shape[1]
    t = lax.broadcasted_iota(jnp.int32, (length, length), 0)
    s = lax.broadcasted_iota(jnp.int32, (length, length), 1)
    valid = (s >= t) if reverse else (s <= t)
    valid_t = (t >= s) if reverse else (t <= s)
    b_col = sum(jnp.dot(valid.astype(BF16), p, preferred_element_type=F32) for p in _split3(gc))
    b_row = sum(jnp.dot(p, valid_t.astype(BF16), preferred_element_type=F32) for p in _split3(gr))
    low = lax.broadcasted_iota(jnp.int32, (1, LANES), 1) < ML_QK_DIM
    ones = jnp.ones((length, ML_V_DIM), BF16)

    for h in range(ML_HEADS):
        pair, half = divmod(h, 2)
        rows = slice(half * ML_QK_DIM, (half + 1) * ML_QK_DIM)
        li_r = gr[li_base + h:li_base + h + 1, :]
        lf_r = gr[lf_base + h:lf_base + h + 1, :]
        br = b_row[lf_base + h:lf_base + h + 1, :]
        m_prev = m_scr[h]
        v_ext = jnp.concatenate([v[:, h * ML_V_DIM:(h + 1) * ML_V_DIM], ones], axis=-1)
        kt_pair = kt[pair * LANES:(pair + 1) * LANES, :]

        if not is_ctx:
            bc = b_col[:, lf_base + h:lf_base + h + 1]
            q2 = q[:, pair * LANES:(pair + 1) * LANES]
            qh = jnp.where(low if half == 0 else jnp.logical_not(low), q2, jnp.zeros_like(q2))
            d_mat = jnp.where(valid, bc - br + li_r, NEG_INF)
            m_inter = bc + m_prev
            m_t = jnp.maximum(m_inter, jnp.max(d_mat, axis=-1, keepdims=True))
            w_intra = jnp.exp(d_mat - m_t)
            w_inter = jnp.exp(m_inter - m_t)
            sc = jnp.dot(qh, kt_pair, preferred_element_type=F32) * w_intra
            both = (w_inter * jnp.dot(qh, c_scr[pair].astype(BF16), preferred_element_type=F32)
                    + jnp.dot(sc.astype(BF16), v_ext, preferred_element_type=F32))
            num = both[:, :ML_V_DIM]
            den = both[:, ML_V_DIM:]
            o_ref[:, h * ML_V_DIM:(h + 1) * ML_V_DIM] = (
                num / jnp.maximum(jnp.abs(den), jnp.exp(-m_t))).astype(BF16)

        total = jnp.sum(lf_r, axis=-1, keepdims=True)
        m_new = jnp.maximum(total + m_prev, jnp.max(total - br + li_r, axis=-1, keepdims=True))
        w_src = jnp.exp(total - br + li_r - m_new)
        w_carry = jnp.exp(total + m_prev - m_new)
        kw = (kt_pair[rows, :].astype(F32) * w_src).astype(BF16)
        c_scr[pair, rows, :] = w_carry * c_scr[pair, rows, :] + jnp.dot(kw, v_ext, preferred_element_type=F32)
        m_scr[h] = m_new


def _mlstm_kernel(qf_ref, kf_ref, vf_ref, gcf_ref, grf_ref, qb_ref, kb_ref, vb_ref, gcb_ref, grb_ref,
                  kc_ref, vc_ref, gcc_ref, gcr_ref, of_ref, ob_ref, cf_scr, mf_scr, cb_scr, mb_scr):
    step = pl.program_id(1)

    @pl.when(step == 0)
    def _():
        for ref in (cf_scr, mf_scr, cb_scr, mb_scr):
            ref[...] = jnp.zeros_like(ref)
        kt = kc_ref[...]
        v = vc_ref[...]
        gc = gcc_ref[...]
        gr = gcr_ref[...]
        _mlstm_direction_step(False, True, None, kt, v, gc, gr, None, cf_scr, mf_scr)
        _mlstm_direction_step(True, True, None, kt, v, gc, gr, None, cb_scr, mb_scr)

    @pl.when(step > 0)
    def _():
        _mlstm_direction_step(False, False, qf_ref[...], kf_ref[...], vf_ref[...], gcf_ref[...], grf_ref[...],
                              of_ref, cf_scr, mf_scr)
        _mlstm_direction_step(True, False, qb_ref[...], kb_ref[...], vb_ref[...], gcb_ref[...], grb_ref[...],
                              ob_ref, cb_scr, mb_scr)


def _mlstm(q, k, v, gc, gr, kc, vc, gcc, gcr):
    b, s, _ = q.shape
    n_ctx = vc.shape[1]
    length = ML_CHUNK
    n_chunks = s // length

    def chunk(step, reverse):
        c = jnp.maximum(step - 1, 0)
        return (n_chunks - 1 - c) if reverse else c

    def stream_specs(reverse):
        def seq(width):
            return pl.BlockSpec((None, length, width), lambda bi, st: (bi, chunk(st, reverse), 0))

        def seq_t(width):
            return pl.BlockSpec((width, length), lambda bi, st: (0, bi * n_chunks + chunk(st, reverse)))
        return [seq(ML_QK_WIDTH), seq_t(ML_QK_WIDTH), seq(ML_WIDTH), seq(N_GATES), seq_t(N_GATES)]

    def out_spec(reverse):
        return pl.BlockSpec((None, length, ML_WIDTH), lambda bi, st: (bi, chunk(st, reverse), 0))

    state = [pltpu.VMEM((ML_HEADS // 2, 2 * ML_QK_DIM, 2 * ML_V_DIM), F32),
             pltpu.VMEM((ML_HEADS, 1, 1), F32)]
    return pl.pallas_call(
        _mlstm_kernel,
        grid=(b, n_chunks + 1),
        in_specs=stream_specs(False) + stream_specs(True) + [
            pl.BlockSpec((ML_QK_WIDTH, n_ctx), lambda bi, st: (0, bi)),
            pl.BlockSpec((None, n_ctx, ML_WIDTH), lambda bi, st: (bi, 0, 0)),
            pl.BlockSpec((None, n_ctx, N_GATES), lambda bi, st: (bi, 0, 0)),
            pl.BlockSpec((N_GATES, n_ctx), lambda bi, st: (0, bi))],
        out_specs=[out_spec(False), out_spec(True)],
        out_shape=[jax.ShapeDtypeStruct((b, s, ML_WIDTH), BF16)] * 2,
        scratch_shapes=state + state,
        compiler_params=_cparams("arbitrary", "arbitrary"),
        name="mlstm",
    )(q, k, v, gc, gr, q, k, v, gc, gr, kc, vc, gcc, gcr)


def _merge_kernel(x_ref, ona_ref, hf_ref, hb_ref, opre_ref, gna_ref, gml_ref, gtm_ref, scf_ref, shf_ref,
                  ghead_ref, wbna_ref, wbml_ref, wout_ref, gpost_ref, gpre_ref, wrh_ref, wrl_ref, br_ref,
                  x1_ref, h2_ref, tope_ref, topw_ref, rank_ref, cnt_ref):
    step = pl.program_id(0)
    tm = x_ref.shape[0]

    @pl.when(step == 0)
    def _():
        cnt_ref[...] = jnp.zeros_like(cnt_ref)

    hsum = hf_ref[...].astype(F32) + hb_ref[...].astype(F32)
    heads = [_rms(hsum[:, h * ML_V_DIM:(h + 1) * ML_V_DIM]) for h in range(ML_HEADS)]
    hn = jnp.concatenate(heads, axis=-1) * ghead_ref[...]
    o_ml = jax.nn.sigmoid(opre_ref[...].astype(F32)) * hn
    merged = (jax.nn.sigmoid(gna_ref[...].astype(F32))
              * jnp.dot(ona_ref[...], wbna_ref[...], preferred_element_type=F32)
              + jax.nn.sigmoid(gml_ref[...].astype(F32))
              * jnp.dot(o_ml.astype(BF16), wbml_ref[...], preferred_element_type=F32))
    mixed = jnp.dot(merged.astype(BF16), wout_ref[...], preferred_element_type=F32)
    x1 = x_ref[...] + gtm_ref[...] * (_rms(mixed) * gpost_ref[...])
    x1_ref[...] = x1
    h2 = _rms(x1) * gpre_ref[...] * (1.0 + scf_ref[...]) + shf_ref[...]
    h2_ref[...] = h2
    h2_hi = h2.astype(BF16)
    h2_lo = (h2 - h2_hi.astype(F32)).astype(BF16)
    logits = (jnp.dot(h2_hi, wrh_ref[...], preferred_element_type=F32)
              + (jnp.dot(h2_hi, wrl_ref[...], preferred_element_type=F32)
                 + jnp.dot(h2_lo, wrh_ref[...], preferred_element_type=F32))) + br_ref[...]

    lane = lax.broadcasted_iota(jnp.int32, logits.shape, 1)
    onehots, top_e, top_l = [], [], []
    for _ in range(TOP_K):
        best = jnp.max(logits, axis=-1, keepdims=True)
        e = jnp.min(jnp.where(logits == best, lane, N_EXPERTS), axis=-1, keepdims=True)
        hit = lane == e
        onehots.append(hit)
        top_e.append(e)
        top_l.append(best)
        logits = jnp.where(hit, -jnp.inf, logits)
    exps = [jnp.exp(l - top_l[0]) for l in top_l]
    total = exps[0] + exps[1] + exps[2] + exps[3]

    counts = (onehots[0].astype(F32) + onehots[1].astype(F32)
              + onehots[2].astype(F32) + onehots[3].astype(F32))
    t = lax.broadcasted_iota(jnp.int32, (tm, tm), 0)
    s = lax.broadcasted_iota(jnp.int32, (tm, tm), 1)
    before = jnp.dot((s < t).astype(BF16), counts.astype(BF16), preferred_element_type=F32) + cnt_ref[...]
    out_lane = lax.broadcasted_iota(jnp.int32, (tm, LANES), 1)
    e_out = jnp.zeros((tm, LANES), jnp.int32)
    w_out = jnp.zeros((tm, LANES), F32)
    r_out = jnp.zeros((tm, LANES), jnp.int32)
    for j in range(TOP_K):
        rank = jnp.sum(jnp.where(onehots[j], before, 0.0), axis=-1, keepdims=True).astype(jnp.int32)
        e_out = jnp.where(out_lane == j, top_e[j], e_out)
        w_out = jnp.where(out_lane == j, exps[j] / total, w_out)
        r_out = jnp.where(out_lane == j, rank, r_out)
    tope_ref[...] = e_out
    topw_ref[...] = w_out
    rank_ref[...] = r_out
    cnt_ref[...] += jnp.sum(counts, axis=0, keepdims=True)


def _merge_and_route(x2, o_na, h_f, h_b, o_pre, g_na, g_ml, mod4, seq, g_head, wbna, wbml, wout,
                     g_post, g_pre, w_router, b_router):
    n, d = x2.shape
    tm = MERGE_ROWS
    per_seq = seq // tm

    def rows(width):
        return pl.BlockSpec((tm, width), lambda i: (i, 0))

    def mod_spec(j):
        return pl.BlockSpec((None, None, 1, d), lambda i, _j=j: (i // per_seq, _j, 0, 0))

    w_router_hi = w_router.astype(BF16)
    return pl.pallas_call(
        _merge_kernel,
        grid=(n // tm,),
        in_specs=[rows(d), rows(NA_WIDTH), rows(ML_WIDTH), rows(ML_WIDTH), rows(ML_WIDTH), rows(d), rows(d),
                  mod_spec(2), mod_spec(4), mod_spec(3),
                  _const_spec((1, ML_WIDTH)), _const_spec(wbna.shape), _const_spec(wbml.shape),
                  _const_spec(wout.shape), _const_spec((1, d)), _const_spec((1, d)),
                  _const_spec(w_router.shape), _const_spec(w_router.shape), _const_spec((1, N_EXPERTS))],
        out_specs=[rows(d), rows(d), rows(LANES), rows(LANES), rows(LANES),
                   pl.BlockSpec((1, N_EXPERTS), lambda i: (0, 0))],
        out_shape=[jax.ShapeDtypeStruct((n, d), F32), jax.ShapeDtypeStruct((n, d), F32),
                   jax.ShapeDtypeStruct((n, LANES), jnp.int32), jax.ShapeDtypeStruct((n, LANES), F32),
                   jax.ShapeDtypeStruct((n, LANES), jnp.int32),
                   jax.ShapeDtypeStruct((1, N_EXPERTS), F32)],
        compiler_params=_cparams("arbitrary"),
        name="merge_route",
    )(x2, o_na, h_f, h_b, o_pre, g_na, g_ml, mod4, mod4, mod4,
      g_head.reshape(1, ML_WIDTH), wbna, wbml, wout, g_post.reshape(1, d), g_pre.reshape(1, d),
      w_router_hi, (w_router - w_router_hi.astype(F32)).astype(BF16), b_router.reshape(1, N_EXPERTS))


def _dispatch_kernel(pad_end_ref, padded_ref, dest_ref, h_ref, xs_ref, zero_scr, sem):
    tm = h_ref.shape[0]
    blk = zero_scr.shape[0]

    @pl.when(pl.program_id(0) == 0)
    def _():
        zero_scr[...] = jnp.zeros_like(zero_scr)

        def zero_copy(e):
            first = pl.multiple_of(pad_end_ref[e] - blk, blk)
            return pltpu.make_async_copy(zero_scr, xs_ref.at[pl.ds(first, blk)], sem)

        for e in range(N_EXPERTS):
            @pl.when(padded_ref[e] > 0)
            def _():
                zero_copy(e).start()
        for e in range(N_EXPERTS):
            @pl.when(padded_ref[e] > 0)
            def _():
                zero_copy(e).wait()

        def tail_copy(b):
            return pltpu.make_async_copy(zero_scr, xs_ref.at[pl.ds(pl.multiple_of(b * blk, blk), blk)], sem)

        def tail_start(b, carry):
            tail_copy(b).start()
            return carry

        def tail_wait(b, carry):
            tail_copy(b).wait()
            return carry

        first_unused = pad_end_ref[N_EXPERTS - 1] // blk
        lax.fori_loop(first_unused, xs_ref.shape[0] // blk, tail_start, 0)
        lax.fori_loop(first_unused, xs_ref.shape[0] // blk, tail_wait, 0)

    tok_per_row = LANES // TOP_K

    def row_copy(r, c):
        return pltpu.make_async_copy(h_ref.at[pl.ds(r * tok_per_row + c // TOP_K, 1)],
                                     xs_ref.at[pl.ds(dest_ref[r, c], 1)], sem)

    for r in range(tm // tok_per_row):
        for c in range(LANES):
            row_copy(r, c).start(priority=c % 2)
    for r in range(tm // tok_per_row):
        for c in range(LANES):
            row_copy(r, c).wait()


def _dispatch(pad_end, padded, dest2, h2, n_rows):
    n = h2.shape[0]
    tm = MOVE_ROWS
    idx_rows = tm * TOP_K // LANES
    row_tile = h2.shape[1:]
    grid_spec = pltpu.PrefetchScalarGridSpec(
        num_scalar_prefetch=2,
        grid=(n // tm,),
        in_specs=[pl.BlockSpec((idx_rows, LANES), lambda i, pe, pd: (i, 0), memory_space=pltpu.SMEM),
                  pl.BlockSpec((tm,) + row_tile, lambda i, pe, pd: (i, 0))],
        out_specs=pl.BlockSpec(memory_space=pl.ANY),
        scratch_shapes=[pltpu.VMEM((EXPERT_ROWS,) + row_tile, h2.dtype), pltpu.SemaphoreType.DMA(())],
    )
    return pl.pallas_call(
        _dispatch_kernel,
        grid_spec=grid_spec,
        out_shape=jax.ShapeDtypeStruct((n_rows,) + row_tile, h2.dtype),
        compiler_params=_cparams("arbitrary"),
        name="moe_dispatch",
    )(pad_end, padded, dest2, h2)


def _expert_kernel(blk_e_ref, n_used_ref, x_ref, wg_ref, bg_ref, wl_ref, bl_ref, wd_ref, bd_ref, y_ref,
                   wg_s, wl_s, wd_s):
    i = pl.program_id(0)
    prev = blk_e_ref[jnp.maximum(i - 1, 0)]
    changed = jnp.logical_or(i == 0, blk_e_ref[i] != prev)
    used = i < n_used_ref[0]

    @pl.when(jnp.logical_and(used, changed))
    def _():
        wg_s[...] = wg_ref[...].astype(BF16)
        wl_s[...] = wl_ref[...].astype(BF16)
        wd_s[...] = wd_ref[...].astype(BF16)

    @pl.when(used)
    def _():
        xb = x_ref[...].astype(BF16)
        g = jnp.dot(xb, wg_s[...], preferred_element_type=F32) + bg_ref[...]
        l = jnp.dot(xb, wl_s[...], preferred_element_type=F32) + bl_ref[...]
        g = jnp.minimum(g, SWIGLU_LIMIT)
        l = jnp.clip(l, -SWIGLU_LIMIT, SWIGLU_LIMIT)
        a = g * jax.nn.sigmoid(SWIGLU_ALPHA * g) * (l + 1.0)
        y_ref[...] = jnp.dot(a.astype(BF16), wd_s[...], preferred_element_type=F32) + bd_ref[...]

    @pl.when(jnp.logical_not(used))
    def _():
        y_ref[...] = jnp.zeros_like(y_ref)


def _experts(blk_e, n_used, xs, w_gate, b_gate, w_lin, b_lin, w_down, b_down):
    n_rows = xs.shape[0]
    row_tile = xs.shape[1:]
    e, d, f = w_gate.shape
    tm = EXPERT_ROWS

    def w_spec(shape):
        return pl.BlockSpec((None,) + shape, lambda i, be, nu: (be[i], 0, 0))

    grid_spec = pltpu.PrefetchScalarGridSpec(
        num_scalar_prefetch=2,
        grid=(n_rows // tm,),
        in_specs=[pl.BlockSpec((tm,) + row_tile, lambda i, be, nu: (i, 0)),
                  w_spec((d, f)), w_spec((1, f)), w_spec((d, f)), w_spec((1, f)),
                  w_spec((f, d)), w_spec((1, d))],
        out_specs=pl.BlockSpec((tm,) + row_tile, lambda i, be, nu: (i, 0)),
        scratch_shapes=[pltpu.VMEM((d, f), BF16), pltpu.VMEM((d, f), BF16), pltpu.VMEM((f, d), BF16)],
    )
    return pl.pallas_call(
        _expert_kernel,
        grid_spec=grid_spec,
        out_shape=jax.ShapeDtypeStruct((n_rows,) + row_tile, F32),
        compiler_params=_cparams("arbitrary"),
        name="moe_experts",
    )(blk_e, n_used, xs, w_gate, b_gate.reshape(e, 1, f), w_lin, b_lin.reshape(e, 1, f),
      w_down, b_down.reshape(e, 1, d))


def _combine_kernel(dest_ref, x1_ref, w_ref, gtf_ref, gpost_ref, y_ref, o_ref, buf, sem):
    tm = x1_ref.shape[0]

    tok_per_row = LANES // TOP_K

    def row_copy(r, c):
        return pltpu.make_async_copy(y_ref.at[pl.ds(dest_ref[r, c], 1)],
                                     buf.at[c % TOP_K, pl.ds(r * tok_per_row + c // TOP_K, 1)], sem)

    for r in range(tm // tok_per_row):
        for c in range(LANES):
            row_copy(r, c).start(priority=c % 2)
    for r in range(tm // tok_per_row):
        for c in range(LANES):
            row_copy(r, c).wait()
    w = w_ref[...]
    ffn = (buf[0] * w[:, 0:1] + buf[1] * w[:, 1:2]) + (buf[2] * w[:, 2:3] + buf[3] * w[:, 3:4])
    o_ref[...] = x1_ref[...] + gtf_ref[...] * (_rms(ffn) * gpost_ref[...])


def _combine(dest2, x1, top_w, mod4, seq, g_post, y):
    n, d = x1.shape
    tm = MOVE_ROWS
    per_seq = seq // tm
    idx_rows = tm * TOP_K // LANES
    return pl.pallas_call(
        _combine_kernel,
        grid=(n // tm,),
        in_specs=[pl.BlockSpec((idx_rows, LANES), lambda i: (i, 0), memory_space=pltpu.SMEM),
                  pl.BlockSpec((tm, d), lambda i: (i, 0)),
                  pl.BlockSpec((tm, LANES), lambda i: (i, 0)),
                  pl.BlockSpec((None, None, 1, d), lambda i: (i // per_seq, 5, 0, 0)),
                  _const_spec((1, d)),
                  pl.BlockSpec(memory_space=pl.ANY)],
        out_specs=pl.BlockSpec((tm, d), lambda i: (i, 0)),
        out_shape=jax.ShapeDtypeStruct((n, d), F32),
        scratch_shapes=[pltpu.VMEM((TOP_K, tm) + y.shape[1:], F32), pltpu.SemaphoreType.DMA(())],
        compiler_params=_cparams("arbitrary"),
        name="moe_combine",
    )(dest2, x1, top_w, mod4, g_post.reshape(1, d), y)


def _rope_partner(w):
    n_freq = ML_QK_DIM // 4
    d = w.shape[0]
    w4 = w.reshape(d, -1, 2, n_freq)
    return jnp.stack([-w4[:, :, 1], w4[:, :, 0]], axis=2).reshape(w.shape)


def _rope_tables(seq):
    n_freq = ML_QK_DIM // 4
    t = jnp.arange(seq)
    row = (t // GRID_W).astype(F32)
    col = (t % GRID_W).astype(F32)
    inv_freq = ROPE_BASE ** (-jnp.arange(n_freq, dtype=F32) / n_freq)
    ang = jnp.concatenate([row[:, None] * inv_freq] * 2 + [col[:, None] * inv_freq] * 2, axis=-1)
    cos = jnp.tile(jnp.cos(ang), (1, ML_HEADS))
    sin = jnp.tile(jnp.sin(ang), (1, ML_HEADS))
    return cos, sin


def _layer(x, ctx, mod4, g_mix_pre, g_mix_post, g_ffn_pre, g_ffn_post, w_in, b_gates, rpb, g_head,
           w_branch_na, w_branch_ml, w_out, w_router, b_router, w_gate, b_gate, w_lin, b_lin, w_down, b_down):
    b, s, d = x.shape
    n = b * s
    n_ctx = ctx.shape[1]
    x2 = x.reshape(n, d)

    ctx_cols = (NA_WIDTH, NA_WIDTH, ML_QK_WIDTH, ML_WIDTH, N_GATES)
    lat_cols = (NA_WIDTH, ML_QK_WIDTH, ML_WIDTH, d, d)
    bounds = np.cumsum(ctx_cols + lat_cols)[:-1].tolist()
    (w_nak, w_nav, w_mk, w_mv, w_g, w_naq, w_mq, w_mo, w_gna, w_gml) = jnp.split(w_in, bounds, axis=-1)
    w_naq = w_naq * (NA_HEAD_DIM ** -0.5 * LOG2_E)
    w_mk = w_mk * ML_QK_DIM ** -0.5
    bf = lambda a: a.astype(BF16)
    lat_w = [bf(w_naq), bf(w_nak), bf(w_nav), bf(w_mq), bf(_rope_partner(w_mq)), bf(w_mk.T),
             bf(_rope_partner(w_mk).T), bf(w_mv), bf(w_mo), bf(w_gna), bf(w_gml), bf(w_g), bf(w_g.T)]
    ctx_w = [bf(w_nak), bf(w_nav), bf(w_mk.T), bf(w_mv), bf(w_g), bf(w_g.T)]
    bg_col = b_gates.reshape(1, N_GATES).astype(F32)
    bg_row = b_gates.reshape(N_GATES, 1).astype(F32)
    cos, sin = _rope_tables(s)
    per_seq = s // PROJ_ROWS

    (na_q, na_k, na_v, ml_q, ml_k, ml_v, ml_o, gate_na, gate_ml, g_col, g_row) = _project(
        x2, mod4, lambda i: i // per_seq, g_mix_pre, bg_col, bg_row, lat_w, (cos, sin, per_seq), True)
    (na_kc, na_vc, ml_kc, ml_vc, gc_col, gc_row) = _project(
        ctx.reshape(b * n_ctx, d), mod4, lambda i: b, g_mix_pre, bg_col, bg_row, ctx_w, None, False)

    def seq3(a, length):
        return a.reshape(b, length, a.shape[-1])

    o_na = _neighbourhood_attention(seq3(na_q, s), seq3(na_k, s), seq3(na_v, s),
                                    seq3(na_kc, n_ctx), seq3(na_vc, n_ctx), rpb)
    ml_args = (seq3(ml_q, s), ml_k, seq3(ml_v, s), seq3(g_col, s), g_row,
               ml_kc, seq3(ml_vc, n_ctx), seq3(gc_col, n_ctx), gc_row)
    h_f, h_b = _mlstm(*ml_args)

    x1, h2, top_e, top_w, rank, counts = _merge_and_route(
        x2, o_na.reshape(n, NA_WIDTH), h_f.reshape(n, ML_WIDTH), h_b.reshape(n, ML_WIDTH), ml_o,
        gate_na, gate_ml, mod4, s, g_head, bf(w_branch_na), bf(w_branch_ml), bf(w_out),
        g_mix_post, g_ffn_pre, w_router, b_router)

    tm = EXPERT_ROWS
    counts = counts.reshape(N_EXPERTS).astype(jnp.int32)
    padded = (counts + tm - 1) // tm * tm
    pad_end = jnp.cumsum(padded)
    pad_start = pad_end - padded
    n_rows = n * TOP_K + N_EXPERTS * tm
    n_blocks = n_rows // tm
    e_sel = top_e[:, :TOP_K, None] == jnp.arange(N_EXPERTS, dtype=jnp.int32)
    dest = jnp.sum(jnp.where(e_sel, pad_start, 0), axis=-1) + rank[:, :TOP_K]
    dest2 = dest.reshape(n * TOP_K // LANES, LANES).astype(jnp.int32)
    blk_start = jnp.arange(n_blocks, dtype=jnp.int32) * tm
    blk_e = jnp.minimum(jnp.sum(blk_start[:, None] >= pad_end[None, :], axis=1), N_EXPERTS - 1).astype(jnp.int32)
    n_used = (pad_end[-1:] // tm).astype(jnp.int32)

    xs = _dispatch(pad_end.astype(jnp.int32), padded.astype(jnp.int32), dest2, h2, n_rows)
    y = _experts(blk_e, n_used, xs, w_gate, b_gate, w_lin, b_lin, w_down, b_down)
    out = _combine(dest2, x1, top_w, mod4, s, g_ffn_post, y)
    return out.reshape(b, s, d)


def kernel(x, c, ctx, c_ctx, w_ada, b_ada, g_mix_pre, g_mix_post, g_ffn_pre, g_ffn_post, w_in, b_mlstm_gates,
           rpb, g_mlstm_head, w_branch_na, w_branch_ml, w_out, w_router, b_router, w_gate, b_gate, w_lin,
           b_lin, w_down, b_down):
    b, s, d = x.shape
    depth = w_ada.shape[0]
    pad = (-(b + 1)) % 8
    c_all = jnp.concatenate([c, c_ctx[None, :], jnp.zeros((pad, d), c.dtype)], axis=0)
    for layer in range(depth):
        mod = _ada(c_all, w_ada[layer], b_ada[layer])
        mod4 = mod.reshape(mod.shape[0], 6, 1, d)
        x = _layer(x, ctx, mod4, g_mix_pre[layer], g_mix_post[layer], g_ffn_pre[layer], g_ffn_post[layer],
                   w_in[layer], b_mlstm_gates[layer], rpb[layer], g_mlstm_head[layer], w_branch_na[layer],
                   w_branch_ml[layer], w_out[layer], w_router[layer], b_router[layer], w_gate[layer],
                   b_gate[layer], w_lin[layer], b_lin[layer], w_down[layer], b_down[layer])
    return x
```

```python
import functools

import numpy as np
import jax
import jax.numpy as jnp
from jax import lax
from jax.experimental import pallas as pl
from jax.experimental.pallas import tpu as pltpu

F32 = jnp.float32
BF16 = jnp.bfloat16
HIGHEST = lax.Precision.HIGHEST

GRID_W = 64
NA_HEADS = 8
NA_HEAD_DIM = 64
NA_WIDTH = NA_HEADS * NA_HEAD_DIM
WIN_H = 8
WIN_W = 16
ML_HEADS = 4
ML_QK_DIM = 64
ML_V_DIM = 128
ML_QK_WIDTH = ML_HEADS * ML_QK_DIM
ML_WIDTH = ML_HEADS * ML_V_DIM
N_GATES = 4 * ML_HEADS
GATE_SOFTCAP = 15.0
ROPE_BASE = 10000.0
N_EXPERTS = 32
TOP_K = 4
SWIGLU_ALPHA = 1.702
SWIGLU_LIMIT = 7.0
NORM_EPS = 1e-6
NEG_INF = -1e30
LOG2_E = 1.4426950408889634

LANES = 128
NA_QROWS = 4
NA_WROWS = 12
ML_CHUNK = 256
PROJ_ROWS = 512
MERGE_ROWS = 512
EXPERT_ROWS = 512
MOVE_ROWS = 512
VMEM_LIMIT = 56 * 1024 * 1024

NT_DIMS = (((1,), (1,)), ((), ()))


def _cparams(*sem):
    return pltpu.CompilerParams(dimension_semantics=sem, vmem_limit_bytes=VMEM_LIMIT)


def _rms(x):
    return x * lax.rsqrt(jnp.mean(x * x, axis=-1, keepdims=True) + NORM_EPS)


def _ada_kernel(c_ref, w_ref, b_ref, o_ref):
    c = c_ref[...]
    s = c * jax.nn.sigmoid(c)
    o_ref[...] = jnp.dot(s, w_ref[...], preferred_element_type=F32, precision=HIGHEST) + b_ref[...]


def _ada(c_all, w_ada, b_ada):
    rows, d = c_all.shape
    n_out = w_ada.shape[1]
    tn = 1536
    return pl.pallas_call(
        _ada_kernel,
        grid=(n_out // tn,),
        in_specs=[pl.BlockSpec((rows, d), lambda j: (0, 0)),
                  pl.BlockSpec((d, tn), lambda j: (0, j)),
                  pl.BlockSpec((1, tn), lambda j: (0, j))],
        out_specs=pl.BlockSpec((rows, tn), lambda j: (0, j)),
        out_shape=jax.ShapeDtypeStruct((rows, n_out), F32),
        compiler_params=_cparams("arbitrary"),
        name="ada_mod",
    )(c_all, w_ada, b_ada.reshape(1, n_out))


def _gate_logs(g, is_forget):
    g = GATE_SOFTCAP * jnp.tanh(g / GATE_SOFTCAP)
    log_sig = jnp.minimum(g, 0.0) - jnp.log(1.0 + jnp.exp(-jnp.abs(g)))
    return jnp.where(is_forget, log_sig, g)


def _proj_kernel(*refs, latent):
    if latent:
        (x_ref, g_ref, sc_ref, sh_ref, cos_ref, sin_ref, cos_t_ref, sin_t_ref, bgc_ref, bgr_ref,
         w_naq, w_nak, w_nav, w_mq, w_mqp, w_mk, w_mkp, w_mv, w_mo, w_gna, w_gml, w_gc, w_gr,
         o_naq, o_nak, o_nav, o_mq, o_mk, o_mv, o_mo, o_gna, o_gml, o_gc, o_gr) = refs
    else:
        (x_ref, g_ref, sc_ref, sh_ref, bgc_ref, bgr_ref,
         w_nak, w_nav, w_mk, w_mv, w_gc, w_gr,
         o_nak, o_nav, o_mk, o_mv, o_gc, o_gr) = refs
    x = x_ref[...]
    h = _rms(x) * g_ref[...]
    h = h * (1.0 + sc_ref[...]) + sh_ref[...]
    hb = h.astype(BF16)

    def mm(w_ref):
        return jnp.dot(hb, w_ref[...], preferred_element_type=F32)

    def mm_t(w_ref):
        return lax.dot_general(w_ref[...], hb, NT_DIMS, preferred_element_type=F32)

    o_nak[...] = mm(w_nak).astype(BF16)
    o_nav[...] = mm(w_nav).astype(BF16)
    o_mv[...] = mm(w_mv).astype(BF16)
    if latent:
        cos = cos_ref[...]
        sin = sin_ref[...]
        o_naq[...] = mm(w_naq).astype(BF16)
        o_mq[...] = (mm(w_mq) * cos + mm(w_mqp) * sin).astype(BF16)
        o_mk[...] = (mm_t(w_mk) * cos_t_ref[...] + mm_t(w_mkp) * sin_t_ref[...]).astype(BF16)
        o_mo[...] = mm(w_mo).astype(BF16)
        o_gna[...] = mm(w_gna).astype(BF16)
        o_gml[...] = mm(w_gml).astype(BF16)
    else:
        o_mk[...] = mm_t(w_mk).astype(BF16)
    gc = mm(w_gc) + bgc_ref[...]
    col_id = lax.broadcasted_iota(jnp.int32, gc.shape, 1)
    o_gc[...] = _gate_logs(gc, (col_id // ML_HEADS) % 2 == 1)
    gr = mm_t(w_gr) + bgr_ref[...]
    row_id = lax.broadcasted_iota(jnp.int32, gr.shape, 0)
    o_gr[...] = _gate_logs(gr, (row_id // ML_HEADS) % 2 == 1)


def _const_spec(shape):
    nd = len(shape)
    return pl.BlockSpec(shape, lambda i, _nd=nd: (0,) * _nd)


def _project(x2, mod4, mod_row_fn, g_pre, bg_col, bg_row, weights, tables, latent):
    n, d = x2.shape
    tm = PROJ_ROWS
    grid = (n // tm,)

    def mod_spec(j):
        return pl.BlockSpec((None, None, 1, d), lambda i, _j=j: (mod_row_fn(i), _j, 0, 0))

    in_specs = [pl.BlockSpec((tm, d), lambda i: (i, 0)), _const_spec((1, d)), mod_spec(1), mod_spec(0)]
    args = [x2, g_pre.reshape(1, d), mod4, mod4]
    if latent:
        cos, sin, tiles_per_seq = tables
        in_specs += [pl.BlockSpec((tm, ML_QK_WIDTH), lambda i: (i % tiles_per_seq, 0))] * 2
        in_specs += [pl.BlockSpec((ML_QK_WIDTH, tm), lambda i: (0, i % tiles_per_seq))] * 2
        args += [cos, sin, cos.T, sin.T]
    in_specs += [_const_spec(bg_col.shape), _const_spec(bg_row.shape)]
    args += [bg_col, bg_row]
    for w in weights:
        in_specs.append(_const_spec(w.shape))
        args.append(w)

    def out(width, dtype=BF16):
        return (jax.ShapeDtypeStruct((n, width), dtype), pl.BlockSpec((tm, width), lambda i: (i, 0)))

    def out_t(width, dtype=BF16):
        return (jax.ShapeDtypeStruct((width, n), dtype), pl.BlockSpec((width, tm), lambda i: (0, i)))

    if latent:
        outs = [out(NA_WIDTH), out(NA_WIDTH), out(NA_WIDTH), out(ML_QK_WIDTH), out_t(ML_QK_WIDTH),
                out(ML_WIDTH), out(ML_WIDTH), out(d), out(d), out(N_GATES, F32)]
    else:
        outs = [out(NA_WIDTH), out(NA_WIDTH), out_t(ML_QK_WIDTH), out(ML_WIDTH), out(N_GATES, F32)]
    outs.append(out_t(N_GATES, F32))
    return pl.pallas_call(
        functools.partial(_proj_kernel, latent=latent),
        grid=grid,
        in_specs=in_specs,
        out_specs=[o[1] for o in outs],
        out_shape=[o[0] for o in outs],
        compiler_params=_cparams("arbitrary"),
        name="in_proj_latent" if latent else "in_proj_ctx",
    )(*args)


def _na_window_start(r0, rows):
    return jnp.clip(r0 - WIN_H // 2, 0, rows - NA_WROWS)


def _na_classes(rows):
    keys, group_class = [], []
    for r0 in range(0, rows, NA_QROWS):
        start = min(max(r0 - WIN_H // 2, 0), rows - NA_WROWS)
        first = tuple(min(max(r0 + i - WIN_H // 2, 0), rows - WIN_H) - start for i in range(NA_QROWS))
        assert all(0 <= f and f + WIN_H <= NA_WROWS for f in first)
        key = (r0 - start, first)
        if key not in keys:
            keys.append(key)
        group_class.append(keys.index(key))
    return keys, np.asarray(group_class, np.int32)


def _na_bias_table(rpb, rows):
    keys, group_class = _na_classes(rows)
    qc = np.arange(GRID_W)[:, None]
    kc = np.arange(GRID_W)[None, :]
    cs = np.clip(qc - WIN_W // 2, 0, GRID_W - WIN_W)
    col_ok = (kc >= cs) & (kc < cs + WIN_W)
    dc = np.clip(kc - qc, -(WIN_W - 1), WIN_W - 1) + (WIN_W - 1)
    i = np.arange(NA_QROWS)[:, None]
    j = np.arange(NA_WROWS)[None, :]
    sel_r, row_ok = [], []
    for off, first in keys:
        f = np.asarray(first)[:, None]
        ok = (j >= f) & (j < f + WIN_H)
        dr = j - off - i + (WIN_H - 1)
        sel_r.append(((dr[:, :, None] == np.arange(2 * WIN_H - 1)) & ok[:, :, None]).astype(np.float32))
        row_ok.append(ok)
    sel_r = np.stack(sel_r)
    valid = np.stack(row_ok)[:, None, :, None, :, None] & col_ok[None, None, None, :, None, :]
    sel_c = (dc[:, :, None] == np.arange(2 * WIN_W - 1)).astype(np.float32)
    t = jnp.einsum('hrc,qkc->hrqk', rpb.astype(F32), sel_c, precision=HIGHEST)
    bias = jnp.einsum('hrqk,xijr->xhiqjk', t, sel_r, precision=HIGHEST)
    bias = jnp.where(valid, bias * LOG2_E, NEG_INF)
    bias = bias.reshape(len(keys), NA_HEADS, NA_QROWS * GRID_W, NA_WROWS * GRID_W)
    return bias.astype(BF16), jnp.asarray(group_class)


def _na_kernel(cls_ref, q_ref, k_ref, v_ref, kc_ref, vc_ref, bias_ref, o_ref, *, rows):
    del cls_ref
    r0 = pl.program_id(1) * NA_QROWS
    start = pl.multiple_of(_na_window_start(r0, rows) * GRID_W, GRID_W)
    n_win = NA_WROWS * GRID_W
    low = lax.broadcasted_iota(jnp.int32, (1, LANES), 1) < NA_HEAD_DIM
    for pair in range(NA_HEADS // 2):
        sl = slice(pair * LANES, (pair + 1) * LANES)
        q2 = q_ref[:, sl]
        k2 = k_ref[pl.ds(start, n_win), sl]
        v2 = v_ref[pl.ds(start, n_win), sl]
        kc2 = kc_ref[:, sl]
        vc2 = vc_ref[:, sl]
        halves = []
        for half in range(2):
            keep = low if half == 0 else jnp.logical_not(low)
            qh = jnp.where(keep, q2, jnp.zeros_like(q2))
            s_loc = (lax.dot_general(qh, k2, NT_DIMS, preferred_element_type=F32)
                     + bias_ref[2 * pair + half].astype(F32))
            s_ctx = lax.dot_general(qh, kc2, NT_DIMS, preferred_element_type=F32)
            m = jnp.maximum(jnp.max(s_loc, axis=-1, keepdims=True), jnp.max(s_ctx, axis=-1, keepdims=True))
            p_loc = jnp.exp2(s_loc - m)
            p_ctx = jnp.exp2(s_ctx - m)
            denom = jnp.sum(p_loc, axis=-1, keepdims=True) + jnp.sum(p_ctx, axis=-1, keepdims=True)
            o = (jnp.dot(p_loc.astype(BF16), v2, preferred_element_type=F32)
                 + jnp.dot(p_ctx.astype(BF16), vc2, preferred_element_type=F32))
            halves.append(o / denom)
        o_ref[:, sl] = jnp.where(low, halves[0], halves[1]).astype(BF16)


def _neighbourhood_attention(q, k, v, kc, vc, rpb):
    b, s, w = q.shape
    rows = s // GRID_W
    n_ctx = kc.shape[1]
    bias, group_class = _na_bias_table(rpb, rows)
    nq = NA_QROWS * GRID_W
    grid_spec = pltpu.PrefetchScalarGridSpec(
        num_scalar_prefetch=1,
        grid=(b, rows // NA_QROWS),
        in_specs=[pl.BlockSpec((None, nq, w), lambda bi, g, cls: (bi, g, 0)),
                  pl.BlockSpec((None, s, w), lambda bi, g, cls: (bi, 0, 0)),
                  pl.BlockSpec((None, s, w), lambda bi, g, cls: (bi, 0, 0)),
                  pl.BlockSpec((None, n_ctx, w), lambda bi, g, cls: (bi, 0, 0)),
                  pl.BlockSpec((None, n_ctx, w), lambda bi, g, cls: (bi, 0, 0)),
                  pl.BlockSpec((None,) + bias.shape[1:], lambda bi, g, cls: (cls[g], 0, 0, 0))],
        out_specs=pl.BlockSpec((None, nq, w), lambda bi, g, cls: (bi, g, 0)),
    )
    return pl.pallas_call(
        functools.partial(_na_kernel, rows=rows),
        grid_spec=grid_spec,
        out_shape=jax.ShapeDtypeStruct((b, s, w), BF16),
        compiler_params=_cparams("arbitrary", "arbitrary"),
        name="na_attention",
    )(group_class, q, k, v, kc, vc, bias)


def _split3(x):
    hi = x.astype(BF16)
    r1 = x - hi.astype(F32)
    mid = r1.astype(BF16)
    lo = (r1 - mid.astype(F32)).astype(BF16)
    return hi, mid, lo


def _mlstm_direction_step(reverse, is_ctx, q, kt, v, gc, gr, o_ref, c_scr, m_scr):
    li_base = 2 * ML_HEADS if reverse else 0
    lf_base = li_base + ML_HEADS
    length = kt.shape[1]
    t = lax.broadcasted_iota(jnp.int32, (length, length), 0)
    s = lax.broadcasted_iota(jnp.int32, (length, length), 1)
    valid = (s >= t) if reverse else (s <= t)
    valid_t = (t >= s) if reverse else (t <= s)
    b_col = sum(jnp.dot(valid.astype(BF16), p, preferred_element_type=F32) for p in _split3(gc))
    b_row = sum(jnp.dot(p, valid_t.astype(BF16), preferred_element_type=F32) for p in _split3(gr))
    low = lax.broadcasted_iota(jnp.int32, (1, LANES), 1) < ML_QK_DIM
    ones = jnp.ones((length, ML_V_DIM), BF16)

    for h in range(ML_HEADS):
        pair, half = divmod(h, 2)
        rows = slice(half * ML_QK_DIM, (half + 1) * ML_QK_DIM)
        li_r = gr[li_base + h:li_base + h + 1, :]
        lf_r = gr[lf_base + h:lf_base + h + 1, :]
        br = b_row[lf_base + h:lf_base + h + 1, :]
        m_prev = m_scr[h]
        v_ext = jnp.concatenate([v[:, h * ML_V_DIM:(h + 1) * ML_V_DIM], ones], axis=-1)
        kt_pair = kt[pair * LANES:(pair + 1) * LANES, :]

        if not is_ctx:
            bc = b_col[:, lf_base + h:lf_base + h + 1]
            q2 = q[:, pair * LANES:(pair + 1) * LANES]
            qh = jnp.where(low if half == 0 else jnp.logical_not(low), q2, jnp.zeros_like(q2))
            d_mat = jnp.where(valid, bc - br + li_r, NEG_INF)
            m_inter = bc + m_prev
            m_t = jnp.maximum(m_inter, jnp.max(d_mat, axis=-1, keepdims=True))
            w_intra = jnp.exp(d_mat - m_t)
            w_inter = jnp.exp(m_inter - m_t)
            sc = jnp.dot(qh, kt_pair, preferred_element_type=F32) * w_intra
            both = (w_inter * jnp.dot(qh, c_scr[pair].astype(BF16), preferred_element_type=F32)
                    + jnp.dot(sc.astype(BF16), v_ext, preferred_element_type=F32))
            num = both[:, :ML_V_DIM]
            den = both[:, ML_V_DIM:]
            o_ref[:, h * ML_V_DIM:(h + 1) * ML_V_DIM] = (
                num / jnp.maximum(jnp.abs(den), jnp.exp(-m_t))).astype(BF16)

        total = jnp.sum(lf_r, axis=-1, keepdims=True)
        m_new = jnp.maximum(total + m_prev, jnp.max(total - br + li_r, axis=-1, keepdims=True))
        w_src = jnp.exp(total - br + li_r - m_new)
        w_carry = jnp.exp(total + m_prev - m_new)
        kw = (kt_pair[rows, :].astype(F32) * w_src).astype(BF16)
        c_scr[pair, rows, :] = w_carry * c_scr[pair, rows, :] + jnp.dot(kw, v_ext, preferred_element_type=F32)
        m_scr[h] = m_new


def _mlstm_kernel(qf_ref, kf_ref, vf_ref, gcf_ref, grf_ref, qb_ref, kb_ref, vb_ref, gcb_ref, grb_ref,
                  kc_ref, vc_ref, gcc_ref, gcr_ref, of_ref, ob_ref, cf_scr, mf_scr, cb_scr, mb_scr):
    step = pl.program_id(1)

    @pl.when(step == 0)
    def _():
        for ref in (cf_scr, mf_scr, cb_scr, mb_scr):
            ref[...] = jnp.zeros_like(ref)
        kt = kc_ref[...]
        v = vc_ref[...]
        gc = gcc_ref[...]
        gr = gcr_ref[...]
        _mlstm_direction_step(False, True, None, kt, v, gc, gr, None, cf_scr, mf_scr)
        _mlstm_direction_step(True, True, None, kt, v, gc, gr, None, cb_scr, mb_scr)

    @pl.when(step > 0)
    def _():
        _mlstm_direction_step(False, False, qf_ref[...], kf_ref[...], vf_ref[...], gcf_ref[...], grf_ref[...],
                              of_ref, cf_scr, mf_scr)
        _mlstm_direction_step(True, False, qb_ref[...], kb_ref[...], vb_ref[...], gcb_ref[...], grb_ref[...],
                              ob_ref, cb_scr, mb_scr)


def _mlstm(q, k, v, gc, gr, kc, vc, gcc, gcr):
    b, s, _ = q.shape
    n_ctx = vc.shape[1]
    length = ML_CHUNK
    n_chunks = s // length

    def chunk(step, reverse):
        c = jnp.maximum(step - 1, 0)
        return (n_chunks - 1 - c) if reverse else c

    def stream_specs(reverse):
        def seq(width):
            return pl.BlockSpec((None, length, width), lambda bi, st: (bi, chunk(st, reverse), 0))

        def seq_t(width):
            return pl.BlockSpec((width, length), lambda bi, st: (0, bi * n_chunks + chunk(st, reverse)))
        return [seq(ML_QK_WIDTH), seq_t(ML_QK_WIDTH), seq(ML_WIDTH), seq(N_GATES), seq_t(N_GATES)]

    def out_spec(reverse):
        return pl.BlockSpec((None, length, ML_WIDTH), lambda bi, st: (bi, chunk(st, reverse), 0))

    state = [pltpu.VMEM((ML_HEADS // 2, 2 * ML_QK_DIM, 2 * ML_V_DIM), F32),
             pltpu.VMEM((ML_HEADS, 1, 1), F32)]
    return pl.pallas_call(
        _mlstm_kernel,
        grid=(b, n_chunks + 1),
        in_specs=stream_specs(False) + stream_specs(True) + [
            pl.BlockSpec((ML_QK_WIDTH, n_ctx), lambda bi, st: (0, bi)),
            pl.BlockSpec((None, n_ctx, ML_WIDTH), lambda bi, st: (bi, 0, 0)),
            pl.BlockSpec((None, n_ctx, N_GATES), lambda bi, st: (bi, 0, 0)),
            pl.BlockSpec((N_GATES, n_ctx), lambda bi, st: (0, bi))],
        out_specs=[out_spec(False), out_spec(True)],
        out_shape=[jax.ShapeDtypeStruct((b, s, ML_WIDTH), BF16)] * 2,
        scratch_shapes=state + state,
        compiler_params=_cparams("arbitrary", "arbitrary"),
        name="mlstm",
    )(q, k, v, gc, gr, q, k, v, gc, gr, kc, vc, gcc, gcr)


def _merge_kernel(x_ref, ona_ref, hf_ref, hb_ref, opre_ref, gna_ref, gml_ref, gtm_ref, scf_ref, shf_ref,
                  ghead_ref, wbna_ref, wbml_ref, wout_ref, gpost_ref, gpre_ref, wrh_ref, wrl_ref, br_ref,
                  x1_ref, h2_ref, tope_ref, topw_ref, rank_ref, cnt_ref):
    step = pl.program_id(0)
    tm = x_ref.shape[0]

    @pl.when(step == 0)
    def _():
        cnt_ref[...] = jnp.zeros_like(cnt_ref)

    hsum = hf_ref[...].astype(F32) + hb_ref[...].astype(F32)
    heads = [_rms(hsum[:, h * ML_V_DIM:(h + 1) * ML_V_DIM]) for h in range(ML_HEADS)]
    hn = jnp.concatenate(heads, axis=-1) * ghead_ref[...]
    o_ml = jax.nn.sigmoid(opre_ref[...].astype(F32)) * hn
    merged = (jax.nn.sigmoid(gna_ref[...].astype(F32))
              * jnp.dot(ona_ref[...], wbna_ref[...], preferred_element_type=F32)
              + jax.nn.sigmoid(gml_ref[...].astype(F32))
              * jnp.dot(o_ml.astype(BF16), wbml_ref[...], preferred_element_type=F32))
    mixed = jnp.dot(merged.astype(BF16), wout_ref[...], preferred_element_type=F32)
    x1 = x_ref[...] + gtm_ref[...] * (_rms(mixed) * gpost_ref[...])
    x1_ref[...] = x1
    h2 = _rms(x1) * gpre_ref[...] * (1.0 + scf_ref[...]) + shf_ref[...]
    h2_ref[...] = h2
    h2_hi = h2.astype(BF16)
    h2_lo = (h2 - h2_hi.astype(F32)).astype(BF16)
    logits = (jnp.dot(h2_hi, wrh_ref[...], preferred_element_type=F32)
              + (jnp.dot(h2_hi, wrl_ref[...], preferred_element_type=F32)
                 + jnp.dot(h2_lo, wrh_ref[...], preferred_element_type=F32))) + br_ref[...]

    lane = lax.broadcasted_iota(jnp.int32, logits.shape, 1)
    onehots, top_e, top_l = [], [], []
    for _ in range(TOP_K):
        best = jnp.max(logits, axis=-1, keepdims=True)
        e = jnp.min(jnp.where(logits == best, lane, N_EXPERTS), axis=-1, keepdims=True)
        hit = lane == e
        onehots.append(hit)
        top_e.append(e)
        top_l.append(best)
        logits = jnp.where(hit, -jnp.inf, logits)
    exps = [jnp.exp(l - top_l[0]) for l in top_l]
    total = exps[0] + exps[1] + exps[2] + exps[3]

    counts = (onehots[0].astype(F32) + onehots[1].astype(F32)
              + onehots[2].astype(F32) + onehots[3].astype(F32))
    t = lax.broadcasted_iota(jnp.int32, (tm, tm), 0)
    s = lax.broadcasted_iota(jnp.int32, (tm, tm), 1)
    before = jnp.dot((s < t).astype(BF16), counts.astype(BF16), preferred_element_type=F32) + cnt_ref[...]
    out_lane = lax.broadcasted_iota(jnp.int32, (tm, LANES), 1)
    e_out = jnp.zeros((tm, LANES), jnp.int32)
    w_out = jnp.zeros((tm, LANES), F32)
    r_out = jnp.zeros((tm, LANES), jnp.int32)
    for j in range(TOP_K):
        rank = jnp.sum(jnp.where(onehots[j], before, 0.0), axis=-1, keepdims=True).astype(jnp.int32)
        e_out = jnp.where(out_lane == j, top_e[j], e_out)
        w_out = jnp.where(out_lane == j, exps[j] / total, w_out)
        r_out = jnp.where(out_lane == j, rank, r_out)
    tope_ref[...] = e_out
    topw_ref[...] = w_out
    rank_ref[...] = r_out
    cnt_ref[...] += jnp.sum(counts, axis=0, keepdims=True)


def _merge_and_route(x2, o_na, h_f, h_b, o_pre, g_na, g_ml, mod4, seq, g_head, wbna, wbml, wout,
                     g_post, g_pre, w_router, b_router):
    n, d = x2.shape
    tm = MERGE_ROWS
    per_seq = seq // tm

    def rows(width):
        return pl.BlockSpec((tm, width), lambda i: (i, 0))

    def mod_spec(j):
        return pl.BlockSpec((None, None, 1, d), lambda i, _j=j: (i // per_seq, _j, 0, 0))

    w_router_hi = w_router.astype(BF16)
    return pl.pallas_call(
        _merge_kernel,
        grid=(n // tm,),
        in_specs=[rows(d), rows(NA_WIDTH), rows(ML_WIDTH), rows(ML_WIDTH), rows(ML_WIDTH), rows(d), rows(d),
                  mod_spec(2), mod_spec(4), mod_spec(3),
                  _const_spec((1, ML_WIDTH)), _const_spec(wbna.shape), _const_spec(wbml.shape),
                  _const_spec(wout.shape), _const_spec((1, d)), _const_spec((1, d)),
                  _const_spec(w_router.shape), _const_spec(w_router.shape), _const_spec((1, N_EXPERTS))],
        out_specs=[rows(d), rows(d), rows(LANES), rows(LANES), rows(LANES),
                   pl.BlockSpec((1, N_EXPERTS), lambda i: (0, 0))],
        out_shape=[jax.ShapeDtypeStruct((n, d), F32), jax.ShapeDtypeStruct((n, d), F32),
                   jax.ShapeDtypeStruct((n, LANES), jnp.int32), jax.ShapeDtypeStruct((n, LANES), F32),
                   jax.ShapeDtypeStruct((n, LANES), jnp.int32),
                   jax.ShapeDtypeStruct((1, N_EXPERTS), F32)],
        compiler_params=_cparams("arbitrary"),
        name="merge_route",
    )(x2, o_na, h_f, h_b, o_pre, g_na, g_ml, mod4, mod4, mod4,
      g_head.reshape(1, ML_WIDTH), wbna, wbml, wout, g_post.reshape(1, d), g_pre.reshape(1, d),
      w_router_hi, (w_router - w_router_hi.astype(F32)).astype(BF16), b_router.reshape(1, N_EXPERTS))


def _dispatch_kernel(pad_end_ref, padded_ref, dest_ref, h_ref, xs_ref, zero_scr, sem):
    tm = h_ref.shape[0]
    blk = zero_scr.shape[0]

    @pl.when(pl.program_id(0) == 0)
    def _():
        zero_scr[...] = jnp.zeros_like(zero_scr)

        def zero_copy(e):
            first = pl.multiple_of(pad_end_ref[e] - blk, blk)
            return pltpu.make_async_copy(zero_scr, xs_ref.at[pl.ds(first, blk)], sem)

        for e in range(N_EXPERTS):
            @pl.when(padded_ref[e] > 0)
            def _():
                zero_copy(e).start()
        for e in range(N_EXPERTS):
            @pl.when(padded_ref[e] > 0)
            def _():
                zero_copy(e).wait()

        def tail_copy(b):
            return pltpu.make_async_copy(zero_scr, xs_ref.at[pl.ds(pl.multiple_of(b * blk, blk), blk)], sem)

        def tail_start(b, carry):
            tail_copy(b).start()
            return carry

        def tail_wait(b, carry):
            tail_copy(b).wait()
            return carry

        first_unused = pad_end_ref[N_EXPERTS - 1] // blk
        lax.fori_loop(first_unused, xs_ref.shape[0] // blk, tail_start, 0)
        lax.fori_loop(first_unused, xs_ref.shape[0] // blk, tail_wait, 0)

    tok_per_row = LANES // TOP_K

    def row_copy(r, c):
        return pltpu.make_async_copy(h_ref.at[pl.ds(r * tok_per_row + c // TOP_K, 1)],
                                     xs_ref.at[pl.ds(dest_ref[r, c], 1)], sem)

    for r in range(tm // tok_per_row):
        for c in range(LANES):
            row_copy(r, c).start(priority=c % 2)
    for r in range(tm // tok_per_row):
        for c in range(LANES):
            row_copy(r, c).wait()


def _dispatch(pad_end, padded, dest2, h2, n_rows):
    n = h2.shape[0]
    tm = MOVE_ROWS
    idx_rows = tm * TOP_K // LANES
    row_tile = h2.shape[1:]
    grid_spec = pltpu.PrefetchScalarGridSpec(
        num_scalar_prefetch=2,
        grid=(n // tm,),
        in_specs=[pl.BlockSpec((idx_rows, LANES), lambda i, pe, pd: (i, 0), memory_space=pltpu.SMEM),
                  pl.BlockSpec((tm,) + row_tile, lambda i, pe, pd: (i, 0))],
        out_specs=pl.BlockSpec(memory_space=pl.ANY),
        scratch_shapes=[pltpu.VMEM((EXPERT_ROWS,) + row_tile, h2.dtype), pltpu.SemaphoreType.DMA(())],
    )
    return pl.pallas_call(
        _dispatch_kernel,
        grid_spec=grid_spec,
        out_shape=jax.ShapeDtypeStruct((n_rows,) + row_tile, h2.dtype),
        compiler_params=_cparams("arbitrary"),
        name="moe_dispatch",
    )(pad_end, padded, dest2, h2)


def _expert_kernel(blk_e_ref, n_used_ref, x_ref, wg_ref, bg_ref, wl_ref, bl_ref, wd_ref, bd_ref, y_ref,
                   wg_s, wl_s, wd_s):
    i = pl.program_id(0)
    prev = blk_e_ref[jnp.maximum(i - 1, 0)]
    changed = jnp.logical_or(i == 0, blk_e_ref[i] != prev)
    used = i < n_used_ref[0]

    @pl.when(jnp.logical_and(used, changed))
    def _():
        wg_s[...] = wg_ref[...].astype(BF16)
        wl_s[...] = wl_ref[...].astype(BF16)
        wd_s[...] = wd_ref[...].astype(BF16)

    @pl.when(used)
    def _():
        xb = x_ref[...].astype(BF16)
        g = jnp.dot(xb, wg_s[...], preferred_element_type=F32) + bg_ref[...]
        l = jnp.dot(xb, wl_s[...], preferred_element_type=F32) + bl_ref[...]
        g = jnp.minimum(g, SWIGLU_LIMIT)
        l = jnp.clip(l, -SWIGLU_LIMIT, SWIGLU_LIMIT)
        a = g * jax.nn.sigmoid(SWIGLU_ALPHA * g) * (l + 1.0)
        y_ref[...] = jnp.dot(a.astype(BF16), wd_s[...], preferred_element_type=F32) + bd_ref[...]

    @pl.when(jnp.logical_not(used))
    def _():
        y_ref[...] = jnp.zeros_like(y_ref)


def _experts(blk_e, n_used, xs, w_gate, b_gate, w_lin, b_lin, w_down, b_down):
    n_rows = xs.shape[0]
    row_tile = xs.shape[1:]
    e, d, f = w_gate.shape
    tm = EXPERT_ROWS

    def w_spec(shape):
        return pl.BlockSpec((None,) + shape, lambda i, be, nu: (be[i], 0, 0))

    grid_spec = pltpu.PrefetchScalarGridSpec(
        num_scalar_prefetch=2,
        grid=(n_rows // tm,),
        in_specs=[pl.BlockSpec((tm,) + row_tile, lambda i, be, nu: (i, 0)),
                  w_spec((d, f)), w_spec((1, f)), w_spec((d, f)), w_spec((1, f)),
                  w_spec((f, d)), w_spec((1, d))],
        out_specs=pl.BlockSpec((tm,) + row_tile, lambda i, be, nu: (i, 0)),
        scratch_shapes=[pltpu.VMEM((d, f), BF16), pltpu.VMEM((d, f), BF16), pltpu.VMEM((f, d), BF16)],
    )
    return pl.pallas_call(
        _expert_kernel,
        grid_spec=grid_spec,
        out_shape=jax.ShapeDtypeStruct((n_rows,) + row_tile, F32),
        compiler_params=_cparams("arbitrary"),
        name="moe_experts",
    )(blk_e, n_used, xs, w_gate, b_gate.reshape(e, 1, f), w_lin, b_lin.reshape(e, 1, f),
      w_down, b_down.reshape(e, 1, d))


def _combine_kernel(dest_ref, x1_ref, w_ref, gtf_ref, gpost_ref, y_ref, o_ref, buf, sem):
    tm = x1_ref.shape[0]

    tok_per_row = LANES // TOP_K

    def row_copy(r, c):
        return pltpu.make_async_copy(y_ref.at[pl.ds(dest_ref[r, c], 1)],
                                     buf.at[c % TOP_K, pl.ds(r * tok_per_row + c // TOP_K, 1)], sem)

    for r in range(tm // tok_per_row):
        for c in range(LANES):
            row_copy(r, c).start(priority=c % 2)
    for r in range(tm // tok_per_row):
        for c in range(LANES):
            row_copy(r, c).wait()
    w = w_ref[...]
    ffn = (buf[0] * w[:, 0:1] + buf[1] * w[:, 1:2]) + (buf[2] * w[:, 2:3] + buf[3] * w[:, 3:4])
    o_ref[...] = x1_ref[...] + gtf_ref[...] * (_rms(ffn) * gpost_ref[...])


def _combine(dest2, x1, top_w, mod4, seq, g_post, y):
    n, d = x1.shape
    tm = MOVE_ROWS
    per_seq = seq // tm
    idx_rows = tm * TOP_K // LANES
    return pl.pallas_call(
        _combine_kernel,
        grid=(n // tm,),
        in_specs=[pl.BlockSpec((idx_rows, LANES), lambda i: (i, 0), memory_space=pltpu.SMEM),
                  pl.BlockSpec((tm, d), lambda i: (i, 0)),
                  pl.BlockSpec((tm, LANES), lambda i: (i, 0)),
                  pl.BlockSpec((None, None, 1, d), lambda i: (i // per_seq, 5, 0, 0)),
                  _const_spec((1, d)),
                  pl.BlockSpec(memory_space=pl.ANY)],
        out_specs=pl.BlockSpec((tm, d), lambda i: (i, 0)),
        out_shape=jax.ShapeDtypeStruct((n, d), F32),
        scratch_shapes=[pltpu.VMEM((TOP_K, tm) + y.shape[1:], F32), pltpu.SemaphoreType.DMA(())],
        compiler_params=_cparams("arbitrary"),
        name="moe_combine",
    )(dest2, x1, top_w, mod4, g_post.reshape(1, d), y)


def _rope_partner(w):
    n_freq = ML_QK_DIM // 4
    d = w.shape[0]
    w4 = w.reshape(d, -1, 2, n_freq)
    return jnp.stack([-w4[:, :, 1], w4[:, :, 0]], axis=2).reshape(w.shape)


def _rope_tables(seq):
    n_freq = ML_QK_DIM // 4
    t = jnp.arange(seq)
    row = (t // GRID_W).astype(F32)
    col = (t % GRID_W).astype(F32)
    inv_freq = ROPE_BASE ** (-jnp.arange(n_freq, dtype=F32) / n_freq)
    ang = jnp.concatenate([row[:, None] * inv_freq] * 2 + [col[:, None] * inv_freq] * 2, axis=-1)
    cos = jnp.tile(jnp.cos(ang), (1, ML_HEADS))
    sin = jnp.tile(jnp.sin(ang), (1, ML_HEADS))
    return cos, sin


def _layer(x, ctx, mod4, g_mix_pre, g_mix_post, g_ffn_pre, g_ffn_post, w_in, b_gates, rpb, g_head,
           w_branch_na, w_branch_ml, w_out, w_router, b_router, w_gate, b_gate, w_lin, b_lin, w_down, b_down):
    b, s, d = x.shape
    n = b * s
    n_ctx = ctx.shape[1]
    x2 = x.reshape(n, d)

    ctx_cols = (NA_WIDTH, NA_WIDTH, ML_QK_WIDTH, ML_WIDTH, N_GATES)
    lat_cols = (NA_WIDTH, ML_QK_WIDTH, ML_WIDTH, d, d)
    bounds = np.cumsum(ctx_cols + lat_cols)[:-1].tolist()
    (w_nak, w_nav, w_mk, w_mv, w_g, w_naq, w_mq, w_mo, w_gna, w_gml) = jnp.split(w_in, bounds, axis=-1)
    w_naq = w_naq * (NA_HEAD_DIM ** -0.5 * LOG2_E)
    w_mk = w_mk * ML_QK_DIM ** -0.5
    bf = lambda a: a.astype(BF16)
    lat_w = [bf(w_naq), bf(w_nak), bf(w_nav), bf(w_mq), bf(_rope_partner(w_mq)), bf(w_mk.T),
             bf(_rope_partner(w_mk).T), bf(w_mv), bf(w_mo), bf(w_gna), bf(w_gml), bf(w_g), bf(w_g.T)]
    ctx_w = [bf(w_nak), bf(w_nav), bf(w_mk.T), bf(w_mv), bf(w_g), bf(w_g.T)]
    bg_col = b_gates.reshape(1, N_GATES).astype(F32)
    bg_row = b_gates.reshape(N_GATES, 1).astype(F32)
    cos, sin = _rope_tables(s)
    per_seq = s // PROJ_ROWS

    (na_q, na_k, na_v, ml_q, ml_k, ml_v, ml_o, gate_na, gate_ml, g_col, g_row) = _project(
        x2, mod4, lambda i: i // per_seq, g_mix_pre, bg_col, bg_row, lat_w, (cos, sin, per_seq), True)
    (na_kc, na_vc, ml_kc, ml_vc, gc_col, gc_row) = _project(
        ctx.reshape(b * n_ctx, d), mod4, lambda i: b, g_mix_pre, bg_col, bg_row, ctx_w, None, False)

    def seq3(a, length):
        return a.reshape(b, length, a.shape[-1])

    o_na = _neighbourhood_attention(seq3(na_q, s), seq3(na_k, s), seq3(na_v, s),
                                    seq3(na_kc, n_ctx), seq3(na_vc, n_ctx), rpb)
    ml_args = (seq3(ml_q, s), ml_k, seq3(ml_v, s), seq3(g_col, s), g_row,
               ml_kc, seq3(ml_vc, n_ctx), seq3(gc_col, n_ctx), gc_row)
    h_f, h_b = _mlstm(*ml_args)

    x1, h2, top_e, top_w, rank, counts = _merge_and_route(
        x2, o_na.reshape(n, NA_WIDTH), h_f.reshape(n, ML_WIDTH), h_b.reshape(n, ML_WIDTH), ml_o,
        gate_na, gate_ml, mod4, s, g_head, bf(w_branch_na), bf(w_branch_ml), bf(w_out),
        g_mix_post, g_ffn_pre, w_router, b_router)

    tm = EXPERT_ROWS
    counts = counts.reshape(N_EXPERTS).astype(jnp.int32)
    padded = (counts + tm - 1) // tm * tm
    pad_end = jnp.cumsum(padded)
    pad_start = pad_end - padded
    n_rows = n * TOP_K + N_EXPERTS * tm
    n_blocks = n_rows // tm
    e_sel = top_e[:, :TOP_K, None] == jnp.arange(N_EXPERTS, dtype=jnp.int32)
    dest = jnp.sum(jnp.where(e_sel, pad_start, 0), axis=-1) + rank[:, :TOP_K]
    dest2 = dest.reshape(n * TOP_K // LANES, LANES).astype(jnp.int32)
    blk_start = jnp.arange(n_blocks, dtype=jnp.int32) * tm
    blk_e = jnp.minimum(jnp.sum(blk_start[:, None] >= pad_end[None, :], axis=1), N_EXPERTS - 1).astype(jnp.int32)
    n_used = (pad_end[-1:] // tm).astype(jnp.int32)

    xs = _dispatch(pad_end.astype(jnp.int32), padded.astype(jnp.int32), dest2, h2, n_rows)
    y = _experts(blk_e, n_used, xs, w_gate, b_gate, w_lin, b_lin, w_down, b_down)
    out = _combine(dest2, x1, top_w, mod4, s, g_ffn_post, y)
    return out.reshape(b, s, d)


def kernel(x, c, ctx, c_ctx, w_ada, b_ada, g_mix_pre, g_mix_post, g_ffn_pre, g_ffn_post, w_in, b_mlstm_gates,
           rpb, g_mlstm_head, w_branch_na, w_branch_ml, w_out, w_router, b_router, w_gate, b_gate, w_lin,
           b_lin, w_down, b_down):
    b, s, d = x.shape
    depth = w_ada.shape[0]
    pad = (-(b + 1)) % 8
    c_all = jnp.concatenate([c, c_ctx[None, :], jnp.zeros((pad, d), c.dtype)], axis=0)
    for layer in range(depth):
        mod = _ada(c_all, w_ada[layer], b_ada[layer])
        mod4 = mod.reshape(mod.shape[0], 6, 1, d)
        x = _layer(x, ctx, mod4, g_mix_pre[layer], g_mix_post[layer], g_ffn_pre[layer], g_ffn_post[layer],
                   w_in[layer], b_mlstm_gates[layer], rpb[layer], g_mlstm_head[layer], w_branch_na[layer],
                   w_branch_ml[layer], w_out[layer], w_router[layer], b_router[layer], w_gate[layer],
                   b_gate[layer], w_lin[layer], b_lin[layer], w_down[layer], b_down[layer])
    return x
```

```python
import functools

import numpy as np
import jax
import jax.numpy as jnp
from jax import lax
from jax.experimental import pallas as pl
from jax.experimental.pallas import tpu as pltpu

F32 = jnp.float32
BF16 = jnp.bfloat16
HIGHEST = lax.Precision.HIGHEST

GRID_W = 64
NA_HEADS = 8
NA_HEAD_DIM = 64
NA_WIDTH = NA_HEADS * NA_HEAD_DIM
WIN_H = 8
WIN_W = 16
ML_HEADS = 4
ML_QK_DIM = 64
ML_V_DIM = 128
ML_QK_WIDTH = ML_HEADS * ML_QK_DIM
ML_WIDTH = ML_HEADS * ML_V_DIM
N_GATES = 4 * ML_HEADS
GATE_SOFTCAP = 15.0
ROPE_BASE = 10000.0
N_EXPERTS = 32
TOP_K = 4
SWIGLU_ALPHA = 1.702
SWIGLU_LIMIT = 7.0
NORM_EPS = 1e-6
NEG_INF = -1e30
LOG2_E = 1.4426950408889634

LANES = 128
NA_QROWS = 4
NA_WROWS = 12
ML_CHUNK = 256
PROJ_ROWS = 512
MERGE_ROWS = 512
EXPERT_ROWS = 512
MOVE_ROWS = 512
VMEM_LIMIT = 56 * 1024 * 1024

NT_DIMS = (((1,), (1,)), ((), ()))


def _cparams(*sem):
    return pltpu.CompilerParams(dimension_semantics=sem, vmem_limit_bytes=VMEM_LIMIT)


def _rms(x):
    return x * lax.rsqrt(jnp.mean(x * x, axis=-1, keepdims=True) + NORM_EPS)


def _ada_kernel(c_ref, w_ref, b_ref, o_ref):
    c = c_ref[...]
    s = c * jax.nn.sigmoid(c)
    o_ref[...] = jnp.dot(s, w_ref[...], preferred_element_type=F32, precision=HIGHEST) + b_ref[...]


def _ada(c_all, w_ada, b_ada):
    rows, d = c_all.shape
    n_out = w_ada.shape[1]
    tn = 1536
    return pl.pallas_call(
        _ada_kernel,
        grid=(n_out // tn,),
        in_specs=[pl.BlockSpec((rows, d), lambda j: (0, 0)),
                  pl.BlockSpec((d, tn), lambda j: (0, j)),
                  pl.BlockSpec((1, tn), lambda j: (0, j))],
        out_specs=pl.BlockSpec((rows, tn), lambda j: (0, j)),
        out_shape=jax.ShapeDtypeStruct((rows, n_out), F32),
        compiler_params=_cparams("arbitrary"),
        name="ada_mod",
    )(c_all, w_ada, b_ada.reshape(1, n_out))


def _gate_logs(g, is_forget):
    g = GATE_SOFTCAP * jnp.tanh(g / GATE_SOFTCAP)
    log_sig = jnp.minimum(g, 0.0) - jnp.log(1.0 + jnp.exp(-jnp.abs(g)))
    return jnp.where(is_forget, log_sig, g)


def _proj_kernel(*refs, latent):
    if latent:
        (x_ref, g_ref, sc_ref, sh_ref, cos_ref, sin_ref, cos_t_ref, sin_t_ref, bgc_ref, bgr_ref,
         w_naq, w_nak, w_nav, w_mq, w_mqp, w_mk, w_mkp, w_mv, w_mo, w_gna, w_gml, w_gc, w_gr,
         o_naq, o_nak, o_nav, o_mq, o_mk, o_mv, o_mo, o_gna, o_gml, o_gc, o_gr) = refs
    else:
        (x_ref, g_ref, sc_ref, sh_ref, bgc_ref, bgr_ref,
         w_nak, w_nav, w_mk, w_mv, w_gc, w_gr,
         o_nak, o_nav, o_mk, o_mv, o_gc, o_gr) = refs
    x = x_ref[...]
    h = _rms(x) * g_ref[...]
    h = h * (1.0 + sc_ref[...]) + sh_ref[...]
    hb = h.astype(BF16)

    def mm(w_ref):
        return jnp.dot(hb, w_ref[...], preferred_element_type=F32)

    def mm_t(w_ref):
        return lax.dot_general(w_ref[...], hb, NT_DIMS, preferred_element_type=F32)

    o_nak[...] = mm(w_nak).astype(BF16)
    o_nav[...] = mm(w_nav).astype(BF16)
    o_mv[...] = mm(w_mv).astype(BF16)
    if latent:
        cos = cos_ref[...]
        sin = sin_ref[...]
        o_naq[...] = mm(w_naq).astype(BF16)
        o_mq[...] = (mm(w_mq) * cos + mm(w_mqp) * sin).astype(BF16)
        o_mk[...] = (mm_t(w_mk) * cos_t_ref[...] + mm_t(w_mkp) * sin_t_ref[...]).astype(BF16)
        o_mo[...] = mm(w_mo).astype(BF16)
        o_gna[...] = mm(w_gna).astype(BF16)
        o_gml[...] = mm(w_gml).astype(BF16)
    else:
        o_mk[...] = mm_t(w_mk).astype(BF16)
    gc = mm(w_gc) + bgc_ref[...]
    col_id = lax.broadcasted_iota(jnp.int32, gc.shape, 1)
    o_gc[...] = _gate_logs(gc, (col_id // ML_HEADS) % 2 == 1)
    gr = mm_t(w_gr) + bgr_ref[...]
    row_id = lax.broadcasted_iota(jnp.int32, gr.shape, 0)
    o_gr[...] = _gate_logs(gr, (row_id // ML_HEADS) % 2 == 1)


def _const_spec(shape):
    nd = len(shape)
    return pl.BlockSpec(shape, lambda i, _nd=nd: (0,) * _nd)


def _project(x2, mod4, mod_row_fn, g_pre, bg_col, bg_row, weights, tables, latent):
    n, d = x2.shape
    tm = PROJ_ROWS
    grid = (n // tm,)

    def mod_spec(j):
        return pl.BlockSpec((None, None, 1, d), lambda i, _j=j: (mod_row_fn(i), _j, 0, 0))

    in_specs = [pl.BlockSpec((tm, d), lambda i: (i, 0)), _const_spec((1, d)), mod_spec(1), mod_spec(0)]
    args = [x2, g_pre.reshape(1, d), mod4, mod4]
    if latent:
        cos, sin, tiles_per_seq = tables
        in_specs += [pl.BlockSpec((tm, ML_QK_WIDTH), lambda i: (i % tiles_per_seq, 0))] * 2
        in_specs += [pl.BlockSpec((ML_QK_WIDTH, tm), lambda i: (0, i % tiles_per_seq))] * 2
        args += [cos, sin, cos.T, sin.T]
    in_specs += [_const_spec(bg_col.shape), _const_spec(bg_row.shape)]
    args += [bg_col, bg_row]
    for w in weights:
        in_specs.append(_const_spec(w.shape))
        args.append(w)

    def out(width, dtype=BF16):
        return (jax.ShapeDtypeStruct((n, width), dtype), pl.BlockSpec((tm, width), lambda i: (i, 0)))

    def out_t(width, dtype=BF16):
        return (jax.ShapeDtypeStruct((width, n), dtype), pl.BlockSpec((width, tm), lambda i: (0, i)))

    if latent:
        outs = [out(NA_WIDTH), out(NA_WIDTH), out(NA_WIDTH), out(ML_QK_WIDTH), out_t(ML_QK_WIDTH),
                out(ML_WIDTH), out(ML_WIDTH), out(d), out(d), out(N_GATES, F32)]
    else:
        outs = [out(NA_WIDTH), out(NA_WIDTH), out_t(ML_QK_WIDTH), out(ML_WIDTH), out(N_GATES, F32)]
    outs.append(out_t(N_GATES, F32))
    return pl.pallas_call(
        functools.partial(_proj_kernel, latent=latent),
        grid=grid,
        in_specs=in_specs,
        out_specs=[o[1] for o in outs],
        out_shape=[o[0] for o in outs],
        compiler_params=_cparams("arbitrary"),
        name="in_proj_latent" if latent else "in_proj_ctx",
    )(*args)


def _na_window_start(r0, rows):
    return jnp.clip(r0 - WIN_H // 2, 0, rows - NA_WROWS)


def _na_classes(rows):
    keys, group_class = [], []
    for r0 in range(0, rows, NA_QROWS):
        start = min(max(r0 - WIN_H // 2, 0), rows - NA_WROWS)
        first = tuple(min(max(r0 + i - WIN_H // 2, 0), rows - WIN_H) - start for i in range(NA_QROWS))
        assert all(0 <= f and f + WIN_H <= NA_WROWS for f in first)
        key = (r0 - start, first)
        if key not in keys:
            keys.append(key)
        group_class.append(keys.index(key))
    return keys, np.asarray(group_class, np.int32)


def _na_bias_table(rpb, rows):
    keys, group_class = _na_classes(rows)
    qc = np.arange(GRID_W)[:, None]
    kc = np.arange(GRID_W)[None, :]
    cs = np.clip(qc - WIN_W // 2, 0, GRID_W - WIN_W)
    col_ok = (kc >= cs) & (kc < cs + WIN_W)
    dc = np.clip(kc - qc, -(WIN_W - 1), WIN_W - 1) + (WIN_W - 1)
    i = np.arange(NA_QROWS)[:, None]
    j = np.arange(NA_WROWS)[None, :]
    sel_r, row_ok = [], []
    for off, first in keys:
        f = np.asarray(first)[:, None]
        ok = (j >= f) & (j < f + WIN_H)
        dr = j - off - i + (WIN_H - 1)
        sel_r.append(((dr[:, :, None] == np.arange(2 * WIN_H - 1)) & ok[:, :, None]).astype(np.float32))
        row_ok.append(ok)
    sel_r = np.stack(sel_r)
    valid = np.stack(row_ok)[:, None, :, None, :, None] & col_ok[None, None, None, :, None, :]
    sel_c = (dc[:, :, None] == np.arange(2 * WIN_W - 1)).astype(np.float32)
    t = jnp.einsum('hrc,qkc->hrqk', rpb.astype(F32), sel_c, precision=HIGHEST)
    bias = jnp.einsum('hrqk,xijr->xhiqjk', t, sel_r, precision=HIGHEST)
    bias = jnp.where(valid, bias * LOG2_E, NEG_INF)
    bias = bias.reshape(len(keys), NA_HEADS, NA_QROWS * GRID_W, NA_WROWS * GRID_W)
    return bias.astype(BF16), jnp.asarray(group_class)


def _na_kernel(cls_ref, q_ref, k_ref, v_ref, kc_ref, vc_ref, bias_ref, o_ref, *, rows):
    del cls_ref
    r0 = pl.program_id(1) * NA_QROWS
    start = pl.multiple_of(_na_window_start(r0, rows) * GRID_W, GRID_W)
    n_win = NA_WROWS * GRID_W
    low = lax.broadcasted_iota(jnp.int32, (1, LANES), 1) < NA_HEAD_DIM
    for pair in range(NA_HEADS // 2):
        sl = slice(pair * LANES, (pair + 1) * LANES)
        q2 = q_ref[:, sl]
        k2 = k_ref[pl.ds(start, n_win), sl]
        v2 = v_ref[pl.ds(start, n_win), sl]
        kc2 = kc_ref[:, sl]
        vc2 = vc_ref[:, sl]
        halves = []
        for half in range(2):
            keep = low if half == 0 else jnp.logical_not(low)
            qh = jnp.where(keep, q2, jnp.zeros_like(q2))
            s_loc = (lax.dot_general(qh, k2, NT_DIMS, preferred_element_type=F32)
                     + bias_ref[2 * pair + half].astype(F32))
            s_ctx = lax.dot_general(qh, kc2, NT_DIMS, preferred_element_type=F32)
            m = jnp.maximum(jnp.max(s_loc, axis=-1, keepdims=True), jnp.max(s_ctx, axis=-1, keepdims=True))
            p_loc = jnp.exp2(s_loc - m)
            p_ctx = jnp.exp2(s_ctx - m)
            denom = jnp.sum(p_loc, axis=-1, keepdims=True) + jnp.sum(p_ctx, axis=-1, keepdims=True)
            o = (jnp.dot(p_loc.astype(BF16), v2, preferred_element_type=F32)
                 + jnp.dot(p_ctx.astype(BF16), vc2, preferred_element_type=F32))
            halves.append(o / denom)
        o_ref[:, sl] = jnp.where(low, halves[0], halves[1]).astype(BF16)


def _neighbourhood_attention(q, k, v, kc, vc, rpb):
    b, s, w = q.shape
    rows = s // GRID_W
    n_ctx = kc.shape[1]
    bias, group_class = _na_bias_table(rpb, rows)
    nq = NA_QROWS * GRID_W
    grid_spec = pltpu.PrefetchScalarGridSpec(
        num_scalar_prefetch=1,
        grid=(b, rows // NA_QROWS),
        in_specs=[pl.BlockSpec((None, nq, w), lambda bi, g, cls: (bi, g, 0)),
                  pl.BlockSpec((None, s, w), lambda bi, g, cls: (bi, 0, 0)),
                  pl.BlockSpec((None, s, w), lambda bi, g, cls: (bi, 0, 0)),
                  pl.BlockSpec((None, n_ctx, w), lambda bi, g, cls: (bi, 0, 0)),
                  pl.BlockSpec((None, n_ctx, w), lambda bi, g, cls: (bi, 0, 0)),
                  pl.BlockSpec((None,) + bias.shape[1:], lambda bi, g, cls: (cls[g], 0, 0, 0))],
        out_specs=pl.BlockSpec((None, nq, w), lambda bi, g, cls: (bi, g, 0)),
    )
    return pl.pallas_call(
        functools.partial(_na_kernel, rows=rows),
        grid_spec=grid_spec,
        out_shape=jax.ShapeDtypeStruct((b, s, w), BF16),
        compiler_params=_cparams("arbitrary", "arbitrary"),
        name="na_attention",
    )(group_class, q, k, v, kc, vc, bias)


def _split3(x):
    hi = x.astype(BF16)
    r1 = x - hi.astype(F32)
    mid = r1.astype(BF16)
    lo = (r1 - mid.astype(F32)).astype(BF16)
    return hi, mid, lo


def _mlstm_direction_step(reverse, is_ctx, q, kt, v, gc, gr, o_ref, c_scr, m_scr):
    li_base = 2 * ML_HEADS if reverse else 0
    lf_base = li_base + ML_HEADS
    length = kt.shape[1]
    t = lax.broadcasted_iota(jnp.int32, (length, length), 0)
    s = lax.broadcasted_iota(jnp.int32, (length, length), 1)
    valid = (s >= t) if reverse else (s <= t)
    valid_t = (t >= s) if reverse else (t <= s)
    b_col = sum(jnp.dot(valid.astype(BF16), p, preferred_element_type=F32) for p in _split3(gc))
    b_row = sum(jnp.dot(p, valid_t.astype(BF16), preferred_element_type=F32) for p in _split3(gr))
    low = lax.broadcasted_iota(jnp.int32, (1, LANES), 1) < ML_QK_DIM
    ones = jnp.ones((length, ML_V_DIM), BF16)

    for h in range(ML_HEADS):
        pair, half = divmod(h, 2)
        rows = slice(half * ML_QK_DIM, (half + 1) * ML_QK_DIM)
        li_r = gr[li_base + h:li_base + h + 1, :]
        lf_r = gr[lf_base + h:lf_base + h + 1, :]
        br = b_row[lf_base + h:lf_base + h + 1, :]
        m_prev = m_scr[h]
        v_ext = jnp.concatenate([v[:, h * ML_V_DIM:(h + 1) * ML_V_DIM], ones], axis=-1)
        kt_pair = kt[pair * LANES:(pair + 1) * LANES, :]

        if not is_ctx:
            bc = b_col[:, lf_base + h:lf_base + h + 1]
            q2 = q[:, pair * LANES:(pair + 1) * LANES]
            qh = jnp.where(low if half == 0 else jnp.logical_not(low), q2, jnp.zeros_like(q2))
            d_mat = jnp.where(valid, bc - br + li_r, NEG_INF)
            m_inter = bc + m_prev
            m_t = jnp.maximum(m_inter, jnp.max(d_mat, axis=-1, keepdims=True))
            w_intra = jnp.exp(d_mat - m_t)
            w_inter = jnp.exp(m_inter - m_t)
            sc = jnp.dot(qh, kt_pair, preferred_element_type=F32) * w_intra
            both = (w_inter * jnp.dot(qh, c_scr[pair].astype(BF16), preferred_element_type=F32)
                    + jnp.dot(sc.astype(BF16), v_ext, preferred_element_type=F32))
            num = both[:, :ML_V_DIM]
            den = both[:, ML_V_DIM:]
            o_ref[:, h * ML_V_DIM:(h + 1) * ML_V_DIM] = (
                num / jnp.maximum(jnp.abs(den), jnp.exp(-m_t))).astype(BF16)

        total = jnp.sum(lf_r, axis=-1, keepdims=True)
        m_new = jnp.maximum(total + m_prev, jnp.max(total - br + li_r, axis=-1, keepdims=True))
        w_src = jnp.exp(total - br + li_r - m_new)
        w_carry = jnp.exp(total + m_prev - m_new)
        kw = (kt_pair[rows, :].astype(F32) * w_src).astype(BF16)
        c_scr[pair, rows, :] = w_carry * c_scr[pair, rows, :] + jnp.dot(kw, v_ext, preferred_element_type=F32)
        m_scr[h] = m_new


def _mlstm_kernel(qf_ref, kf_ref, vf_ref, gcf_ref, grf_ref, qb_ref, kb_ref, vb_ref, gcb_ref, grb_ref,
                  kc_ref, vc_ref, gcc_ref, gcr_ref, of_ref, ob_ref, cf_scr, mf_scr, cb_scr, mb_scr):
    step = pl.program_id(1)

    @pl.when(step == 0)
    def _():
        for ref in (cf_scr, mf_scr, cb_scr, mb_scr):
            ref[...] = jnp.zeros_like(ref)
        kt = kc_ref[...]
        v = vc_ref[...]
        gc = gcc_ref[...]
        gr = gcr_ref[...]
        _mlstm_direction_step(False, True, None, kt, v, gc, gr, None, cf_scr, mf_scr)
        _mlstm_direction_step(True, True, None, kt, v, gc, gr, None, cb_scr, mb_scr)

    @pl.when(step > 0)
    def _():
        _mlstm_direction_step(False, False, qf_ref[...], kf_ref[...], vf_ref[...], gcf_ref[...], grf_ref[...],
                              of_ref, cf_scr, mf_scr)
        _mlstm_direction_step(True, False, qb_ref[...], kb_ref[...], vb_ref[...], gcb_ref[...], grb_ref[...],
                              ob_ref, cb_scr, mb_scr)


def _mlstm(q, k, v, gc, gr, kc, vc, gcc, gcr):
    b, s, _ = q.shape
    n_ctx = vc.shape[1]
    length = ML_CHUNK
    n_chunks = s // length

    def chunk(step, reverse):
        c = jnp.maximum(step - 1, 0)
        return (n_chunks - 1 - c) if reverse else c

    def stream_specs(reverse):
        def seq(width):
            return pl.BlockSpec((None, length, width), lambda bi, st: (bi, chunk(st, reverse), 0))

        def seq_t(width):
            return pl.BlockSpec((width, length), lambda bi, st: (0, bi * n_chunks + chunk(st, reverse)))
        return [seq(ML_QK_WIDTH), seq_t(ML_QK_WIDTH), seq(ML_WIDTH), seq(N_GATES), seq_t(N_GATES)]

    def out_spec(reverse):
        return pl.BlockSpec((None, length, ML_WIDTH), lambda bi, st: (bi, chunk(st, reverse), 0))

    state = [pltpu.VMEM((ML_HEADS // 2, 2 * ML_QK_DIM, 2 * ML_V_DIM), F32),
             pltpu.VMEM((ML_HEADS, 1, 1), F32)]
    return pl.pallas_call(
        _mlstm_kernel,
        grid=(b, n_chunks + 1),
        in_specs=stream_specs(False) + stream_specs(True) + [
            pl.BlockSpec((ML_QK_WIDTH, n_ctx), lambda bi, st: (0, bi)),
            pl.BlockSpec((None, n_ctx, ML_WIDTH), lambda bi, st: (bi, 0, 0)),
            pl.BlockSpec((None, n_ctx, N_GATES), lambda bi, st: (bi, 0, 0)),
            pl.BlockSpec((N_GATES, n_ctx), lambda bi, st: (0, bi))],
        out_specs=[out_spec(False), out_spec(True)],
        out_shape=[jax.ShapeDtypeStruct((b, s, ML_WIDTH), BF16)] * 2,
        scratch_shapes=state + state,
        compiler_params=_cparams("arbitrary", "arbitrary"),
        name="mlstm",
    )(q, k, v, gc, gr, q, k, v, gc, gr, kc, vc, gcc, gcr)


def _merge_kernel(x_ref, ona_ref, hf_ref, hb_ref, opre_ref, gna_ref, gml_ref, gtm_ref, scf_ref, shf_ref,
                  ghead_ref, wbna_ref, wbml_ref, wout_ref, gpost_ref, gpre_ref, wrh_ref, wrl_ref, br_ref,
                  x1_ref, h2_ref, tope_ref, topw_ref, rank_ref, cnt_ref):
    step = pl.program_id(0)
    tm = x_ref.shape[0]

    @pl.when(step == 0)
    def _():
        cnt_ref[...] = jnp.zeros_like(cnt_ref)

    hsum = hf_ref[...].astype(F32) + hb_ref[...].astype(F32)
    heads = [_rms(hsum[:, h * ML_V_DIM:(h + 1) * ML_V_DIM]) for h in range(ML_HEADS)]
    hn = jnp.concatenate(heads, axis=-1) * ghead_ref[...]
    o_ml = jax.nn.sigmoid(opre_ref[...].astype(F32)) * hn
    merged = (jax.nn.sigmoid(gna_ref[...].astype(F32))
              * jnp.dot(ona_ref[...], wbna_ref[...], preferred_element_type=F32)
              + jax.nn.sigmoid(gml_ref[...].astype(F32))
              * jnp.dot(o_ml.astype(BF16), wbml_ref[...], preferred_element_type=F32))
    mixed = jnp.dot(merged.astype(BF16), wout_ref[...], preferred_element_type=F32)
    x1 = x_ref[...] + gtm_ref[...] * (_rms(mixed) * gpost_ref[...])
    x1_ref[...] = x1
    h2 = _rms(x1) * gpre_ref[...] * (1.0 + scf_ref[...]) + shf_ref[...]
    h2_ref[...] = h2
    h2_hi = h2.astype(BF16)
    h2_lo = (h2 - h2_hi.astype(F32)).astype(BF16)
    logits = (jnp.dot(h2_hi, wrh_ref[...], preferred_element_type=F32)
              + (jnp.dot(h2_hi, wrl_ref[...], preferred_element_type=F32)
                 + jnp.dot(h2_lo, wrh_ref[...], preferred_element_type=F32))) + br_ref[...]

    lane = lax.broadcasted_iota(jnp.int32, logits.shape, 1)
    onehots, top_e, top_l = [], [], []
    for _ in range(TOP_K):
        best = jnp.max(logits, axis=-1, keepdims=True)
        e = jnp.min(jnp.where(logits == best, lane, N_EXPERTS), axis=-1, keepdims=True)
        hit = lane == e
        onehots.append(hit)
        top_e.append(e)
        top_l.append(best)
        logits = jnp.where(hit, -jnp.inf, logits)
    exps = [jnp.exp(l - top_l[0]) for l in top_l]
    total = exps[0] + exps[1] + exps[2] + exps[3]

    counts = (onehots[0].astype(F32) + onehots[1].astype(F32)
              + onehots[2].astype(F32) + onehots[3].astype(F32))
    t = lax.broadcasted_iota(jnp.int32, (tm, tm), 0)
    s = lax.broadcasted_iota(jnp.int32, (tm, tm), 1)
    before = jnp.dot((s < t).astype(BF16), counts.astype(BF16), preferred_element_type=F32) + cnt_ref[...]
    out_lane = lax.broadcasted_iota(jnp.int32, (tm, LANES), 1)
    e_out = jnp.zeros((tm, LANES), jnp.int32)
    w_out = jnp.zeros((tm, LANES), F32)
    r_out = jnp.zeros((tm, LANES), jnp.int32)
    for j in range(TOP_K):
        rank = jnp.sum(jnp.where(onehots[j], before, 0.0), axis=-1, keepdims=True).astype(jnp.int32)
        e_out = jnp.where(out_lane == j, top_e[j], e_out)
        w_out = jnp.where(out_lane == j, exps[j] / total, w_out)
        r_out = jnp.where(out_lane == j, rank, r_out)
    tope_ref[...] = e_out
    topw_ref[...] = w_out
    rank_ref[...] = r_out
    cnt_ref[...] += jnp.sum(counts, axis=0, keepdims=True)


def _merge_and_route(x2, o_na, h_f, h_b, o_pre, g_na, g_ml, mod4, seq, g_head, wbna, wbml, wout,
                     g_post, g_pre, w_router, b_router):
    n, d = x2.shape
    tm = MERGE_ROWS
    per_seq = seq // tm

    def rows(width):
        return pl.BlockSpec((tm, width), lambda i: (i, 0))

    def mod_spec(j):
        return pl.BlockSpec((None, None, 1, d), lambda i, _j=j: (i // per_seq, _j, 0, 0))

    w_router_hi = w_router.astype(BF16)
    return pl.pallas_call(
        _merge_kernel,
        grid=(n // tm,),
        in_specs=[rows(d), rows(NA_WIDTH), rows(ML_WIDTH), rows(ML_WIDTH), rows(ML_WIDTH), rows(d), rows(d),
                  mod_spec(2), mod_spec(4), mod_spec(3),
                  _const_spec((1, ML_WIDTH)), _const_spec(wbna.shape), _const_spec(wbml.shape),
                  _const_spec(wout.shape), _const_spec((1, d)), _const_spec((1, d)),
                  _const_spec(w_router.shape), _const_spec(w_router.shape), _const_spec((1, N_EXPERTS))],
        out_specs=[rows(d), rows(d), rows(LANES), rows(LANES), rows(LANES),
                   pl.BlockSpec((1, N_EXPERTS), lambda i: (0, 0))],
        out_shape=[jax.ShapeDtypeStruct((n, d), F32), jax.ShapeDtypeStruct((n, d), F32),
                   jax.ShapeDtypeStruct((n, LANES), jnp.int32), jax.ShapeDtypeStruct((n, LANES), F32),
                   jax.ShapeDtypeStruct((n, LANES), jnp.int32),
                   jax.ShapeDtypeStruct((1, N_EXPERTS), F32)],
        compiler_params=_cparams("arbitrary"),
        name="merge_route",
    )(x2, o_na, h_f, h_b, o_pre, g_na, g_ml, mod4, mod4, mod4,
      g_head.reshape(1, ML_WIDTH), wbna, wbml, wout, g_post.reshape(1, d), g_pre.reshape(1, d),
      w_router_hi, (w_router - w_router_hi.astype(F32)).astype(BF16), b_router.reshape(1, N_EXPERTS))


def _dispatch_kernel(pad_end_ref, padded_ref, dest_ref, h_ref, xs_ref, zero_scr, sem):
    tm = h_ref.shape[0]
    blk = zero_scr.shape[0]

    @pl.when(pl.program_id(0) == 0)
    def _():
        zero_scr[...] = jnp.zeros_like(zero_scr)

        def zero_copy(e):
            first = pl.multiple_of(pad_end_ref[e] - blk, blk)
            return pltpu.make_async_copy(zero_scr, xs_ref.at[pl.ds(first, blk)], sem)

        for e in range(N_EXPERTS):
            @pl.when(padded_ref[e] > 0)
            def _():
                zero_copy(e).start()
        for e in range(N_EXPERTS):
            @pl.when(padded_ref[e] > 0)
            def _():
                zero_copy(e).wait()

        def tail_copy(b):
            return pltpu.make_async_copy(zero_scr, xs_ref.at[pl.ds(pl.multiple_of(b * blk, blk), blk)], sem)

        def tail_start(b, carry):
            tail_copy(b).start()
            return carry

        def tail_wait(b, carry):
            tail_copy(b).wait()
            return carry

        first_unused = pad_end_ref[N_EXPERTS - 1] // blk
        lax.fori_loop(first_unused, xs_ref.shape[0] // blk, tail_start, 0)
        lax.fori_loop(first_unused, xs_ref.shape[0] // blk, tail_wait, 0)

    tok_per_row = LANES // TOP_K

    def row_copy(r, c):
        return pltpu.make_async_copy(h_ref.at[pl.ds(r * tok_per_row + c // TOP_K, 1)],
                                     xs_ref.at[pl.ds(dest_ref[r, c], 1)], sem)

    for r in range(tm // tok_per_row):
        for c in range(LANES):
            row_copy(r, c).start(priority=c % 2)
    for r in range(tm // tok_per_row):
        for c in range(LANES):
            row_copy(r, c).wait()


def _dispatch(pad_end, padded, dest2, h2, n_rows):
    n = h2.shape[0]
    tm = MOVE_ROWS
    idx_rows = tm * TOP_K // LANES
    row_tile = h2.shape[1:]
    grid_spec = pltpu.PrefetchScalarGridSpec(
        num_scalar_prefetch=2,
        grid=(n // tm,),
        in_specs=[pl.BlockSpec((idx_rows, LANES), lambda i, pe, pd: (i, 0), memory_space=pltpu.SMEM),
                  pl.BlockSpec((tm,) + row_tile, lambda i, pe, pd: (i, 0))],
        out_specs=pl.BlockSpec(memory_space=pl.ANY),
        scratch_shapes=[pltpu.VMEM((EXPERT_ROWS,) + row_tile, h2.dtype), pltpu.SemaphoreType.DMA(())],
    )
    return pl.pallas_call(
        _dispatch_kernel,
        grid_spec=grid_spec,
        out_shape=jax.ShapeDtypeStruct((n_rows,) + row_tile, h2.dtype),
        compiler_params=_cparams("arbitrary"),
        name="moe_dispatch",
    )(pad_end, padded, dest2, h2)


def _expert_kernel(blk_e_ref, n_used_ref, x_ref, wg_ref, bg_ref, wl_ref, bl_ref, wd_ref, bd_ref, y_ref,
                   wg_s, wl_s, wd_s):
    i = pl.program_id(0)
    prev = blk_e_ref[jnp.maximum(i - 1, 0)]
    changed = jnp.logical_or(i == 0, blk_e_ref[i] != prev)
    used = i < n_used_ref[0]

    @pl.when(jnp.logical_and(used, changed))
    def _():
        wg_s[...] = wg_ref[...].astype(BF16)
        wl_s[...] = wl_ref[...].astype(BF16)
        wd_s[...] = wd_ref[...].astype(BF16)

    @pl.when(used)
    def _():
        xb = x_ref[...].astype(BF16)
        g = jnp.dot(xb, wg_s[...], preferred_element_type=F32) + bg_ref[...]
        l = jnp.dot(xb, wl_s[...], preferred_element_type=F32) + bl_ref[...]
        g = jnp.minimum(g, SWIGLU_LIMIT)
        l = jnp.clip(l, -SWIGLU_LIMIT, SWIGLU_LIMIT)
        a = g * jax.nn.sigmoid(SWIGLU_ALPHA * g) * (l + 1.0)
        y_ref[...] = jnp.dot(a.astype(BF16), wd_s[...], preferred_element_type=F32) + bd_ref[...]

    @pl.when(jnp.logical_not(used))
    def _():
        y_ref[...] = jnp.zeros_like(y_ref)


def _experts(blk_e, n_used, xs, w_gate, b_gate, w_lin, b_lin, w_down, b_down):
    n_rows = xs.shape[0]
    row_tile = xs.shape[1:]
    e, d, f = w_gate.shape
    tm = EXPERT_ROWS

    def w_spec(shape):
        return pl.BlockSpec((None,) + shape, lambda i, be, nu: (be[i], 0, 0))

    grid_spec = pltpu.PrefetchScalarGridSpec(
        num_scalar_prefetch=2,
        grid=(n_rows // tm,),
        in_specs=[pl.BlockSpec((tm,) + row_tile, lambda i, be, nu: (i, 0)),
                  w_spec((d, f)), w_spec((1, f)), w_spec((d, f)), w_spec((1, f)),
                  w_spec((f, d)), w_spec((1, d))],
        out_specs=pl.BlockSpec((tm,) + row_tile, lambda i, be, nu: (i, 0)),
        scratch_shapes=[pltpu.VMEM((d, f), BF16), pltpu.VMEM((d, f), BF16), pltpu.VMEM((f, d), BF16)],
    )
    return pl.pallas_call(
        _expert_kernel,
        grid_spec=grid_spec,
        out_shape=jax.ShapeDtypeStruct((n_rows,) + row_tile, F32),
        compiler_params=_cparams("arbitrary"),
        name="moe_experts",
    )(blk_e, n_used, xs, w_gate, b_gate.reshape(e, 1, f), w_lin, b_lin.reshape(e, 1, f),
      w_down, b_down.reshape(e, 1, d))


def _combine_kernel(dest_ref, dest_next_ref, x1_ref, w_ref, gtf_ref, gpost_ref, y_ref, o_ref,
                    buf_a, buf_b, sem_a, sem_b):
    step = pl.program_id(0)
    half = x1_ref.shape[0] // 2
    tok_per_row = LANES // TOP_K
    rows_per_half = half // tok_per_row

    def row_copy(idx_ref, first_row, buf, sem, r, c):
        return pltpu.make_async_copy(y_ref.at[pl.ds(idx_ref[first_row + r, c], 1)],
                                     buf.at[c % TOP_K, pl.ds(r * tok_per_row + c // TOP_K, 1)], sem)

    def each_copy(fn):
        for r in range(rows_per_half):
            for c in range(LANES):
                fn(r, c)

    def start_all(idx_ref, first_row, buf, sem):
        each_copy(lambda r, c: row_copy(idx_ref, first_row, buf, sem, r, c).start(priority=c % 2))

    def wait_all(buf, sem):
        each_copy(lambda r, c: row_copy(dest_ref, 0, buf, sem, r, c).wait())

    def finish(buf, rows):
        w = w_ref[rows, :]
        ffn = (buf[0] * w[:, 0:1] + buf[1] * w[:, 1:2]) + (buf[2] * w[:, 2:3] + buf[3] * w[:, 3:4])
        o_ref[rows, :] = x1_ref[rows, :] + gtf_ref[...] * (_rms(ffn) * gpost_ref[...])

    @pl.when(step == 0)
    def _():
        start_all(dest_ref, 0, buf_a, sem_a)

    wait_all(buf_a, sem_a)
    start_all(dest_ref, rows_per_half, buf_b, sem_b)
    finish(buf_a, slice(0, half))
    wait_all(buf_b, sem_b)
    start_all(dest_next_ref, 0, buf_a, sem_a)
    finish(buf_b, slice(half, 2 * half))

    @pl.when(step == pl.num_programs(0) - 1)
    def _():
        wait_all(buf_a, sem_a)


def _combine(dest2, x1, top_w, mod4, seq, g_post, y):
    n, d = x1.shape
    tm = MOVE_ROWS
    steps = n // tm
    per_seq = seq // tm
    idx_rows = tm * TOP_K // LANES
    half_buf = pltpu.VMEM((TOP_K, tm // 2, d), F32)
    return pl.pallas_call(
        _combine_kernel,
        grid=(steps,),
        in_specs=[pl.BlockSpec((idx_rows, LANES), lambda i: (i, 0), memory_space=pltpu.SMEM),
                  pl.BlockSpec((idx_rows, LANES), lambda i: (jnp.minimum(i + 1, steps - 1), 0),
                               memory_space=pltpu.SMEM),
                  pl.BlockSpec((tm, d), lambda i: (i, 0)),
                  pl.BlockSpec((tm, LANES), lambda i: (i, 0)),
                  pl.BlockSpec((None, None, 1, d), lambda i: (i // per_seq, 5, 0, 0)),
                  _const_spec((1, d)),
                  pl.BlockSpec(memory_space=pl.ANY)],
        out_specs=pl.BlockSpec((tm, d), lambda i: (i, 0)),
        out_shape=jax.ShapeDtypeStruct((n, d), F32),
        scratch_shapes=[half_buf, half_buf, pltpu.SemaphoreType.DMA(()), pltpu.SemaphoreType.DMA(())],
        compiler_params=_cparams("arbitrary"),
        name="moe_combine",
    )(dest2, dest2, x1, top_w, mod4, g_post.reshape(1, d), y)


def _rope_partner(w):
    n_freq = ML_QK_DIM // 4
    d = w.shape[0]
    w4 = w.reshape(d, -1, 2, n_freq)
    return jnp.stack([-w4[:, :, 1], w4[:, :, 0]], axis=2).reshape(w.shape)


def _rope_tables(seq):
    n_freq = ML_QK_DIM // 4
    t = jnp.arange(seq)
    row = (t // GRID_W).astype(F32)
    col = (t % GRID_W).astype(F32)
    inv_freq = ROPE_BASE ** (-jnp.arange(n_freq, dtype=F32) / n_freq)
    ang = jnp.concatenate([row[:, None] * inv_freq] * 2 + [col[:, None] * inv_freq] * 2, axis=-1)
    cos = jnp.tile(jnp.cos(ang), (1, ML_HEADS))
    sin = jnp.tile(jnp.sin(ang), (1, ML_HEADS))
    return cos, sin


def _layer(x, ctx, mod4, g_mix_pre, g_mix_post, g_ffn_pre, g_ffn_post, w_in, b_gates, rpb, g_head,
           w_branch_na, w_branch_ml, w_out, w_router, b_router, w_gate, b_gate, w_lin, b_lin, w_down, b_down):
    b, s, d = x.shape
    n = b * s
    n_ctx = ctx.shape[1]
    x2 = x.reshape(n, d)

    ctx_cols = (NA_WIDTH, NA_WIDTH, ML_QK_WIDTH, ML_WIDTH, N_GATES)
    lat_cols = (NA_WIDTH, ML_QK_WIDTH, ML_WIDTH, d, d)
    bounds = np.cumsum(ctx_cols + lat_cols)[:-1].tolist()
    (w_nak, w_nav, w_mk, w_mv, w_g, w_naq, w_mq, w_mo, w_gna, w_gml) = jnp.split(w_in, bounds, axis=-1)
    w_naq = w_naq * (NA_HEAD_DIM ** -0.5 * LOG2_E)
    w_mk = w_mk * ML_QK_DIM ** -0.5
    bf = lambda a: a.astype(BF16)
    lat_w = [bf(w_naq), bf(w_nak), bf(w_nav), bf(w_mq), bf(_rope_partner(w_mq)), bf(w_mk.T),
             bf(_rope_partner(w_mk).T), bf(w_mv), bf(w_mo), bf(w_gna), bf(w_gml), bf(w_g), bf(w_g.T)]
    ctx_w = [bf(w_nak), bf(w_nav), bf(w_mk.T), bf(w_mv), bf(w_g), bf(w_g.T)]
    bg_col = b_gates.reshape(1, N_GATES).astype(F32)
    bg_row = b_gates.reshape(N_GATES, 1).astype(F32)
    cos, sin = _rope_tables(s)
    per_seq = s // PROJ_ROWS

    (na_q, na_k, na_v, ml_q, ml_k, ml_v, ml_o, gate_na, gate_ml, g_col, g_row) = _project(
        x2, mod4, lambda i: i // per_seq, g_mix_pre, bg_col, bg_row, lat_w, (cos, sin, per_seq), True)
    (na_kc, na_vc, ml_kc, ml_vc, gc_col, gc_row) = _project(
        ctx.reshape(b * n_ctx, d), mod4, lambda i: b, g_mix_pre, bg_col, bg_row, ctx_w, None, False)

    def seq3(a, length):
        return a.reshape(b, length, a.shape[-1])

    o_na = _neighbourhood_attention(seq3(na_q, s), seq3(na_k, s), seq3(na_v, s),
                                    seq3(na_kc, n_ctx), seq3(na_vc, n_ctx), rpb)
    ml_args = (seq3(ml_q, s), ml_k, seq3(ml_v, s), seq3(g_col, s), g_row,
               ml_kc, seq3(ml_vc, n_ctx), seq3(gc_col, n_ctx), gc_row)
    h_f, h_b = _mlstm(*ml_args)

    x1, h2, top_e, top_w, rank, counts = _merge_and_route(
        x2, o_na.reshape(n, NA_WIDTH), h_f.reshape(n, ML_WIDTH), h_b.reshape(n, ML_WIDTH), ml_o,
        gate_na, gate_ml, mod4, s, g_head, bf(w_branch_na), bf(w_branch_ml), bf(w_out),
        g_mix_post, g_ffn_pre, w_router, b_router)

    tm = EXPERT_ROWS
    counts = counts.reshape(N_EXPERTS).astype(jnp.int32)
    padded = (counts + tm - 1) // tm * tm
    pad_end = jnp.cumsum(padded)
    pad_start = pad_end - padded
    n_rows = n * TOP_K + N_EXPERTS * tm
    n_blocks = n_rows // tm
    e_sel = top_e[:, :TOP_K, None] == jnp.arange(N_EXPERTS, dtype=jnp.int32)
    dest = jnp.sum(jnp.where(e_sel, pad_start, 0), axis=-1) + rank[:, :TOP_K]
    dest2 = dest.reshape(n * TOP_K // LANES, LANES).astype(jnp.int32)
    blk_start = jnp.arange(n_blocks, dtype=jnp.int32) * tm
    blk_e = jnp.minimum(jnp.sum(blk_start[:, None] >= pad_end[None, :], axis=1), N_EXPERTS - 1).astype(jnp.int32)
    n_used = (pad_end[-1:] // tm).astype(jnp.int32)

    xs = _dispatch(pad_end.astype(jnp.int32), padded.astype(jnp.int32), dest2, h2, n_rows)
    y = _experts(blk_e, n_used, xs, w_gate, b_gate, w_lin, b_lin, w_down, b_down)
    out = _combine(dest2, x1, top_w, mod4, s, g_ffn_post, y)
    return out.reshape(b, s, d)


def kernel(x, c, ctx, c_ctx, w_ada, b_ada, g_mix_pre, g_mix_post, g_ffn_pre, g_ffn_post, w_in, b_mlstm_gates,
           rpb, g_mlstm_head, w_branch_na, w_branch_ml, w_out, w_router, b_router, w_gate, b_gate, w_lin,
           b_lin, w_down, b_down):
    b, s, d = x.shape
    depth = w_ada.shape[0]
    pad = (-(b + 1)) % 8
    c_all = jnp.concatenate([c, c_ctx[None, :], jnp.zeros((pad, d), c.dtype)], axis=0)
    for layer in range(depth):
        mod = _ada(c_all, w_ada[layer], b_ada[layer])
        mod4 = mod.reshape(mod.shape[0], 6, 1, d)
        x = _layer(x, ctx, mod4, g_mix_pre[layer], g_mix_post[layer], g_ffn_pre[layer], g_ffn_post[layer],
                   w_in[layer], b_mlstm_gates[layer], rpb[layer], g_mlstm_head[layer], w_branch_na[layer],
                   w_branch_ml[layer], w_out[layer], w_router[layer], b_router[layer], w_gate[layer],
                   b_gate[layer], w_lin[layer], b_lin[layer], w_down[layer], b_down[layer])
    return x
```

```python
import functools

import numpy as np
import jax
import jax.numpy as jnp
from jax import lax
from jax.experimental import pallas as pl
from jax.experimental.pallas import tpu as pltpu

F32 = jnp.float32
BF16 = jnp.bfloat16
HIGHEST = lax.Precision.HIGHEST

GRID_W = 64
NA_HEADS = 8
NA_HEAD_DIM = 64
NA_WIDTH = NA_HEADS * NA_HEAD_DIM
WIN_H = 8
WIN_W = 16
ML_HEADS = 4
ML_QK_DIM = 64
ML_V_DIM = 128
ML_QK_WIDTH = ML_HEADS * ML_QK_DIM
ML_WIDTH = ML_HEADS * ML_V_DIM
N_GATES = 4 * ML_HEADS
GATE_SOFTCAP = 15.0
ROPE_BASE = 10000.0
N_EXPERTS = 32
TOP_K = 4
SWIGLU_ALPHA = 1.702
SWIGLU_LIMIT = 7.0
NORM_EPS = 1e-6
NEG_INF = -1e30
LOG2_E = 1.4426950408889634

LANES = 128
NA_QROWS = 4
NA_WROWS = 12
ML_CHUNK = 256
PROJ_ROWS = 512
MERGE_ROWS = 512
EXPERT_ROWS = 512
MOVE_ROWS = 512
VMEM_LIMIT = 56 * 1024 * 1024

NT_DIMS = (((1,), (1,)), ((), ()))


def _cparams(*sem):
    return pltpu.CompilerParams(dimension_semantics=sem, vmem_limit_bytes=VMEM_LIMIT)


def _rms(x):
    return x * lax.rsqrt(jnp.mean(x * x, axis=-1, keepdims=True) + NORM_EPS)


def _ada_kernel(c_ref, w_ref, b_ref, o_ref):
    c = c_ref[...]
    s = c * jax.nn.sigmoid(c)
    o_ref[...] = jnp.dot(s, w_ref[...], preferred_element_type=F32, precision=HIGHEST) + b_ref[...]


def _ada(c_all, w_ada, b_ada):
    rows, d = c_all.shape
    n_out = w_ada.shape[1]
    tn = 1536
    return pl.pallas_call(
        _ada_kernel,
        grid=(n_out // tn,),
        in_specs=[pl.BlockSpec((rows, d), lambda j: (0, 0)),
                  pl.BlockSpec((d, tn), lambda j: (0, j)),
                  pl.BlockSpec((1, tn), lambda j: (0, j))],
        out_specs=pl.BlockSpec((rows, tn), lambda j: (0, j)),
        out_shape=jax.ShapeDtypeStruct((rows, n_out), F32),
        compiler_params=_cparams("arbitrary"),
        name="ada_mod",
    )(c_all, w_ada, b_ada.reshape(1, n_out))


def _gate_logs(g, is_forget):
    g = GATE_SOFTCAP * jnp.tanh(g / GATE_SOFTCAP)
    log_sig = jnp.minimum(g, 0.0) - jnp.log(1.0 + jnp.exp(-jnp.abs(g)))
    return jnp.where(is_forget, log_sig, g)


def _proj_kernel(*refs, latent):
    if latent:
        (x_ref, g_ref, sc_ref, sh_ref, cos_ref, sin_ref, cos_t_ref, sin_t_ref, bgc_ref, bgr_ref,
         w_naq, w_nak, w_nav, w_mq, w_mqp, w_mk, w_mkp, w_mv, w_mo, w_gna, w_gml, w_gc, w_gr,
         o_naq, o_nak, o_nav, o_mq, o_mk, o_mv, o_mo, o_gna, o_gml, o_gc, o_gr) = refs
    else:
        (x_ref, g_ref, sc_ref, sh_ref, bgc_ref, bgr_ref,
         w_nak, w_nav, w_mk, w_mv, w_gc, w_gr,
         o_nak, o_nav, o_mk, o_mv, o_gc, o_gr) = refs
    x = x_ref[...]
    h = _rms(x) * g_ref[...]
    h = h * (1.0 + sc_ref[...]) + sh_ref[...]
    hb = h.astype(BF16)

    def mm(w_ref):
        return jnp.dot(hb, w_ref[...], preferred_element_type=F32)

    def mm_t(w_ref):
        return lax.dot_general(w_ref[...], hb, NT_DIMS, preferred_element_type=F32)

    o_nak[...] = mm(w_nak).astype(BF16)
    o_nav[...] = mm(w_nav).astype(BF16)
    o_mv[...] = mm(w_mv).astype(BF16)
    if latent:
        cos = cos_ref[...]
        sin = sin_ref[...]
        o_naq[...] = mm(w_naq).astype(BF16)
        o_mq[...] = (mm(w_mq) * cos + mm(w_mqp) * sin).astype(BF16)
        o_mk[...] = (mm_t(w_mk) * cos_t_ref[...] + mm_t(w_mkp) * sin_t_ref[...]).astype(BF16)
        o_mo[...] = mm(w_mo).astype(BF16)
        o_gna[...] = mm(w_gna).astype(BF16)
        o_gml[...] = mm(w_gml).astype(BF16)
    else:
        o_mk[...] = mm_t(w_mk).astype(BF16)
    gc = mm(w_gc) + bgc_ref[...]
    col_id = lax.broadcasted_iota(jnp.int32, gc.shape, 1)
    o_gc[...] = _gate_logs(gc, (col_id // ML_HEADS) % 2 == 1)
    gr = mm_t(w_gr) + bgr_ref[...]
    row_id = lax.broadcasted_iota(jnp.int32, gr.shape, 0)
    o_gr[...] = _gate_logs(gr, (row_id // ML_HEADS) % 2 == 1)


def _const_spec(shape):
    nd = len(shape)
    return pl.BlockSpec(shape, lambda i, _nd=nd: (0,) * _nd)


def _project(x2, mod4, mod_row_fn, g_pre, bg_col, bg_row, weights, tables, latent):
    n, d = x2.shape
    tm = PROJ_ROWS
    grid = (n // tm,)

    def mod_spec(j):
        return pl.BlockSpec((None, None, 1, d), lambda i, _j=j: (mod_row_fn(i), _j, 0, 0))

    in_specs = [pl.BlockSpec((tm, d), lambda i: (i, 0)), _const_spec((1, d)), mod_spec(1), mod_spec(0)]
    args = [x2, g_pre.reshape(1, d), mod4, mod4]
    if latent:
        cos, sin, tiles_per_seq = tables
        in_specs += [pl.BlockSpec((tm, ML_QK_WIDTH), lambda i: (i % tiles_per_seq, 0))] * 2
        in_specs += [pl.BlockSpec((ML_QK_WIDTH, tm), lambda i: (0, i % tiles_per_seq))] * 2
        args += [cos, sin, cos.T, sin.T]
    in_specs += [_const_spec(bg_col.shape), _const_spec(bg_row.shape)]
    args += [bg_col, bg_row]
    for w in weights:
        in_specs.append(_const_spec(w.shape))
        args.append(w)

    def out(width, dtype=BF16):
        return (jax.ShapeDtypeStruct((n, width), dtype), pl.BlockSpec((tm, width), lambda i: (i, 0)))

    def out_t(width, dtype=BF16):
        return (jax.ShapeDtypeStruct((width, n), dtype), pl.BlockSpec((width, tm), lambda i: (0, i)))

    if latent:
        outs = [out(NA_WIDTH), out(NA_WIDTH), out(NA_WIDTH), out(ML_QK_WIDTH), out_t(ML_QK_WIDTH),
                out(ML_WIDTH), out(ML_WIDTH), out(d), out(d), out(N_GATES, F32)]
    else:
        outs = [out(NA_WIDTH), out(NA_WIDTH), out_t(ML_QK_WIDTH), out(ML_WIDTH), out(N_GATES, F32)]
    outs.append(out_t(N_GATES, F32))
    return pl.pallas_call(
        functools.partial(_proj_kernel, latent=latent),
        grid=grid,
        in_specs=in_specs,
        out_specs=[o[1] for o in outs],
        out_shape=[o[0] for o in outs],
        compiler_params=_cparams("arbitrary"),
        name="in_proj_latent" if latent else "in_proj_ctx",
    )(*args)


def _na_window_start(r0, rows):
    return jnp.clip(r0 - WIN_H // 2, 0, rows - NA_WROWS)


def _na_classes(rows):
    keys, group_class = [], []
    for r0 in range(0, rows, NA_QROWS):
        start = min(max(r0 - WIN_H // 2, 0), rows - NA_WROWS)
        first = tuple(min(max(r0 + i - WIN_H // 2, 0), rows - WIN_H) - start for i in range(NA_QROWS))
        assert all(0 <= f and f + WIN_H <= NA_WROWS for f in first)
        key = (r0 - start, first)
        if key not in keys:
            keys.append(key)
        group_class.append(keys.index(key))
    return keys, np.asarray(group_class, np.int32)


def _na_bias_table(rpb, rows):
    keys, group_class = _na_classes(rows)
    qc = np.arange(GRID_W)[:, None]
    kc = np.arange(GRID_W)[None, :]
    cs = np.clip(qc - WIN_W // 2, 0, GRID_W - WIN_W)
    col_ok = (kc >= cs) & (kc < cs + WIN_W)
    dc = np.clip(kc - qc, -(WIN_W - 1), WIN_W - 1) + (WIN_W - 1)
    i = np.arange(NA_QROWS)[:, None]
    j = np.arange(NA_WROWS)[None, :]
    sel_r, row_ok = [], []
    for off, first in keys:
        f = np.asarray(first)[:, None]
        ok = (j >= f) & (j < f + WIN_H)
        dr = j - off - i + (WIN_H - 1)
        sel_r.append(((dr[:, :, None] == np.arange(2 * WIN_H - 1)) & ok[:, :, None]).astype(np.float32))
        row_ok.append(ok)
    sel_r = np.stack(sel_r)
    valid = np.stack(row_ok)[:, None, :, None, :, None] & col_ok[None, None, None, :, None, :]
    sel_c = (dc[:, :, None] == np.arange(2 * WIN_W - 1)).astype(np.float32)
    t = jnp.einsum('hrc,qkc->hrqk', rpb.astype(F32), sel_c, precision=HIGHEST)
    bias = jnp.einsum('hrqk,xijr->xhiqjk', t, sel_r, precision=HIGHEST)
    bias = jnp.where(valid, bias * LOG2_E, NEG_INF)
    bias = bias.reshape(len(keys), NA_HEADS, NA_QROWS * GRID_W, NA_WROWS * GRID_W)
    return bias.astype(BF16), jnp.asarray(group_class)


def _na_kernel(cls_ref, q_ref, k_ref, v_ref, kc_ref, vc_ref, bias_ref, o_ref, *, rows):
    del cls_ref
    r0 = pl.program_id(1) * NA_QROWS
    start = pl.multiple_of(_na_window_start(r0, rows) * GRID_W, GRID_W)
    n_win = NA_WROWS * GRID_W
    low = lax.broadcasted_iota(jnp.int32, (1, LANES), 1) < NA_HEAD_DIM
    for pair in range(NA_HEADS // 2):
        sl = slice(pair * LANES, (pair + 1) * LANES)
        q2 = q_ref[:, sl]
        k2 = k_ref[pl.ds(start, n_win), sl]
        v2 = v_ref[pl.ds(start, n_win), sl]
        kc2 = kc_ref[:, sl]
        vc2 = vc_ref[:, sl]
        scores = []
        for half in range(2):
            keep = low if half == 0 else jnp.logical_not(low)
            qh = jnp.where(keep, q2, jnp.zeros_like(q2))
            s_loc = (lax.dot_general(qh, k2, NT_DIMS, preferred_element_type=F32)
                     + bias_ref[2 * pair + half].astype(F32))
            s_ctx = lax.dot_general(qh, kc2, NT_DIMS, preferred_element_type=F32)
            scores.append((s_loc, s_ctx))
        probs = []
        for s_loc, s_ctx in scores:
            m = jnp.maximum(jnp.max(s_loc, axis=-1, keepdims=True), jnp.max(s_ctx, axis=-1, keepdims=True))
            p_loc = jnp.exp2(s_loc - m)
            p_ctx = jnp.exp2(s_ctx - m)
            denom = jnp.sum(p_loc, axis=-1, keepdims=True) + jnp.sum(p_ctx, axis=-1, keepdims=True)
            probs.append((p_loc.astype(BF16), p_ctx.astype(BF16), denom))
        halves = []
        for p_loc, p_ctx, denom in probs:
            o = (jnp.dot(p_loc, v2, preferred_element_type=F32)
                 + jnp.dot(p_ctx, vc2, preferred_element_type=F32))
            halves.append(o / denom)
        o_ref[:, sl] = jnp.where(low, halves[0], halves[1]).astype(BF16)


def _neighbourhood_attention(q, k, v, kc, vc, rpb):
    b, s, w = q.shape
    rows = s // GRID_W
    n_ctx = kc.shape[1]
    bias, group_class = _na_bias_table(rpb, rows)
    nq = NA_QROWS * GRID_W
    grid_spec = pltpu.PrefetchScalarGridSpec(
        num_scalar_prefetch=1,
        grid=(b, rows // NA_QROWS),
        in_specs=[pl.BlockSpec((None, nq, w), lambda bi, g, cls: (bi, g, 0)),
                  pl.BlockSpec((None, s, w), lambda bi, g, cls: (bi, 0, 0)),
                  pl.BlockSpec((None, s, w), lambda bi, g, cls: (bi, 0, 0)),
                  pl.BlockSpec((None, n_ctx, w), lambda bi, g, cls: (bi, 0, 0)),
                  pl.BlockSpec((None, n_ctx, w), lambda bi, g, cls: (bi, 0, 0)),
                  pl.BlockSpec((None,) + bias.shape[1:], lambda bi, g, cls: (cls[g], 0, 0, 0))],
        out_specs=pl.BlockSpec((None, nq, w), lambda bi, g, cls: (bi, g, 0)),
    )
    return pl.pallas_call(
        functools.partial(_na_kernel, rows=rows),
        grid_spec=grid_spec,
        out_shape=jax.ShapeDtypeStruct((b, s, w), BF16),
        compiler_params=_cparams("arbitrary", "arbitrary"),
        name="na_attention",
    )(group_class, q, k, v, kc, vc, bias)


def _split3(x):
    hi = x.astype(BF16)
    r1 = x - hi.astype(F32)
    mid = r1.astype(BF16)
    lo = (r1 - mid.astype(F32)).astype(BF16)
    return hi, mid, lo


def _mlstm_direction_step(reverse, is_ctx, q, kt, v, gc, gr, o_ref, c_scr, m_scr):
    li_base = 2 * ML_HEADS if reverse else 0
    lf_base = li_base + ML_HEADS
    length = kt.shape[1]
    t = lax.broadcasted_iota(jnp.int32, (length, length), 0)
    s = lax.broadcasted_iota(jnp.int32, (length, length), 1)
    valid = (s >= t) if reverse else (s <= t)
    valid_t = (t >= s) if reverse else (t <= s)
    b_col = sum(jnp.dot(valid.astype(BF16), p, preferred_element_type=F32) for p in _split3(gc))
    b_row = sum(jnp.dot(p, valid_t.astype(BF16), preferred_element_type=F32) for p in _split3(gr))
    low = lax.broadcasted_iota(jnp.int32, (1, LANES), 1) < ML_QK_DIM
    ones = jnp.ones((length, ML_V_DIM), BF16)
    heads = range(ML_HEADS)
    li_r = [gr[li_base + h:li_base + h + 1, :] for h in heads]
    lf_r = [gr[lf_base + h:lf_base + h + 1, :] for h in heads]
    br = [b_row[lf_base + h:lf_base + h + 1, :] for h in heads]
    m_prev = [m_scr[h] for h in heads]
    v_ext = [jnp.concatenate([v[:, h * ML_V_DIM:(h + 1) * ML_V_DIM], ones], axis=-1) for h in heads]
    kt_pair = [kt[(h // 2) * LANES:(h // 2 + 1) * LANES, :] for h in heads]

    yield
    if not is_ctx:
        weights = []
        for h in heads:
            bc = b_col[:, lf_base + h:lf_base + h + 1]
            d_mat = jnp.where(valid, bc - br[h] + li_r[h], NEG_INF)
            m_inter = bc + m_prev[h]
            m_t = jnp.maximum(m_inter, jnp.max(d_mat, axis=-1, keepdims=True))
            weights.append((jnp.exp(d_mat - m_t), jnp.exp(m_inter - m_t), jnp.exp(-m_t)))
        yield
        for h in heads:
            w_intra, w_inter, floor = weights[h]
            q2 = q[:, (h // 2) * LANES:(h // 2 + 1) * LANES]
            qh = jnp.where(low if h % 2 == 0 else jnp.logical_not(low), q2, jnp.zeros_like(q2))
            sc = jnp.dot(qh, kt_pair[h], preferred_element_type=F32) * w_intra
            both = (w_inter * jnp.dot(qh, c_scr[h // 2].astype(BF16), preferred_element_type=F32)
                    + jnp.dot(sc.astype(BF16), v_ext[h], preferred_element_type=F32))
            num = both[:, :ML_V_DIM]
            den = both[:, ML_V_DIM:]
            o_ref[:, h * ML_V_DIM:(h + 1) * ML_V_DIM] = (num / jnp.maximum(jnp.abs(den), floor)).astype(BF16)

    yield
    for h in heads:
        rows = slice((h % 2) * ML_QK_DIM, (h % 2 + 1) * ML_QK_DIM)
        total = jnp.sum(lf_r[h], axis=-1, keepdims=True)
        m_new = jnp.maximum(total + m_prev[h], jnp.max(total - br[h] + li_r[h], axis=-1, keepdims=True))
        w_src = jnp.exp(total - br[h] + li_r[h] - m_new)
        w_carry = jnp.exp(total + m_prev[h] - m_new)
        kw = (kt_pair[h][rows, :].astype(F32) * w_src).astype(BF16)
        c_scr[h // 2, rows, :] = (w_carry * c_scr[h // 2, rows, :]
                                  + jnp.dot(kw, v_ext[h], preferred_element_type=F32))
        m_scr[h] = m_new


def _alternate(*phased):
    live = list(phased)
    while live:
        live = [g for g in live if next(g, StopIteration) is not StopIteration]


def _mlstm_kernel(qf_ref, kf_ref, vf_ref, gcf_ref, grf_ref, qb_ref, kb_ref, vb_ref, gcb_ref, grb_ref,
                  kc_ref, vc_ref, gcc_ref, gcr_ref, of_ref, ob_ref, cf_scr, mf_scr, cb_scr, mb_scr):
    step = pl.program_id(1)

    @pl.when(step == 0)
    def _():
        for ref in (cf_scr, mf_scr, cb_scr, mb_scr):
            ref[...] = jnp.zeros_like(ref)
        kt = kc_ref[...]
        v = vc_ref[...]
        gc = gcc_ref[...]
        gr = gcr_ref[...]
        _alternate(_mlstm_direction_step(False, True, None, kt, v, gc, gr, None, cf_scr, mf_scr),
                   _mlstm_direction_step(True, True, None, kt, v, gc, gr, None, cb_scr, mb_scr))

    @pl.when(step > 0)
    def _():
        _alternate(_mlstm_direction_step(False, False, qf_ref[...], kf_ref[...], vf_ref[...], gcf_ref[...],
                                         grf_ref[...], of_ref, cf_scr, mf_scr),
                   _mlstm_direction_step(True, False, qb_ref[...], kb_ref[...], vb_ref[...], gcb_ref[...],
                                         grb_ref[...], ob_ref, cb_scr, mb_scr))


def _mlstm(q, k, v, gc, gr, kc, vc, gcc, gcr):
    b, s, _ = q.shape
    n_ctx = vc.shape[1]
    length = ML_CHUNK
    n_chunks = s // length

    def chunk(step, reverse):
        c = jnp.maximum(step - 1, 0)
        return (n_chunks - 1 - c) if reverse else c

    def stream_specs(reverse):
        def seq(width):
            return pl.BlockSpec((None, length, width), lambda bi, st: (bi, chunk(st, reverse), 0))

        def seq_t(width):
            return pl.BlockSpec((width, length), lambda bi, st: (0, bi * n_chunks + chunk(st, reverse)))
        return [seq(ML_QK_WIDTH), seq_t(ML_QK_WIDTH), seq(ML_WIDTH), seq(N_GATES), seq_t(N_GATES)]

    def out_spec(reverse):
        return pl.BlockSpec((None, length, ML_WIDTH), lambda bi, st: (bi, chunk(st, reverse), 0))

    state = [pltpu.VMEM((ML_HEADS // 2, 2 * ML_QK_DIM, 2 * ML_V_DIM), F32),
             pltpu.VMEM((ML_HEADS, 1, 1), F32)]
    return pl.pallas_call(
        _mlstm_kernel,
        grid=(b, n_chunks + 1),
        in_specs=stream_specs(False) + stream_specs(True) + [
            pl.BlockSpec((ML_QK_WIDTH, n_ctx), lambda bi, st: (0, bi)),
            pl.BlockSpec((None, n_ctx, ML_WIDTH), lambda bi, st: (bi, 0, 0)),
            pl.BlockSpec((None, n_ctx, N_GATES), lambda bi, st: (bi, 0, 0)),
            pl.BlockSpec((N_GATES, n_ctx), lambda bi, st: (0, bi))],
        out_specs=[out_spec(False), out_spec(True)],
        out_shape=[jax.ShapeDtypeStruct((b, s, ML_WIDTH), BF16)] * 2,
        scratch_shapes=state + state,
        compiler_params=_cparams("arbitrary", "arbitrary"),
        name="mlstm",
    )(q, k, v, gc, gr, q, k, v, gc, gr, kc, vc, gcc, gcr)


def _merge_kernel(x_ref, ona_ref, hf_ref, hb_ref, opre_ref, gna_ref, gml_ref, gtm_ref, scf_ref, shf_ref,
                  ghead_ref, wbna_ref, wbml_ref, wout_ref, gpost_ref, gpre_ref, wrh_ref, wrl_ref, br_ref,
                  x1_ref, h2_ref, tope_ref, topw_ref, rank_ref, cnt_ref):
    step = pl.program_id(0)
    tm = x_ref.shape[0]

    @pl.when(step == 0)
    def _():
        cnt_ref[...] = jnp.zeros_like(cnt_ref)

    hsum = hf_ref[...].astype(F32) + hb_ref[...].astype(F32)
    heads = [_rms(hsum[:, h * ML_V_DIM:(h + 1) * ML_V_DIM]) for h in range(ML_HEADS)]
    hn = jnp.concatenate(heads, axis=-1) * ghead_ref[...]
    o_ml = jax.nn.sigmoid(opre_ref[...].astype(F32)) * hn
    merged = (jax.nn.sigmoid(gna_ref[...].astype(F32))
              * jnp.dot(ona_ref[...], wbna_ref[...], preferred_element_type=F32)
              + jax.nn.sigmoid(gml_ref[...].astype(F32))
              * jnp.dot(o_ml.astype(BF16), wbml_ref[...], preferred_element_type=F32))
    mixed = jnp.dot(merged.astype(BF16), wout_ref[...], preferred_element_type=F32)
    x1 = x_ref[...] + gtm_ref[...] * (_rms(mixed) * gpost_ref[...])
    x1_ref[...] = x1
    h2 = _rms(x1) * gpre_ref[...] * (1.0 + scf_ref[...]) + shf_ref[...]
    h2_ref[...] = h2
    h2_hi = h2.astype(BF16)
    h2_lo = (h2 - h2_hi.astype(F32)).astype(BF16)
    logits = (jnp.dot(h2_hi, wrh_ref[...], preferred_element_type=F32)
              + (jnp.dot(h2_hi, wrl_ref[...], preferred_element_type=F32)
                 + jnp.dot(h2_lo, wrh_ref[...], preferred_element_type=F32))) + br_ref[...]

    lane = lax.broadcasted_iota(jnp.int32, logits.shape, 1)
    onehots, top_e, top_l = [], [], []
    for _ in range(TOP_K):
        best = jnp.max(logits, axis=-1, keepdims=True)
        e = jnp.min(jnp.where(logits == best, lane, N_EXPERTS), axis=-1, keepdims=True)
        hit = lane == e
        onehots.append(hit)
        top_e.append(e)
        top_l.append(best)
        logits = jnp.where(hit, -jnp.inf, logits)
    exps = [jnp.exp(l - top_l[0]) for l in top_l]
    total = exps[0] + exps[1] + exps[2] + exps[3]

    counts = (onehots[0].astype(F32) + onehots[1].astype(F32)
              + onehots[2].astype(F32) + onehots[3].astype(F32))
    t = lax.broadcasted_iota(jnp.int32, (tm, tm), 0)
    s = lax.broadcasted_iota(jnp.int32, (tm, tm), 1)
    before = jnp.dot((s < t).astype(BF16), counts.astype(BF16), preferred_element_type=F32) + cnt_ref[...]
    out_lane = lax.broadcasted_iota(jnp.int32, (tm, LANES), 1)
    e_out = jnp.zeros((tm, LANES), jnp.int32)
    w_out = jnp.zeros((tm, LANES), F32)
    r_out = jnp.zeros((tm, LANES), jnp.int32)
    for j in range(TOP_K):
        rank = jnp.sum(jnp.where(onehots[j], before, 0.0), axis=-1, keepdims=True).astype(jnp.int32)
        e_out = jnp.where(out_lane == j, top_e[j], e_out)
        w_out = jnp.where(out_lane == j, exps[j] / total, w_out)
        r_out = jnp.where(out_lane == j, rank, r_out)
    tope_ref[...] = e_out
    topw_ref[...] = w_out
    rank_ref[...] = r_out
    cnt_ref[...] += jnp.sum(counts, axis=0, keepdims=True)


def _merge_and_route(x2, o_na, h_f, h_b, o_pre, g_na, g_ml, mod4, seq, g_head, wbna, wbml, wout,
                     g_post, g_pre, w_router, b_router):
    n, d = x2.shape
    tm = MERGE_ROWS
    per_seq = seq // tm

    def rows(width):
        return pl.BlockSpec((tm, width), lambda i: (i, 0))

    def mod_spec(j):
        return pl.BlockSpec((None, None, 1, d), lambda i, _j=j: (i // per_seq, _j, 0, 0))

    w_router_hi = w_router.astype(BF16)
    return pl.pallas_call(
        _merge_kernel,
        grid=(n // tm,),
        in_specs=[rows(d), rows(NA_WIDTH), rows(ML_WIDTH), rows(ML_WIDTH), rows(ML_WIDTH), rows(d), rows(d),
                  mod_spec(2), mod_spec(4), mod_spec(3),
                  _const_spec((1, ML_WIDTH)), _const_spec(wbna.shape), _const_spec(wbml.shape),
                  _const_spec(wout.shape), _const_spec((1, d)), _const_spec((1, d)),
                  _const_spec(w_router.shape), _const_spec(w_router.shape), _const_spec((1, N_EXPERTS))],
        out_specs=[rows(d), rows(d), rows(LANES), rows(LANES), rows(LANES),
                   pl.BlockSpec((1, N_EXPERTS), lambda i: (0, 0))],
        out_shape=[jax.ShapeDtypeStruct((n, d), F32), jax.ShapeDtypeStruct((n, d), F32),
                   jax.ShapeDtypeStruct((n, LANES), jnp.int32), jax.ShapeDtypeStruct((n, LANES), F32),
                   jax.ShapeDtypeStruct((n, LANES), jnp.int32),
                   jax.ShapeDtypeStruct((1, N_EXPERTS), F32)],
        compiler_params=_cparams("arbitrary"),
        name="merge_route",
    )(x2, o_na, h_f, h_b, o_pre, g_na, g_ml, mod4, mod4, mod4,
      g_head.reshape(1, ML_WIDTH), wbna, wbml, wout, g_post.reshape(1, d), g_pre.reshape(1, d),
      w_router_hi, (w_router - w_router_hi.astype(F32)).astype(BF16), b_router.reshape(1, N_EXPERTS))


def _dispatch_kernel(pad_end_ref, padded_ref, dest_ref, h_ref, xs_ref, zero_scr, sem):
    tm = h_ref.shape[0]
    blk = zero_scr.shape[0]

    @pl.when(pl.program_id(0) == 0)
    def _():
        zero_scr[...] = jnp.zeros_like(zero_scr)

        def zero_copy(e):
            first = pl.multiple_of(pad_end_ref[e] - blk, blk)
            return pltpu.make_async_copy(zero_scr, xs_ref.at[pl.ds(first, blk)], sem)

        for e in range(N_EXPERTS):
            @pl.when(padded_ref[e] > 0)
            def _():
                zero_copy(e).start()
        for e in range(N_EXPERTS):
            @pl.when(padded_ref[e] > 0)
            def _():
                zero_copy(e).wait()

        def tail_copy(b):
            return pltpu.make_async_copy(zero_scr, xs_ref.at[pl.ds(pl.multiple_of(b * blk, blk), blk)], sem)

        def tail_start(b, carry):
            tail_copy(b).start()
            return carry

        def tail_wait(b, carry):
            tail_copy(b).wait()
            return carry

        first_unused = pad_end_ref[N_EXPERTS - 1] // blk
        lax.fori_loop(first_unused, xs_ref.shape[0] // blk, tail_start, 0)
        lax.fori_loop(first_unused, xs_ref.shape[0] // blk, tail_wait, 0)

    tok_per_row = LANES // TOP_K

    def row_copy(r, c):
        return pltpu.make_async_copy(h_ref.at[pl.ds(r * tok_per_row + c // TOP_K, 1)],
                                     xs_ref.at[pl.ds(dest_ref[r, c], 1)], sem)

    for r in range(tm // tok_per_row):
        for c in range(LANES):
            row_copy(r, c).start(priority=c % 2)
    for r in range(tm // tok_per_row):
        for c in range(LANES):
            row_copy(r, c).wait()


def _dispatch(pad_end, padded, dest2, h2, n_rows):
    n = h2.shape[0]
    tm = MOVE_ROWS
    idx_rows = tm * TOP_K // LANES
    row_tile = h2.shape[1:]
    grid_spec = pltpu.PrefetchScalarGridSpec(
        num_scalar_prefetch=2,
        grid=(n // tm,),
        in_specs=[pl.BlockSpec((idx_rows, LANES), lambda i, pe, pd: (i, 0), memory_space=pltpu.SMEM),
                  pl.BlockSpec((tm,) + row_tile, lambda i, pe, pd: (i, 0))],
        out_specs=pl.BlockSpec(memory_space=pl.ANY),
        scratch_shapes=[pltpu.VMEM((EXPERT_ROWS,) + row_tile, h2.dtype), pltpu.SemaphoreType.DMA(())],
    )
    return pl.pallas_call(
        _dispatch_kernel,
        grid_spec=grid_spec,
        out_shape=jax.ShapeDtypeStruct((n_rows,) + row_tile, h2.dtype),
        compiler_params=_cparams("arbitrary"),
        name="moe_dispatch",
    )(pad_end, padded, dest2, h2)


def _expert_kernel(blk_e_ref, n_used_ref, x_ref, wg_ref, bg_ref, wl_ref, bl_ref, wd_ref, bd_ref, y_ref,
                   wg_s, wl_s, wd_s):
    i = pl.program_id(0)
    prev = blk_e_ref[jnp.maximum(i - 1, 0)]
    changed = jnp.logical_or(i == 0, blk_e_ref[i] != prev)
    used = i < n_used_ref[0]

    @pl.when(jnp.logical_and(used, changed))
    def _():
        wg_s[...] = wg_ref[...].astype(BF16)
        wl_s[...] = wl_ref[...].astype(BF16)
        wd_s[...] = wd_ref[...].astype(BF16)

    @pl.when(used)
    def _():
        xb = x_ref[...].astype(BF16)
        g = jnp.dot(xb, wg_s[...], preferred_element_type=F32) + bg_ref[...]
        l = jnp.dot(xb, wl_s[...], preferred_element_type=F32) + bl_ref[...]
        g = jnp.minimum(g, SWIGLU_LIMIT)
        l = jnp.clip(l, -SWIGLU_LIMIT, SWIGLU_LIMIT)
        a = g * jax.nn.sigmoid(SWIGLU_ALPHA * g) * (l + 1.0)
        y_ref[...] = jnp.dot(a.astype(BF16), wd_s[...], preferred_element_type=F32) + bd_ref[...]

    @pl.when(jnp.logical_not(used))
    def _():
        y_ref[...] = jnp.zeros_like(y_ref)


def _experts(blk_e, n_used, xs, w_gate, b_gate, w_lin, b_lin, w_down, b_down):
    n_rows = xs.shape[0]
    row_tile = xs.shape[1:]
    e, d, f = w_gate.shape
    tm = EXPERT_ROWS

    def w_spec(shape):
        return pl.BlockSpec((None,) + shape, lambda i, be, nu: (be[i], 0, 0))

    grid_spec = pltpu.PrefetchScalarGridSpec(
        num_scalar_prefetch=2,
        grid=(n_rows // tm,),
        in_specs=[pl.BlockSpec((tm,) + row_tile, lambda i, be, nu: (i, 0)),
                  w_spec((d, f)), w_spec((1, f)), w_spec((d, f)), w_spec((1, f)),
                  w_spec((f, d)), w_spec((1, d))],
        out_specs=pl.BlockSpec((tm,) + row_tile, lambda i, be, nu: (i, 0)),
        scratch_shapes=[pltpu.VMEM((d, f), BF16), pltpu.VMEM((d, f), BF16), pltpu.VMEM((f, d), BF16)],
    )
    return pl.pallas_call(
        _expert_kernel,
        grid_spec=grid_spec,
        out_shape=jax.ShapeDtypeStruct((n_rows,) + row_tile, F32),
        compiler_params=_cparams("arbitrary"),
        name="moe_experts",
    )(blk_e, n_used, xs, w_gate, b_gate.reshape(e, 1, f), w_lin, b_lin.reshape(e, 1, f),
      w_down, b_down.reshape(e, 1, d))


def _combine_kernel(dest_ref, dest_next_ref, x1_ref, w_ref, gtf_ref, gpost_ref, y_ref, o_ref,
                    buf_a, buf_b, sem_a, sem_b):
    step = pl.program_id(0)
    half = x1_ref.shape[0] // 2
    tok_per_row = LANES // TOP_K
    rows_per_half = half // tok_per_row

    def row_copy(idx_ref, first_row, buf, sem, r, c):
        return pltpu.make_async_copy(y_ref.at[pl.ds(idx_ref[first_row + r, c], 1)],
                                     buf.at[c % TOP_K, pl.ds(r * tok_per_row + c // TOP_K, 1)], sem)

    def each_copy(fn):
        for r in range(rows_per_half):
            for c in range(LANES):
                fn(r, c)

    def start_all(idx_ref, first_row, buf, sem):
        each_copy(lambda r, c: row_copy(idx_ref, first_row, buf, sem, r, c).start(priority=c % 2))

    def wait_all(buf, sem):
        each_copy(lambda r, c: row_copy(dest_ref, 0, buf, sem, r, c).wait())

    def finish(buf, rows):
        w = w_ref[rows, :]
        ffn = (buf[0] * w[:, 0:1] + buf[1] * w[:, 1:2]) + (buf[2] * w[:, 2:3] + buf[3] * w[:, 3:4])
        o_ref[rows, :] = x1_ref[rows, :] + gtf_ref[...] * (_rms(ffn) * gpost_ref[...])

    @pl.when(step == 0)
    def _():
        start_all(dest_ref, 0, buf_a, sem_a)

    wait_all(buf_a, sem_a)
    start_all(dest_ref, rows_per_half, buf_b, sem_b)
    finish(buf_a, slice(0, half))
    wait_all(buf_b, sem_b)
    start_all(dest_next_ref, 0, buf_a, sem_a)
    finish(buf_b, slice(half, 2 * half))

    @pl.when(step == pl.num_programs(0) - 1)
    def _():
        wait_all(buf_a, sem_a)


def _combine(dest2, x1, top_w, mod4, seq, g_post, y):
    n, d = x1.shape
    tm = MOVE_ROWS
    steps = n // tm
    per_seq = seq // tm
    idx_rows = tm * TOP_K // LANES
    half_buf = pltpu.VMEM((TOP_K, tm // 2, d), F32)
    return pl.pallas_call(
        _combine_kernel,
        grid=(steps,),
        in_specs=[pl.BlockSpec((idx_rows, LANES), lambda i: (i, 0), memory_space=pltpu.SMEM),
                  pl.BlockSpec((idx_rows, LANES), lambda i: (jnp.minimum(i + 1, steps - 1), 0),
                               memory_space=pltpu.SMEM),
                  pl.BlockSpec((tm, d), lambda i: (i, 0)),
                  pl.BlockSpec((tm, LANES), lambda i: (i, 0)),
                  pl.BlockSpec((None, None, 1, d), lambda i: (i // per_seq, 5, 0, 0)),
                  _const_spec((1, d)),
                  pl.BlockSpec(memory_space=pl.ANY)],
        out_specs=pl.BlockSpec((tm, d), lambda i: (i, 0)),
        out_shape=jax.ShapeDtypeStruct((n, d), F32),
        scratch_shapes=[half_buf, half_buf, pltpu.SemaphoreType.DMA(()), pltpu.SemaphoreType.DMA(())],
        compiler_params=_cparams("arbitrary"),
        name="moe_combine",
    )(dest2, dest2, x1, top_w, mod4, g_post.reshape(1, d), y)


def _rope_partner(w):
    n_freq = ML_QK_DIM // 4
    d = w.shape[0]
    w4 = w.reshape(d, -1, 2, n_freq)
    return jnp.stack([-w4[:, :, 1], w4[:, :, 0]], axis=2).reshape(w.shape)


def _rope_tables(seq):
    n_freq = ML_QK_DIM // 4
    t = jnp.arange(seq)
    row = (t // GRID_W).astype(F32)
    col = (t % GRID_W).astype(F32)
    inv_freq = ROPE_BASE ** (-jnp.arange(n_freq, dtype=F32) / n_freq)
    ang = jnp.concatenate([row[:, None] * inv_freq] * 2 + [col[:, None] * inv_freq] * 2, axis=-1)
    cos = jnp.tile(jnp.cos(ang), (1, ML_HEADS))
    sin = jnp.tile(jnp.sin(ang), (1, ML_HEADS))
    return cos, sin


def _layer(x, ctx, mod4, g_mix_pre, g_mix_post, g_ffn_pre, g_ffn_post, w_in, b_gates, rpb, g_head,
           w_branch_na, w_branch_ml, w_out, w_router, b_router, w_gate, b_gate, w_lin, b_lin, w_down, b_down):
    b, s, d = x.shape
    n = b * s
    n_ctx = ctx.shape[1]
    x2 = x.reshape(n, d)

    ctx_cols = (NA_WIDTH, NA_WIDTH, ML_QK_WIDTH, ML_WIDTH, N_GATES)
    lat_cols = (NA_WIDTH, ML_QK_WIDTH, ML_WIDTH, d, d)
    bounds = np.cumsum(ctx_cols + lat_cols)[:-1].tolist()
    (w_nak, w_nav, w_mk, w_mv, w_g, w_naq, w_mq, w_mo, w_gna, w_gml) = jnp.split(w_in, bounds, axis=-1)
    w_naq = w_naq * (NA_HEAD_DIM ** -0.5 * LOG2_E)
    w_mk = w_mk * ML_QK_DIM ** -0.5
    bf = lambda a: a.astype(BF16)
    lat_w = [bf(w_naq), bf(w_nak), bf(w_nav), bf(w_mq), bf(_rope_partner(w_mq)), bf(w_mk.T),
             bf(_rope_partner(w_mk).T), bf(w_mv), bf(w_mo), bf(w_gna), bf(w_gml), bf(w_g), bf(w_g.T)]
    ctx_w = [bf(w_nak), bf(w_nav), bf(w_mk.T), bf(w_mv), bf(w_g), bf(w_g.T)]
    bg_col = b_gates.reshape(1, N_GATES).astype(F32)
    bg_row = b_gates.reshape(N_GATES, 1).astype(F32)
    cos, sin = _rope_tables(s)
    per_seq = s // PROJ_ROWS

    (na_q, na_k, na_v, ml_q, ml_k, ml_v, ml_o, gate_na, gate_ml, g_col, g_row) = _project(
        x2, mod4, lambda i: i // per_seq, g_mix_pre, bg_col, bg_row, lat_w, (cos, sin, per_seq), True)
    (na_kc, na_vc, ml_kc, ml_vc, gc_col, gc_row) = _project(
        ctx.reshape(b * n_ctx, d), mod4, lambda i: b, g_mix_pre, bg_col, bg_row, ctx_w, None, False)

    def seq3(a, length):
        return a.reshape(b, length, a.shape[-1])

    o_na = _neighbourhood_attention(seq3(na_q, s), seq3(na_k, s), seq3(na_v, s),
                                    seq3(na_kc, n_ctx), seq3(na_vc, n_ctx), rpb)
    ml_args = (seq3(ml_q, s), ml_k, seq3(ml_v, s), seq3(g_col, s), g_row,
               ml_kc, seq3(ml_vc, n_ctx), seq3(gc_col, n_ctx), gc_row)
    h_f, h_b = _mlstm(*ml_args)

    x1, h2, top_e, top_w, rank, counts = _merge_and_route(
        x2, o_na.reshape(n, NA_WIDTH), h_f.reshape(n, ML_WIDTH), h_b.reshape(n, ML_WIDTH), ml_o,
        gate_na, gate_ml, mod4, s, g_head, bf(w_branch_na), bf(w_branch_ml), bf(w_out),
        g_mix_post, g_ffn_pre, w_router, b_router)

    tm = EXPERT_ROWS
    counts = counts.reshape(N_EXPERTS).astype(jnp.int32)
    padded = (counts + tm - 1) // tm * tm
    pad_end = jnp.cumsum(padded)
    pad_start = pad_end - padded
    n_rows = n * TOP_K + N_EXPERTS * tm
    n_blocks = n_rows // tm
    e_sel = top_e[:, :TOP_K, None] == jnp.arange(N_EXPERTS, dtype=jnp.int32)
    dest = jnp.sum(jnp.where(e_sel, pad_start, 0), axis=-1) + rank[:, :TOP_K]
    dest2 = dest.reshape(n * TOP_K // LANES, LANES).astype(jnp.int32)
    blk_start = jnp.arange(n_blocks, dtype=jnp.int32) * tm
    blk_e = jnp.minimum(jnp.sum(blk_start[:, None] >= pad_end[None, :], axis=1), N_EXPERTS - 1).astype(jnp.int32)
    n_used = (pad_end[-1:] // tm).astype(jnp.int32)

    xs = _dispatch(pad_end.astype(jnp.int32), padded.astype(jnp.int32), dest2, h2, n_rows)
    y = _experts(blk_e, n_used, xs, w_gate, b_gate, w_lin, b_lin, w_down, b_down)
    out = _combine(dest2, x1, top_w, mod4, s, g_ffn_post, y)
    return out.reshape(b, s, d)


def kernel(x, c, ctx, c_ctx, w_ada, b_ada, g_mix_pre, g_mix_post, g_ffn_pre, g_ffn_post, w_in, b_mlstm_gates,
           rpb, g_mlstm_head, w_branch_na, w_branch_ml, w_out, w_router, b_router, w_gate, b_gate, w_lin,
           b_lin, w_down, b_down):
    b, s, d = x.shape
    depth = w_ada.shape[0]
    pad = (-(b + 1)) % 8
    c_all = jnp.concatenate([c, c_ctx[None, :], jnp.zeros((pad, d), c.dtype)], axis=0)
    for layer in range(depth):
        mod = _ada(c_all, w_ada[layer], b_ada[layer])
        mod4 = mod.reshape(mod.shape[0], 6, 1, d)
        x = _layer(x, ctx, mod4, g_mix_pre[layer], g_mix_post[layer], g_ffn_pre[layer], g_ffn_post[layer],
                   w_in[layer], b_mlstm_gates[layer], rpb[layer], g_mlstm_head[layer], w_branch_na[layer],
                   w_branch_ml[layer], w_out[layer], w_router[layer], b_router[layer], w_gate[layer],
                   b_gate[layer], w_lin[layer], b_lin[layer], w_down[layer], b_down[layer])
    return x
```

```python
import functools

import numpy as np
import jax
import jax.numpy as jnp
from jax import lax
from jax.experimental import pallas as pl
from jax.experimental.pallas import tpu as pltpu

F32 = jnp.float32
BF16 = jnp.bfloat16
HIGHEST = lax.Precision.HIGHEST

GRID_W = 64
NA_HEADS = 8
NA_HEAD_DIM = 64
NA_WIDTH = NA_HEADS * NA_HEAD_DIM
WIN_H = 8
WIN_W = 16
ML_HEADS = 4
ML_QK_DIM = 64
ML_V_DIM = 128
ML_QK_WIDTH = ML_HEADS * ML_QK_DIM
ML_WIDTH = ML_HEADS * ML_V_DIM
N_GATES = 4 * ML_HEADS
GATE_SOFTCAP = 15.0
ROPE_BASE = 10000.0
N_EXPERTS = 32
TOP_K = 4
SWIGLU_ALPHA = 1.702
SWIGLU_LIMIT = 7.0
NORM_EPS = 1e-6
NEG_INF = -1e30
LOG2_E = 1.4426950408889634

LANES = 128
NA_QROWS = 4
NA_WROWS = 12
ML_CHUNK = 256
PROJ_ROWS = 512
MERGE_ROWS = 512
EXPERT_ROWS = 512
MOVE_ROWS = 512
VMEM_LIMIT = 56 * 1024 * 1024

NT_DIMS = (((1,), (1,)), ((), ()))


def _cparams(*sem):
    return pltpu.CompilerParams(dimension_semantics=sem, vmem_limit_bytes=VMEM_LIMIT)


def _rms(x):
    return x * lax.rsqrt(jnp.mean(x * x, axis=-1, keepdims=True) + NORM_EPS)


def _ada_kernel(c_ref, w_ref, b_ref, o_ref):
    c = c_ref[...]
    s = c * jax.nn.sigmoid(c)
    o_ref[...] = jnp.dot(s, w_ref[...], preferred_element_type=F32, precision=HIGHEST) + b_ref[...]


def _ada(c_all, w_ada, b_ada):
    rows, d = c_all.shape
    n_out = w_ada.shape[1]
    tn = 1536
    return pl.pallas_call(
        _ada_kernel,
        grid=(n_out // tn,),
        in_specs=[pl.BlockSpec((rows, d), lambda j: (0, 0)),
                  pl.BlockSpec((d, tn), lambda j: (0, j)),
                  pl.BlockSpec((1, tn), lambda j: (0, j))],
        out_specs=pl.BlockSpec((rows, tn), lambda j: (0, j)),
        out_shape=jax.ShapeDtypeStruct((rows, n_out), F32),
        compiler_params=_cparams("arbitrary"),
        name="ada_mod",
    )(c_all, w_ada, b_ada.reshape(1, n_out))


def _gate_logs(g, is_forget):
    g = GATE_SOFTCAP * jnp.tanh(g / GATE_SOFTCAP)
    log_sig = jnp.minimum(g, 0.0) - jnp.log(1.0 + jnp.exp(-jnp.abs(g)))
    return jnp.where(is_forget, log_sig, g)


def _proj_kernel(*refs, latent):
    if latent:
        (x_ref, g_ref, sc_ref, sh_ref, cos_ref, sin_ref, cos_t_ref, sin_t_ref, bgc_ref, bgr_ref,
         w_naq, w_nak, w_nav, w_mq, w_mqp, w_mk, w_mkp, w_mv, w_mo, w_gna, w_gml, w_gc, w_gr,
         o_naq, o_nak, o_nav, o_mq, o_mk, o_mv, o_mo, o_gna, o_gml, o_gc, o_gr) = refs
    else:
        (x_ref, g_ref, sc_ref, sh_ref, bgc_ref, bgr_ref,
         w_nak, w_nav, w_mk, w_mv, w_gc, w_gr,
         o_nak, o_nav, o_mk, o_mv, o_gc, o_gr) = refs
    x = x_ref[...]
    h = _rms(x) * g_ref[...]
    h = h * (1.0 + sc_ref[...]) + sh_ref[...]
    hb = h.astype(BF16)

    def mm(w_ref):
        return jnp.dot(hb, w_ref[...], preferred_element_type=F32)

    def mm_t(w_ref):
        return lax.dot_general(w_ref[...], hb, NT_DIMS, preferred_element_type=F32)

    o_nak[...] = mm(w_nak).astype(BF16)
    o_nav[...] = mm(w_nav).astype(BF16)
    o_mv[...] = mm(w_mv).astype(BF16)
    if latent:
        cos = cos_ref[...]
        sin = sin_ref[...]
        o_naq[...] = mm(w_naq).astype(BF16)
        o_mq[...] = (mm(w_mq) * cos + mm(w_mqp) * sin).astype(BF16)
        o_mk[...] = (mm_t(w_mk) * cos_t_ref[...] + mm_t(w_mkp) * sin_t_ref[...]).astype(BF16)
        o_mo[...] = mm(w_mo).astype(BF16)
        o_gna[...] = mm(w_gna).astype(BF16)
        o_gml[...] = mm(w_gml).astype(BF16)
    else:
        o_mk[...] = mm_t(w_mk).astype(BF16)
    gc = mm(w_gc) + bgc_ref[...]
    col_id = lax.broadcasted_iota(jnp.int32, gc.shape, 1)
    o_gc[...] = _gate_logs(gc, (col_id // ML_HEADS) % 2 == 1)
    gr = mm_t(w_gr) + bgr_ref[...]
    row_id = lax.broadcasted_iota(jnp.int32, gr.shape, 0)
    o_gr[...] = _gate_logs(gr, (row_id // ML_HEADS) % 2 == 1)


def _const_spec(shape):
    nd = len(shape)
    return pl.BlockSpec(shape, lambda i, _nd=nd: (0,) * _nd)


def _project(x2, mod4, mod_row_fn, g_pre, bg_col, bg_row, weights, tables, latent):
    n, d = x2.shape
    tm = PROJ_ROWS
    grid = (n // tm,)

    def mod_spec(j):
        return pl.BlockSpec((None, None, 1, d), lambda i, _j=j: (mod_row_fn(i), _j, 0, 0))

    in_specs = [pl.BlockSpec((tm, d), lambda i: (i, 0)), _const_spec((1, d)), mod_spec(1), mod_spec(0)]
    args = [x2, g_pre.reshape(1, d), mod4, mod4]
    if latent:
        cos, sin, tiles_per_seq = tables
        in_specs += [pl.BlockSpec((tm, ML_QK_WIDTH), lambda i: (i % tiles_per_seq, 0))] * 2
        in_specs += [pl.BlockSpec((ML_QK_WIDTH, tm), lambda i: (0, i % tiles_per_seq))] * 2
        args += [cos, sin, cos.T, sin.T]
    in_specs += [_const_spec(bg_col.shape), _const_spec(bg_row.shape)]
    args += [bg_col, bg_row]
    for w in weights:
        in_specs.append(_const_spec(w.shape))
        args.append(w)

    def out(width, dtype=BF16):
        return (jax.ShapeDtypeStruct((n, width), dtype), pl.BlockSpec((tm, width), lambda i: (i, 0)))

    def out_t(width, dtype=BF16):
        return (jax.ShapeDtypeStruct((width, n), dtype), pl.BlockSpec((width, tm), lambda i: (0, i)))

    if latent:
        outs = [out(NA_WIDTH), out(NA_WIDTH), out(NA_WIDTH), out(ML_QK_WIDTH), out_t(ML_QK_WIDTH),
                out(ML_WIDTH), out(ML_WIDTH), out(d), out(d), out(N_GATES, F32)]
    else:
        outs = [out(NA_WIDTH), out(NA_WIDTH), out_t(ML_QK_WIDTH), out(ML_WIDTH), out(N_GATES, F32)]
    outs.append(out_t(N_GATES, F32))
    return pl.pallas_call(
        functools.partial(_proj_kernel, latent=latent),
        grid=grid,
        in_specs=in_specs,
        out_specs=[o[1] for o in outs],
        out_shape=[o[0] for o in outs],
        compiler_params=_cparams("arbitrary"),
        name="in_proj_latent" if latent else "in_proj_ctx",
    )(*args)


def _na_window_start(r0, rows):
    return jnp.clip(r0 - WIN_H // 2, 0, rows - NA_WROWS)


def _na_classes(rows):
    keys, group_class = [], []
    for r0 in range(0, rows, NA_QROWS):
        start = min(max(r0 - WIN_H // 2, 0), rows - NA_WROWS)
        first = tuple(min(max(r0 + i - WIN_H // 2, 0), rows - WIN_H) - start for i in range(NA_QROWS))
        assert all(0 <= f and f + WIN_H <= NA_WROWS for f in first)
        key = (r0 - start, first)
        if key not in keys:
            keys.append(key)
        group_class.append(keys.index(key))
    return keys, np.asarray(group_class, np.int32)


def _na_bias_table(rpb, rows):
    keys, group_class = _na_classes(rows)
    qc = np.arange(GRID_W)[:, None]
    kc = np.arange(GRID_W)[None, :]
    cs = np.clip(qc - WIN_W // 2, 0, GRID_W - WIN_W)
    col_ok = (kc >= cs) & (kc < cs + WIN_W)
    dc = np.clip(kc - qc, -(WIN_W - 1), WIN_W - 1) + (WIN_W - 1)
    sel_c = (dc[:, :, None] == np.arange(2 * WIN_W - 1)).astype(np.float32)
    block = jnp.einsum('hrc,qkc->hrqk', rpb.astype(F32), sel_c, precision=HIGHEST)
    block = jnp.where(col_ok, block * LOG2_E, NEG_INF).astype(BF16)
    masked = jnp.full((NA_HEADS, GRID_W, GRID_W), NEG_INF, BF16)
    tables = []
    for off, first in keys:
        q_rows = []
        for i in range(NA_QROWS):
            in_window = [first[i] <= j < first[i] + WIN_H for j in range(NA_WROWS)]
            q_rows.append(jnp.concatenate(
                [block[:, j - off - i + (WIN_H - 1)] if in_window[j] else masked for j in range(NA_WROWS)],
                axis=-1))
        tables.append(jnp.concatenate(q_rows, axis=1))
    return jnp.stack(tables), jnp.asarray(group_class)


def _na_kernel(cls_ref, q_ref, k_ref, v_ref, kc_ref, vc_ref, bias_ref, o_ref, *, rows):
    del cls_ref
    r0 = pl.program_id(1) * NA_QROWS
    start = pl.multiple_of(_na_window_start(r0, rows) * GRID_W, GRID_W)
    n_win = NA_WROWS * GRID_W
    low = lax.broadcasted_iota(jnp.int32, (1, LANES), 1) < NA_HEAD_DIM
    for pair in range(NA_HEADS // 2):
        sl = slice(pair * LANES, (pair + 1) * LANES)
        q2 = q_ref[:, sl]
        k2 = k_ref[pl.ds(start, n_win), sl]
        v2 = v_ref[pl.ds(start, n_win), sl]
        kc2 = kc_ref[:, sl]
        vc2 = vc_ref[:, sl]
        scores = []
        for half in range(2):
            keep = low if half == 0 else jnp.logical_not(low)
            qh = jnp.where(keep, q2, jnp.zeros_like(q2))
            s_loc = (lax.dot_general(qh, k2, NT_DIMS, preferred_element_type=F32)
                     + bias_ref[2 * pair + half].astype(F32))
            s_ctx = lax.dot_general(qh, kc2, NT_DIMS, preferred_element_type=F32)
            scores.append((s_loc, s_ctx))
        probs = []
        for s_loc, s_ctx in scores:
            m = jnp.maximum(jnp.max(s_loc, axis=-1, keepdims=True), jnp.max(s_ctx, axis=-1, keepdims=True))
            p_loc = jnp.exp2(s_loc - m)
            p_ctx = jnp.exp2(s_ctx - m)
            denom = jnp.sum(p_loc, axis=-1, keepdims=True) + jnp.sum(p_ctx, axis=-1, keepdims=True)
            probs.append((p_loc.astype(BF16), p_ctx.astype(BF16), denom))
        halves = []
        for p_loc, p_ctx, denom in probs:
            o = (jnp.dot(p_loc, v2, preferred_element_type=F32)
                 + jnp.dot(p_ctx, vc2, preferred_element_type=F32))
            halves.append(o / denom)
        o_ref[:, sl] = jnp.where(low, halves[0], halves[1]).astype(BF16)


def _neighbourhood_attention(q, k, v, kc, vc, rpb):
    b, s, w = q.shape
    rows = s // GRID_W
    n_ctx = kc.shape[1]
    bias, group_class = _na_bias_table(rpb, rows)
    nq = NA_QROWS * GRID_W
    grid_spec = pltpu.PrefetchScalarGridSpec(
        num_scalar_prefetch=1,
        grid=(b, rows // NA_QROWS),
        in_specs=[pl.BlockSpec((None, nq, w), lambda bi, g, cls: (bi, g, 0)),
                  pl.BlockSpec((None, s, w), lambda bi, g, cls: (bi, 0, 0)),
                  pl.BlockSpec((None, s, w), lambda bi, g, cls: (bi, 0, 0)),
                  pl.BlockSpec((None, n_ctx, w), lambda bi, g, cls: (bi, 0, 0)),
                  pl.BlockSpec((None, n_ctx, w), lambda bi, g, cls: (bi, 0, 0)),
                  pl.BlockSpec((None,) + bias.shape[1:], lambda bi, g, cls: (cls[g], 0, 0, 0))],
        out_specs=pl.BlockSpec((None, nq, w), lambda bi, g, cls: (bi, g, 0)),
    )
    return pl.pallas_call(
        functools.partial(_na_kernel, rows=rows),
        grid_spec=grid_spec,
        out_shape=jax.ShapeDtypeStruct((b, s, w), BF16),
        compiler_params=_cparams("arbitrary", "arbitrary"),
        name="na_attention",
    )(group_class, q, k, v, kc, vc, bias)


def _split3(x):
    hi = x.astype(BF16)
    r1 = x - hi.astype(F32)
    mid = r1.astype(BF16)
    lo = (r1 - mid.astype(F32)).astype(BF16)
    return hi, mid, lo


def _mlstm_direction_step(reverse, is_ctx, q, kt, v, gc, gr, o_ref, c_scr, m_scr):
    li_base = 2 * ML_HEADS if reverse else 0
    lf_base = li_base + ML_HEADS
    length = kt.shape[1]
    t = lax.broadcasted_iota(jnp.int32, (length, length), 0)
    s = lax.broadcasted_iota(jnp.int32, (length, length), 1)
    valid = (s >= t) if reverse else (s <= t)
    valid_t = (t >= s) if reverse else (t <= s)
    b_col = sum(jnp.dot(valid.astype(BF16), p, preferred_element_type=F32) for p in _split3(gc))
    b_row = sum(jnp.dot(p, valid_t.astype(BF16), preferred_element_type=F32) for p in _split3(gr))
    low = lax.broadcasted_iota(jnp.int32, (1, LANES), 1) < ML_QK_DIM
    ones = jnp.ones((length, ML_V_DIM), BF16)
    heads = range(ML_HEADS)
    li_r = [gr[li_base + h:li_base + h + 1, :] for h in heads]
    lf_r = [gr[lf_base + h:lf_base + h + 1, :] for h in heads]
    br = [b_row[lf_base + h:lf_base + h + 1, :] for h in heads]
    m_prev = [m_scr[h] for h in heads]
    v_ext = [jnp.concatenate([v[:, h * ML_V_DIM:(h + 1) * ML_V_DIM], ones], axis=-1) for h in heads]
    kt_pair = [kt[(h // 2) * LANES:(h // 2 + 1) * LANES, :] for h in heads]

    yield
    if not is_ctx:
        weights = []
        for h in heads:
            bc = b_col[:, lf_base + h:lf_base + h + 1]
            d_mat = jnp.where(valid, bc - br[h] + li_r[h], NEG_INF)
            m_inter = bc + m_prev[h]
            m_t = jnp.maximum(m_inter, jnp.max(d_mat, axis=-1, keepdims=True))
            weights.append((jnp.exp(d_mat - m_t), jnp.exp(m_inter - m_t), jnp.exp(-m_t)))
        yield
        for h in heads:
            w_intra, w_inter, floor = weights[h]
            q2 = q[:, (h // 2) * LANES:(h // 2 + 1) * LANES]
            qh = jnp.where(low if h % 2 == 0 else jnp.logical_not(low), q2, jnp.zeros_like(q2))
            sc = jnp.dot(qh, kt_pair[h], preferred_element_type=F32) * w_intra
            both = (w_inter * jnp.dot(qh, c_scr[h // 2].astype(BF16), preferred_element_type=F32)
                    + jnp.dot(sc.astype(BF16), v_ext[h], preferred_element_type=F32))
            num = both[:, :ML_V_DIM]
            den = both[:, ML_V_DIM:]
            o_ref[:, h * ML_V_DIM:(h + 1) * ML_V_DIM] = (num / jnp.maximum(jnp.abs(den), floor)).astype(BF16)

    yield
    for h in heads:
        rows = slice((h % 2) * ML_QK_DIM, (h % 2 + 1) * ML_QK_DIM)
        total = jnp.sum(lf_r[h], axis=-1, keepdims=True)
        m_new = jnp.maximum(total + m_prev[h], jnp.max(total - br[h] + li_r[h], axis=-1, keepdims=True))
        w_src = jnp.exp(total - br[h] + li_r[h] - m_new)
        w_carry = jnp.exp(total + m_prev[h] - m_new)
        kw = (kt_pair[h][rows, :].astype(F32) * w_src).astype(BF16)
        c_scr[h // 2, rows, :] = (w_carry * c_scr[h // 2, rows, :]
                                  + jnp.dot(kw, v_ext[h], preferred_element_type=F32))
        m_scr[h] = m_new


def _alternate(*phased):
    live = list(phased)
    while live:
        live = [g for g in live if next(g, StopIteration) is not StopIteration]


def _mlstm_kernel(qf_ref, kf_ref, vf_ref, gcf_ref, grf_ref, qb_ref, kb_ref, vb_ref, gcb_ref, grb_ref,
                  kc_ref, vc_ref, gcc_ref, gcr_ref, of_ref, ob_ref, cf_scr, mf_scr, cb_scr, mb_scr):
    step = pl.program_id(1)

    @pl.when(step == 0)
    def _():
        for ref in (cf_scr, mf_scr, cb_scr, mb_scr):
            ref[...] = jnp.zeros_like(ref)
        kt = kc_ref[...]
        v = vc_ref[...]
        gc = gcc_ref[...]
        gr = gcr_ref[...]
        _alternate(_mlstm_direction_step(False, True, None, kt, v, gc, gr, None, cf_scr, mf_scr),
                   _mlstm_direction_step(True, True, None, kt, v, gc, gr, None, cb_scr, mb_scr))

    @pl.when(step > 0)
    def _():
        _alternate(_mlstm_direction_step(False, False, qf_ref[...], kf_ref[...], vf_ref[...], gcf_ref[...],
                                         grf_ref[...], of_ref, cf_scr, mf_scr),
                   _mlstm_direction_step(True, False, qb_ref[...], kb_ref[...], vb_ref[...], gcb_ref[...],
                                         grb_ref[...], ob_ref, cb_scr, mb_scr))


def _mlstm(q, k, v, gc, gr, kc, vc, gcc, gcr):
    b, s, _ = q.shape
    n_ctx = vc.shape[1]
    length = ML_CHUNK
    n_chunks = s // length

    def chunk(step, reverse):
        c = jnp.maximum(step - 1, 0)
        return (n_chunks - 1 - c) if reverse else c

    def stream_specs(reverse):
        def seq(width):
            return pl.BlockSpec((None, length, width), lambda bi, st: (bi, chunk(st, reverse), 0))

        def seq_t(width):
            return pl.BlockSpec((width, length), lambda bi, st: (0, bi * n_chunks + chunk(st, reverse)))
        return [seq(ML_QK_WIDTH), seq_t(ML_QK_WIDTH), seq(ML_WIDTH), seq(N_GATES), seq_t(N_GATES)]

    def out_spec(reverse):
        return pl.BlockSpec((None, length, ML_WIDTH), lambda bi, st: (bi, chunk(st, reverse), 0))

    state = [pltpu.VMEM((ML_HEADS // 2, 2 * ML_QK_DIM, 2 * ML_V_DIM), F32),
             pltpu.VMEM((ML_HEADS, 1, 1), F32)]
    return pl.pallas_call(
        _mlstm_kernel,
        grid=(b, n_chunks + 1),
        in_specs=stream_specs(False) + stream_specs(True) + [
            pl.BlockSpec((ML_QK_WIDTH, n_ctx), lambda bi, st: (0, bi)),
            pl.BlockSpec((None, n_ctx, ML_WIDTH), lambda bi, st: (bi, 0, 0)),
            pl.BlockSpec((None, n_ctx, N_GATES), lambda bi, st: (bi, 0, 0)),
            pl.BlockSpec((N_GATES, n_ctx), lambda bi, st: (0, bi))],
        out_specs=[out_spec(False), out_spec(True)],
        out_shape=[jax.ShapeDtypeStruct((b, s, ML_WIDTH), BF16)] * 2,
        scratch_shapes=state + state,
        compiler_params=_cparams("arbitrary", "arbitrary"),
        name="mlstm",
    )(q, k, v, gc, gr, q, k, v, gc, gr, kc, vc, gcc, gcr)


def _merge_kernel(x_ref, ona_ref, hf_ref, hb_ref, opre_ref, gna_ref, gml_ref, gtm_ref, scf_ref, shf_ref,
                  ghead_ref, wbna_ref, wbml_ref, wout_ref, gpost_ref, gpre_ref, wrh_ref, wrl_ref, br_ref,
                  x1_ref, h2_ref, tope_ref, topw_ref, rank_ref, cnt_ref):
    step = pl.program_id(0)
    tm = x_ref.shape[0]

    @pl.when(step == 0)
    def _():
        cnt_ref[...] = jnp.zeros_like(cnt_ref)

    hsum = hf_ref[...].astype(F32) + hb_ref[...].astype(F32)
    heads = [_rms(hsum[:, h * ML_V_DIM:(h + 1) * ML_V_DIM]) for h in range(ML_HEADS)]
    hn = jnp.concatenate(heads, axis=-1) * ghead_ref[...]
    o_ml = jax.nn.sigmoid(opre_ref[...].astype(F32)) * hn
    merged = (jax.nn.sigmoid(gna_ref[...].astype(F32))
              * jnp.dot(ona_ref[...], wbna_ref[...], preferred_element_type=F32)
              + jax.nn.sigmoid(gml_ref[...].astype(F32))
              * jnp.dot(o_ml.astype(BF16), wbml_ref[...], preferred_element_type=F32))
    mixed = jnp.dot(merged.astype(BF16), wout_ref[...], preferred_element_type=F32)
    x1 = x_ref[...] + gtm_ref[...] * (_rms(mixed) * gpost_ref[...])
    x1_ref[...] = x1
    h2 = _rms(x1) * gpre_ref[...] * (1.0 + scf_ref[...]) + shf_ref[...]
    h2_ref[...] = h2
    h2_hi = h2.astype(BF16)
    h2_lo = (h2 - h2_hi.astype(F32)).astype(BF16)
    logits = (jnp.dot(h2_hi, wrh_ref[...], preferred_element_type=F32)
              + (jnp.dot(h2_hi, wrl_ref[...], preferred_element_type=F32)
                 + jnp.dot(h2_lo, wrh_ref[...], preferred_element_type=F32))) + br_ref[...]

    lane = lax.broadcasted_iota(jnp.int32, logits.shape, 1)
    onehots, top_e, top_l = [], [], []
    for _ in range(TOP_K):
        best = jnp.max(logits, axis=-1, keepdims=True)
        e = jnp.min(jnp.where(logits == best, lane, N_EXPERTS), axis=-1, keepdims=True)
        hit = lane == e
        onehots.append(hit)
        top_e.append(e)
        top_l.append(best)
        logits = jnp.where(hit, -jnp.inf, logits)
    exps = [jnp.exp(l - top_l[0]) for l in top_l]
    total = exps[0] + exps[1] + exps[2] + exps[3]

    counts = (onehots[0].astype(F32) + onehots[1].astype(F32)
              + onehots[2].astype(F32) + onehots[3].astype(F32))
    t = lax.broadcasted_iota(jnp.int32, (tm, tm), 0)
    s = lax.broadcasted_iota(jnp.int32, (tm, tm), 1)
    before = jnp.dot((s < t).astype(BF16), counts.astype(BF16), preferred_element_type=F32) + cnt_ref[...]
    out_lane = lax.broadcasted_iota(jnp.int32, (tm, LANES), 1)
    e_out = jnp.zeros((tm, LANES), jnp.int32)
    w_out = jnp.zeros((tm, LANES), F32)
    r_out = jnp.zeros((tm, LANES), jnp.int32)
    for j in range(TOP_K):
        rank = jnp.sum(jnp.where(onehots[j], before, 0.0), axis=-1, keepdims=True).astype(jnp.int32)
        e_out = jnp.where(out_lane == j, top_e[j], e_out)
        w_out = jnp.where(out_lane == j, exps[j] / total, w_out)
        r_out = jnp.where(out_lane == j, rank, r_out)
    tope_ref[...] = e_out
    topw_ref[...] = w_out
    rank_ref[...] = r_out
    cnt_ref[...] += jnp.sum(counts, axis=0, keepdims=True)


def _merge_and_route(x2, o_na, h_f, h_b, o_pre, g_na, g_ml, mod4, seq, g_head, wbna, wbml, wout,
                     g_post, g_pre, w_router, b_router):
    n, d = x2.shape
    tm = MERGE_ROWS
    per_seq = seq // tm

    def rows(width):
        return pl.BlockSpec((tm, width), lambda i: (i, 0))

    def mod_spec(j):
        return pl.BlockSpec((None, None, 1, d), lambda i, _j=j: (i // per_seq, _j, 0, 0))

    w_router_hi = w_router.astype(BF16)
    return pl.pallas_call(
        _merge_kernel,
        grid=(n // tm,),
        in_specs=[rows(d), rows(NA_WIDTH), rows(ML_WIDTH), rows(ML_WIDTH), rows(ML_WIDTH), rows(d), rows(d),
                  mod_spec(2), mod_spec(4), mod_spec(3),
                  _const_spec((1, ML_WIDTH)), _const_spec(wbna.shape), _const_spec(wbml.shape),
                  _const_spec(wout.shape), _const_spec((1, d)), _const_spec((1, d)),
                  _const_spec(w_router.shape), _const_spec(w_router.shape), _const_spec((1, N_EXPERTS))],
        out_specs=[rows(d), rows(d), rows(LANES), rows(LANES), rows(LANES),
                   pl.BlockSpec((1, N_EXPERTS), lambda i: (0, 0))],
        out_shape=[jax.ShapeDtypeStruct((n, d), F32), jax.ShapeDtypeStruct((n, d), F32),
                   jax.ShapeDtypeStruct((n, LANES), jnp.int32), jax.ShapeDtypeStruct((n, LANES), F32),
                   jax.ShapeDtypeStruct((n, LANES), jnp.int32),
                   jax.ShapeDtypeStruct((1, N_EXPERTS), F32)],
        compiler_params=_cparams("arbitrary"),
        name="merge_route",
    )(x2, o_na, h_f, h_b, o_pre, g_na, g_ml, mod4, mod4, mod4,
      g_head.reshape(1, ML_WIDTH), wbna, wbml, wout, g_post.reshape(1, d), g_pre.reshape(1, d),
      w_router_hi, (w_router - w_router_hi.astype(F32)).astype(BF16), b_router.reshape(1, N_EXPERTS))


def _dispatch_kernel(pad_end_ref, padded_ref, dest_ref, h_ref, xs_ref, zero_scr, sem):
    tm = h_ref.shape[0]
    blk = zero_scr.shape[0]

    @pl.when(pl.program_id(0) == 0)
    def _():
        zero_scr[...] = jnp.zeros_like(zero_scr)

        def zero_copy(e):
            first = pl.multiple_of(pad_end_ref[e] - blk, blk)
            return pltpu.make_async_copy(zero_scr, xs_ref.at[pl.ds(first, blk)], sem)

        for e in range(N_EXPERTS):
            @pl.when(padded_ref[e] > 0)
            def _():
                zero_copy(e).start()
        for e in range(N_EXPERTS):
            @pl.when(padded_ref[e] > 0)
            def _():
                zero_copy(e).wait()

        def tail_copy(b):
            return pltpu.make_async_copy(zero_scr, xs_ref.at[pl.ds(pl.multiple_of(b * blk, blk), blk)], sem)

        def tail_start(b, carry):
            tail_copy(b).start()
            return carry

        def tail_wait(b, carry):
            tail_copy(b).wait()
            return carry

        first_unused = pad_end_ref[N_EXPERTS - 1] // blk
        lax.fori_loop(first_unused, xs_ref.shape[0] // blk, tail_start, 0)
        lax.fori_loop(first_unused, xs_ref.shape[0] // blk, tail_wait, 0)

    tok_per_row = LANES // TOP_K

    def row_copy(r, c):
        return pltpu.make_async_copy(h_ref.at[pl.ds(r * tok_per_row + c // TOP_K, 1)],
                                     xs_ref.at[pl.ds(dest_ref[r, c], 1)], sem)

    for r in range(tm // tok_per_row):
        for c in range(LANES):
            row_copy(r, c).start(priority=c % 2)
    for r in range(tm // tok_per_row):
        for c in range(LANES):
            row_copy(r, c).wait()


def _dispatch(pad_end, padded, dest2, h2, n_rows):
    n = h2.shape[0]
    tm = MOVE_ROWS
    idx_rows = tm * TOP_K // LANES
    row_tile = h2.shape[1:]
    grid_spec = pltpu.PrefetchScalarGridSpec(
        num_scalar_prefetch=2,
        grid=(n // tm,),
        in_specs=[pl.BlockSpec((idx_rows, LANES), lambda i, pe, pd: (i, 0), memory_space=pltpu.SMEM),
                  pl.BlockSpec((tm,) + row_tile, lambda i, pe, pd: (i, 0))],
        out_specs=pl.BlockSpec(memory_space=pl.ANY),
        scratch_shapes=[pltpu.VMEM((EXPERT_ROWS,) + row_tile, h2.dtype), pltpu.SemaphoreType.DMA(())],
    )
    return pl.pallas_call(
        _dispatch_kernel,
        grid_spec=grid_spec,
        out_shape=jax.ShapeDtypeStruct((n_rows,) + row_tile, h2.dtype),
        compiler_params=_cparams("arbitrary"),
        name="moe_dispatch",
    )(pad_end, padded, dest2, h2)


def _expert_kernel(blk_e_ref, n_used_ref, x_ref, wg_ref, bg_ref, wl_ref, bl_ref, wd_ref, bd_ref, y_ref,
                   wg_s, wl_s, wd_s):
    i = pl.program_id(0)
    prev = blk_e_ref[jnp.maximum(i - 1, 0)]
    changed = jnp.logical_or(i == 0, blk_e_ref[i] != prev)
    used = i < n_used_ref[0]

    @pl.when(jnp.logical_and(used, changed))
    def _():
        wg_s[...] = wg_ref[...].astype(BF16)
        wl_s[...] = wl_ref[...].astype(BF16)
        wd_s[...] = wd_ref[...].astype(BF16)

    @pl.when(used)
    def _():
        xb = x_ref[...].astype(BF16)
        g = jnp.dot(xb, wg_s[...], preferred_element_type=F32) + bg_ref[...]
        l = jnp.dot(xb, wl_s[...], preferred_element_type=F32) + bl_ref[...]
        g = jnp.minimum(g, SWIGLU_LIMIT)
        l = jnp.clip(l, -SWIGLU_LIMIT, SWIGLU_LIMIT)
        a = g * jax.nn.sigmoid(SWIGLU_ALPHA * g) * (l + 1.0)
        y_ref[...] = jnp.dot(a.astype(BF16), wd_s[...], preferred_element_type=F32) + bd_ref[...]

    @pl.when(jnp.logical_not(used))
    def _():
        y_ref[...] = jnp.zeros_like(y_ref)


def _experts(blk_e, n_used, xs, w_gate, b_gate, w_lin, b_lin, w_down, b_down):
    n_rows = xs.shape[0]
    row_tile = xs.shape[1:]
    e, d, f = w_gate.shape
    tm = EXPERT_ROWS

    def w_spec(shape):
        return pl.BlockSpec((None,) + shape, lambda i, be, nu: (be[i], 0, 0))

    grid_spec = pltpu.PrefetchScalarGridSpec(
        num_scalar_prefetch=2,
        grid=(n_rows // tm,),
        in_specs=[pl.BlockSpec((tm,) + row_tile, lambda i, be, nu: (i, 0)),
                  w_spec((d, f)), w_spec((1, f)), w_spec((d, f)), w_spec((1, f)),
                  w_spec((f, d)), w_spec((1, d))],
        out_specs=pl.BlockSpec((tm,) + row_tile, lambda i, be, nu: (i, 0)),
        scratch_shapes=[pltpu.VMEM((d, f), BF16), pltpu.VMEM((d, f), BF16), pltpu.VMEM((f, d), BF16)],
    )
    return pl.pallas_call(
        _expert_kernel,
        grid_spec=grid_spec,
        out_shape=jax.ShapeDtypeStruct((n_rows,) + row_tile, F32),
        compiler_params=_cparams("arbitrary"),
        name="moe_experts",
    )(blk_e, n_used, xs, w_gate, b_gate.reshape(e, 1, f), w_lin, b_lin.reshape(e, 1, f),
      w_down, b_down.reshape(e, 1, d))


def _combine_kernel(dest_ref, dest_next_ref, x1_ref, w_ref, gtf_ref, gpost_ref, y_ref, o_ref,
                    buf_a, buf_b, sem_a, sem_b):
    step = pl.program_id(0)
    half = x1_ref.shape[0] // 2
    tok_per_row = LANES // TOP_K
    rows_per_half = half // tok_per_row

    def row_copy(idx_ref, first_row, buf, sem, r, c):
        return pltpu.make_async_copy(y_ref.at[pl.ds(idx_ref[first_row + r, c], 1)],
                                     buf.at[c % TOP_K, pl.ds(r * tok_per_row + c // TOP_K, 1)], sem)

    def each_copy(fn):
        for r in range(rows_per_half):
            for c in range(LANES):
                fn(r, c)

    def start_all(idx_ref, first_row, buf, sem):
        each_copy(lambda r, c: row_copy(idx_ref, first_row, buf, sem, r, c).start(priority=c % 2))

    def wait_all(buf, sem):
        each_copy(lambda r, c: row_copy(dest_ref, 0, buf, sem, r, c).wait())

    def finish(buf, rows):
        w = w_ref[rows, :]
        ffn = (buf[0] * w[:, 0:1] + buf[1] * w[:, 1:2]) + (buf[2] * w[:, 2:3] + buf[3] * w[:, 3:4])
        o_ref[rows, :] = x1_ref[rows, :] + gtf_ref[...] * (_rms(ffn) * gpost_ref[...])

    @pl.when(step == 0)
    def _():
        start_all(dest_ref, 0, buf_a, sem_a)

    wait_all(buf_a, sem_a)
    start_all(dest_ref, rows_per_half, buf_b, sem_b)
    finish(buf_a, slice(0, half))
    wait_all(buf_b, sem_b)
    start_all(dest_next_ref, 0, buf_a, sem_a)
    finish(buf_b, slice(half, 2 * half))

    @pl.when(step == pl.num_programs(0) - 1)
    def _():
        wait_all(buf_a, sem_a)


def _combine(dest2, x1, top_w, mod4, seq, g_post, y):
    n, d = x1.shape
    tm = MOVE_ROWS
    steps = n // tm
    per_seq = seq // tm
    idx_rows = tm * TOP_K // LANES
    half_buf = pltpu.VMEM((TOP_K, tm // 2, d), F32)
    return pl.pallas_call(
        _combine_kernel,
        grid=(steps,),
        in_specs=[pl.BlockSpec((idx_rows, LANES), lambda i: (i, 0), memory_space=pltpu.SMEM),
                  pl.BlockSpec((idx_rows, LANES), lambda i: (jnp.minimum(i + 1, steps - 1), 0),
                               memory_space=pltpu.SMEM),
                  pl.BlockSpec((tm, d), lambda i: (i, 0)),
                  pl.BlockSpec((tm, LANES), lambda i: (i, 0)),
                  pl.BlockSpec((None, None, 1, d), lambda i: (i // per_seq, 5, 0, 0)),
                  _const_spec((1, d)),
                  pl.BlockSpec(memory_space=pl.ANY)],
        out_specs=pl.BlockSpec((tm, d), lambda i: (i, 0)),
        out_shape=jax.ShapeDtypeStruct((n, d), F32),
        scratch_shapes=[half_buf, half_buf, pltpu.SemaphoreType.DMA(()), pltpu.SemaphoreType.DMA(())],
        compiler_params=_cparams("arbitrary"),
        name="moe_combine",
    )(dest2, dest2, x1, top_w, mod4, g_post.reshape(1, d), y)


def _rope_partner(w):
    n_freq = ML_QK_DIM // 4
    d = w.shape[0]
    w4 = w.reshape(d, -1, 2, n_freq)
    return jnp.stack([-w4[:, :, 1], w4[:, :, 0]], axis=2).reshape(w.shape)


def _rope_tables(seq):
    n_freq = ML_QK_DIM // 4
    t = jnp.arange(seq)
    row = (t // GRID_W).astype(F32)
    col = (t % GRID_W).astype(F32)
    inv_freq = ROPE_BASE ** (-jnp.arange(n_freq, dtype=F32) / n_freq)
    ang = jnp.concatenate([row[:, None] * inv_freq] * 2 + [col[:, None] * inv_freq] * 2, axis=-1)
    cos = jnp.tile(jnp.cos(ang), (1, ML_HEADS))
    sin = jnp.tile(jnp.sin(ang), (1, ML_HEADS))
    return cos, sin


def _layer(x, ctx, mod4, g_mix_pre, g_mix_post, g_ffn_pre, g_ffn_post, w_in, b_gates, rpb, g_head,
           w_branch_na, w_branch_ml, w_out, w_router, b_router, w_gate, b_gate, w_lin, b_lin, w_down, b_down):
    b, s, d = x.shape
    n = b * s
    n_ctx = ctx.shape[1]
    x2 = x.reshape(n, d)

    ctx_cols = (NA_WIDTH, NA_WIDTH, ML_QK_WIDTH, ML_WIDTH, N_GATES)
    lat_cols = (NA_WIDTH, ML_QK_WIDTH, ML_WIDTH, d, d)
    bounds = np.cumsum(ctx_cols + lat_cols)[:-1].tolist()
    (w_nak, w_nav, w_mk, w_mv, w_g, w_naq, w_mq, w_mo, w_gna, w_gml) = jnp.split(w_in, bounds, axis=-1)
    w_naq = w_naq * (NA_HEAD_DIM ** -0.5 * LOG2_E)
    w_mk = w_mk * ML_QK_DIM ** -0.5
    bf = lambda a: a.astype(BF16)
    lat_w = [bf(w_naq), bf(w_nak), bf(w_nav), bf(w_mq), bf(_rope_partner(w_mq)), bf(w_mk.T),
             bf(_rope_partner(w_mk).T), bf(w_mv), bf(w_mo), bf(w_gna), bf(w_gml), bf(w_g), bf(w_g.T)]
    ctx_w = [bf(w_nak), bf(w_nav), bf(w_mk.T), bf(w_mv), bf(w_g), bf(w_g.T)]
    bg_col = b_gates.reshape(1, N_GATES).astype(F32)
    bg_row = b_gates.reshape(N_GATES, 1).astype(F32)
    cos, sin = _rope_tables(s)
    per_seq = s // PROJ_ROWS

    (na_q, na_k, na_v, ml_q, ml_k, ml_v, ml_o, gate_na, gate_ml, g_col, g_row) = _project(
        x2, mod4, lambda i: i // per_seq, g_mix_pre, bg_col, bg_row, lat_w, (cos, sin, per_seq), True)
    (na_kc, na_vc, ml_kc, ml_vc, gc_col, gc_row) = _project(
        ctx.reshape(b * n_ctx, d), mod4, lambda i: b, g_mix_pre, bg_col, bg_row, ctx_w, None, False)

    def seq3(a, length):
        return a.reshape(b, length, a.shape[-1])

    o_na = _neighbourhood_attention(seq3(na_q, s), seq3(na_k, s), seq3(na_v, s),
                                    seq3(na_kc, n_ctx), seq3(na_vc, n_ctx), rpb)
    ml_args = (seq3(ml_q, s), ml_k, seq3(ml_v, s), seq3(g_col, s), g_row,
               ml_kc, seq3(ml_vc, n_ctx), seq3(gc_col, n_ctx), gc_row)
    h_f, h_b = _mlstm(*ml_args)

    x1, h2, top_e, top_w, rank, counts = _merge_and_route(
        x2, o_na.reshape(n, NA_WIDTH), h_f.reshape(n, ML_WIDTH), h_b.reshape(n, ML_WIDTH), ml_o,
        gate_na, gate_ml, mod4, s, g_head, bf(w_branch_na), bf(w_branch_ml), bf(w_out),
        g_mix_post, g_ffn_pre, w_router, b_router)

    tm = EXPERT_ROWS
    counts = counts.reshape(N_EXPERTS).astype(jnp.int32)
    padded = (counts + tm - 1) // tm * tm
    pad_end = jnp.cumsum(padded)
    pad_start = pad_end - padded
    n_rows = n * TOP_K + N_EXPERTS * tm
    n_blocks = n_rows // tm
    e_sel = top_e[:, :TOP_K, None] == jnp.arange(N_EXPERTS, dtype=jnp.int32)
    dest = jnp.sum(jnp.where(e_sel, pad_start, 0), axis=-1) + rank[:, :TOP_K]
    dest2 = dest.reshape(n * TOP_K // LANES, LANES).astype(jnp.int32)
    blk_start = jnp.arange(n_blocks, dtype=jnp.int32) * tm
    blk_e = jnp.minimum(jnp.sum(blk_start[:, None] >= pad_end[None, :], axis=1), N_EXPERTS - 1).astype(jnp.int32)
    n_used = (pad_end[-1:] // tm).astype(jnp.int32)

    xs = _dispatch(pad_end.astype(jnp.int32), padded.astype(jnp.int32), dest2, h2, n_rows)
    y = _experts(blk_e, n_used, xs, w_gate, b_gate, w_lin, b_lin, w_down, b_down)
    out = _combine(dest2, x1, top_w, mod4, s, g_ffn_post, y)
    return out.reshape(b, s, d)


def kernel(x, c, ctx, c_ctx, w_ada, b_ada, g_mix_pre, g_mix_post, g_ffn_pre, g_ffn_post, w_in, b_mlstm_gates,
           rpb, g_mlstm_head, w_branch_na, w_branch_ml, w_out, w_router, b_router, w_gate, b_gate, w_lin,
           b_lin, w_down, b_down):
    b, s, d = x.shape
    depth = w_ada.shape[0]
    pad = (-(b + 1)) % 8
    c_all = jnp.concatenate([c, c_ctx[None, :], jnp.zeros((pad, d), c.dtype)], axis=0)
    for layer in range(depth):
        mod = _ada(c_all, w_ada[layer], b_ada[layer])
        mod4 = mod.reshape(mod.shape[0], 6, 1, d)
        x = _layer(x, ctx, mod4, g_mix_pre[layer], g_mix_post[layer], g_ffn_pre[layer], g_ffn_post[layer],
                   w_in[layer], b_mlstm_gates[layer], rpb[layer], g_mlstm_head[layer], w_branch_na[layer],
                   w_branch_ml[layer], w_out[layer], w_router[layer], b_router[layer], w_gate[layer],
                   b_gate[layer], w_lin[layer], b_lin[layer], w_down[layer], b_down[layer])
    return x
```

```python
import functools

import numpy as np
import jax
import jax.numpy as jnp
from jax import lax
from jax.experimental import pallas as pl
from jax.experimental.pallas import tpu as pltpu

F32 = jnp.float32
BF16 = jnp.bfloat16
HIGHEST = lax.Precision.HIGHEST

GRID_W = 64
NA_HEADS = 8
NA_HEAD_DIM = 64
NA_WIDTH = NA_HEADS * NA_HEAD_DIM
WIN_H = 8
WIN_W = 16
ML_HEADS = 4
ML_QK_DIM = 64
ML_V_DIM = 128
ML_QK_WIDTH = ML_HEADS * ML_QK_DIM
ML_WIDTH = ML_HEADS * ML_V_DIM
N_GATES = 4 * ML_HEADS
GATE_SOFTCAP = 15.0
ROPE_BASE = 10000.0
N_EXPERTS = 32
TOP_K = 4
SWIGLU_ALPHA = 1.702
SWIGLU_LIMIT = 7.0
NORM_EPS = 1e-6
NEG_INF = -1e30
LOG2_E = 1.4426950408889634

LANES = 128
NA_QROWS = 4
NA_WROWS = 12
ML_CHUNK = 256
PROJ_ROWS = 512
MERGE_ROWS = 512
EXPERT_ROWS = 512
MOVE_ROWS = 512
VMEM_LIMIT = 56 * 1024 * 1024

NT_DIMS = (((1,), (1,)), ((), ()))


def _cparams(*sem):
    return pltpu.CompilerParams(dimension_semantics=sem, vmem_limit_bytes=VMEM_LIMIT)


def _rms(x):
    return x * lax.rsqrt(jnp.mean(x * x, axis=-1, keepdims=True) + NORM_EPS)


def _ada_kernel(c_ref, w_ref, b_ref, o_ref):
    c = c_ref[...]
    s = c * jax.nn.sigmoid(c)
    o_ref[...] = jnp.dot(s, w_ref[...], preferred_element_type=F32, precision=HIGHEST) + b_ref[...]


def _ada(c_all, w_ada, b_ada):
    rows, d = c_all.shape
    n_out = w_ada.shape[1]
    tn = 1536
    return pl.pallas_call(
        _ada_kernel,
        grid=(n_out // tn,),
        in_specs=[pl.BlockSpec((rows, d), lambda j: (0, 0)),
                  pl.BlockSpec((d, tn), lambda j: (0, j)),
                  pl.BlockSpec((1, tn), lambda j: (0, j))],
        out_specs=pl.BlockSpec((rows, tn), lambda j: (0, j)),
        out_shape=jax.ShapeDtypeStruct((rows, n_out), F32),
        compiler_params=_cparams("arbitrary"),
        name="ada_mod",
    )(c_all, w_ada, b_ada.reshape(1, n_out))


def _gate_logs(g, is_forget):
    g = GATE_SOFTCAP * jnp.tanh(g / GATE_SOFTCAP)
    log_sig = jnp.minimum(g, 0.0) - jnp.log(1.0 + jnp.exp(-jnp.abs(g)))
    return jnp.where(is_forget, log_sig, g)


def _proj_kernel(*refs, latent):
    if latent:
        (x_ref, g_ref, sc_ref, sh_ref, cos_ref, sin_ref, cos_t_ref, sin_t_ref, bgc_ref, bgr_ref,
         w_naq, w_nak, w_nav, w_mq, w_mqp, w_mk, w_mkp, w_mv, w_mo, w_gna, w_gml, w_gc, w_gr,
         o_naq, o_nak, o_nav, o_mq, o_mk, o_mv, o_mo, o_gna, o_gml, o_gc, o_gr) = refs
    else:
        (x_ref, g_ref, sc_ref, sh_ref, bgc_ref, bgr_ref,
         w_nak, w_nav, w_mk, w_mv, w_gc, w_gr,
         o_nak, o_nav, o_mk, o_mv, o_gc, o_gr) = refs
    x = x_ref[...]
    h = _rms(x) * g_ref[...]
    h = h * (1.0 + sc_ref[...]) + sh_ref[...]
    hb = h.astype(BF16)

    def mm(w_ref):
        return jnp.dot(hb, w_ref[...], preferred_element_type=F32)

    def mm_t(w_ref):
        return lax.dot_general(w_ref[...], hb, NT_DIMS, preferred_element_type=F32)

    o_nak[...] = mm(w_nak).astype(BF16)
    o_nav[...] = mm(w_nav).astype(BF16)
    o_mv[...] = mm(w_mv).astype(BF16)
    if latent:
        cos = cos_ref[...]
        sin = sin_ref[...]
        o_naq[...] = mm(w_naq).astype(BF16)
        o_mq[...] = (mm(w_mq) * cos + mm(w_mqp) * sin).astype(BF16)
        o_mk[...] = (mm_t(w_mk) * cos_t_ref[...] + mm_t(w_mkp) * sin_t_ref[...]).astype(BF16)
        o_mo[...] = mm(w_mo).astype(BF16)
        o_gna[...] = mm(w_gna).astype(BF16)
        o_gml[...] = mm(w_gml).astype(BF16)
    else:
        o_mk[...] = mm_t(w_mk).astype(BF16)
    gc = mm(w_gc) + bgc_ref[...]
    col_id = lax.broadcasted_iota(jnp.int32, gc.shape, 1)
    o_gc[...] = _gate_logs(gc, (col_id // ML_HEADS) % 2 == 1)
    gr = mm_t(w_gr) + bgr_ref[...]
    row_id = lax.broadcasted_iota(jnp.int32, gr.shape, 0)
    o_gr[...] = _gate_logs(gr, (row_id // ML_HEADS) % 2 == 1)


def _const_spec(shape):
    nd = len(shape)
    return pl.BlockSpec(shape, lambda i, _nd=nd: (0,) * _nd)


def _project(x2, mod4, mod_row_fn, g_pre, bg_col, bg_row, weights, tables, latent):
    n, d = x2.shape
    tm = PROJ_ROWS
    grid = (n // tm,)

    def mod_spec(j):
        return pl.BlockSpec((None, None, 1, d), lambda i, _j=j: (mod_row_fn(i), _j, 0, 0))

    in_specs = [pl.BlockSpec((tm, d), lambda i: (i, 0)), _const_spec((1, d)), mod_spec(1), mod_spec(0)]
    args = [x2, g_pre.reshape(1, d), mod4, mod4]
    if latent:
        cos, sin, tiles_per_seq = tables
        in_specs += [pl.BlockSpec((tm, ML_QK_WIDTH), lambda i: (i % tiles_per_seq, 0))] * 2
        in_specs += [pl.BlockSpec((ML_QK_WIDTH, tm), lambda i: (0, i % tiles_per_seq))] * 2
        args += [cos, sin, cos.T, sin.T]
    in_specs += [_const_spec(bg_col.shape), _const_spec(bg_row.shape)]
    args += [bg_col, bg_row]
    for w in weights:
        in_specs.append(_const_spec(w.shape))
        args.append(w)

    def out(width, dtype=BF16):
        return (jax.ShapeDtypeStruct((n, width), dtype), pl.BlockSpec((tm, width), lambda i: (i, 0)))

    def out_t(width, dtype=BF16):
        return (jax.ShapeDtypeStruct((width, n), dtype), pl.BlockSpec((width, tm), lambda i: (0, i)))

    if latent:
        outs = [out(NA_WIDTH), out(NA_WIDTH), out(NA_WIDTH), out(ML_QK_WIDTH), out_t(ML_QK_WIDTH),
                out(ML_WIDTH), out(ML_WIDTH), out(d), out(d), out(N_GATES, F32)]
    else:
        outs = [out(NA_WIDTH), out(NA_WIDTH), out_t(ML_QK_WIDTH), out(ML_WIDTH), out(N_GATES, F32)]
    outs.append(out_t(N_GATES, F32))
    return pl.pallas_call(
        functools.partial(_proj_kernel, latent=latent),
        grid=grid,
        in_specs=in_specs,
        out_specs=[o[1] for o in outs],
        out_shape=[o[0] for o in outs],
        compiler_params=_cparams("arbitrary"),
        name="in_proj_latent" if latent else "in_proj_ctx",
    )(*args)


def _na_window_start(r0, rows):
    return jnp.clip(r0 - WIN_H // 2, 0, rows - NA_WROWS)


def _na_classes(rows):
    keys, group_class = [], []
    for r0 in range(0, rows, NA_QROWS):
        start = min(max(r0 - WIN_H // 2, 0), rows - NA_WROWS)
        first = tuple(min(max(r0 + i - WIN_H // 2, 0), rows - WIN_H) - start for i in range(NA_QROWS))
        assert all(0 <= f and f + WIN_H <= NA_WROWS for f in first)
        key = (r0 - start, first)
        if key not in keys:
            keys.append(key)
        group_class.append(keys.index(key))
    return keys, np.asarray(group_class, np.int32)


def _na_bias_table(rpb, rows):
    keys, group_class = _na_classes(rows)
    qc = np.arange(GRID_W)[:, None]
    kc = np.arange(GRID_W)[None, :]
    cs = np.clip(qc - WIN_W // 2, 0, GRID_W - WIN_W)
    col_ok = (kc >= cs) & (kc < cs + WIN_W)
    dc = np.clip(kc - qc, -(WIN_W - 1), WIN_W - 1) + (WIN_W - 1)
    sel_c = (dc[:, :, None] == np.arange(2 * WIN_W - 1)).astype(np.float32)
    block = jnp.einsum('hrc,qkc->hrqk', rpb.astype(F32), sel_c, precision=HIGHEST)
    block = jnp.where(col_ok, block * LOG2_E, NEG_INF).astype(BF16)
    masked = jnp.full((NA_HEADS, GRID_W, GRID_W), NEG_INF, BF16)
    tables = []
    for off, first in keys:
        q_rows = []
        for i in range(NA_QROWS):
            in_window = [first[i] <= j < first[i] + WIN_H for j in range(NA_WROWS)]
            q_rows.append(jnp.concatenate(
                [block[:, j - off - i + (WIN_H - 1)] if in_window[j] else masked for j in range(NA_WROWS)],
                axis=-1))
        tables.append(jnp.concatenate(q_rows, axis=1))
    return jnp.stack(tables), jnp.asarray(group_class)


def _na_kernel(cls_ref, q_ref, k_ref, v_ref, kc_ref, vc_ref, bias_ref, o_ref, *, rows):
    del cls_ref
    r0 = pl.program_id(1) * NA_QROWS
    start = pl.multiple_of(_na_window_start(r0, rows) * GRID_W, GRID_W)
    n_win = NA_WROWS * GRID_W
    low = lax.broadcasted_iota(jnp.int32, (1, LANES), 1) < NA_HEAD_DIM
    for pair in range(NA_HEADS // 2):
        sl = slice(pair * LANES, (pair + 1) * LANES)
        q2 = q_ref[:, sl]
        k2 = k_ref[pl.ds(start, n_win), sl]
        v2 = v_ref[pl.ds(start, n_win), sl]
        kc2 = kc_ref[:, sl]
        vc2 = vc_ref[:, sl]
        scores = []
        for half in range(2):
            keep = low if half == 0 else jnp.logical_not(low)
            qh = jnp.where(keep, q2, jnp.zeros_like(q2))
            s_loc = (lax.dot_general(qh, k2, NT_DIMS, preferred_element_type=F32)
                     + bias_ref[2 * pair + half].astype(F32))
            s_ctx = lax.dot_general(qh, kc2, NT_DIMS, preferred_element_type=F32)
            scores.append((s_loc, s_ctx))
        probs = []
        for s_loc, s_ctx in scores:
            m = jnp.maximum(jnp.max(s_loc, axis=-1, keepdims=True), jnp.max(s_ctx, axis=-1, keepdims=True))
            p_loc = jnp.exp2(s_loc - m)
            p_ctx = jnp.exp2(s_ctx - m)
            denom = jnp.sum(p_loc, axis=-1, keepdims=True) + jnp.sum(p_ctx, axis=-1, keepdims=True)
            probs.append((p_loc.astype(BF16), p_ctx.astype(BF16), denom))
        halves = []
        for p_loc, p_ctx, denom in probs:
            o = (jnp.dot(p_loc, v2, preferred_element_type=F32)
                 + jnp.dot(p_ctx, vc2, preferred_element_type=F32))
            halves.append(o / denom)
        o_ref[:, sl] = jnp.where(low, halves[0], halves[1]).astype(BF16)


def _neighbourhood_attention(q, k, v, kc, vc, rpb):
    b, s, w = q.shape
    rows = s // GRID_W
    n_ctx = kc.shape[1]
    bias, group_class = _na_bias_table(rpb, rows)
    nq = NA_QROWS * GRID_W
    grid_spec = pltpu.PrefetchScalarGridSpec(
        num_scalar_prefetch=1,
        grid=(b, rows // NA_QROWS),
        in_specs=[pl.BlockSpec((None, nq, w), lambda bi, g, cls: (bi, g, 0)),
                  pl.BlockSpec((None, s, w), lambda bi, g, cls: (bi, 0, 0)),
                  pl.BlockSpec((None, s, w), lambda bi, g, cls: (bi, 0, 0)),
                  pl.BlockSpec((None, n_ctx, w), lambda bi, g, cls: (bi, 0, 0)),
                  pl.BlockSpec((None, n_ctx, w), lambda bi, g, cls: (bi, 0, 0)),
                  pl.BlockSpec((None,) + bias.shape[1:], lambda bi, g, cls: (cls[g], 0, 0, 0))],
        out_specs=pl.BlockSpec((None, nq, w), lambda bi, g, cls: (bi, g, 0)),
    )
    return pl.pallas_call(
        functools.partial(_na_kernel, rows=rows),
        grid_spec=grid_spec,
        out_shape=jax.ShapeDtypeStruct((b, s, w), BF16),
        compiler_params=_cparams("arbitrary", "arbitrary"),
        name="na_attention",
    )(group_class, q, k, v, kc, vc, bias)


def _split3(x):
    hi = x.astype(BF16)
    r1 = x - hi.astype(F32)
    mid = r1.astype(BF16)
    lo = (r1 - mid.astype(F32)).astype(BF16)
    return hi, mid, lo


def _mlstm_direction_step(reverse, is_ctx, q, kt, v, gc, gr, o_ref, c_scr, m_scr):
    li_base = 2 * ML_HEADS if reverse else 0
    lf_base = li_base + ML_HEADS
    length = kt.shape[1]
    t = lax.broadcasted_iota(jnp.int32, (length, length), 0)
    s = lax.broadcasted_iota(jnp.int32, (length, length), 1)
    valid = (s >= t) if reverse else (s <= t)
    valid_t = (t >= s) if reverse else (t <= s)
    b_col = sum(jnp.dot(valid.astype(BF16), p, preferred_element_type=F32) for p in _split3(gc))
    b_row = sum(jnp.dot(p, valid_t.astype(BF16), preferred_element_type=F32) for p in _split3(gr))
    low = lax.broadcasted_iota(jnp.int32, (1, LANES), 1) < ML_QK_DIM
    ones = jnp.ones((length, ML_V_DIM), BF16)
    heads = range(ML_HEADS)
    li_r = [gr[li_base + h:li_base + h + 1, :] for h in heads]
    lf_r = [gr[lf_base + h:lf_base + h + 1, :] for h in heads]
    br = [b_row[lf_base + h:lf_base + h + 1, :] for h in heads]
    m_prev = [m_scr[h] for h in heads]
    v_ext = [jnp.concatenate([v[:, h * ML_V_DIM:(h + 1) * ML_V_DIM], ones], axis=-1) for h in heads]
    kt_pair = [kt[(h // 2) * LANES:(h // 2 + 1) * LANES, :] for h in heads]

    yield
    if not is_ctx:
        weights = []
        for h in heads:
            bc = b_col[:, lf_base + h:lf_base + h + 1]
            d_mat = jnp.where(valid, bc - br[h] + li_r[h], NEG_INF)
            m_inter = bc + m_prev[h]
            m_t = jnp.maximum(m_inter, jnp.max(d_mat, axis=-1, keepdims=True))
            weights.append((jnp.exp(d_mat - m_t), jnp.exp(m_inter - m_t), jnp.exp(-m_t)))
        yield
        for h in heads:
            w_intra, w_inter, floor = weights[h]
            q2 = q[:, (h // 2) * LANES:(h // 2 + 1) * LANES]
            qh = jnp.where(low if h % 2 == 0 else jnp.logical_not(low), q2, jnp.zeros_like(q2))
            sc = jnp.dot(qh, kt_pair[h], preferred_element_type=F32) * w_intra
            both = (w_inter * jnp.dot(qh, c_scr[h // 2].astype(BF16), preferred_element_type=F32)
                    + jnp.dot(sc.astype(BF16), v_ext[h], preferred_element_type=F32))
            num = both[:, :ML_V_DIM]
            den = both[:, ML_V_DIM:]
            o_ref[:, h * ML_V_DIM:(h + 1) * ML_V_DIM] = (num / jnp.maximum(jnp.abs(den), floor)).astype(BF16)

    yield
    for h in heads:
        rows = slice((h % 2) * ML_QK_DIM, (h % 2 + 1) * ML_QK_DIM)
        total = jnp.sum(lf_r[h], axis=-1, keepdims=True)
        m_new = jnp.maximum(total + m_prev[h], jnp.max(total - br[h] + li_r[h], axis=-1, keepdims=True))
        w_src = jnp.exp(total - br[h] + li_r[h] - m_new)
        w_carry = jnp.exp(total + m_prev[h] - m_new)
        kw = (kt_pair[h][rows, :].astype(F32) * w_src).astype(BF16)
        c_scr[h // 2, rows, :] = (w_carry * c_scr[h // 2, rows, :]
                                  + jnp.dot(kw, v_ext[h], preferred_element_type=F32))
        m_scr[h] = m_new


def _alternate(*phased):
    live = list(phased)
    while live:
        live = [g for g in live if next(g, StopIteration) is not StopIteration]


def _mlstm_kernel(qf_ref, kf_ref, vf_ref, gcf_ref, grf_ref, qb_ref, kb_ref, vb_ref, gcb_ref, grb_ref,
                  kc_ref, vc_ref, gcc_ref, gcr_ref, of_ref, ob_ref, cf_scr, mf_scr, cb_scr, mb_scr):
    step = pl.program_id(1)

    @pl.when(step == 0)
    def _():
        for ref in (cf_scr, mf_scr, cb_scr, mb_scr):
            ref[...] = jnp.zeros_like(ref)
        kt = kc_ref[...]
        v = vc_ref[...]
        gc = gcc_ref[...]
        gr = gcr_ref[...]
        _alternate(_mlstm_direction_step(False, True, None, kt, v, gc, gr, None, cf_scr, mf_scr),
                   _mlstm_direction_step(True, True, None, kt, v, gc, gr, None, cb_scr, mb_scr))

    @pl.when(step > 0)
    def _():
        _alternate(_mlstm_direction_step(False, False, qf_ref[...], kf_ref[...], vf_ref[...], gcf_ref[...],
                                         grf_ref[...], of_ref, cf_scr, mf_scr),
                   _mlstm_direction_step(True, False, qb_ref[...], kb_ref[...], vb_ref[...], gcb_ref[...],
                                         grb_ref[...], ob_ref, cb_scr, mb_scr))


def _mlstm(q, k, v, gc, gr, kc, vc, gcc, gcr):
    b, s, _ = q.shape
    n_ctx = vc.shape[1]
    length = ML_CHUNK
    n_chunks = s // length

    def chunk(step, reverse):
        c = jnp.maximum(step - 1, 0)
        return (n_chunks - 1 - c) if reverse else c

    def stream_specs(reverse):
        def seq(width):
            return pl.BlockSpec((None, length, width), lambda bi, st: (bi, chunk(st, reverse), 0))

        def seq_t(width):
            return pl.BlockSpec((width, length), lambda bi, st: (0, bi * n_chunks + chunk(st, reverse)))
        return [seq(ML_QK_WIDTH), seq_t(ML_QK_WIDTH), seq(ML_WIDTH), seq(N_GATES), seq_t(N_GATES)]

    def out_spec(reverse):
        return pl.BlockSpec((None, length, ML_WIDTH), lambda bi, st: (bi, chunk(st, reverse), 0))

    state = [pltpu.VMEM((ML_HEADS // 2, 2 * ML_QK_DIM, 2 * ML_V_DIM), F32),
             pltpu.VMEM((ML_HEADS, 1, 1), F32)]
    return pl.pallas_call(
        _mlstm_kernel,
        grid=(b, n_chunks + 1),
        in_specs=stream_specs(False) + stream_specs(True) + [
            pl.BlockSpec((ML_QK_WIDTH, n_ctx), lambda bi, st: (0, bi)),
            pl.BlockSpec((None, n_ctx, ML_WIDTH), lambda bi, st: (bi, 0, 0)),
            pl.BlockSpec((None, n_ctx, N_GATES), lambda bi, st: (bi, 0, 0)),
            pl.BlockSpec((N_GATES, n_ctx), lambda bi, st: (0, bi))],
        out_specs=[out_spec(False), out_spec(True)],
        out_shape=[jax.ShapeDtypeStruct((b, s, ML_WIDTH), BF16)] * 2,
        scratch_shapes=state + state,
        compiler_params=_cparams("arbitrary", "arbitrary"),
        name="mlstm",
    )(q, k, v, gc, gr, q, k, v, gc, gr, kc, vc, gcc, gcr)


def _merge_kernel(x_ref, ona_ref, hf_ref, hb_ref, opre_ref, gna_ref, gml_ref, gtm_ref, scf_ref, shf_ref,
                  ghead_ref, wbna_ref, wbml_ref, wout_ref, gpost_ref, gpre_ref, wrh_ref, wrl_ref, br_ref,
                  x1_ref, h2_ref, topw_ref, route_ref, cnt_ref):
    step = pl.program_id(0)
    tm = x_ref.shape[0]

    @pl.when(step == 0)
    def _():
        cnt_ref[...] = jnp.zeros_like(cnt_ref)

    hsum = hf_ref[...].astype(F32) + hb_ref[...].astype(F32)
    heads = [_rms(hsum[:, h * ML_V_DIM:(h + 1) * ML_V_DIM]) for h in range(ML_HEADS)]
    hn = jnp.concatenate(heads, axis=-1) * ghead_ref[...]
    o_ml = jax.nn.sigmoid(opre_ref[...].astype(F32)) * hn
    merged = (jax.nn.sigmoid(gna_ref[...].astype(F32))
              * jnp.dot(ona_ref[...], wbna_ref[...], preferred_element_type=F32)
              + jax.nn.sigmoid(gml_ref[...].astype(F32))
              * jnp.dot(o_ml.astype(BF16), wbml_ref[...], preferred_element_type=F32))
    mixed = jnp.dot(merged.astype(BF16), wout_ref[...], preferred_element_type=F32)
    x1 = x_ref[...] + gtm_ref[...] * (_rms(mixed) * gpost_ref[...])
    x1_ref[...] = x1
    h2 = _rms(x1) * gpre_ref[...] * (1.0 + scf_ref[...]) + shf_ref[...]
    h2_ref[...] = h2
    h2_hi = h2.astype(BF16)
    h2_lo = (h2 - h2_hi.astype(F32)).astype(BF16)
    logits = (jnp.dot(h2_hi, wrh_ref[...], preferred_element_type=F32)
              + (jnp.dot(h2_hi, wrl_ref[...], preferred_element_type=F32)
                 + jnp.dot(h2_lo, wrh_ref[...], preferred_element_type=F32))) + br_ref[...]

    lane = lax.broadcasted_iota(jnp.int32, logits.shape, 1)
    onehots, top_e, top_l = [], [], []
    for _ in range(TOP_K):
        best = jnp.max(logits, axis=-1, keepdims=True)
        e = jnp.min(jnp.where(logits == best, lane, N_EXPERTS), axis=-1, keepdims=True)
        hit = lane == e
        onehots.append(hit)
        top_e.append(e)
        top_l.append(best)
        logits = jnp.where(hit, -jnp.inf, logits)
    exps = [jnp.exp(l - top_l[0]) for l in top_l]
    total = exps[0] + exps[1] + exps[2] + exps[3]

    counts = (onehots[0].astype(F32) + onehots[1].astype(F32)
              + onehots[2].astype(F32) + onehots[3].astype(F32))
    t = lax.broadcasted_iota(jnp.int32, (tm, tm), 0)
    s = lax.broadcasted_iota(jnp.int32, (tm, tm), 1)
    before = jnp.dot((s < t).astype(BF16), counts.astype(BF16), preferred_element_type=F32) + cnt_ref[...]
    out_lane = lax.broadcasted_iota(jnp.int32, (tm, LANES), 1)
    w_out = jnp.zeros((tm, LANES), F32)
    route = jnp.zeros((tm, LANES), F32)
    for j in range(TOP_K):
        rank = jnp.sum(jnp.where(onehots[j], before, 0.0), axis=-1, keepdims=True)
        w_out = jnp.where(out_lane == j, exps[j] / total, w_out)
        route = jnp.where(out_lane == j, top_e[j].astype(F32), route)
        route = jnp.where(out_lane == TOP_K + j, rank, route)
    topw_ref[...] = w_out
    route_ref[...] = jnp.transpose(route)[:2 * TOP_K, :].astype(jnp.int32)
    cnt_ref[...] += jnp.sum(counts, axis=0, keepdims=True)


def _merge_and_route(x2, o_na, h_f, h_b, o_pre, g_na, g_ml, mod4, seq, g_head, wbna, wbml, wout,
                     g_post, g_pre, w_router, b_router):
    n, d = x2.shape
    tm = MERGE_ROWS
    per_seq = seq // tm

    def rows(width):
        return pl.BlockSpec((tm, width), lambda i: (i, 0))

    def mod_spec(j):
        return pl.BlockSpec((None, None, 1, d), lambda i, _j=j: (i // per_seq, _j, 0, 0))

    w_router_hi = w_router.astype(BF16)
    return pl.pallas_call(
        _merge_kernel,
        grid=(n // tm,),
        in_specs=[rows(d), rows(NA_WIDTH), rows(ML_WIDTH), rows(ML_WIDTH), rows(ML_WIDTH), rows(d), rows(d),
                  mod_spec(2), mod_spec(4), mod_spec(3),
                  _const_spec((1, ML_WIDTH)), _const_spec(wbna.shape), _const_spec(wbml.shape),
                  _const_spec(wout.shape), _const_spec((1, d)), _const_spec((1, d)),
                  _const_spec(w_router.shape), _const_spec(w_router.shape), _const_spec((1, N_EXPERTS))],
        out_specs=[rows(d), rows(d), rows(LANES), pl.BlockSpec((2 * TOP_K, tm), lambda i: (0, i)),
                   pl.BlockSpec((1, N_EXPERTS), lambda i: (0, 0))],
        out_shape=[jax.ShapeDtypeStruct((n, d), F32), jax.ShapeDtypeStruct((n, d), F32),
                   jax.ShapeDtypeStruct((n, LANES), F32), jax.ShapeDtypeStruct((2 * TOP_K, n), jnp.int32),
                   jax.ShapeDtypeStruct((1, N_EXPERTS), F32)],
        compiler_params=_cparams("arbitrary"),
        name="merge_route",
    )(x2, o_na, h_f, h_b, o_pre, g_na, g_ml, mod4, mod4, mod4,
      g_head.reshape(1, ML_WIDTH), wbna, wbml, wout, g_post.reshape(1, d), g_pre.reshape(1, d),
      w_router_hi, (w_router - w_router_hi.astype(F32)).astype(BF16), b_router.reshape(1, N_EXPERTS))


def _dispatch_kernel(pad_end_ref, padded_ref, dest_ref, h_ref, xs_ref, zero_scr, sem):
    tm = h_ref.shape[0]
    blk = zero_scr.shape[0]

    @pl.when(pl.program_id(0) == 0)
    def _():
        zero_scr[...] = jnp.zeros_like(zero_scr)

        def zero_copy(e):
            first = pl.multiple_of(pad_end_ref[e] - blk, blk)
            return pltpu.make_async_copy(zero_scr, xs_ref.at[pl.ds(first, blk)], sem)

        for e in range(N_EXPERTS):
            @pl.when(padded_ref[e] > 0)
            def _():
                zero_copy(e).start()
        for e in range(N_EXPERTS):
            @pl.when(padded_ref[e] > 0)
            def _():
                zero_copy(e).wait()

        def tail_copy(b):
            return pltpu.make_async_copy(zero_scr, xs_ref.at[pl.ds(pl.multiple_of(b * blk, blk), blk)], sem)

        def tail_start(b, carry):
            tail_copy(b).start()
            return carry

        def tail_wait(b, carry):
            tail_copy(b).wait()
            return carry

        first_unused = pad_end_ref[N_EXPERTS - 1] // blk
        lax.fori_loop(first_unused, xs_ref.shape[0] // blk, tail_start, 0)
        lax.fori_loop(first_unused, xs_ref.shape[0] // blk, tail_wait, 0)

    def row_copy(t, k):
        return pltpu.make_async_copy(h_ref.at[pl.ds(t, 1)], xs_ref.at[pl.ds(dest_ref[k, t], 1)], sem)

    for t in range(tm):
        for k in range(TOP_K):
            row_copy(t, k).start(priority=k % 2)
    for t in range(tm):
        for k in range(TOP_K):
            row_copy(t, k).wait()


def _dispatch(pad_end, padded, dest, h2, n_rows):
    n = h2.shape[0]
    tm = MOVE_ROWS
    row_tile = h2.shape[1:]
    grid_spec = pltpu.PrefetchScalarGridSpec(
        num_scalar_prefetch=2,
        grid=(n // tm,),
        in_specs=[pl.BlockSpec((TOP_K, tm), lambda i, pe, pd: (0, i), memory_space=pltpu.SMEM),
                  pl.BlockSpec((tm,) + row_tile, lambda i, pe, pd: (i, 0))],
        out_specs=pl.BlockSpec(memory_space=pl.ANY),
        scratch_shapes=[pltpu.VMEM((EXPERT_ROWS,) + row_tile, h2.dtype), pltpu.SemaphoreType.DMA(())],
    )
    return pl.pallas_call(
        _dispatch_kernel,
        grid_spec=grid_spec,
        out_shape=jax.ShapeDtypeStruct((n_rows,) + row_tile, h2.dtype),
        compiler_params=_cparams("arbitrary"),
        name="moe_dispatch",
    )(pad_end, padded, dest, h2)


def _expert_kernel(blk_e_ref, n_used_ref, x_ref, wg_ref, bg_ref, wl_ref, bl_ref, wd_ref, bd_ref, y_ref,
                   wg_s, wl_s, wd_s):
    i = pl.program_id(0)
    prev = blk_e_ref[jnp.maximum(i - 1, 0)]
    changed = jnp.logical_or(i == 0, blk_e_ref[i] != prev)
    used = i < n_used_ref[0]

    @pl.when(jnp.logical_and(used, changed))
    def _():
        wg_s[...] = wg_ref[...].astype(BF16)
        wl_s[...] = wl_ref[...].astype(BF16)
        wd_s[...] = wd_ref[...].astype(BF16)

    @pl.when(used)
    def _():
        xb = x_ref[...].astype(BF16)
        g = jnp.dot(xb, wg_s[...], preferred_element_type=F32) + bg_ref[...]
        l = jnp.dot(xb, wl_s[...], preferred_element_type=F32) + bl_ref[...]
        g = jnp.minimum(g, SWIGLU_LIMIT)
        l = jnp.clip(l, -SWIGLU_LIMIT, SWIGLU_LIMIT)
        a = g * jax.nn.sigmoid(SWIGLU_ALPHA * g) * (l + 1.0)
        y_ref[...] = jnp.dot(a.astype(BF16), wd_s[...], preferred_element_type=F32) + bd_ref[...]

    @pl.when(jnp.logical_not(used))
    def _():
        y_ref[...] = jnp.zeros_like(y_ref)


def _experts(blk_e, n_used, xs, w_gate, b_gate, w_lin, b_lin, w_down, b_down):
    n_rows = xs.shape[0]
    row_tile = xs.shape[1:]
    e, d, f = w_gate.shape
    tm = EXPERT_ROWS

    def w_spec(shape):
        return pl.BlockSpec((None,) + shape, lambda i, be, nu: (be[i], 0, 0))

    grid_spec = pltpu.PrefetchScalarGridSpec(
        num_scalar_prefetch=2,
        grid=(n_rows // tm,),
        in_specs=[pl.BlockSpec((tm,) + row_tile, lambda i, be, nu: (i, 0)),
                  w_spec((d, f)), w_spec((1, f)), w_spec((d, f)), w_spec((1, f)),
                  w_spec((f, d)), w_spec((1, d))],
        out_specs=pl.BlockSpec((tm,) + row_tile, lambda i, be, nu: (i, 0)),
        scratch_shapes=[pltpu.VMEM((d, f), BF16), pltpu.VMEM((d, f), BF16), pltpu.VMEM((f, d), BF16)],
    )
    return pl.pallas_call(
        _expert_kernel,
        grid_spec=grid_spec,
        out_shape=jax.ShapeDtypeStruct((n_rows,) + row_tile, F32),
        compiler_params=_cparams("arbitrary"),
        name="moe_experts",
    )(blk_e, n_used, xs, w_gate, b_gate.reshape(e, 1, f), w_lin, b_lin.reshape(e, 1, f),
      w_down, b_down.reshape(e, 1, d))


def _combine_kernel(dest_ref, dest_next_ref, x1_ref, w_ref, gtf_ref, gpost_ref, y_ref, o_ref,
                    buf_a, buf_b, sem_a, sem_b):
    step = pl.program_id(0)
    half = x1_ref.shape[0] // 2

    def row_copy(idx_ref, first_tok, buf, sem, t, k):
        return pltpu.make_async_copy(y_ref.at[pl.ds(idx_ref[k, first_tok + t], 1)],
                                     buf.at[k, pl.ds(t, 1)], sem)

    def each_copy(fn):
        for t in range(half):
            for k in range(TOP_K):
                fn(t, k)

    def start_all(idx_ref, first_tok, buf, sem):
        each_copy(lambda t, k: row_copy(idx_ref, first_tok, buf, sem, t, k).start(priority=k % 2))

    def wait_all(buf, sem):
        each_copy(lambda t, k: row_copy(dest_ref, 0, buf, sem, t, k).wait())

    def finish(buf, rows):
        w = w_ref[rows, :]
        ffn = (buf[0] * w[:, 0:1] + buf[1] * w[:, 1:2]) + (buf[2] * w[:, 2:3] + buf[3] * w[:, 3:4])
        o_ref[rows, :] = x1_ref[rows, :] + gtf_ref[...] * (_rms(ffn) * gpost_ref[...])

    @pl.when(step == 0)
    def _():
        start_all(dest_ref, 0, buf_a, sem_a)

    wait_all(buf_a, sem_a)
    start_all(dest_ref, half, buf_b, sem_b)
    finish(buf_a, slice(0, half))
    wait_all(buf_b, sem_b)
    start_all(dest_next_ref, 0, buf_a, sem_a)
    finish(buf_b, slice(half, 2 * half))

    @pl.when(step == pl.num_programs(0) - 1)
    def _():
        wait_all(buf_a, sem_a)


def _combine(dest, x1, top_w, mod4, seq, g_post, y):
    n, d = x1.shape
    tm = MOVE_ROWS
    steps = n // tm
    per_seq = seq // tm
    half_buf = pltpu.VMEM((TOP_K, tm // 2, d), F32)
    return pl.pallas_call(
        _combine_kernel,
        grid=(steps,),
        in_specs=[pl.BlockSpec((TOP_K, tm), lambda i: (0, i), memory_space=pltpu.SMEM),
                  pl.BlockSpec((TOP_K, tm), lambda i: (0, jnp.minimum(i + 1, steps - 1)),
                               memory_space=pltpu.SMEM),
                  pl.BlockSpec((tm, d), lambda i: (i, 0)),
                  pl.BlockSpec((tm, LANES), lambda i: (i, 0)),
                  pl.BlockSpec((None, None, 1, d), lambda i: (i // per_seq, 5, 0, 0)),
                  _const_spec((1, d)),
                  pl.BlockSpec(memory_space=pl.ANY)],
        out_specs=pl.BlockSpec((tm, d), lambda i: (i, 0)),
        out_shape=jax.ShapeDtypeStruct((n, d), F32),
        scratch_shapes=[half_buf, half_buf, pltpu.SemaphoreType.DMA(()), pltpu.SemaphoreType.DMA(())],
        compiler_params=_cparams("arbitrary"),
        name="moe_combine",
    )(dest, dest, x1, top_w, mod4, g_post.reshape(1, d), y)


def _rope_partner(w):
    n_freq = ML_QK_DIM // 4
    d = w.shape[0]
    w4 = w.reshape(d, -1, 2, n_freq)
    return jnp.stack([-w4[:, :, 1], w4[:, :, 0]], axis=2).reshape(w.shape)


def _rope_tables(seq):
    n_freq = ML_QK_DIM // 4
    t = jnp.arange(seq)
    row = (t // GRID_W).astype(F32)
    col = (t % GRID_W).astype(F32)
    inv_freq = ROPE_BASE ** (-jnp.arange(n_freq, dtype=F32) / n_freq)
    ang = jnp.concatenate([row[:, None] * inv_freq] * 2 + [col[:, None] * inv_freq] * 2, axis=-1)
    cos = jnp.tile(jnp.cos(ang), (1, ML_HEADS))
    sin = jnp.tile(jnp.sin(ang), (1, ML_HEADS))
    return cos, sin


def _layer(x, ctx, mod4, g_mix_pre, g_mix_post, g_ffn_pre, g_ffn_post, w_in, b_gates, rpb, g_head,
           w_branch_na, w_branch_ml, w_out, w_router, b_router, w_gate, b_gate, w_lin, b_lin, w_down, b_down):
    b, s, d = x.shape
    n = b * s
    n_ctx = ctx.shape[1]
    x2 = x.reshape(n, d)

    ctx_cols = (NA_WIDTH, NA_WIDTH, ML_QK_WIDTH, ML_WIDTH, N_GATES)
    lat_cols = (NA_WIDTH, ML_QK_WIDTH, ML_WIDTH, d, d)
    bounds = np.cumsum(ctx_cols + lat_cols)[:-1].tolist()
    (w_nak, w_nav, w_mk, w_mv, w_g, w_naq, w_mq, w_mo, w_gna, w_gml) = jnp.split(w_in, bounds, axis=-1)
    w_naq = w_naq * (NA_HEAD_DIM ** -0.5 * LOG2_E)
    w_mk = w_mk * ML_QK_DIM ** -0.5
    bf = lambda a: a.astype(BF16)
    lat_w = [bf(w_naq), bf(w_nak), bf(w_nav), bf(w_mq), bf(_rope_partner(w_mq)), bf(w_mk.T),
             bf(_rope_partner(w_mk).T), bf(w_mv), bf(w_mo), bf(w_gna), bf(w_gml), bf(w_g), bf(w_g.T)]
    ctx_w = [bf(w_nak), bf(w_nav), bf(w_mk.T), bf(w_mv), bf(w_g), bf(w_g.T)]
    bg_col = b_gates.reshape(1, N_GATES).astype(F32)
    bg_row = b_gates.reshape(N_GATES, 1).astype(F32)
    cos, sin = _rope_tables(s)
    per_seq = s // PROJ_ROWS

    (na_q, na_k, na_v, ml_q, ml_k, ml_v, ml_o, gate_na, gate_ml, g_col, g_row) = _project(
        x2, mod4, lambda i: i // per_seq, g_mix_pre, bg_col, bg_row, lat_w, (cos, sin, per_seq), True)
    (na_kc, na_vc, ml_kc, ml_vc, gc_col, gc_row) = _project(
        ctx.reshape(b * n_ctx, d), mod4, lambda i: b, g_mix_pre, bg_col, bg_row, ctx_w, None, False)

    def seq3(a, length):
        return a.reshape(b, length, a.shape[-1])

    o_na = _neighbourhood_attention(seq3(na_q, s), seq3(na_k, s), seq3(na_v, s),
                                    seq3(na_kc, n_ctx), seq3(na_vc, n_ctx), rpb)
    ml_args = (seq3(ml_q, s), ml_k, seq3(ml_v, s), seq3(g_col, s), g_row,
               ml_kc, seq3(ml_vc, n_ctx), seq3(gc_col, n_ctx), gc_row)
    h_f, h_b = _mlstm(*ml_args)

    x1, h2, top_w, route, counts = _merge_and_route(
        x2, o_na.reshape(n, NA_WIDTH), h_f.reshape(n, ML_WIDTH), h_b.reshape(n, ML_WIDTH), ml_o,
        gate_na, gate_ml, mod4, s, g_head, bf(w_branch_na), bf(w_branch_ml), bf(w_out),
        g_mix_post, g_ffn_pre, w_router, b_router)

    tm = EXPERT_ROWS
    counts = counts.reshape(N_EXPERTS).astype(jnp.int32)
    padded = (counts + tm - 1) // tm * tm
    pad_end = jnp.cumsum(padded)
    pad_start = pad_end - padded
    n_rows = n * TOP_K + N_EXPERTS * tm
    n_blocks = n_rows // tm
    top_e, rank = route[:TOP_K], route[TOP_K:]
    e_sel = top_e[None] == jnp.arange(N_EXPERTS, dtype=jnp.int32)[:, None, None]
    dest = (jnp.sum(jnp.where(e_sel, pad_start[:, None, None], 0), axis=0) + rank).astype(jnp.int32)
    blk_start = jnp.arange(n_blocks, dtype=jnp.int32) * tm
    blk_e = jnp.minimum(jnp.sum(blk_start[:, None] >= pad_end[None, :], axis=1), N_EXPERTS - 1).astype(jnp.int32)
    n_used = (pad_end[-1:] // tm).astype(jnp.int32)

    xs = _dispatch(pad_end.astype(jnp.int32), padded.astype(jnp.int32), dest, h2, n_rows)
    y = _experts(blk_e, n_used, xs, w_gate, b_gate, w_lin, b_lin, w_down, b_down)
    out = _combine(dest, x1, top_w, mod4, s, g_ffn_post, y)
    return out.reshape(b, s, d)


def kernel(x, c, ctx, c_ctx, w_ada, b_ada, g_mix_pre, g_mix_post, g_ffn_pre, g_ffn_post, w_in, b_mlstm_gates,
           rpb, g_mlstm_head, w_branch_na, w_branch_ml, w_out, w_router, b_router, w_gate, b_gate, w_lin,
           b_lin, w_down, b_down):
    b, s, d = x.shape
    depth = w_ada.shape[0]
    pad = (-(b + 1)) % 8
    c_all = jnp.concatenate([c, c_ctx[None, :], jnp.zeros((pad, d), c.dtype)], axis=0)
    for layer in range(depth):
        mod = _ada(c_all, w_ada[layer], b_ada[layer])
        mod4 = mod.reshape(mod.shape[0], 6, 1, d)
        x = _layer(x, ctx, mod4, g_mix_pre[layer], g_mix_post[layer], g_ffn_pre[layer], g_ffn_post[layer],
                   w_in[layer], b_mlstm_gates[layer], rpb[layer], g_mlstm_head[layer], w_branch_na[layer],
                   w_branch_ml[layer], w_out[layer], w_router[layer], b_router[layer], w_gate[layer],
                   b_gate[layer], w_lin[layer], b_lin[layer], w_down[layer], b_down[layer])
    return x
```

```python
import functools

import numpy as np
import jax
import jax.numpy as jnp
from jax import lax
from jax.experimental import pallas as pl
from jax.experimental.pallas import tpu as pltpu

F32 = jnp.float32
BF16 = jnp.bfloat16
HIGHEST = lax.Precision.HIGHEST

GRID_W = 64
NA_HEADS = 8
NA_HEAD_DIM = 64
NA_WIDTH = NA_HEADS * NA_HEAD_DIM
WIN_H = 8
WIN_W = 16
ML_HEADS = 4
ML_QK_DIM = 64
ML_V_DIM = 128
ML_QK_WIDTH = ML_HEADS * ML_QK_DIM
ML_WIDTH = ML_HEADS * ML_V_DIM
N_GATES = 4 * ML_HEADS
GATE_SOFTCAP = 15.0
ROPE_BASE = 10000.0
N_EXPERTS = 32
TOP_K = 4
SWIGLU_ALPHA = 1.702
SWIGLU_LIMIT = 7.0
NORM_EPS = 1e-6
NEG_INF = -1e30
LOG2_E = 1.4426950408889634

LANES = 128
NA_QROWS = 4
NA_WROWS = 12
ML_CHUNK = 256
PROJ_ROWS = 512
MERGE_ROWS = 512
EXPERT_ROWS = 512
MOVE_ROWS = 512
VMEM_LIMIT = 56 * 1024 * 1024

NT_DIMS = (((1,), (1,)), ((), ()))


def _cparams(*sem):
    return pltpu.CompilerParams(dimension_semantics=sem, vmem_limit_bytes=VMEM_LIMIT)


def _rms(x):
    return x * lax.rsqrt(jnp.mean(x * x, axis=-1, keepdims=True) + NORM_EPS)


def _ada_kernel(c_ref, w_ref, b_ref, o_ref):
    c = c_ref[...]
    s = c * jax.nn.sigmoid(c)
    o_ref[...] = jnp.dot(s, w_ref[...], preferred_element_type=F32, precision=HIGHEST) + b_ref[...]


def _ada(c_all, w_ada, b_ada):
    rows, d = c_all.shape
    n_out = w_ada.shape[1]
    tn = 1536
    return pl.pallas_call(
        _ada_kernel,
        grid=(n_out // tn,),
        in_specs=[pl.BlockSpec((rows, d), lambda j: (0, 0)),
                  pl.BlockSpec((d, tn), lambda j: (0, j)),
                  pl.BlockSpec((1, tn), lambda j: (0, j))],
        out_specs=pl.BlockSpec((rows, tn), lambda j: (0, j)),
        out_shape=jax.ShapeDtypeStruct((rows, n_out), F32),
        compiler_params=_cparams("arbitrary"),
        name="ada_mod",
    )(c_all, w_ada, b_ada.reshape(1, n_out))


def _gate_logs(g, is_forget):
    g = GATE_SOFTCAP * jnp.tanh(g / GATE_SOFTCAP)
    log_sig = jnp.minimum(g, 0.0) - jnp.log(1.0 + jnp.exp(-jnp.abs(g)))
    return jnp.where(is_forget, log_sig, g)


def _proj_kernel(*refs, latent):
    if latent:
        (x_ref, g_ref, sc_ref, sh_ref, cos_ref, sin_ref, cos_t_ref, sin_t_ref, bgc_ref, bgr_ref,
         w_naq, w_nak, w_nav, w_mq, w_mk, w_mv, w_mo, w_gna, w_gml, w_gc, w_gr,
         o_naq, o_nak, o_nav, o_mq, o_mk, o_mv, o_mo, o_gna, o_gml, o_gc, o_gr) = refs
    else:
        (x_ref, g_ref, sc_ref, sh_ref, bgc_ref, bgr_ref,
         w_nak, w_nav, w_mk, w_mv, w_gc, w_gr,
         o_nak, o_nav, o_mk, o_mv, o_gc, o_gr) = refs
    x = x_ref[...]
    h = _rms(x) * g_ref[...]
    h = h * (1.0 + sc_ref[...]) + sh_ref[...]
    hb = h.astype(BF16)

    def mm(w_ref):
        return jnp.dot(hb, w_ref[...], preferred_element_type=F32)

    def mm_t(w_ref):
        return lax.dot_general(w_ref[...], hb, NT_DIMS, preferred_element_type=F32)

    o_nak[...] = mm(w_nak).astype(BF16)
    o_nav[...] = mm(w_nav).astype(BF16)
    o_mv[...] = mm(w_mv).astype(BF16)
    if latent:
        cos = cos_ref[...]
        sin = sin_ref[...]
        o_naq[...] = mm(w_naq).astype(BF16)
        n_freq = ML_QK_DIM // 4
        mq = mm(w_mq)
        lane = lax.broadcasted_iota(jnp.int32, (1, ML_QK_WIDTH), 1)
        mq_partner = jnp.where(lane % (2 * n_freq) < n_freq,
                               pltpu.roll(mq, ML_QK_WIDTH - n_freq, 1), pltpu.roll(mq, n_freq, 1))
        o_mq[...] = (mq * cos + mq_partner * sin).astype(BF16)
        mk = mm_t(w_mk)
        mk_partner = jnp.concatenate(
            [mk[b + (n_freq if half == 0 else 0):b + (2 * n_freq if half == 0 else n_freq), :]
             for b in range(0, ML_QK_WIDTH, 2 * n_freq) for half in range(2)], axis=0)
        o_mk[...] = (mk * cos_t_ref[...] + mk_partner * sin_t_ref[...]).astype(BF16)
        o_mo[...] = mm(w_mo).astype(BF16)
        o_gna[...] = mm(w_gna).astype(BF16)
        o_gml[...] = mm(w_gml).astype(BF16)
    else:
        o_mk[...] = mm_t(w_mk).astype(BF16)
    gc = mm(w_gc) + bgc_ref[...]
    col_id = lax.broadcasted_iota(jnp.int32, gc.shape, 1)
    o_gc[...] = _gate_logs(gc, (col_id // ML_HEADS) % 2 == 1)
    gr = mm_t(w_gr) + bgr_ref[...]
    row_id = lax.broadcasted_iota(jnp.int32, gr.shape, 0)
    o_gr[...] = _gate_logs(gr, (row_id // ML_HEADS) % 2 == 1)


def _const_spec(shape):
    nd = len(shape)
    return pl.BlockSpec(shape, lambda i, _nd=nd: (0,) * _nd)


def _project(x2, mod4, mod_row_fn, g_pre, bg_col, bg_row, weights, tables, latent):
    n, d = x2.shape
    tm = PROJ_ROWS
    grid = (n // tm,)

    def mod_spec(j):
        return pl.BlockSpec((None, None, 1, d), lambda i, _j=j: (mod_row_fn(i), _j, 0, 0))

    in_specs = [pl.BlockSpec((tm, d), lambda i: (i, 0)), _const_spec((1, d)), mod_spec(1), mod_spec(0)]
    args = [x2, g_pre.reshape(1, d), mod4, mod4]
    if latent:
        cos, sin, tiles_per_seq = tables
        in_specs += [pl.BlockSpec((tm, ML_QK_WIDTH), lambda i: (i % tiles_per_seq, 0))] * 2
        in_specs += [pl.BlockSpec((ML_QK_WIDTH, tm), lambda i: (0, i % tiles_per_seq))] * 2
        args += [cos, sin, cos.T, sin.T]
    in_specs += [_const_spec(bg_col.shape), _const_spec(bg_row.shape)]
    args += [bg_col, bg_row]
    for w in weights:
        in_specs.append(_const_spec(w.shape))
        args.append(w)

    def out(width, dtype=BF16):
        return (jax.ShapeDtypeStruct((n, width), dtype), pl.BlockSpec((tm, width), lambda i: (i, 0)))

    def out_t(width, dtype=BF16):
        return (jax.ShapeDtypeStruct((width, n), dtype), pl.BlockSpec((width, tm), lambda i: (0, i)))

    if latent:
        outs = [out(NA_WIDTH), out(NA_WIDTH), out(NA_WIDTH), out(ML_QK_WIDTH), out_t(ML_QK_WIDTH),
                out(ML_WIDTH), out(ML_WIDTH), out(d), out(d), out(N_GATES, F32)]
    else:
        outs = [out(NA_WIDTH), out(NA_WIDTH), out_t(ML_QK_WIDTH), out(ML_WIDTH), out(N_GATES, F32)]
    outs.append(out_t(N_GATES, F32))
    return pl.pallas_call(
        functools.partial(_proj_kernel, latent=latent),
        grid=grid,
        in_specs=in_specs,
        out_specs=[o[1] for o in outs],
        out_shape=[o[0] for o in outs],
        compiler_params=_cparams("arbitrary"),
        name="in_proj_latent" if latent else "in_proj_ctx",
    )(*args)


def _na_window_start(r0, rows):
    return jnp.clip(r0 - WIN_H // 2, 0, rows - NA_WROWS)


def _na_classes(rows):
    keys, group_class = [], []
    for r0 in range(0, rows, NA_QROWS):
        start = min(max(r0 - WIN_H // 2, 0), rows - NA_WROWS)
        first = tuple(min(max(r0 + i - WIN_H // 2, 0), rows - WIN_H) - start for i in range(NA_QROWS))
        assert all(0 <= f and f + WIN_H <= NA_WROWS for f in first)
        key = (r0 - start, first)
        if key not in keys:
            keys.append(key)
        group_class.append(keys.index(key))
    return keys, np.asarray(group_class, np.int32)


def _na_bias_table(rpb, rows):
    keys, group_class = _na_classes(rows)
    qc = np.arange(GRID_W)[:, None]
    kc = np.arange(GRID_W)[None, :]
    cs = np.clip(qc - WIN_W // 2, 0, GRID_W - WIN_W)
    col_ok = (kc >= cs) & (kc < cs + WIN_W)
    dc = np.clip(kc - qc, -(WIN_W - 1), WIN_W - 1) + (WIN_W - 1)
    sel_c = (dc[:, :, None] == np.arange(2 * WIN_W - 1)).astype(np.float32)
    block = jnp.einsum('hrc,qkc->hrqk', rpb.astype(F32), sel_c, precision=HIGHEST)
    block = jnp.where(col_ok, block * LOG2_E, NEG_INF).astype(BF16)
    masked = jnp.full((NA_HEADS, GRID_W, GRID_W), NEG_INF, BF16)
    tables = []
    for off, first in keys:
        q_rows = []
        for i in range(NA_QROWS):
            in_window = [first[i] <= j < first[i] + WIN_H for j in range(NA_WROWS)]
            q_rows.append(jnp.concatenate(
                [block[:, j - off - i + (WIN_H - 1)] if in_window[j] else masked for j in range(NA_WROWS)],
                axis=-1))
        tables.append(jnp.concatenate(q_rows, axis=1))
    return jnp.stack(tables), jnp.asarray(group_class)


def _na_kernel(cls_ref, q_ref, k_ref, v_ref, kc_ref, vc_ref, bias_ref, o_ref, *, rows):
    del cls_ref
    r0 = pl.program_id(1) * NA_QROWS
    start = pl.multiple_of(_na_window_start(r0, rows) * GRID_W, GRID_W)
    n_win = NA_WROWS * GRID_W
    low = lax.broadcasted_iota(jnp.int32, (1, LANES), 1) < NA_HEAD_DIM
    for pair in range(NA_HEADS // 2):
        sl = slice(pair * LANES, (pair + 1) * LANES)
        q2 = q_ref[:, sl]
        k2 = k_ref[pl.ds(start, n_win), sl]
        v2 = v_ref[pl.ds(start, n_win), sl]
        kc2 = kc_ref[:, sl]
        vc2 = vc_ref[:, sl]
        scores = []
        for half in range(2):
            keep = low if half == 0 else jnp.logical_not(low)
            qh = jnp.where(keep, q2, jnp.zeros_like(q2))
            s_loc = (lax.dot_general(qh, k2, NT_DIMS, preferred_element_type=F32)
                     + bias_ref[2 * pair + half].astype(F32))
            s_ctx = lax.dot_general(qh, kc2, NT_DIMS, preferred_element_type=F32)
            scores.append((s_loc, s_ctx))
        probs = []
        for s_loc, s_ctx in scores:
            m = jnp.maximum(jnp.max(s_loc, axis=-1, keepdims=True), jnp.max(s_ctx, axis=-1, keepdims=True))
            p_loc = jnp.exp2(s_loc - m)
            p_ctx = jnp.exp2(s_ctx - m)
            denom = jnp.sum(p_loc, axis=-1, keepdims=True) + jnp.sum(p_ctx, axis=-1, keepdims=True)
            probs.append((p_loc.astype(BF16), p_ctx.astype(BF16), denom))
        halves = []
        for p_loc, p_ctx, denom in probs:
            o = (jnp.dot(p_loc, v2, preferred_element_type=F32)
                 + jnp.dot(p_ctx, vc2, preferred_element_type=F32))
            halves.append(o / denom)
        o_ref[:, sl] = jnp.where(low, halves[0], halves[1]).astype(BF16)


def _neighbourhood_attention(q, k, v, kc, vc, rpb):
    b, s, w = q.shape
    rows = s // GRID_W
    n_ctx = kc.shape[1]
    bias, group_class = _na_bias_table(rpb, rows)
    nq = NA_QROWS * GRID_W
    grid_spec = pltpu.PrefetchScalarGridSpec(
        num_scalar_prefetch=1,
        grid=(b, rows // NA_QROWS),
        in_specs=[pl.BlockSpec((None, nq, w), lambda bi, g, cls: (bi, g, 0)),
                  pl.BlockSpec((None, s, w), lambda bi, g, cls: (bi, 0, 0)),
                  pl.BlockSpec((None, s, w), lambda bi, g, cls: (bi, 0, 0)),
                  pl.BlockSpec((None, n_ctx, w), lambda bi, g, cls: (bi, 0, 0)),
                  pl.BlockSpec((None, n_ctx, w), lambda bi, g, cls: (bi, 0, 0)),
                  pl.BlockSpec((None,) + bias.shape[1:], lambda bi, g, cls: (cls[g], 0, 0, 0))],
        out_specs=pl.BlockSpec((None, nq, w), lambda bi, g, cls: (bi, g, 0)),
    )
    return pl.pallas_call(
        functools.partial(_na_kernel, rows=rows),
        grid_spec=grid_spec,
        out_shape=jax.ShapeDtypeStruct((b, s, w), BF16),
        compiler_params=_cparams("arbitrary", "arbitrary"),
        name="na_attention",
    )(group_class, q, k, v, kc, vc, bias)


def _split3(x):
    hi = x.astype(BF16)
    r1 = x - hi.astype(F32)
    mid = r1.astype(BF16)
    lo = (r1 - mid.astype(F32)).astype(BF16)
    return hi, mid, lo


def _mlstm_direction_step(reverse, is_ctx, q, kt, v, gc, gr, o_ref, c_scr, m_scr):
    li_base = 2 * ML_HEADS if reverse else 0
    lf_base = li_base + ML_HEADS
    length = kt.shape[1]
    t = lax.broadcasted_iota(jnp.int32, (length, length), 0)
    s = lax.broadcasted_iota(jnp.int32, (length, length), 1)
    valid = (s >= t) if reverse else (s <= t)
    valid_t = (t >= s) if reverse else (t <= s)
    b_col = sum(jnp.dot(valid.astype(BF16), p, preferred_element_type=F32) for p in _split3(gc))
    b_row = sum(jnp.dot(p, valid_t.astype(BF16), preferred_element_type=F32) for p in _split3(gr))
    low = lax.broadcasted_iota(jnp.int32, (1, LANES), 1) < ML_QK_DIM
    ones = jnp.ones((length, ML_V_DIM), BF16)
    heads = range(ML_HEADS)
    li_r = [gr[li_base + h:li_base + h + 1, :] for h in heads]
    lf_r = [gr[lf_base + h:lf_base + h + 1, :] for h in heads]
    br = [b_row[lf_base + h:lf_base + h + 1, :] for h in heads]
    m_prev = [m_scr[h] for h in heads]
    v_ext = [jnp.concatenate([v[:, h * ML_V_DIM:(h + 1) * ML_V_DIM], ones], axis=-1) for h in heads]
    kt_pair = [kt[(h // 2) * LANES:(h // 2 + 1) * LANES, :] for h in heads]

    yield
    if not is_ctx:
        weights = []
        for h in heads:
            bc = b_col[:, lf_base + h:lf_base + h + 1]
            d_mat = jnp.where(valid, bc - br[h] + li_r[h], NEG_INF)
            m_inter = bc + m_prev[h]
            m_t = jnp.maximum(m_inter, jnp.max(d_mat, axis=-1, keepdims=True))
            weights.append((jnp.exp(d_mat - m_t), jnp.exp(m_inter - m_t), jnp.exp(-m_t)))
        yield
        for h in heads:
            w_intra, w_inter, floor = weights[h]
            q2 = q[:, (h // 2) * LANES:(h // 2 + 1) * LANES]
            qh = jnp.where(low if h % 2 == 0 else jnp.logical_not(low), q2, jnp.zeros_like(q2))
            sc = jnp.dot(qh, kt_pair[h], preferred_element_type=F32) * w_intra
            both = (w_inter * jnp.dot(qh, c_scr[h // 2].astype(BF16), preferred_element_type=F32)
                    + jnp.dot(sc.astype(BF16), v_ext[h], preferred_element_type=F32))
            num = both[:, :ML_V_DIM]
            den = both[:, ML_V_DIM:]
            o_ref[:, h * ML_V_DIM:(h + 1) * ML_V_DIM] = (num / jnp.maximum(jnp.abs(den), floor)).astype(BF16)

    yield
    for h in heads:
        rows = slice((h % 2) * ML_QK_DIM, (h % 2 + 1) * ML_QK_DIM)
        total = jnp.sum(lf_r[h], axis=-1, keepdims=True)
        m_new = jnp.maximum(total + m_prev[h], jnp.max(total - br[h] + li_r[h], axis=-1, keepdims=True))
        w_src = jnp.exp(total - br[h] + li_r[h] - m_new)
        w_carry = jnp.exp(total + m_prev[h] - m_new)
        kw = (kt_pair[h][rows, :].astype(F32) * w_src).astype(BF16)
        c_scr[h // 2, rows, :] = (w_carry * c_scr[h // 2, rows, :]
                                  + jnp.dot(kw, v_ext[h], preferred_element_type=F32))
        m_scr[h] = m_new


def _alternate(*phased):
    live = list(phased)
    while live:
        live = [g for g in live if next(g, StopIteration) is not StopIteration]


def _mlstm_kernel(qf_ref, kf_ref, vf_ref, gcf_ref, grf_ref, qb_ref, kb_ref, vb_ref, gcb_ref, grb_ref,
                  kc_ref, vc_ref, gcc_ref, gcr_ref, of_ref, ob_ref, cf_scr, mf_scr, cb_scr, mb_scr):
    step = pl.program_id(1)

    @pl.when(step == 0)
    def _():
        for ref in (cf_scr, mf_scr, cb_scr, mb_scr):
            ref[...] = jnp.zeros_like(ref)
        kt = kc_ref[...]
        v = vc_ref[...]
        gc = gcc_ref[...]
        gr = gcr_ref[...]
        _alternate(_mlstm_direction_step(False, True, None, kt, v, gc, gr, None, cf_scr, mf_scr),
                   _mlstm_direction_step(True, True, None, kt, v, gc, gr, None, cb_scr, mb_scr))

    @pl.when(step > 0)
    def _():
        _alternate(_mlstm_direction_step(False, False, qf_ref[...], kf_ref[...], vf_ref[...], gcf_ref[...],
                                         grf_ref[...], of_ref, cf_scr, mf_scr),
                   _mlstm_direction_step(True, False, qb_ref[...], kb_ref[...], vb_ref[...], gcb_ref[...],
                                         grb_ref[...], ob_ref, cb_scr, mb_scr))


def _mlstm(q, k, v, gc, gr, kc, vc, gcc, gcr):
    b, s, _ = q.shape
    n_ctx = vc.shape[1]
    length = ML_CHUNK
    n_chunks = s // length

    def chunk(step, reverse):
        c = jnp.maximum(step - 1, 0)
        return (n_chunks - 1 - c) if reverse else c

    def stream_specs(reverse):
        def seq(width):
            return pl.BlockSpec((None, length, width), lambda bi, st: (bi, chunk(st, reverse), 0))

        def seq_t(width):
            return pl.BlockSpec((width, length), lambda bi, st: (0, bi * n_chunks + chunk(st, reverse)))
        return [seq(ML_QK_WIDTH), seq_t(ML_QK_WIDTH), seq(ML_WIDTH), seq(N_GATES), seq_t(N_GATES)]

    def out_spec(reverse):
        return pl.BlockSpec((None, length, ML_WIDTH), lambda bi, st: (bi, chunk(st, reverse), 0))

    state = [pltpu.VMEM((ML_HEADS // 2, 2 * ML_QK_DIM, 2 * ML_V_DIM), F32),
             pltpu.VMEM((ML_HEADS, 1, 1), F32)]
    return pl.pallas_call(
        _mlstm_kernel,
        grid=(b, n_chunks + 1),
        in_specs=stream_specs(False) + stream_specs(True) + [
            pl.BlockSpec((ML_QK_WIDTH, n_ctx), lambda bi, st: (0, bi)),
            pl.BlockSpec((None, n_ctx, ML_WIDTH), lambda bi, st: (bi, 0, 0)),
            pl.BlockSpec((None, n_ctx, N_GATES), lambda bi, st: (bi, 0, 0)),
            pl.BlockSpec((N_GATES, n_ctx), lambda bi, st: (0, bi))],
        out_specs=[out_spec(False), out_spec(True)],
        out_shape=[jax.ShapeDtypeStruct((b, s, ML_WIDTH), BF16)] * 2,
        scratch_shapes=state + state,
        compiler_params=_cparams("arbitrary", "arbitrary"),
        name="mlstm",
    )(q, k, v, gc, gr, q, k, v, gc, gr, kc, vc, gcc, gcr)


def _merge_kernel(x_ref, ona_ref, hf_ref, hb_ref, opre_ref, gna_ref, gml_ref, gtm_ref, scf_ref, shf_ref,
                  ghead_ref, wbna_ref, wbml_ref, wout_ref, gpost_ref, gpre_ref, wrh_ref, wrl_ref, br_ref,
                  x1_ref, h2_ref, topw_ref, route_ref, cnt_ref):
    step = pl.program_id(0)
    tm = x_ref.shape[0]

    @pl.when(step == 0)
    def _():
        cnt_ref[...] = jnp.zeros_like(cnt_ref)

    hsum = hf_ref[...].astype(F32) + hb_ref[...].astype(F32)
    heads = [_rms(hsum[:, h * ML_V_DIM:(h + 1) * ML_V_DIM]) for h in range(ML_HEADS)]
    hn = jnp.concatenate(heads, axis=-1) * ghead_ref[...]
    o_ml = jax.nn.sigmoid(opre_ref[...].astype(F32)) * hn
    merged = (jax.nn.sigmoid(gna_ref[...].astype(F32))
              * jnp.dot(ona_ref[...], wbna_ref[...], preferred_element_type=F32)
              + jax.nn.sigmoid(gml_ref[...].astype(F32))
              * jnp.dot(o_ml.astype(BF16), wbml_ref[...], preferred_element_type=F32))
    mixed = jnp.dot(merged.astype(BF16), wout_ref[...], preferred_element_type=F32)
    x1 = x_ref[...] + gtm_ref[...] * (_rms(mixed) * gpost_ref[...])
    x1_ref[...] = x1
    h2 = _rms(x1) * gpre_ref[...] * (1.0 + scf_ref[...]) + shf_ref[...]
    h2_ref[...] = h2
    h2_hi = h2.astype(BF16)
    h2_lo = (h2 - h2_hi.astype(F32)).astype(BF16)
    logits = (jnp.dot(h2_hi, wrh_ref[...], preferred_element_type=F32)
              + (jnp.dot(h2_hi, wrl_ref[...], preferred_element_type=F32)
                 + jnp.dot(h2_lo, wrh_ref[...], preferred_element_type=F32))) + br_ref[...]

    lane = lax.broadcasted_iota(jnp.int32, logits.shape, 1)
    onehots, top_e, top_l = [], [], []
    for _ in range(TOP_K):
        best = jnp.max(logits, axis=-1, keepdims=True)
        e = jnp.min(jnp.where(logits == best, lane, N_EXPERTS), axis=-1, keepdims=True)
        hit = lane == e
        onehots.append(hit)
        top_e.append(e)
        top_l.append(best)
        logits = jnp.where(hit, -jnp.inf, logits)
    exps = [jnp.exp(l - top_l[0]) for l in top_l]
    total = exps[0] + exps[1] + exps[2] + exps[3]

    counts = (onehots[0].astype(F32) + onehots[1].astype(F32)
              + onehots[2].astype(F32) + onehots[3].astype(F32))
    t = lax.broadcasted_iota(jnp.int32, (tm, tm), 0)
    s = lax.broadcasted_iota(jnp.int32, (tm, tm), 1)
    before = jnp.dot((s < t).astype(BF16), counts.astype(BF16), preferred_element_type=F32) + cnt_ref[...]
    out_lane = lax.broadcasted_iota(jnp.int32, (tm, LANES), 1)
    w_out = jnp.zeros((tm, LANES), F32)
    route = jnp.zeros((tm, LANES), F32)
    for j in range(TOP_K):
        rank = jnp.sum(jnp.where(onehots[j], before, 0.0), axis=-1, keepdims=True)
        w_out = jnp.where(out_lane == j, exps[j] / total, w_out)
        route = jnp.where(out_lane == j, top_e[j].astype(F32), route)
        route = jnp.where(out_lane == TOP_K + j, rank, route)
    topw_ref[...] = w_out
    route_ref[...] = jnp.transpose(route)[:2 * TOP_K, :].astype(jnp.int32)
    cnt_ref[...] += jnp.sum(counts, axis=0, keepdims=True)


def _merge_and_route(x2, o_na, h_f, h_b, o_pre, g_na, g_ml, mod4, seq, g_head, wbna, wbml, wout,
                     g_post, g_pre, w_router, b_router):
    n, d = x2.shape
    tm = MERGE_ROWS
    per_seq = seq // tm

    def rows(width):
        return pl.BlockSpec((tm, width), lambda i: (i, 0))

    def mod_spec(j):
        return pl.BlockSpec((None, None, 1, d), lambda i, _j=j: (i // per_seq, _j, 0, 0))

    w_router_hi = w_router.astype(BF16)
    return pl.pallas_call(
        _merge_kernel,
        grid=(n // tm,),
        in_specs=[rows(d), rows(NA_WIDTH), rows(ML_WIDTH), rows(ML_WIDTH), rows(ML_WIDTH), rows(d), rows(d),
                  mod_spec(2), mod_spec(4), mod_spec(3),
                  _const_spec((1, ML_WIDTH)), _const_spec(wbna.shape), _const_spec(wbml.shape),
                  _const_spec(wout.shape), _const_spec((1, d)), _const_spec((1, d)),
                  _const_spec(w_router.shape), _const_spec(w_router.shape), _const_spec((1, N_EXPERTS))],
        out_specs=[rows(d), rows(d), rows(LANES), pl.BlockSpec((2 * TOP_K, tm), lambda i: (0, i)),
                   pl.BlockSpec((1, N_EXPERTS), lambda i: (0, 0))],
        out_shape=[jax.ShapeDtypeStruct((n, d), F32), jax.ShapeDtypeStruct((n, d), F32),
                   jax.ShapeDtypeStruct((n, LANES), F32), jax.ShapeDtypeStruct((2 * TOP_K, n), jnp.int32),
                   jax.ShapeDtypeStruct((1, N_EXPERTS), F32)],
        compiler_params=_cparams("arbitrary"),
        name="merge_route",
    )(x2, o_na, h_f, h_b, o_pre, g_na, g_ml, mod4, mod4, mod4,
      g_head.reshape(1, ML_WIDTH), wbna, wbml, wout, g_post.reshape(1, d), g_pre.reshape(1, d),
      w_router_hi, (w_router - w_router_hi.astype(F32)).astype(BF16), b_router.reshape(1, N_EXPERTS))


def _dispatch_kernel(pad_end_ref, padded_ref, dest_ref, h_ref, xs_ref, zero_scr, sem):
    tm = h_ref.shape[0]
    blk = zero_scr.shape[0]

    @pl.when(pl.program_id(0) == 0)
    def _():
        zero_scr[...] = jnp.zeros_like(zero_scr)

        def zero_copy(e):
            first = pl.multiple_of(pad_end_ref[e] - blk, blk)
            return pltpu.make_async_copy(zero_scr, xs_ref.at[pl.ds(first, blk)], sem)

        for e in range(N_EXPERTS):
            @pl.when(padded_ref[e] > 0)
            def _():
                zero_copy(e).start()
        for e in range(N_EXPERTS):
            @pl.when(padded_ref[e] > 0)
            def _():
                zero_copy(e).wait()

        def tail_copy(b):
            return pltpu.make_async_copy(zero_scr, xs_ref.at[pl.ds(pl.multiple_of(b * blk, blk), blk)], sem)

        def tail_start(b, carry):
            tail_copy(b).start()
            return carry

        def tail_wait(b, carry):
            tail_copy(b).wait()
            return carry

        first_unused = pad_end_ref[N_EXPERTS - 1] // blk
        lax.fori_loop(first_unused, xs_ref.shape[0] // blk, tail_start, 0)
        lax.fori_loop(first_unused, xs_ref.shape[0] // blk, tail_wait, 0)

    def row_copy(t, k):
        return pltpu.make_async_copy(h_ref.at[pl.ds(t, 1)], xs_ref.at[pl.ds(dest_ref[k, t], 1)], sem)

    for t in range(tm):
        for k in range(TOP_K):
            row_copy(t, k).start(priority=k % 2)
    for t in range(tm):
        for k in range(TOP_K):
            row_copy(t, k).wait()


def _dispatch(pad_end, padded, dest, h2, n_rows):
    n = h2.shape[0]
    tm = MOVE_ROWS
    row_tile = h2.shape[1:]
    grid_spec = pltpu.PrefetchScalarGridSpec(
        num_scalar_prefetch=2,
        grid=(n // tm,),
        in_specs=[pl.BlockSpec((TOP_K, tm), lambda i, pe, pd: (0, i), memory_space=pltpu.SMEM),
                  pl.BlockSpec((tm,) + row_tile, lambda i, pe, pd: (i, 0))],
        out_specs=pl.BlockSpec(memory_space=pl.ANY),
        scratch_shapes=[pltpu.VMEM((EXPERT_ROWS,) + row_tile, h2.dtype), pltpu.SemaphoreType.DMA(())],
    )
    return pl.pallas_call(
        _dispatch_kernel,
        grid_spec=grid_spec,
        out_shape=jax.ShapeDtypeStruct((n_rows,) + row_tile, h2.dtype),
        compiler_params=_cparams("arbitrary"),
        name="moe_dispatch",
    )(pad_end, padded, dest, h2)


def _expert_kernel(blk_e_ref, n_used_ref, x_ref, wg_ref, bg_ref, wl_ref, bl_ref, wd_ref, bd_ref, y_ref,
                   wg_s, wl_s, wd_s):
    i = pl.program_id(0)
    prev = blk_e_ref[jnp.maximum(i - 1, 0)]
    changed = jnp.logical_or(i == 0, blk_e_ref[i] != prev)
    used = i < n_used_ref[0]

    @pl.when(jnp.logical_and(used, changed))
    def _():
        wg_s[...] = wg_ref[...].astype(BF16)
        wl_s[...] = wl_ref[...].astype(BF16)
        wd_s[...] = wd_ref[...].astype(BF16)

    @pl.when(used)
    def _():
        xb = x_ref[...].astype(BF16)
        g = jnp.dot(xb, wg_s[...], preferred_element_type=F32) + bg_ref[...]
        l = jnp.dot(xb, wl_s[...], preferred_element_type=F32) + bl_ref[...]
        g = jnp.minimum(g, SWIGLU_LIMIT)
        l = jnp.clip(l, -SWIGLU_LIMIT, SWIGLU_LIMIT)
        a = g * jax.nn.sigmoid(SWIGLU_ALPHA * g) * (l + 1.0)
        y_ref[...] = jnp.dot(a.astype(BF16), wd_s[...], preferred_element_type=F32) + bd_ref[...]

    @pl.when(jnp.logical_not(used))
    def _():
        y_ref[...] = jnp.zeros_like(y_ref)


def _experts(blk_e, n_used, xs, w_gate, b_gate, w_lin, b_lin, w_down, b_down):
    n_rows = xs.shape[0]
    row_tile = xs.shape[1:]
    e, d, f = w_gate.shape
    tm = EXPERT_ROWS

    def w_spec(shape):
        return pl.BlockSpec((None,) + shape, lambda i, be, nu: (be[i], 0, 0))

    grid_spec = pltpu.PrefetchScalarGridSpec(
        num_scalar_prefetch=2,
        grid=(n_rows // tm,),
        in_specs=[pl.BlockSpec((tm,) + row_tile, lambda i, be, nu: (i, 0)),
                  w_spec((d, f)), w_spec((1, f)), w_spec((d, f)), w_spec((1, f)),
                  w_spec((f, d)), w_spec((1, d))],
        out_specs=pl.BlockSpec((tm,) + row_tile, lambda i, be, nu: (i, 0)),
        scratch_shapes=[pltpu.VMEM((d, f), BF16), pltpu.VMEM((d, f), BF16), pltpu.VMEM((f, d), BF16)],
    )
    return pl.pallas_call(
        _expert_kernel,
        grid_spec=grid_spec,
        out_shape=jax.ShapeDtypeStruct((n_rows,) + row_tile, F32),
        compiler_params=_cparams("arbitrary"),
        name="moe_experts",
    )(blk_e, n_used, xs, w_gate, b_gate.reshape(e, 1, f), w_lin, b_lin.reshape(e, 1, f),
      w_down, b_down.reshape(e, 1, d))


def _combine_kernel(dest_ref, dest_next_ref, x1_ref, w_ref, gtf_ref, gpost_ref, y_ref, o_ref,
                    buf_a, buf_b, sem_a, sem_b):
    step = pl.program_id(0)
    half = x1_ref.shape[0] // 2

    def row_copy(idx_ref, first_tok, buf, sem, t, k):
        return pltpu.make_async_copy(y_ref.at[pl.ds(idx_ref[k, first_tok + t], 1)],
                                     buf.at[k, pl.ds(t, 1)], sem)

    def each_copy(fn):
        for t in range(half):
            for k in range(TOP_K):
                fn(t, k)

    def start_all(idx_ref, first_tok, buf, sem):
        each_copy(lambda t, k: row_copy(idx_ref, first_tok, buf, sem, t, k).start(priority=k % 2))

    def wait_all(buf, sem):
        each_copy(lambda t, k: row_copy(dest_ref, 0, buf, sem, t, k).wait())

    def finish(buf, rows):
        w = w_ref[rows, :]
        ffn = (buf[0] * w[:, 0:1] + buf[1] * w[:, 1:2]) + (buf[2] * w[:, 2:3] + buf[3] * w[:, 3:4])
        o_ref[rows, :] = x1_ref[rows, :] + gtf_ref[...] * (_rms(ffn) * gpost_ref[...])

    @pl.when(step == 0)
    def _():
        start_all(dest_ref, 0, buf_a, sem_a)

    wait_all(buf_a, sem_a)
    start_all(dest_ref, half, buf_b, sem_b)
    finish(buf_a, slice(0, half))
    wait_all(buf_b, sem_b)
    start_all(dest_next_ref, 0, buf_a, sem_a)
    finish(buf_b, slice(half, 2 * half))

    @pl.when(step == pl.num_programs(0) - 1)
    def _():
        wait_all(buf_a, sem_a)


def _combine(dest, x1, top_w, mod4, seq, g_post, y):
    n, d = x1.shape
    tm = MOVE_ROWS
    steps = n // tm
    per_seq = seq // tm
    half_buf = pltpu.VMEM((TOP_K, tm // 2, d), F32)
    return pl.pallas_call(
        _combine_kernel,
        grid=(steps,),
        in_specs=[pl.BlockSpec((TOP_K, tm), lambda i: (0, i), memory_space=pltpu.SMEM),
                  pl.BlockSpec((TOP_K, tm), lambda i: (0, jnp.minimum(i + 1, steps - 1)),
                               memory_space=pltpu.SMEM),
                  pl.BlockSpec((tm, d), lambda i: (i, 0)),
                  pl.BlockSpec((tm, LANES), lambda i: (i, 0)),
                  pl.BlockSpec((None, None, 1, d), lambda i: (i // per_seq, 5, 0, 0)),
                  _const_spec((1, d)),
                  pl.BlockSpec(memory_space=pl.ANY)],
        out_specs=pl.BlockSpec((tm, d), lambda i: (i, 0)),
        out_shape=jax.ShapeDtypeStruct((n, d), F32),
        scratch_shapes=[half_buf, half_buf, pltpu.SemaphoreType.DMA(()), pltpu.SemaphoreType.DMA(())],
        compiler_params=_cparams("arbitrary"),
        name="moe_combine",
    )(dest, dest, x1, top_w, mod4, g_post.reshape(1, d), y)


def _rope_tables(seq):
    n_freq = ML_QK_DIM // 4
    t = jnp.arange(seq)
    row = (t // GRID_W).astype(F32)
    col = (t % GRID_W).astype(F32)
    inv_freq = ROPE_BASE ** (-jnp.arange(n_freq, dtype=F32) / n_freq)
    ang = jnp.concatenate([row[:, None] * inv_freq] * 2 + [col[:, None] * inv_freq] * 2, axis=-1)
    cos = jnp.tile(jnp.cos(ang), (1, ML_HEADS))
    sign = np.tile(np.repeat(np.array([-1.0, 1.0], np.float32), n_freq), 2)
    sin = jnp.tile(jnp.sin(ang) * sign, (1, ML_HEADS))
    return cos, sin


def _layer(x, ctx, mod4, g_mix_pre, g_mix_post, g_ffn_pre, g_ffn_post, w_in, b_gates, rpb, g_head,
           w_branch_na, w_branch_ml, w_out, w_router, b_router, w_gate, b_gate, w_lin, b_lin, w_down, b_down):
    b, s, d = x.shape
    n = b * s
    n_ctx = ctx.shape[1]
    x2 = x.reshape(n, d)

    ctx_cols = (NA_WIDTH, NA_WIDTH, ML_QK_WIDTH, ML_WIDTH, N_GATES)
    lat_cols = (NA_WIDTH, ML_QK_WIDTH, ML_WIDTH, d, d)
    bounds = np.cumsum(ctx_cols + lat_cols)[:-1].tolist()
    (w_nak, w_nav, w_mk, w_mv, w_g, w_naq, w_mq, w_mo, w_gna, w_gml) = jnp.split(w_in, bounds, axis=-1)
    w_naq = w_naq * (NA_HEAD_DIM ** -0.5 * LOG2_E)
    w_mk = w_mk * ML_QK_DIM ** -0.5
    bf = lambda a: a.astype(BF16)
    lat_w = [bf(w_naq), bf(w_nak), bf(w_nav), bf(w_mq), bf(w_mk.T), bf(w_mv), bf(w_mo), bf(w_gna), bf(w_gml),
             bf(w_g), bf(w_g.T)]
    ctx_w = [bf(w_nak), bf(w_nav), bf(w_mk.T), bf(w_mv), bf(w_g), bf(w_g.T)]
    bg_col = b_gates.reshape(1, N_GATES).astype(F32)
    bg_row = b_gates.reshape(N_GATES, 1).astype(F32)
    cos, sin = _rope_tables(s)
    per_seq = s // PROJ_ROWS

    (na_q, na_k, na_v, ml_q, ml_k, ml_v, ml_o, gate_na, gate_ml, g_col, g_row) = _project(
        x2, mod4, lambda i: i // per_seq, g_mix_pre, bg_col, bg_row, lat_w, (cos, sin, per_seq), True)
    (na_kc, na_vc, ml_kc, ml_vc, gc_col, gc_row) = _project(
        ctx.reshape(b * n_ctx, d), mod4, lambda i: b, g_mix_pre, bg_col, bg_row, ctx_w, None, False)

    def seq3(a, length):
        return a.reshape(b, length, a.shape[-1])

    o_na = _neighbourhood_attention(seq3(na_q, s), seq3(na_k, s), seq3(na_v, s),
                                    seq3(na_kc, n_ctx), seq3(na_vc, n_ctx), rpb)
    ml_args = (seq3(ml_q, s), ml_k, seq3(ml_v, s), seq3(g_col, s), g_row,
               ml_kc, seq3(ml_vc, n_ctx), seq3(gc_col, n_ctx), gc_row)
    h_f, h_b = _mlstm(*ml_args)

    x1, h2, top_w, route, counts = _merge_and_route(
        x2, o_na.reshape(n, NA_WIDTH), h_f.reshape(n, ML_WIDTH), h_b.reshape(n, ML_WIDTH), ml_o,
        gate_na, gate_ml, mod4, s, g_head, bf(w_branch_na), bf(w_branch_ml), bf(w_out),
        g_mix_post, g_ffn_pre, w_router, b_router)

    tm = EXPERT_ROWS
    counts = counts.reshape(N_EXPERTS).astype(jnp.int32)
    padded = (counts + tm - 1) // tm * tm
    pad_end = jnp.cumsum(padded)
    pad_start = pad_end - padded
    n_rows = n * TOP_K + N_EXPERTS * tm
    n_blocks = n_rows // tm
    top_e, rank = route[:TOP_K], route[TOP_K:]
    e_sel = top_e[None] == jnp.arange(N_EXPERTS, dtype=jnp.int32)[:, None, None]
    dest = (jnp.sum(jnp.where(e_sel, pad_start[:, None, None], 0), axis=0) + rank).astype(jnp.int32)
    blk_start = jnp.arange(n_blocks, dtype=jnp.int32) * tm
    blk_e = jnp.minimum(jnp.sum(blk_start[:, None] >= pad_end[None, :], axis=1), N_EXPERTS - 1).astype(jnp.int32)
    n_used = (pad_end[-1:] // tm).astype(jnp.int32)

    xs = _dispatch(pad_end.astype(jnp.int32), padded.astype(jnp.int32), dest, h2, n_rows)
    y = _experts(blk_e, n_used, xs, w_gate, b_gate, w_lin, b_lin, w_down, b_down)
    out = _combine(dest, x1, top_w, mod4, s, g_ffn_post, y)
    return out.reshape(b, s, d)


def kernel(x, c, ctx, c_ctx, w_ada, b_ada, g_mix_pre, g_mix_post, g_ffn_pre, g_ffn_post, w_in, b_mlstm_gates,
           rpb, g_mlstm_head, w_branch_na, w_branch_ml, w_out, w_router, b_router, w_gate, b_gate, w_lin,
           b_lin, w_down, b_down):
    b, s, d = x.shape
    depth = w_ada.shape[0]
    pad = (-(b + 1)) % 8
    c_all = jnp.concatenate([c, c_ctx[None, :], jnp.zeros((pad, d), c.dtype)], axis=0)
    for layer in range(depth):
        mod = _ada(c_all, w_ada[layer], b_ada[layer])
        mod4 = mod.reshape(mod.shape[0], 6, 1, d)
        x = _layer(x, ctx, mod4, g_mix_pre[layer], g_mix_post[layer], g_ffn_pre[layer], g_ffn_post[layer],
                   w_in[layer], b_mlstm_gates[layer], rpb[layer], g_mlstm_head[layer], w_branch_na[layer],
                   w_branch_ml[layer], w_out[layer], w_router[layer], b_router[layer], w_gate[layer],
                   b_gate[layer], w_lin[layer], b_lin[layer], w_down[layer], b_down[layer])
    return x
```

```python
import functools

import numpy as np
import jax
import jax.numpy as jnp
from jax import lax
from jax.experimental import pallas as pl
from jax.experimental.pallas import tpu as pltpu

F32 = jnp.float32
BF16 = jnp.bfloat16
HIGHEST = lax.Precision.HIGHEST

GRID_W = 64
NA_HEADS = 8
NA_HEAD_DIM = 64
NA_WIDTH = NA_HEADS * NA_HEAD_DIM
WIN_H = 8
WIN_W = 16
ML_HEADS = 4
ML_QK_DIM = 64
ML_V_DIM = 128
ML_QK_WIDTH = ML_HEADS * ML_QK_DIM
ML_WIDTH = ML_HEADS * ML_V_DIM
N_GATES = 4 * ML_HEADS
GATE_SOFTCAP = 15.0
ROPE_BASE = 10000.0
N_EXPERTS = 32
TOP_K = 4
SWIGLU_ALPHA = 1.702
SWIGLU_LIMIT = 7.0
NORM_EPS = 1e-6
NEG_INF = -1e30
LOG2_E = 1.4426950408889634

LANES = 128
NA_QROWS = 4
NA_WROWS = 12
ML_CHUNK = 256
PROJ_ROWS = 1024
MERGE_ROWS = 512
EXPERT_ROWS = 512
MOVE_ROWS = 512
VMEM_LIMIT = 56 * 1024 * 1024

NT_DIMS = (((1,), (1,)), ((), ()))


def _cparams(*sem):
    return pltpu.CompilerParams(dimension_semantics=sem, vmem_limit_bytes=VMEM_LIMIT)


def _rms(x):
    return x * lax.rsqrt(jnp.mean(x * x, axis=-1, keepdims=True) + NORM_EPS)


def _ada_kernel(c_ref, w_ref, b_ref, o_ref):
    c = c_ref[...]
    s = c * jax.nn.sigmoid(c)
    o_ref[...] = jnp.dot(s, w_ref[...], preferred_element_type=F32, precision=HIGHEST) + b_ref[...]


def _ada(c_all, w_ada, b_ada):
    rows, d = c_all.shape
    n_out = w_ada.shape[1]
    tn = 1536
    return pl.pallas_call(
        _ada_kernel,
        grid=(n_out // tn,),
        in_specs=[pl.BlockSpec((rows, d), lambda j: (0, 0)),
                  pl.BlockSpec((d, tn), lambda j: (0, j)),
                  pl.BlockSpec((1, tn), lambda j: (0, j))],
        out_specs=pl.BlockSpec((rows, tn), lambda j: (0, j)),
        out_shape=jax.ShapeDtypeStruct((rows, n_out), F32),
        compiler_params=_cparams("arbitrary"),
        name="ada_mod",
    )(c_all, w_ada, b_ada.reshape(1, n_out))


def _gate_logs(g, is_forget):
    g = GATE_SOFTCAP * jnp.tanh(g / GATE_SOFTCAP)
    log_sig = jnp.minimum(g, 0.0) - jnp.log(1.0 + jnp.exp(-jnp.abs(g)))
    return jnp.where(is_forget, log_sig, g)


def _proj_kernel(*refs, latent):
    if latent:
        (x_ref, g_ref, sc_ref, sh_ref, cos_ref, sin_ref, cos_t_ref, sin_t_ref, bgc_ref, bgr_ref,
         w_naq, w_nak, w_nav, w_mq, w_mk, w_mv, w_mo, w_gna, w_gml, w_gc, w_gr,
         o_naq, o_nak, o_nav, o_mq, o_mk, o_mv, o_mo, o_gna, o_gml, o_gc, o_gr) = refs
    else:
        (x_ref, g_ref, sc_ref, sh_ref, bgc_ref, bgr_ref,
         w_nak, w_nav, w_mk, w_mv, w_gc, w_gr,
         o_nak, o_nav, o_mk, o_mv, o_gc, o_gr) = refs
    x = x_ref[...]
    h = _rms(x) * g_ref[...]
    h = h * (1.0 + sc_ref[...]) + sh_ref[...]
    hb = h.astype(BF16)

    def mm(w_ref):
        return jnp.dot(hb, w_ref[...], preferred_element_type=F32)

    def mm_t(w_ref):
        return lax.dot_general(w_ref[...], hb, NT_DIMS, preferred_element_type=F32)

    o_nak[...] = mm(w_nak).astype(BF16)
    o_nav[...] = mm(w_nav).astype(BF16)
    o_mv[...] = mm(w_mv).astype(BF16)
    if latent:
        cos = cos_ref[...]
        sin = sin_ref[...]
        o_naq[...] = mm(w_naq).astype(BF16)
        n_freq = ML_QK_DIM // 4
        mq = mm(w_mq)
        lane = lax.broadcasted_iota(jnp.int32, (1, ML_QK_WIDTH), 1)
        mq_partner = jnp.where(lane % (2 * n_freq) < n_freq,
                               pltpu.roll(mq, ML_QK_WIDTH - n_freq, 1), pltpu.roll(mq, n_freq, 1))
        o_mq[...] = (mq * cos + mq_partner * sin).astype(BF16)
        mk = mm_t(w_mk)
        mk_partner = jnp.concatenate(
            [mk[b + (n_freq if half == 0 else 0):b + (2 * n_freq if half == 0 else n_freq), :]
             for b in range(0, ML_QK_WIDTH, 2 * n_freq) for half in range(2)], axis=0)
        o_mk[...] = (mk * cos_t_ref[...] + mk_partner * sin_t_ref[...]).astype(BF16)
        o_mo[...] = mm(w_mo).astype(BF16)
        o_gna[...] = mm(w_gna).astype(BF16)
        o_gml[...] = mm(w_gml).astype(BF16)
    else:
        o_mk[...] = mm_t(w_mk).astype(BF16)
    gc = mm(w_gc) + bgc_ref[...]
    col_id = lax.broadcasted_iota(jnp.int32, gc.shape, 1)
    o_gc[...] = _gate_logs(gc, (col_id // ML_HEADS) % 2 == 1)
    gr = mm_t(w_gr) + bgr_ref[...]
    row_id = lax.broadcasted_iota(jnp.int32, gr.shape, 0)
    o_gr[...] = _gate_logs(gr, (row_id // ML_HEADS) % 2 == 1)


def _const_spec(shape):
    nd = len(shape)
    return pl.BlockSpec(shape, lambda i, _nd=nd: (0,) * _nd)


def _project(x2, mod4, mod_row_fn, g_pre, bg_col, bg_row, weights, tables, latent):
    n, d = x2.shape
    tm = PROJ_ROWS
    grid = (n // tm,)

    def mod_spec(j):
        return pl.BlockSpec((None, None, 1, d), lambda i, _j=j: (mod_row_fn(i), _j, 0, 0))

    in_specs = [pl.BlockSpec((tm, d), lambda i: (i, 0)), _const_spec((1, d)), mod_spec(1), mod_spec(0)]
    args = [x2, g_pre.reshape(1, d), mod4, mod4]
    if latent:
        cos, sin, tiles_per_seq = tables
        in_specs += [pl.BlockSpec((tm, ML_QK_WIDTH), lambda i: (i % tiles_per_seq, 0))] * 2
        in_specs += [pl.BlockSpec((ML_QK_WIDTH, tm), lambda i: (0, i % tiles_per_seq))] * 2
        args += [cos, sin, cos.T, sin.T]
    in_specs += [_const_spec(bg_col.shape), _const_spec(bg_row.shape)]
    args += [bg_col, bg_row]
    for w in weights:
        in_specs.append(pl.BlockSpec(w.shape, lambda i: (0, 0), pipeline_mode=pl.Buffered(1)))
        args.append(w)

    def out(width, dtype=BF16):
        return (jax.ShapeDtypeStruct((n, width), dtype), pl.BlockSpec((tm, width), lambda i: (i, 0)))

    def out_t(width, dtype=BF16):
        return (jax.ShapeDtypeStruct((width, n), dtype), pl.BlockSpec((width, tm), lambda i: (0, i)))

    if latent:
        outs = [out(NA_WIDTH), out(NA_WIDTH), out(NA_WIDTH), out(ML_QK_WIDTH), out_t(ML_QK_WIDTH),
                out(ML_WIDTH), out(ML_WIDTH), out(d), out(d), out(N_GATES, F32)]
    else:
        outs = [out(NA_WIDTH), out(NA_WIDTH), out_t(ML_QK_WIDTH), out(ML_WIDTH), out(N_GATES, F32)]
    outs.append(out_t(N_GATES, F32))
    return pl.pallas_call(
        functools.partial(_proj_kernel, latent=latent),
        grid=grid,
        in_specs=in_specs,
        out_specs=[o[1] for o in outs],
        out_shape=[o[0] for o in outs],
        compiler_params=_cparams("arbitrary"),
        name="in_proj_latent" if latent else "in_proj_ctx",
    )(*args)


def _na_window_start(r0, rows):
    return jnp.clip(r0 - WIN_H // 2, 0, rows - NA_WROWS)


def _na_classes(rows):
    keys, group_class = [], []
    for r0 in range(0, rows, NA_QROWS):
        start = min(max(r0 - WIN_H // 2, 0), rows - NA_WROWS)
        first = tuple(min(max(r0 + i - WIN_H // 2, 0), rows - WIN_H) - start for i in range(NA_QROWS))
        assert all(0 <= f and f + WIN_H <= NA_WROWS for f in first)
        key = (r0 - start, first)
        if key not in keys:
            keys.append(key)
        group_class.append(keys.index(key))
    return keys, np.asarray(group_class, np.int32)


def _na_bias_table(rpb, rows):
    keys, group_class = _na_classes(rows)
    qc = np.arange(GRID_W)[:, None]
    kc = np.arange(GRID_W)[None, :]
    cs = np.clip(qc - WIN_W // 2, 0, GRID_W - WIN_W)
    col_ok = (kc >= cs) & (kc < cs + WIN_W)
    dc = np.clip(kc - qc, -(WIN_W - 1), WIN_W - 1) + (WIN_W - 1)
    sel_c = (dc[:, :, None] == np.arange(2 * WIN_W - 1)).astype(np.float32)
    block = jnp.einsum('hrc,qkc->hrqk', rpb.astype(F32), sel_c, precision=HIGHEST)
    block = jnp.where(col_ok, block * LOG2_E, NEG_INF).astype(BF16)
    masked = jnp.full((NA_HEADS, GRID_W, GRID_W), NEG_INF, BF16)
    tables = []
    for off, first in keys:
        q_rows = []
        for i in range(NA_QROWS):
            in_window = [first[i] <= j < first[i] + WIN_H for j in range(NA_WROWS)]
            q_rows.append(jnp.concatenate(
                [block[:, j - off - i + (WIN_H - 1)] if in_window[j] else masked for j in range(NA_WROWS)],
                axis=-1))
        tables.append(jnp.concatenate(q_rows, axis=1))
    return jnp.stack(tables), jnp.asarray(group_class)


def _na_kernel(cls_ref, q_ref, k_ref, v_ref, kc_ref, vc_ref, bias_ref, o_ref, *, rows):
    del cls_ref
    r0 = pl.program_id(1) * NA_QROWS
    start = pl.multiple_of(_na_window_start(r0, rows) * GRID_W, GRID_W)
    n_win = NA_WROWS * GRID_W
    low = lax.broadcasted_iota(jnp.int32, (1, LANES), 1) < NA_HEAD_DIM
    for pair in range(NA_HEADS // 2):
        sl = slice(pair * LANES, (pair + 1) * LANES)
        q2 = q_ref[:, sl]
        k2 = k_ref[pl.ds(start, n_win), sl]
        v2 = v_ref[pl.ds(start, n_win), sl]
        kc2 = kc_ref[:, sl]
        vc2 = vc_ref[:, sl]
        scores = []
        for half in range(2):
            keep = low if half == 0 else jnp.logical_not(low)
            qh = jnp.where(keep, q2, jnp.zeros_like(q2))
            s_loc = (lax.dot_general(qh, k2, NT_DIMS, preferred_element_type=F32)
                     + bias_ref[2 * pair + half].astype(F32))
            s_ctx = lax.dot_general(qh, kc2, NT_DIMS, preferred_element_type=F32)
            scores.append((s_loc, s_ctx))
        probs = []
        for s_loc, s_ctx in scores:
            m = jnp.maximum(jnp.max(s_loc, axis=-1, keepdims=True), jnp.max(s_ctx, axis=-1, keepdims=True))
            p_loc = jnp.exp2(s_loc - m)
            p_ctx = jnp.exp2(s_ctx - m)
            denom = jnp.sum(p_loc, axis=-1, keepdims=True) + jnp.sum(p_ctx, axis=-1, keepdims=True)
            probs.append((p_loc.astype(BF16), p_ctx.astype(BF16), denom))
        halves = []
        for p_loc, p_ctx, denom in probs:
            o = (jnp.dot(p_loc, v2, preferred_element_type=F32)
                 + jnp.dot(p_ctx, vc2, preferred_element_type=F32))
            halves.append(o / denom)
        o_ref[:, sl] = jnp.where(low, halves[0], halves[1]).astype(BF16)


def _neighbourhood_attention(q, k, v, kc, vc, rpb):
    b, s, w = q.shape
    rows = s // GRID_W
    n_ctx = kc.shape[1]
    bias, group_class = _na_bias_table(rpb, rows)
    nq = NA_QROWS * GRID_W
    grid_spec = pltpu.PrefetchScalarGridSpec(
        num_scalar_prefetch=1,
        grid=(b, rows // NA_QROWS),
        in_specs=[pl.BlockSpec((None, nq, w), lambda bi, g, cls: (bi, g, 0)),
                  pl.BlockSpec((None, s, w), lambda bi, g, cls: (bi, 0, 0)),
                  pl.BlockSpec((None, s, w), lambda bi, g, cls: (bi, 0, 0)),
                  pl.BlockSpec((None, n_ctx, w), lambda bi, g, cls: (bi, 0, 0)),
                  pl.BlockSpec((None, n_ctx, w), lambda bi, g, cls: (bi, 0, 0)),
                  pl.BlockSpec((None,) + bias.shape[1:], lambda bi, g, cls: (cls[g], 0, 0, 0))],
        out_specs=pl.BlockSpec((None, nq, w), lambda bi, g, cls: (bi, g, 0)),
    )
    return pl.pallas_call(
        functools.partial(_na_kernel, rows=rows),
        grid_spec=grid_spec,
        out_shape=jax.ShapeDtypeStruct((b, s, w), BF16),
        compiler_params=_cparams("arbitrary", "arbitrary"),
        name="na_attention",
    )(group_class, q, k, v, kc, vc, bias)


def _split3(x):
    hi = x.astype(BF16)
    r1 = x - hi.astype(F32)
    mid = r1.astype(BF16)
    lo = (r1 - mid.astype(F32)).astype(BF16)
    return hi, mid, lo


def _mlstm_direction_step(reverse, is_ctx, q, kt, v, gc, gr, o_ref, c_scr, m_scr):
    li_base = 2 * ML_HEADS if reverse else 0
    lf_base = li_base + ML_HEADS
    length = kt.shape[1]
    t = lax.broadcasted_iota(jnp.int32, (length, length), 0)
    s = lax.broadcasted_iota(jnp.int32, (length, length), 1)
    valid = (s >= t) if reverse else (s <= t)
    valid_t = (t >= s) if reverse else (t <= s)
    b_col = sum(jnp.dot(valid.astype(BF16), p, preferred_element_type=F32) for p in _split3(gc))
    b_row = sum(jnp.dot(p, valid_t.astype(BF16), preferred_element_type=F32) for p in _split3(gr))
    low = lax.broadcasted_iota(jnp.int32, (1, LANES), 1) < ML_QK_DIM
    ones = jnp.ones((length, ML_V_DIM), BF16)
    heads = range(ML_HEADS)
    li_r = [gr[li_base + h:li_base + h + 1, :] for h in heads]
    lf_r = [gr[lf_base + h:lf_base + h + 1, :] for h in heads]
    br = [b_row[lf_base + h:lf_base + h + 1, :] for h in heads]
    m_prev = [m_scr[h] for h in heads]
    v_ext = [jnp.concatenate([v[:, h * ML_V_DIM:(h + 1) * ML_V_DIM], ones], axis=-1) for h in heads]
    kt_pair = [kt[(h // 2) * LANES:(h // 2 + 1) * LANES, :] for h in heads]

    yield
    if not is_ctx:
        weights = []
        for h in heads:
            bc = b_col[:, lf_base + h:lf_base + h + 1]
            d_mat = jnp.where(valid, bc - br[h] + li_r[h], NEG_INF)
            m_inter = bc + m_prev[h]
            m_t = jnp.maximum(m_inter, jnp.max(d_mat, axis=-1, keepdims=True))
            weights.append((jnp.exp(d_mat - m_t), jnp.exp(m_inter - m_t), jnp.exp(-m_t)))
        yield
        for h in heads:
            w_intra, w_inter, floor = weights[h]
            q2 = q[:, (h // 2) * LANES:(h // 2 + 1) * LANES]
            qh = jnp.where(low if h % 2 == 0 else jnp.logical_not(low), q2, jnp.zeros_like(q2))
            sc = jnp.dot(qh, kt_pair[h], preferred_element_type=F32) * w_intra
            both = (w_inter * jnp.dot(qh, c_scr[h // 2].astype(BF16), preferred_element_type=F32)
                    + jnp.dot(sc.astype(BF16), v_ext[h], preferred_element_type=F32))
            num = both[:, :ML_V_DIM]
            den = both[:, ML_V_DIM:]
            o_ref[:, h * ML_V_DIM:(h + 1) * ML_V_DIM] = (num / jnp.maximum(jnp.abs(den), floor)).astype(BF16)

    yield
    for h in heads:
        rows = slice((h % 2) * ML_QK_DIM, (h % 2 + 1) * ML_QK_DIM)
        total = jnp.sum(lf_r[h], axis=-1, keepdims=True)
        m_new = jnp.maximum(total + m_prev[h], jnp.max(total - br[h] + li_r[h], axis=-1, keepdims=True))
        w_src = jnp.exp(total - br[h] + li_r[h] - m_new)
        w_carry = jnp.exp(total + m_prev[h] - m_new)
        kw = (kt_pair[h][rows, :].astype(F32) * w_src).astype(BF16)
        c_scr[h // 2, rows, :] = (w_carry * c_scr[h // 2, rows, :]
                                  + jnp.dot(kw, v_ext[h], preferred_element_type=F32))
        m_scr[h] = m_new


def _alternate(*phased):
    live = list(phased)
    while live:
        live = [g for g in live if next(g, StopIteration) is not StopIteration]


def _mlstm_kernel(qf_ref, kf_ref, vf_ref, gcf_ref, grf_ref, qb_ref, kb_ref, vb_ref, gcb_ref, grb_ref,
                  kc_ref, vc_ref, gcc_ref, gcr_ref, of_ref, ob_ref, cf_scr, mf_scr, cb_scr, mb_scr):
    step = pl.program_id(1)

    @pl.when(step == 0)
    def _():
        for ref in (cf_scr, mf_scr, cb_scr, mb_scr):
            ref[...] = jnp.zeros_like(ref)
        kt = kc_ref[...]
        v = vc_ref[...]
        gc = gcc_ref[...]
        gr = gcr_ref[...]
        _alternate(_mlstm_direction_step(False, True, None, kt, v, gc, gr, None, cf_scr, mf_scr),
                   _mlstm_direction_step(True, True, None, kt, v, gc, gr, None, cb_scr, mb_scr))

    @pl.when(step > 0)
    def _():
        _alternate(_mlstm_direction_step(False, False, qf_ref[...], kf_ref[...], vf_ref[...], gcf_ref[...],
                                         grf_ref[...], of_ref, cf_scr, mf_scr),
                   _mlstm_direction_step(True, False, qb_ref[...], kb_ref[...], vb_ref[...], gcb_ref[...],
                                         grb_ref[...], ob_ref, cb_scr, mb_scr))


def _mlstm(q, k, v, gc, gr, kc, vc, gcc, gcr):
    b, s, _ = q.shape
    n_ctx = vc.shape[1]
    length = ML_CHUNK
    n_chunks = s // length

    def chunk(step, reverse):
        c = jnp.maximum(step - 1, 0)
        return (n_chunks - 1 - c) if reverse else c

    def stream_specs(reverse):
        def seq(width):
            return pl.BlockSpec((None, length, width), lambda bi, st: (bi, chunk(st, reverse), 0))

        def seq_t(width):
            return pl.BlockSpec((width, length), lambda bi, st: (0, bi * n_chunks + chunk(st, reverse)))
        return [seq(ML_QK_WIDTH), seq_t(ML_QK_WIDTH), seq(ML_WIDTH), seq(N_GATES), seq_t(N_GATES)]

    def out_spec(reverse):
        return pl.BlockSpec((None, length, ML_WIDTH), lambda bi, st: (bi, chunk(st, reverse), 0))

    state = [pltpu.VMEM((ML_HEADS // 2, 2 * ML_QK_DIM, 2 * ML_V_DIM), F32),
             pltpu.VMEM((ML_HEADS, 1, 1), F32)]
    return pl.pallas_call(
        _mlstm_kernel,
        grid=(b, n_chunks + 1),
        in_specs=stream_specs(False) + stream_specs(True) + [
            pl.BlockSpec((ML_QK_WIDTH, n_ctx), lambda bi, st: (0, bi)),
            pl.BlockSpec((None, n_ctx, ML_WIDTH), lambda bi, st: (bi, 0, 0)),
            pl.BlockSpec((None, n_ctx, N_GATES), lambda bi, st: (bi, 0, 0)),
            pl.BlockSpec((N_GATES, n_ctx), lambda bi, st: (0, bi))],
        out_specs=[out_spec(False), out_spec(True)],
        out_shape=[jax.ShapeDtypeStruct((b, s, ML_WIDTH), BF16)] * 2,
        scratch_shapes=state + state,
        compiler_params=_cparams("arbitrary", "arbitrary"),
        name="mlstm",
    )(q, k, v, gc, gr, q, k, v, gc, gr, kc, vc, gcc, gcr)


def _merge_kernel(x_ref, ona_ref, hf_ref, hb_ref, opre_ref, gna_ref, gml_ref, gtm_ref, scf_ref, shf_ref,
                  ghead_ref, wbna_ref, wbml_ref, wout_ref, gpost_ref, gpre_ref, wrh_ref, wrl_ref, br_ref,
                  x1_ref, h2_ref, topw_ref, route_ref, cnt_ref):
    step = pl.program_id(0)
    tm = x_ref.shape[0]

    @pl.when(step == 0)
    def _():
        cnt_ref[...] = jnp.zeros_like(cnt_ref)

    hsum = hf_ref[...].astype(F32) + hb_ref[...].astype(F32)
    heads = [_rms(hsum[:, h * ML_V_DIM:(h + 1) * ML_V_DIM]) for h in range(ML_HEADS)]
    hn = jnp.concatenate(heads, axis=-1) * ghead_ref[...]
    o_ml = jax.nn.sigmoid(opre_ref[...].astype(F32)) * hn
    merged = (jax.nn.sigmoid(gna_ref[...].astype(F32))
              * jnp.dot(ona_ref[...], wbna_ref[...], preferred_element_type=F32)
              + jax.nn.sigmoid(gml_ref[...].astype(F32))
              * jnp.dot(o_ml.astype(BF16), wbml_ref[...], preferred_element_type=F32))
    mixed = jnp.dot(merged.astype(BF16), wout_ref[...], preferred_element_type=F32)
    x1 = x_ref[...] + gtm_ref[...] * (_rms(mixed) * gpost_ref[...])
    x1_ref[...] = x1
    h2 = _rms(x1) * gpre_ref[...] * (1.0 + scf_ref[...]) + shf_ref[...]
    h2_ref[...] = h2
    h2_hi = h2.astype(BF16)
    h2_lo = (h2 - h2_hi.astype(F32)).astype(BF16)
    logits = (jnp.dot(h2_hi, wrh_ref[...], preferred_element_type=F32)
              + (jnp.dot(h2_hi, wrl_ref[...], preferred_element_type=F32)
                 + jnp.dot(h2_lo, wrh_ref[...], preferred_element_type=F32))) + br_ref[...]

    lane = lax.broadcasted_iota(jnp.int32, logits.shape, 1)
    onehots, top_e, top_l = [], [], []
    for _ in range(TOP_K):
        best = jnp.max(logits, axis=-1, keepdims=True)
        e = jnp.min(jnp.where(logits == best, lane, N_EXPERTS), axis=-1, keepdims=True)
        hit = lane == e
        onehots.append(hit)
        top_e.append(e)
        top_l.append(best)
        logits = jnp.where(hit, -jnp.inf, logits)
    exps = [jnp.exp(l - top_l[0]) for l in top_l]
    total = exps[0] + exps[1] + exps[2] + exps[3]

    counts = (onehots[0].astype(F32) + onehots[1].astype(F32)
              + onehots[2].astype(F32) + onehots[3].astype(F32))
    t = lax.broadcasted_iota(jnp.int32, (tm, tm), 0)
    s = lax.broadcasted_iota(jnp.int32, (tm, tm), 1)
    before = jnp.dot((s < t).astype(BF16), counts.astype(BF16), preferred_element_type=F32) + cnt_ref[...]
    out_lane = lax.broadcasted_iota(jnp.int32, (tm, LANES), 1)
    w_out = jnp.zeros((tm, LANES), F32)
    route = jnp.zeros((tm, LANES), F32)
    for j in range(TOP_K):
        rank = jnp.sum(jnp.where(onehots[j], before, 0.0), axis=-1, keepdims=True)
        w_out = jnp.where(out_lane == j, exps[j] / total, w_out)
        route = jnp.where(out_lane == j, top_e[j].astype(F32), route)
        route = jnp.where(out_lane == TOP_K + j, rank, route)
    topw_ref[...] = w_out
    route_ref[...] = jnp.transpose(route)[:2 * TOP_K, :].astype(jnp.int32)
    cnt_ref[...] += jnp.sum(counts, axis=0, keepdims=True)


def _merge_and_route(x2, o_na, h_f, h_b, o_pre, g_na, g_ml, mod4, seq, g_head, wbna, wbml, wout,
                     g_post, g_pre, w_router, b_router):
    n, d = x2.shape
    tm = MERGE_ROWS
    per_seq = seq // tm

    def rows(width):
        return pl.BlockSpec((tm, width), lambda i: (i, 0))

    def mod_spec(j):
        return pl.BlockSpec((None, None, 1, d), lambda i, _j=j: (i // per_seq, _j, 0, 0))

    w_router_hi = w_router.astype(BF16)
    return pl.pallas_call(
        _merge_kernel,
        grid=(n // tm,),
        in_specs=[rows(d), rows(NA_WIDTH), rows(ML_WIDTH), rows(ML_WIDTH), rows(ML_WIDTH), rows(d), rows(d),
                  mod_spec(2), mod_spec(4), mod_spec(3),
                  _const_spec((1, ML_WIDTH)), _const_spec(wbna.shape), _const_spec(wbml.shape),
                  _const_spec(wout.shape), _const_spec((1, d)), _const_spec((1, d)),
                  _const_spec(w_router.shape), _const_spec(w_router.shape), _const_spec((1, N_EXPERTS))],
        out_specs=[rows(d), rows(d), rows(LANES), pl.BlockSpec((2 * TOP_K, tm), lambda i: (0, i)),
                   pl.BlockSpec((1, N_EXPERTS), lambda i: (0, 0))],
        out_shape=[jax.ShapeDtypeStruct((n, d), F32), jax.ShapeDtypeStruct((n, d), F32),
                   jax.ShapeDtypeStruct((n, LANES), F32), jax.ShapeDtypeStruct((2 * TOP_K, n), jnp.int32),
                   jax.ShapeDtypeStruct((1, N_EXPERTS), F32)],
        compiler_params=_cparams("arbitrary"),
        name="merge_route",
    )(x2, o_na, h_f, h_b, o_pre, g_na, g_ml, mod4, mod4, mod4,
      g_head.reshape(1, ML_WIDTH), wbna, wbml, wout, g_post.reshape(1, d), g_pre.reshape(1, d),
      w_router_hi, (w_router - w_router_hi.astype(F32)).astype(BF16), b_router.reshape(1, N_EXPERTS))


def _dispatch_kernel(pad_end_ref, padded_ref, dest_ref, h_ref, xs_ref, zero_scr, sem):
    tm = h_ref.shape[0]
    blk = zero_scr.shape[0]

    @pl.when(pl.program_id(0) == 0)
    def _():
        zero_scr[...] = jnp.zeros_like(zero_scr)

        def zero_copy(e):
            first = pl.multiple_of(pad_end_ref[e] - blk, blk)
            return pltpu.make_async_copy(zero_scr, xs_ref.at[pl.ds(first, blk)], sem)

        for e in range(N_EXPERTS):
            @pl.when(padded_ref[e] > 0)
            def _():
                zero_copy(e).start()
        for e in range(N_EXPERTS):
            @pl.when(padded_ref[e] > 0)
            def _():
                zero_copy(e).wait()

        def tail_copy(b):
            return pltpu.make_async_copy(zero_scr, xs_ref.at[pl.ds(pl.multiple_of(b * blk, blk), blk)], sem)

        def tail_start(b, carry):
            tail_copy(b).start()
            return carry

        def tail_wait(b, carry):
            tail_copy(b).wait()
            return carry

        first_unused = pad_end_ref[N_EXPERTS - 1] // blk
        lax.fori_loop(first_unused, xs_ref.shape[0] // blk, tail_start, 0)
        lax.fori_loop(first_unused, xs_ref.shape[0] // blk, tail_wait, 0)

    def row_copy(t, k):
        return pltpu.make_async_copy(h_ref.at[pl.ds(t, 1)], xs_ref.at[pl.ds(dest_ref[k, t], 1)], sem)

    for t in range(tm):
        for k in range(TOP_K):
            row_copy(t, k).start(priority=k % 2)
    for t in range(tm):
        for k in range(TOP_K):
            row_copy(t, k).wait()


def _dispatch(pad_end, padded, dest, h2, n_rows):
    n = h2.shape[0]
    tm = MOVE_ROWS
    row_tile = h2.shape[1:]
    grid_spec = pltpu.PrefetchScalarGridSpec(
        num_scalar_prefetch=2,
        grid=(n // tm,),
        in_specs=[pl.BlockSpec((TOP_K, tm), lambda i, pe, pd: (0, i), memory_space=pltpu.SMEM),
                  pl.BlockSpec((tm,) + row_tile, lambda i, pe, pd: (i, 0))],
        out_specs=pl.BlockSpec(memory_space=pl.ANY),
        scratch_shapes=[pltpu.VMEM((EXPERT_ROWS,) + row_tile, h2.dtype), pltpu.SemaphoreType.DMA(())],
    )
    return pl.pallas_call(
        _dispatch_kernel,
        grid_spec=grid_spec,
        out_shape=jax.ShapeDtypeStruct((n_rows,) + row_tile, h2.dtype),
        compiler_params=_cparams("arbitrary"),
        name="moe_dispatch",
    )(pad_end, padded, dest, h2)


def _expert_kernel(blk_e_ref, n_used_ref, x_ref, wg_ref, bg_ref, wl_ref, bl_ref, wd_ref, bd_ref, y_ref,
                   wg_s, wl_s, wd_s):
    i = pl.program_id(0)
    prev = blk_e_ref[jnp.maximum(i - 1, 0)]
    changed = jnp.logical_or(i == 0, blk_e_ref[i] != prev)
    used = i < n_used_ref[0]

    @pl.when(jnp.logical_and(used, changed))
    def _():
        wg_s[...] = wg_ref[...].astype(BF16)
        wl_s[...] = wl_ref[...].astype(BF16)
        wd_s[...] = wd_ref[...].astype(BF16)

    @pl.when(used)
    def _():
        xb = x_ref[...].astype(BF16)
        g = jnp.dot(xb, wg_s[...], preferred_element_type=F32) + bg_ref[...]
        l = jnp.dot(xb, wl_s[...], preferred_element_type=F32) + bl_ref[...]
        g = jnp.minimum(g, SWIGLU_LIMIT)
        l = jnp.clip(l, -SWIGLU_LIMIT, SWIGLU_LIMIT)
        a = g * jax.nn.sigmoid(SWIGLU_ALPHA * g) * (l + 1.0)
        y_ref[...] = jnp.dot(a.astype(BF16), wd_s[...], preferred_element_type=F32) + bd_ref[...]

    @pl.when(jnp.logical_not(used))
    def _():
        y_ref[...] = jnp.zeros_like(y_ref)


def _experts(blk_e, n_used, xs, w_gate, b_gate, w_lin, b_lin, w_down, b_down):
    n_rows = xs.shape[0]
    row_tile = xs.shape[1:]
    e, d, f = w_gate.shape
    tm = EXPERT_ROWS

    def w_spec(shape):
        return pl.BlockSpec((None,) + shape, lambda i, be, nu: (be[i], 0, 0))

    grid_spec = pltpu.PrefetchScalarGridSpec(
        num_scalar_prefetch=2,
        grid=(n_rows // tm,),
        in_specs=[pl.BlockSpec((tm,) + row_tile, lambda i, be, nu: (i, 0)),
                  w_spec((d, f)), w_spec((1, f)), w_spec((d, f)), w_spec((1, f)),
                  w_spec((f, d)), w_spec((1, d))],
        out_specs=pl.BlockSpec((tm,) + row_tile, lambda i, be, nu: (i, 0)),
        scratch_shapes=[pltpu.VMEM((d, f), BF16), pltpu.VMEM((d, f), BF16), pltpu.VMEM((f, d), BF16)],
    )
    return pl.pallas_call(
        _expert_kernel,
        grid_spec=grid_spec,
        out_shape=jax.ShapeDtypeStruct((n_rows,) + row_tile, F32),
        compiler_params=_cparams("arbitrary"),
        name="moe_experts",
    )(blk_e, n_used, xs, w_gate, b_gate.reshape(e, 1, f), w_lin, b_lin.reshape(e, 1, f),
      w_down, b_down.reshape(e, 1, d))


def _combine_kernel(dest_ref, dest_next_ref, x1_ref, w_ref, gtf_ref, gpost_ref, y_ref, o_ref,
                    buf_a, buf_b, sem_a, sem_b):
    step = pl.program_id(0)
    half = x1_ref.shape[0] // 2

    def row_copy(idx_ref, first_tok, buf, sem, t, k):
        return pltpu.make_async_copy(y_ref.at[pl.ds(idx_ref[k, first_tok + t], 1)],
                                     buf.at[k, pl.ds(t, 1)], sem)

    def each_copy(fn):
        for t in range(half):
            for k in range(TOP_K):
                fn(t, k)

    def start_all(idx_ref, first_tok, buf, sem):
        each_copy(lambda t, k: row_copy(idx_ref, first_tok, buf, sem, t, k).start(priority=k % 2))

    def wait_all(buf, sem):
        each_copy(lambda t, k: row_copy(dest_ref, 0, buf, sem, t, k).wait())

    def finish(buf, rows):
        w = w_ref[rows, :]
        ffn = (buf[0] * w[:, 0:1] + buf[1] * w[:, 1:2]) + (buf[2] * w[:, 2:3] + buf[3] * w[:, 3:4])
        o_ref[rows, :] = x1_ref[rows, :] + gtf_ref[...] * (_rms(ffn) * gpost_ref[...])

    @pl.when(step == 0)
    def _():
        start_all(dest_ref, 0, buf_a, sem_a)

    wait_all(buf_a, sem_a)
    start_all(dest_ref, half, buf_b, sem_b)
    finish(buf_a, slice(0, half))
    wait_all(buf_b, sem_b)
    start_all(dest_next_ref, 0, buf_a, sem_a)
    finish(buf_b, slice(half, 2 * half))

    @pl.when(step == pl.num_programs(0) - 1)
    def _():
        wait_all(buf_a, sem_a)


def _combine(dest, x1, top_w, mod4, seq, g_post, y):
    n, d = x1.shape
    tm = MOVE_ROWS
    steps = n // tm
    per_seq = seq // tm
    half_buf = pltpu.VMEM((TOP_K, tm // 2, d), F32)
    return pl.pallas_call(
        _combine_kernel,
        grid=(steps,),
        in_specs=[pl.BlockSpec((TOP_K, tm), lambda i: (0, i), memory_space=pltpu.SMEM),
                  pl.BlockSpec((TOP_K, tm), lambda i: (0, jnp.minimum(i + 1, steps - 1)),
                               memory_space=pltpu.SMEM),
                  pl.BlockSpec((tm, d), lambda i: (i, 0)),
                  pl.BlockSpec((tm, LANES), lambda i: (i, 0)),
                  pl.BlockSpec((None, None, 1, d), lambda i: (i // per_seq, 5, 0, 0)),
                  _const_spec((1, d)),
                  pl.BlockSpec(memory_space=pl.ANY)],
        out_specs=pl.BlockSpec((tm, d), lambda i: (i, 0)),
        out_shape=jax.ShapeDtypeStruct((n, d), F32),
        scratch_shapes=[half_buf, half_buf, pltpu.SemaphoreType.DMA(()), pltpu.SemaphoreType.DMA(())],
        compiler_params=_cparams("arbitrary"),
        name="moe_combine",
    )(dest, dest, x1, top_w, mod4, g_post.reshape(1, d), y)


def _rope_tables(seq):
    n_freq = ML_QK_DIM // 4
    t = jnp.arange(seq)
    row = (t // GRID_W).astype(F32)
    col = (t % GRID_W).astype(F32)
    inv_freq = ROPE_BASE ** (-jnp.arange(n_freq, dtype=F32) / n_freq)
    ang = jnp.concatenate([row[:, None] * inv_freq] * 2 + [col[:, None] * inv_freq] * 2, axis=-1)
    cos = jnp.tile(jnp.cos(ang), (1, ML_HEADS))
    sign = np.tile(np.repeat(np.array([-1.0, 1.0], np.float32), n_freq), 2)
    sin = jnp.tile(jnp.sin(ang) * sign, (1, ML_HEADS))
    return cos, sin


def _layer(x, ctx, mod4, g_mix_pre, g_mix_post, g_ffn_pre, g_ffn_post, w_in, b_gates, rpb, g_head,
           w_branch_na, w_branch_ml, w_out, w_router, b_router, w_gate, b_gate, w_lin, b_lin, w_down, b_down):
    b, s, d = x.shape
    n = b * s
    n_ctx = ctx.shape[1]
    x2 = x.reshape(n, d)

    ctx_cols = (NA_WIDTH, NA_WIDTH, ML_QK_WIDTH, ML_WIDTH, N_GATES)
    lat_cols = (NA_WIDTH, ML_QK_WIDTH, ML_WIDTH, d, d)
    bounds = np.cumsum(ctx_cols + lat_cols)[:-1].tolist()
    (w_nak, w_nav, w_mk, w_mv, w_g, w_naq, w_mq, w_mo, w_gna, w_gml) = jnp.split(w_in, bounds, axis=-1)
    w_naq = w_naq * (NA_HEAD_DIM ** -0.5 * LOG2_E)
    w_mk = w_mk * ML_QK_DIM ** -0.5
    bf = lambda a: a.astype(BF16)
    lat_w = [bf(w_naq), bf(w_nak), bf(w_nav), bf(w_mq), bf(w_mk.T), bf(w_mv), bf(w_mo), bf(w_gna), bf(w_gml),
             bf(w_g), bf(w_g.T)]
    ctx_w = [bf(w_nak), bf(w_nav), bf(w_mk.T), bf(w_mv), bf(w_g), bf(w_g.T)]
    bg_col = b_gates.reshape(1, N_GATES).astype(F32)
    bg_row = b_gates.reshape(N_GATES, 1).astype(F32)
    cos, sin = _rope_tables(s)
    per_seq = s // PROJ_ROWS

    (na_q, na_k, na_v, ml_q, ml_k, ml_v, ml_o, gate_na, gate_ml, g_col, g_row) = _project(
        x2, mod4, lambda i: i // per_seq, g_mix_pre, bg_col, bg_row, lat_w, (cos, sin, per_seq), True)
    (na_kc, na_vc, ml_kc, ml_vc, gc_col, gc_row) = _project(
        ctx.reshape(b * n_ctx, d), mod4, lambda i: b, g_mix_pre, bg_col, bg_row, ctx_w, None, False)

    def seq3(a, length):
        return a.reshape(b, length, a.shape[-1])

    o_na = _neighbourhood_attention(seq3(na_q, s), seq3(na_k, s), seq3(na_v, s),
                                    seq3(na_kc, n_ctx), seq3(na_vc, n_ctx), rpb)
    ml_args = (seq3(ml_q, s), ml_k, seq3(ml_v, s), seq3(g_col, s), g_row,
               ml_kc, seq3(ml_vc, n_ctx), seq3(gc_col, n_ctx), gc_row)
    h_f, h_b = _mlstm(*ml_args)

    x1, h2, top_w, route, counts = _merge_and_route(
        x2, o_na.reshape(n, NA_WIDTH), h_f.reshape(n, ML_WIDTH), h_b.reshape(n, ML_WIDTH), ml_o,
        gate_na, gate_ml, mod4, s, g_head, bf(w_branch_na), bf(w_branch_ml), bf(w_out),
        g_mix_post, g_ffn_pre, w_router, b_router)

    tm = EXPERT_ROWS
    counts = counts.reshape(N_EXPERTS).astype(jnp.int32)
    padded = (counts + tm - 1) // tm * tm
    pad_end = jnp.cumsum(padded)
    pad_start = pad_end - padded
    n_rows = n * TOP_K + N_EXPERTS * tm
    n_blocks = n_rows // tm
    top_e, rank = route[:TOP_K], route[TOP_K:]
    e_sel = top_e[None] == jnp.arange(N_EXPERTS, dtype=jnp.int32)[:, None, None]
    dest = (jnp.sum(jnp.where(e_sel, pad_start[:, None, None], 0), axis=0) + rank).astype(jnp.int32)
    blk_start = jnp.arange(n_blocks, dtype=jnp.int32) * tm
    blk_e = jnp.minimum(jnp.sum(blk_start[:, None] >= pad_end[None, :], axis=1), N_EXPERTS - 1).astype(jnp.int32)
    n_used = (pad_end[-1:] // tm).astype(jnp.int32)

    xs = _dispatch(pad_end.astype(jnp.int32), padded.astype(jnp.int32), dest, h2, n_rows)
    y = _experts(blk_e, n_used, xs, w_gate, b_gate, w_lin, b_lin, w_down, b_down)
    out = _combine(dest, x1, top_w, mod4, s, g_ffn_post, y)
    return out.reshape(b, s, d)


def kernel(x, c, ctx, c_ctx, w_ada, b_ada, g_mix_pre, g_mix_post, g_ffn_pre, g_ffn_post, w_in, b_mlstm_gates,
           rpb, g_mlstm_head, w_branch_na, w_branch_ml, w_out, w_router, b_router, w_gate, b_gate, w_lin,
           b_lin, w_down, b_down):
    b, s, d = x.shape
    depth = w_ada.shape[0]
    pad = (-(b + 1)) % 8
    c_all = jnp.concatenate([c, c_ctx[None, :], jnp.zeros((pad, d), c.dtype)], axis=0)
    for layer in range(depth):
        mod = _ada(c_all, w_ada[layer], b_ada[layer])
        mod4 = mod.reshape(mod.shape[0], 6, 1, d)
        x = _layer(x, ctx, mod4, g_mix_pre[layer], g_mix_post[layer], g_ffn_pre[layer], g_ffn_post[layer],
                   w_in[layer], b_mlstm_gates[layer], rpb[layer], g_mlstm_head[layer], w_branch_na[layer],
                   w_branch_ml[layer], w_out[layer], w_router[layer], b_router[layer], w_gate[layer],
                   b_gate[layer], w_lin[layer], b_lin[layer], w_down[layer], b_down[layer])
    return x
```
